```python
import math
import jax, jax.numpy as jnp
from jax import lax
import numpy as np

D_MODEL = 2048
BATCH = 1
SEQ = 8192
DEPTH = 1

N_MEM = 256
MEM_HEADS = 4
MEM_HD = 256
BRANCH_WIDTH = 1024
MEM_WIDTH = MEM_HEADS * MEM_HD
HG_HEADS = 8
HG_DK = 128
HG_DV = 128
HG_WIDTH = HG_HEADS * HG_DK
HG_CHUNK = 64
MLA_HEADS = 8
MLA_Q_RANK = 512
MLA_KV_RANK = 512
MLA_NOPE = 128
MLA_ROPE = 64
MLA_V = 128
MLA_QK = MLA_NOPE + MLA_ROPE
MLA_WIDTH = MLA_HEADS * MLA_V
ROPE_THETA = 10000.0
Q_BLOCK = 128
N_BRANCH = 3
D_FF = -(-8 * D_MODEL // (3 * 256)) * 256
ALPHA = (2.0 * DEPTH) ** 0.25
BETA = (8.0 * DEPTH) ** -0.25
LN_EPS = 1e-5
RMS_EPS = 1e-6
IN_SPLITS = (HG_WIDTH, HG_WIDTH, HG_WIDTH, HG_WIDTH, HG_WIDTH,
             MLA_Q_RANK, MLA_KV_RANK, MLA_ROPE, MEM_WIDTH, N_BRANCH * D_MODEL)
IN_WIDTH = sum(IN_SPLITS)

kernel_name = "hybrid_hgrn2_mla_memory_deepnorm_encoder"


def _split(a, sizes):
    idx, acc = [], 0
    for s in sizes[:-1]:
        acc += s
        idx.append(acc)
    return jnp.split(a, idx, axis=-1)


def _layernorm(x, g, b):
    xf = x.astype(jnp.float32)
    mu = jnp.mean(xf, -1, keepdims=True)
    xc = xf - mu
    var = jnp.mean(xc * xc, -1, keepdims=True)
    return (xc * lax.rsqrt(var + LN_EPS) * g.astype(jnp.float32) + b.astype(jnp.float32)).astype(x.dtype)


def _rmsnorm(x, g):
    xf = x.astype(jnp.float32)
    ms = jnp.mean(xf * xf, -1, keepdims=True)
    return (xf * lax.rsqrt(ms + RMS_EPS) * g.astype(jnp.float32)).astype(x.dtype)


def _rope(x, cos, sin):
    xf = x.astype(jnp.float32)
    x1, x2 = jnp.split(xf, 2, axis=-1)
    return jnp.concatenate([x1 * cos - x2 * sin, x2 * cos + x1 * sin], -1).astype(x.dtype)


def _chunk_gla(q, k, v, log_f):
    Z, B, H, T, dk = q.shape
    dv = v.shape[-1]
    n = T // HG_CHUNK

    def to_chunks(a):
        a = a.reshape(Z, B, H, n, HG_CHUNK, a.shape[-1])
        return jnp.moveaxis(a, 3, 0)

    tri = jnp.tril(jnp.ones((HG_CHUNK, HG_CHUNK), dtype=bool))[:, :, None]

    def step(S, inp):
        qc, kc, vc, gc = inp
        b = jnp.cumsum(gc, axis=-2)
        inter = jnp.einsum('zbhtk,zbhkv->zbhtv', qc * jnp.exp(b), S)
        diff = b[..., :, None, :] - b[..., None, :, :]
        decay = jnp.exp(jnp.where(tri, diff, -jnp.inf))
        att = jnp.einsum('zbhtk,zbhsk,zbhtsk->zbhts', qc, kc, decay)
        intra = jnp.einsum('zbhts,zbhsv->zbhtv', att, vc)
        b_last = b[..., -1:, :]
        S = jnp.swapaxes(jnp.exp(b_last), -1, -2) * S + jnp.einsum(
            'zbhsk,zbhsv->zbhkv', kc * jnp.exp(b_last - b), vc)
        return S, inter + intra

    S0 = jnp.zeros((Z, B, H, dk, dv), jnp.float32)
    _, o = lax.scan(step, S0, (to_chunks(q), to_chunks(k), to_chunks(v), to_chunks(log_f)))
    o = jnp.moveaxis(o, 0, 3)
    return o.reshape(Z, B, H, T, dv)


def _hgrn2_branch(q_raw, i_raw, f_fw_raw, f_bw_raw, g_raw, lb, norm_g):
    B, S, _ = q_raw.shape
    f32 = jnp.float32

    def heads(a):
        a = a.reshape(a.shape[:-1] + (HG_HEADS, -1))
        return jnp.swapaxes(a, -2, -3)

    q = heads(jax.nn.silu(q_raw.astype(f32)))
    v = heads(i_raw.astype(f32))
    f_raw = jnp.stack([f_fw_raw, f_bw_raw]).astype(f32)
    lbz = lb[:, None, None, :]
    f = lbz + (1.0 - lbz) * jax.nn.sigmoid(f_raw)
    k = heads(1.0 - f)
    log_f = heads(jnp.log(f))
    flip = lambda a: jnp.flip(a, axis=-2)
    qz = jnp.stack([q, flip(q)])
    vz = jnp.stack([v, flip(v)])
    kz = jnp.stack([k[0], flip(k[1])])
    gz = jnp.stack([log_f[0], flip(log_f[1])])
    o = _chunk_gla(qz, kz, vz, gz)
    o = o[0] + flip(o[1])
    o = _rmsnorm(jnp.swapaxes(o, 1, 2), norm_g)
    o = o.reshape(B, S, HG_WIDTH) * jax.nn.sigmoid(g_raw.astype(f32))
    return o.astype(q_raw.dtype)


def _mla_branch(cq_raw, ckv_raw, krope_raw, cos, sin, g_cq, g_ckv, w_uq, w_ukv):
    B, S, _ = cq_raw.shape
    q = (_rmsnorm(cq_raw, g_cq) @ w_uq).reshape(B, S, MLA_HEADS, MLA_QK)
    q = jnp.concatenate([q[..., :MLA_NOPE],
                         _rope(q[..., MLA_NOPE:], cos[:, :, None, :], sin[:, :, None, :])], -1)
    kv = (_rmsnorm(ckv_raw, g_ckv) @ w_ukv).reshape(B, S, MLA_HEADS, MLA_NOPE + MLA_V)
    k_rope = _rope(krope_raw, cos, sin)
    k = jnp.concatenate([kv[..., :MLA_NOPE],
                         jnp.broadcast_to(k_rope[:, :, None, :], (B, S, MLA_HEADS, MLA_ROPE))], -1)
    v = kv[..., MLA_NOPE:]
    scale = MLA_QK ** -0.5
    qb = jnp.moveaxis(q.reshape(B, S // Q_BLOCK, Q_BLOCK, MLA_HEADS, MLA_QK), 1, 0)

    def attend(qblk):
        s = jnp.einsum('bqhd,bkhd->bhqk', qblk, k).astype(jnp.float32) * scale
        p = jax.nn.softmax(s, axis=-1).astype(v.dtype)
        return jnp.einsum('bhqk,bkhd->bqhd', p, v)

    o = lax.map(attend, qb)
    return jnp.moveaxis(o, 0, 1).reshape(B, S, MLA_WIDTH)


def _memory_branch(q_raw, mem, w_kv):
    B, S, _ = q_raw.shape
    M = mem.shape[1]
    q = q_raw.reshape(B, S, MEM_HEADS, MEM_HD)
    kv = (mem @ w_kv).reshape(B, M, 2, MEM_HEADS, MEM_HD)
    k, v = kv[:, :, 0], kv[:, :, 1]
    s = jnp.einsum('bqhd,bmhd->bhqm', q, k).astype(jnp.float32) * (MEM_HD ** -0.5)
    p = jax.nn.softmax(s, axis=-1).astype(v.dtype)
    return jnp.einsum('bhqm,bmhd->bqhd', p, v).reshape(B, S, MEM_WIDTH)


def setup_inputs(seed: int = 0) -> dict:
    key = jax.random.key(seed)
    ks = jax.random.split(key, 24)
    f32 = jnp.float32
    L = DEPTH

    def nrm(k, shape, scale):
        return jax.random.normal(k, shape, f32) * scale

    def gain(k, shape):
        return 1.0 + 0.02 * jax.random.normal(k, shape, f32)

    return {
        "x": nrm(ks[0], (BATCH, SEQ, D_MODEL), 1.0),
        "mem": nrm(ks[1], (BATCH, N_MEM, D_MODEL), 1.0),
        "positions": jnp.tile(jnp.arange(SEQ, dtype=jnp.int32)[None, :], (BATCH, 1)),
        "ln_emb_g": gain(ks[2], (D_MODEL,)),
        "ln_emb_b": nrm(ks[3], (D_MODEL,), 0.02),
        "hgrn_lb_logits": nrm(ks[4], (2, DEPTH + 1, HG_WIDTH), 0.5),
        "w_in": nrm(ks[5], (L, D_MODEL, IN_WIDTH), D_MODEL ** -0.5),
        "hgrn_norm_g": gain(ks[6], (L, HG_DV)),
        "mla_g_cq": gain(ks[7], (L, MLA_Q_RANK)),
        "mla_g_ckv": gain(ks[8], (L, MLA_KV_RANK)),
        "mla_w_uq": nrm(ks[9], (L, MLA_Q_RANK, MLA_HEADS * MLA_QK), MLA_Q_RANK ** -0.5),
        "mla_w_ukv": nrm(ks[10], (L, MLA_KV_RANK, MLA_HEADS * (MLA_NOPE + MLA_V)), MLA_KV_RANK ** -0.5),
        "mem_w_kv": nrm(ks[11], (L, D_MODEL, 2 * MEM_WIDTH), D_MODEL ** -0.5),
        "w_branch": nrm(ks[12], (L, N_BRANCH, BRANCH_WIDTH, D_MODEL), BETA * BRANCH_WIDTH ** -0.5),
        "w_o": nrm(ks[13], (L, D_MODEL, D_MODEL), BETA * D_MODEL ** -0.5),
        "ln1_g": gain(ks[14], (L, D_MODEL)),
        "ln1_b": nrm(ks[15], (L, D_MODEL), 0.02),
        "w_ffn_gate": nrm(ks[16], (L, D_MODEL, D_FF), D_MODEL ** -0.5),
        "w_ffn_up": nrm(ks[17], (L, D_MODEL, D_FF), D_MODEL ** -0.5),
        "w_ffn_down": nrm(ks[18], (L, D_FF, D_MODEL), BETA * D_FF ** -0.5),
        "ln2_g": gain(ks[19], (L, D_MODEL)),
        "ln2_b": nrm(ks[20], (L, D_MODEL), 0.02),
    }


def reference(x, mem, positions, ln_emb_g, ln_emb_b, hgrn_lb_logits, w_in, hgrn_norm_g,
              mla_g_cq, mla_g_ckv, mla_w_uq, mla_w_ukv, mem_w_kv, w_branch, w_o,
              ln1_g, ln1_b, w_ffn_gate, w_ffn_up, w_ffn_down, ln2_g, ln2_b):
    B, S, D = x.shape
    f32 = jnp.float32
    half = MLA_ROPE // 2
    inv_freq = jnp.power(ROPE_THETA, -jnp.arange(half, dtype=f32) / half)
    ang = positions.astype(f32)[..., None] * inv_freq
    cos, sin = jnp.cos(ang), jnp.sin(ang)
    lb_all = jnp.cumsum(jax.nn.softmax(hgrn_lb_logits.astype(f32), axis=1), axis=1)

    h = _layernorm(x, ln_emb_g, ln_emb_b)
    for l in range(DEPTH):
        proj = h @ w_in[l]
        (q_hg, i_hg, f_fw, f_bw, g_hg, c_q, c_kv, k_rope, q_mem, gates) = _split(proj, IN_SPLITS)
        y_hg = _hgrn2_branch(q_hg, i_hg, f_fw, f_bw, g_hg, lb_all[:, l], hgrn_norm_g[l])
        y_mla = _mla_branch(c_q, c_kv, k_rope, cos, sin, mla_g_cq[l], mla_g_ckv[l],
                            mla_w_uq[l], mla_w_ukv[l])
        y_mem = _memory_branch(q_mem, mem, mem_w_kv[l])
        gsig = jax.nn.sigmoid(gates.astype(f32)).astype(h.dtype).reshape(B, S, N_BRANCH, D)
        merged = (gsig[:, :, 0] * (y_hg @ w_branch[l, 0])
                  + gsig[:, :, 1] * (y_mla @ w_branch[l, 1])
                  + gsig[:, :, 2] * (y_mem @ w_branch[l, 2]))
        mix = merged @ w_o[l]
        h = _layernorm(ALPHA * h + mix, ln1_g[l], ln1_b[l])
        ff = (jax.nn.silu(h @ w_ffn_gate[l]) * (h @ w_ffn_up[l])) @ w_ffn_down[l]
        h = _layernorm(ALPHA * h + ff, ln2_g[l], ln2_b[l])
    return h
```

```python
import functools

import jax
import jax.numpy as jnp
from jax import lax
from jax.experimental import pallas as pl
from jax.experimental.pallas import tpu as pltpu

HG_HEADS = 8
HG_D = 128
MLA_HEADS = 8
MLA_NOPE = 128
MLA_ROPE = 64
MLA_V = 128
MLA_QK = MLA_NOPE + MLA_ROPE
MLA_QK_PAD = 256
MEM_HEADS = 4
MEM_HD = 256
N_BRANCH = 3
ROPE_THETA = 10000.0
LN_EPS = 1e-5
RMS_EPS = 1e-6
DEPTH = 1
ALPHA = (2.0 * DEPTH) ** 0.25

LANES = 128
SUBLANES = 8
VMEM_LIMIT = 56 * 1024 * 1024

HG_CHUNK = 64
HG_SUB = SUBLANES

bf16 = jnp.bfloat16
f32 = jnp.float32


def _cparams(sem):
    return pltpu.CompilerParams(dimension_semantics=sem, vmem_limit_bytes=VMEM_LIMIT)


def _dot(a, b):
    return jnp.dot(a, b, preferred_element_type=f32)


def _dot_nt(a, b):
    return lax.dot_general(a, b, (((1,), (1,)), ((), ())), preferred_element_type=f32)


def _dot_tn(a, b):
    return lax.dot_general(a, b, (((0,), (0,)), ((), ())), preferred_element_type=f32)


def _ln_rows(x, g, b):
    mu = jnp.mean(x, axis=-1, keepdims=True)
    xc = x - mu
    var = jnp.mean(xc * xc, axis=-1, keepdims=True)
    return xc * lax.rsqrt(var + LN_EPS) * g + b


def _ln_kernel(x_ref, g_ref, b_ref, o32_ref, o16_ref):
    y = _ln_rows(x_ref[...], g_ref[...], b_ref[...])
    o32_ref[...] = y
    o16_ref[...] = y.astype(bf16)


def _layernorm(x, g, b, tm=512):
    m, d = x.shape
    return pl.pallas_call(
        _ln_kernel,
        grid=(m // tm,),
        in_specs=[pl.BlockSpec((tm, d), lambda i: (i, 0)),
                  pl.BlockSpec((1, d), lambda i: (0, 0)),
                  pl.BlockSpec((1, d), lambda i: (0, 0))],
        out_specs=[pl.BlockSpec((tm, d), lambda i: (i, 0)),
                   pl.BlockSpec((tm, d), lambda i: (i, 0))],
        out_shape=[jax.ShapeDtypeStruct((m, d), f32), jax.ShapeDtypeStruct((m, d), bf16)],
        compiler_params=_cparams(("parallel",)),
        name="ln_embed",
    )(x, g.reshape(1, d), b.reshape(1, d))


def _mm_kernel(a_ref, w_ref, *rest, epilogue):
    o_ref = rest[-1]
    acc = _dot(a_ref[...], w_ref[...])
    o_ref[...] = epilogue(acc, *[r[...] for r in rest[:-1]]).astype(o_ref.dtype)


def _matmul(a, w, *, tm, tn, out_dtype, epilogue=None, row_extras=(), col_extras=(), name):
    m, k = a.shape
    n = w.shape[1]
    if epilogue is None:
        epilogue = lambda acc: acc
    in_specs = [pl.BlockSpec((tm, k), lambda i, j: (i, 0)),
                pl.BlockSpec((k, tn), lambda i, j: (0, j))]
    for e in row_extras:
        in_specs.append(pl.BlockSpec((tm, e.shape[1]), lambda i, j: (i, 0)))
    for e in col_extras:
        in_specs.append(pl.BlockSpec((e.shape[0], tn), lambda i, j: (0, j)))
    return pl.pallas_call(
        functools.partial(_mm_kernel, epilogue=epilogue),
        grid=(m // tm, n // tn),
        in_specs=in_specs,
        out_specs=pl.BlockSpec((tm, tn), lambda i, j: (i, j)),
        out_shape=jax.ShapeDtypeStruct((m, n), out_dtype),
        compiler_params=_cparams(("parallel", "parallel")),
        name=name,
    )(a, w, *row_extras, *col_extras)


def _rms_epilogue(acc, g):
    ms = jnp.mean(acc * acc, axis=-1, keepdims=True)
    return acc * lax.rsqrt(ms + RMS_EPS) * g


def _rope_block(blk, c, s1, s2):
    half = MLA_ROPE // 2
    return blk * c + pltpu.roll(blk, LANES - half, 1) * s1 + pltpu.roll(blk, half, 1) * s2


def _krope_epilogue(acc, c, s1, s2):
    return _rope_block(acc, c, s1, s2)


def _mla_q_kernel(cq_ref, w_ref, c_ref, s1_ref, s2_ref, o_ref):
    q = _dot(cq_ref[...], w_ref[...]) * (MLA_QK ** -0.5)
    c, s1, s2 = c_ref[...], s1_ref[...], s2_ref[...]
    for h in range(MLA_HEADS):
        base = h * MLA_QK_PAD
        o_ref[:, base:base + MLA_NOPE] = q[:, base:base + MLA_NOPE].astype(bf16)
        o_ref[:, base + MLA_NOPE:base + MLA_QK_PAD] = _rope_block(
            q[:, base + MLA_NOPE:base + MLA_QK_PAD], c, s1, s2).astype(bf16)


def _mla_q(cqn, w_q, rope_c, rope_s1, rope_s2, tm=512):
    m, r = cqn.shape
    n = w_q.shape[1]
    return pl.pallas_call(
        _mla_q_kernel,
        grid=(m // tm,),
        in_specs=[pl.BlockSpec((tm, r), lambda i: (i, 0)),
                  pl.BlockSpec((r, n), lambda i: (0, 0)),
                  pl.BlockSpec((tm, LANES), lambda i: (i, 0)),
                  pl.BlockSpec((tm, LANES), lambda i: (i, 0)),
                  pl.BlockSpec((tm, LANES), lambda i: (i, 0))],
        out_specs=pl.BlockSpec((tm, n), lambda i: (i, 0)),
        out_shape=jax.ShapeDtypeStruct((m, n), bf16),
        compiler_params=_cparams(("parallel",)),
        name="mla_q_up",
    )(cqn, w_q, rope_c, rope_s1, rope_s2)


def _mla_kv_kernel(ckv_ref, w_ref, kr_ref, k_ref, v_ref):
    kv = _dot(ckv_ref[...], w_ref[...])
    kr = kr_ref[...]
    nk = MLA_HEADS * MLA_NOPE
    for h in range(MLA_HEADS):
        base = h * MLA_QK_PAD
        k_ref[:, base:base + MLA_NOPE] = kv[:, h * MLA_NOPE:(h + 1) * MLA_NOPE].astype(bf16)
        k_ref[:, base + MLA_NOPE:base + MLA_QK_PAD] = kr
    v_ref[...] = kv[:, nk:].astype(bf16)


def _mla_kv(ckvn, w_kv, krope, tm=512):
    m, r = ckvn.shape
    n = w_kv.shape[1]
    nk = MLA_HEADS * MLA_QK_PAD
    nv = MLA_HEADS * MLA_V
    return pl.pallas_call(
        _mla_kv_kernel,
        grid=(m // tm,),
        in_specs=[pl.BlockSpec((tm, r), lambda i: (i, 0)),
                  pl.BlockSpec((r, n), lambda i: (0, 0)),
                  pl.BlockSpec((tm, LANES), lambda i: (i, 0))],
        out_specs=[pl.BlockSpec((tm, nk), lambda i: (i, 0)),
                   pl.BlockSpec((tm, nv), lambda i: (i, 0))],
        out_shape=[jax.ShapeDtypeStruct((m, nk), bf16), jax.ShapeDtypeStruct((m, nv), bf16)],
        compiler_params=_cparams(("parallel",)),
        name="mla_kv_up",
    )(ckvn, w_kv, krope)


def _mla_attn_kernel(q_ref, k_ref, v_ref, o_ref, *, tk):
    q = q_ref[...]
    tq = q.shape[0]
    nkv = k_ref.shape[0] // tk

    def body(c, carry):
        m, l, acc = carry
        off = pl.multiple_of(c * tk, tk)
        s = _dot_nt(q, k_ref[pl.ds(off, tk), :])
        m_new = jnp.maximum(m, jnp.max(s, axis=-1, keepdims=True))
        a = jnp.exp(m - m_new)
        p = jnp.exp(s - m_new)
        l = a * l + jnp.sum(p, axis=-1, keepdims=True)
        acc = a * acc + _dot(p.astype(bf16), v_ref[pl.ds(off, tk), :])
        return m_new, l, acc

    m0 = jnp.full((tq, 1), -jnp.inf, f32)
    l0 = jnp.zeros((tq, 1), f32)
    a0 = jnp.zeros((tq, MLA_V), f32)
    _, l, acc = lax.fori_loop(0, nkv, body, (m0, l0, a0))
    o_ref[...] = (acc / l).astype(o_ref.dtype)


def _mla_attention(q, k, v, tq=256, tk=512):
    s = q.shape[0]
    return pl.pallas_call(
        functools.partial(_mla_attn_kernel, tk=tk),
        grid=(MLA_HEADS, s // tq),
        in_specs=[pl.BlockSpec((tq, MLA_QK_PAD), lambda h, i: (i, h)),
                  pl.BlockSpec((s, MLA_QK_PAD), lambda h, i: (0, h)),
                  pl.BlockSpec((s, MLA_V), lambda h, i: (0, h))],
        out_specs=pl.BlockSpec((tq, MLA_V), lambda h, i: (i, h)),
        out_shape=jax.ShapeDtypeStruct((s, MLA_HEADS * MLA_V), bf16),
        compiler_params=_cparams(("parallel", "parallel")),
        name="mla_attention",
    )(q, k, v)


def _mem_attn_kernel(q_ref, k_ref, v_ref, o_ref):
    for h in range(MEM_HEADS):
        sl = slice(h * MEM_HD, (h + 1) * MEM_HD)
        s = _dot_nt(q_ref[:, sl], k_ref[:, sl])
        m = jnp.max(s, axis=-1, keepdims=True)
        p = jnp.exp(s - m)
        l = jnp.sum(p, axis=-1, keepdims=True)
        o = _dot(p.astype(bf16), v_ref[:, sl])
        o_ref[:, sl] = (o / l).astype(o_ref.dtype)


def _mem_attention(q, k, v, tm=512):
    s, w = q.shape
    nm = k.shape[0]
    return pl.pallas_call(
        _mem_attn_kernel,
        grid=(s // tm,),
        in_specs=[pl.BlockSpec((tm, w), lambda i: (i, 0)),
                  pl.BlockSpec((nm, w), lambda i: (0, 0)),
                  pl.BlockSpec((nm, w), lambda i: (0, 0))],
        out_specs=pl.BlockSpec((tm, w), lambda i: (i, 0)),
        out_shape=jax.ShapeDtypeStruct((s, w), bf16),
        compiler_params=_cparams(("parallel",)),
        name="mem_attention",
    )(q, k, v)


def _split3(x):
    hi = x.astype(bf16)
    r1 = x - hi.astype(f32)
    mid = r1.astype(bf16)
    lo = (r1 - mid.astype(f32)).astype(bf16)
    return hi, mid, lo


def _hgrn_chunk(q, v16, fraw, lb, state, tri, kbuf, bbuf, *, rev):
    C = q.shape[0]
    nb = C // HG_SUB
    f = lb + (1.0 - lb) * jax.nn.sigmoid(fraw)
    k = 1.0 - f
    g = jnp.log(f)
    g_hi, g_mid, g_lo = _split3(g)
    b = _dot(tri, g_hi) + _dot(tri, g_mid) + _dot(tri, g_lo)
    kbuf[...] = k
    bbuf[...] = b
    edge = (C - 1) if not rev else 0
    b_edge = bbuf[edge:edge + 1, :]

    o = _dot_nt((q * jnp.exp(b)).astype(bf16), state.astype(bf16))
    kdec = (k * jnp.exp(b_edge - b)).astype(bf16)
    new_state = jnp.exp(b_edge) * state + _dot_tn(v16, kdec)

    lane = lax.broadcasted_iota(jnp.int32, (HG_SUB, C), 1)
    sub = lax.broadcasted_iota(jnp.int32, (HG_SUB, C), 0)
    rows = []
    for i in range(nb):
        r0 = i * HG_SUB
        q_i = q[r0:r0 + HG_SUB]
        b_i = b[r0:r0 + HG_SUB]
        a_blk = jnp.zeros((HG_SUB, C), f32)
        for s in range(HG_SUB):
            ks = kbuf[r0 + s:r0 + s + 1, :]
            bs = bbuf[r0 + s:r0 + s + 1, :]
            w = q_i * (ks * jnp.exp(b_i - bs))
            a_blk = jnp.where(lane == r0 + s, jnp.sum(w, axis=-1, keepdims=True), a_blk)
        keep = (sub >= lane - r0) if not rev else (sub <= lane - r0)
        a_blk = jnp.where(keep, a_blk, 0.0)
        if not rev and i > 0:
            piv = bbuf[r0 - 1:r0, :]
            qd = (q_i * jnp.exp(b_i - piv)).astype(bf16)
            kd = k[:r0] * jnp.exp(piv - b[:r0])
            kd = jnp.concatenate([kd, jnp.zeros((C - r0, kd.shape[1]), f32)], axis=0).astype(bf16)
            a_blk = a_blk + _dot_nt(qd, kd)
        if rev and i < nb - 1:
            r1 = r0 + HG_SUB
            piv = bbuf[r1:r1 + 1, :]
            qd = (q_i * jnp.exp(b_i - piv)).astype(bf16)
            kd = k[r1:] * jnp.exp(piv - b[r1:])
            kd = jnp.concatenate([jnp.zeros((r1, kd.shape[1]), f32), kd], axis=0).astype(bf16)
            a_blk = a_blk + _dot_nt(qd, kd)
        rows.append(a_blk)
    a = jnp.concatenate(rows, axis=0).astype(bf16)
    o = o + _dot(a, v16)
    return o, new_state


def _hgrn_kernel(qf_ref, vf_ref, ff_ref, qb_ref, vb_ref, fb_ref, lb_ref, of_ref, ob_ref,
                 sf_ref, sb_ref, kbuf_f, bbuf_f, kbuf_b, bbuf_b):
    n = pl.program_id(1)
    C = HG_CHUNK
    nchunk = qf_ref.shape[0] // C

    @pl.when(n == 0)
    def _():
        sf_ref[...] = jnp.zeros_like(sf_ref)
        sb_ref[...] = jnp.zeros_like(sb_ref)

    r = lax.broadcasted_iota(jnp.int32, (C, C), 0)
    c = lax.broadcasted_iota(jnp.int32, (C, C), 1)
    tri_f = (c <= r).astype(bf16)
    tri_b = (c >= r).astype(bf16)
    lb_f = lb_ref[0:1, :]
    lb_b = lb_ref[1:2, :]

    def body(j, carry):
        off_f = pl.multiple_of(j * C, C)
        off_b = pl.multiple_of((nchunk - 1 - j) * C, C)
        qf = jax.nn.silu(qf_ref[pl.ds(off_f, C), :])
        o, s_new = _hgrn_chunk(qf, vf_ref[pl.ds(off_f, C), :].astype(bf16), ff_ref[pl.ds(off_f, C), :],
                               lb_f, sf_ref[...], tri_f, kbuf_f, bbuf_f, rev=False)
        of_ref[pl.ds(off_f, C), :] = o
        sf_ref[...] = s_new
        qb = jax.nn.silu(qb_ref[pl.ds(off_b, C), :])
        o, s_new = _hgrn_chunk(qb, vb_ref[pl.ds(off_b, C), :].astype(bf16), fb_ref[pl.ds(off_b, C), :],
                               lb_b, sb_ref[...], tri_b, kbuf_b, bbuf_b, rev=True)
        ob_ref[pl.ds(off_b, C), :] = o
        sb_ref[...] = s_new
        return carry

    lax.fori_loop(0, nchunk, body, 0)


def _hgrn_scan(proj, lb, tb=512):
    s = proj.shape[0]
    w = HG_HEADS * HG_D
    nb = s // tb
    hb = w // HG_D
    def fw(seg):
        return pl.BlockSpec((tb, HG_D), lambda h, n: (n, seg * hb + h))

    def bw(seg):
        return pl.BlockSpec((tb, HG_D), lambda h, n: (nb - 1 - n, seg * hb + h))

    return pl.pallas_call(
        _hgrn_kernel,
        grid=(HG_HEADS, nb),
        in_specs=[fw(0), fw(1), fw(2), bw(0), bw(1), bw(3),
                  pl.BlockSpec((2, HG_D), lambda h, n: (0, h))],
        out_specs=[pl.BlockSpec((tb, HG_D), lambda h, n: (n, h)),
                   pl.BlockSpec((tb, HG_D), lambda h, n: (nb - 1 - n, h))],
        out_shape=[jax.ShapeDtypeStruct((s, w), f32), jax.ShapeDtypeStruct((s, w), f32)],
        scratch_shapes=[pltpu.VMEM((HG_D, HG_D), f32), pltpu.VMEM((HG_D, HG_D), f32)]
        + [pltpu.VMEM((HG_CHUNK, HG_D), f32)] * 4,
        compiler_params=_cparams(("parallel", "arbitrary")),
        name="hgrn_scan",
    )(proj, proj, proj, proj, proj, proj, lb)


def _hgrn_out_kernel(of_ref, ob_ref, g_ref, ng_ref, y_ref):
    ng = ng_ref[...]
    for h in range(HG_HEADS):
        sl = slice(h * HG_D, (h + 1) * HG_D)
        o = of_ref[:, sl] + ob_ref[:, sl]
        ms = jnp.mean(o * o, axis=-1, keepdims=True)
        o = o * lax.rsqrt(ms + RMS_EPS) * ng
        y_ref[:, sl] = (o * jax.nn.sigmoid(g_ref[:, sl])).astype(y_ref.dtype)


def _hgrn_out(o_f, o_b, proj, norm_g, tm=512):
    s, w = o_f.shape
    gate_blk = 4
    return pl.pallas_call(
        _hgrn_out_kernel,
        grid=(s // tm,),
        in_specs=[pl.BlockSpec((tm, w), lambda i: (i, 0)),
                  pl.BlockSpec((tm, w), lambda i: (i, 0)),
                  pl.BlockSpec((tm, w), lambda i: (i, gate_blk)),
                  pl.BlockSpec((1, HG_D), lambda i: (0, 0))],
        out_specs=pl.BlockSpec((tm, w), lambda i: (i, 0)),
        out_shape=jax.ShapeDtypeStruct((s, w), bf16),
        compiler_params=_cparams(("parallel",)),
        name="hgrn_out",
    )(o_f, o_b, proj, norm_g.reshape(1, HG_D))


def _lb_kernel(logit_ref, lb_ref, *, layer):
    x = logit_ref[...]
    e = jnp.exp(x - jnp.max(x, axis=1, keepdims=True))
    p = e / jnp.sum(e, axis=1, keepdims=True)
    acc = p[:, 0, :]
    for l in range(1, layer + 1):
        acc = acc + p[:, l, :]
    lb_ref[...] = acc


def _lower_bounds(logits, layer):
    z, _, w = logits.shape
    return pl.pallas_call(
        functools.partial(_lb_kernel, layer=layer),
        out_shape=jax.ShapeDtypeStruct((z, w), f32),
        name="hgrn_lower_bounds",
    )(logits)


def _merge_kernel(y0_ref, y1_ref, y2_ref, w_ref, g0_ref, g1_ref, g2_ref, o_ref):
    acc = g0_ref[...].astype(f32) * _dot(y0_ref[...], w_ref[0])
    acc = acc + g1_ref[...].astype(f32) * _dot(y1_ref[...], w_ref[1])
    acc = acc + g2_ref[...].astype(f32) * _dot(y2_ref[...], w_ref[2])
    o_ref[...] = acc.astype(o_ref.dtype)


def _merge(y0, y1, y2, w_branch, gsig, tm=1024, tn=512):
    m, kb = y0.shape
    d = w_branch.shape[2]
    nj = d // tn
    yspec = pl.BlockSpec((tm, kb), lambda i, j: (i, 0))
    def gspec(b):
        return pl.BlockSpec((tm, tn), lambda i, j: (i, b * nj + j))

    return pl.pallas_call(
        _merge_kernel,
        grid=(m // tm, nj),
        in_specs=[yspec, yspec, yspec,
                  pl.BlockSpec((N_BRANCH, kb, tn), lambda i, j: (0, 0, j)),
                  gspec(0), gspec(1), gspec(2)],
        out_specs=pl.BlockSpec((tm, tn), lambda i, j: (i, j)),
        out_shape=jax.ShapeDtypeStruct((m, d), bf16),
        compiler_params=_cparams(("parallel", "parallel")),
        name="branch_merge",
    )(y0, y1, y2, w_branch, gsig, gsig, gsig)


def _proj_ln_kernel(a_ref, w_ref, res_ref, g_ref, b_ref, o32_ref, o16_ref):
    x = ALPHA * res_ref[...] + _dot(a_ref[...], w_ref[...])
    y = _ln_rows(x, g_ref[...], b_ref[...])
    o32_ref[...] = y
    o16_ref[...] = y.astype(bf16)


def _proj_residual_ln(a, w, res, g, b, tm=512):
    m, k = a.shape
    d = w.shape[1]
    return pl.pallas_call(
        _proj_ln_kernel,
        grid=(m // tm,),
        in_specs=[pl.BlockSpec((tm, k), lambda i: (i, 0)),
                  pl.BlockSpec((k, d), lambda i: (0, 0)),
                  pl.BlockSpec((tm, d), lambda i: (i, 0)),
                  pl.BlockSpec((1, d), lambda i: (0, 0)),
                  pl.BlockSpec((1, d), lambda i: (0, 0))],
        out_specs=[pl.BlockSpec((tm, d), lambda i: (i, 0)),
                   pl.BlockSpec((tm, d), lambda i: (i, 0))],
        out_shape=[jax.ShapeDtypeStruct((m, d), f32), jax.ShapeDtypeStruct((m, d), bf16)],
        compiler_params=_cparams(("parallel",)),
        name="out_proj_ln",
    )(a, w, res, g.reshape(1, d), b.reshape(1, d))


def _ffn_kernel(x_ref, wg_ref, wu_ref, wd_ref, res_ref, g_ref, b_ref, o_ref, acc_ref):
    j = pl.program_id(1)

    @pl.when(j == 0)
    def _():
        acc_ref[...] = jnp.zeros_like(acc_ref)

    x = x_ref[...]
    hid = jax.nn.silu(_dot(x, wg_ref[...])) * _dot(x, wu_ref[...])
    acc_ref[...] += _dot(hid.astype(bf16), wd_ref[...])

    @pl.when(j == pl.num_programs(1) - 1)
    def _():
        o_ref[...] = _ln_rows(ALPHA * res_ref[...] + acc_ref[...], g_ref[...], b_ref[...])


def _ffn_residual_ln(x16, x32, wg, wu, wd, g, b, tm=512, tf=512):
    m, d = x16.shape
    ff = wg.shape[1]
    return pl.pallas_call(
        _ffn_kernel,
        grid=(m // tm, ff // tf),
        in_specs=[pl.BlockSpec((tm, d), lambda i, j: (i, 0)),
                  pl.BlockSpec((d, tf), lambda i, j: (0, j)),
                  pl.BlockSpec((d, tf), lambda i, j: (0, j)),
                  pl.BlockSpec((tf, d), lambda i, j: (j, 0)),
                  pl.BlockSpec((tm, d), lambda i, j: (i, 0)),
                  pl.BlockSpec((1, d), lambda i, j: (0, 0)),
                  pl.BlockSpec((1, d), lambda i, j: (0, 0))],
        out_specs=pl.BlockSpec((tm, d), lambda i, j: (i, 0)),
        out_shape=jax.ShapeDtypeStruct((m, d), f32),
        scratch_shapes=[pltpu.VMEM((tm, d), f32)],
        compiler_params=_cparams(("parallel", "arbitrary")),
        name="ffn_ln",
    )(x16, wg, wu, wd, x32, g.reshape(1, d), b.reshape(1, d))


def _rope_tables(positions):
    half = MLA_ROPE // 2
    inv_freq = jnp.power(ROPE_THETA, -jnp.arange(half, dtype=f32) / half)
    ang = positions.astype(f32)[..., None] * inv_freq
    cos, sin = jnp.cos(ang), jnp.sin(ang)
    z = jnp.zeros_like(cos)
    pad = jnp.zeros((ang.shape[0], LANES - MLA_ROPE), f32)
    c = jnp.concatenate([cos, cos, pad], axis=-1)
    s1 = jnp.concatenate([-sin, z, pad], axis=-1)
    s2 = jnp.concatenate([z, sin, pad], axis=-1)
    return c, s1, s2


def _layer(h32, h16, mem16, rope, lb, w_in, hgrn_norm_g, g_cq, g_ckv, w_uq, w_ukv, w_memkv, w_branch, w_o,
           ln1_g, ln1_b, w_gate, w_up, w_down, ln2_g, ln2_b):
    s, d = h32.shape
    hgw = HG_HEADS * HG_D
    q_rank = g_cq.shape[0]
    kv_rank = g_ckv.shape[0]
    memw = MEM_HEADS * MEM_HD
    o_cq = 5 * hgw
    o_ckv = o_cq + q_rank
    o_kr = o_ckv + kv_rank
    o_qm = o_kr + MLA_ROPE
    o_gt = o_qm + memw
    rope_c, rope_s1, rope_s2 = rope

    w_hg = w_in[:, :o_cq].astype(bf16)
    w_cq = w_in[:, o_cq:o_ckv].astype(bf16)
    w_ckv = w_in[:, o_ckv:o_kr].astype(bf16)
    w_kr = jnp.pad(w_in[:, o_kr:o_qm], ((0, 0), (0, LANES - MLA_ROPE))).astype(bf16)
    w_qm = w_in[:, o_qm:o_gt].astype(bf16)
    w_gt = w_in[:, o_gt:].astype(bf16)

    proj_hg = _matmul(h16, w_hg, tm=1024, tn=512, out_dtype=f32, name="in_proj_hgrn")
    cqn = _matmul(h16, w_cq, tm=1024, tn=q_rank, out_dtype=bf16, epilogue=_rms_epilogue,
                  col_extras=(g_cq.reshape(1, -1),), name="in_proj_cq")
    ckvn = _matmul(h16, w_ckv, tm=1024, tn=kv_rank, out_dtype=bf16, epilogue=_rms_epilogue,
                   col_extras=(g_ckv.reshape(1, -1),), name="in_proj_ckv")
    krope = _matmul(h16, w_kr, tm=1024, tn=LANES, out_dtype=bf16, epilogue=_krope_epilogue,
                    row_extras=(rope_c, rope_s1, rope_s2), name="in_proj_krope")
    q_mem = _matmul(h16, w_qm, tm=1024, tn=512, out_dtype=bf16,
                    epilogue=lambda acc: acc * (MEM_HD ** -0.5), name="in_proj_qmem")
    gsig = _matmul(h16, w_gt, tm=1024, tn=512, out_dtype=bf16, epilogue=jax.nn.sigmoid, name="in_proj_gates")

    o_f, o_b = _hgrn_scan(proj_hg, lb)
    y_hg = _hgrn_out(o_f, o_b, proj_hg, hgrn_norm_g)

    zq = jnp.zeros((q_rank, MLA_HEADS, MLA_QK_PAD - MLA_QK), f32)
    w_q = jnp.concatenate([w_uq.reshape(q_rank, MLA_HEADS, MLA_QK), zq], axis=-1)
    w_q = w_q.reshape(q_rank, MLA_HEADS * MLA_QK_PAD).astype(bf16)
    w_kv3 = w_ukv.reshape(kv_rank, MLA_HEADS, MLA_NOPE + MLA_V)
    w_kv = jnp.concatenate([w_kv3[:, :, :MLA_NOPE].reshape(kv_rank, -1),
                            w_kv3[:, :, MLA_NOPE:].reshape(kv_rank, -1)], axis=-1).astype(bf16)
    q = _mla_q(cqn, w_q, rope_c, rope_s1, rope_s2)
    k, v = _mla_kv(ckvn, w_kv, krope)
    y_mla = _mla_attention(q, k, v)

    kv_mem = _matmul(mem16, w_memkv.astype(bf16), tm=mem16.shape[0], tn=512, out_dtype=bf16, name="mem_kv_proj")
    y_mem = _mem_attention(q_mem, kv_mem[:, :memw], kv_mem[:, memw:])

    merged = _merge(y_hg, y_mla, y_mem, w_branch.astype(bf16), gsig)
    x32, x16 = _proj_residual_ln(merged, w_o.astype(bf16), h32, ln1_g, ln1_b)
    out32 = _ffn_residual_ln(x16, x32, w_gate.astype(bf16), w_up.astype(bf16), w_down.astype(bf16), ln2_g, ln2_b)
    return out32


def kernel(x, mem, positions, ln_emb_g, ln_emb_b, hgrn_lb_logits, w_in, hgrn_norm_g, mla_g_cq, mla_g_ckv,
           mla_w_uq, mla_w_ukv, mem_w_kv, w_branch, w_o, ln1_g, ln1_b, w_ffn_gate, w_ffn_up, w_ffn_down,
           ln2_g, ln2_b):
    bsz, s, d = x.shape
    depth = w_in.shape[0]
    outs = []
    for bi in range(bsz):
        rope = _rope_tables(positions[bi])
        mem16 = mem[bi].astype(bf16)
        h32, h16 = _layernorm(x[bi], ln_emb_g, ln_emb_b)
        for l in range(depth):
            lb = _lower_bounds(hgrn_lb_logits, l)
            h32 = _layer(h32, h16, mem16, rope, lb, w_in[l], hgrn_norm_g[l], mla_g_cq[l], mla_g_ckv[l],
                         mla_w_uq[l], mla_w_ukv[l], mem_w_kv[l], w_branch[l], w_o[l], ln1_g[l], ln1_b[l],
                         w_ffn_gate[l], w_ffn_up[l], w_ffn_down[l], ln2_g[l], ln2_b[l])
            if l + 1 < depth:
                h16 = h32.astype(bf16)
        outs.append(h32)
    return jnp.stack(outs)
```

```python
import functools

import jax
import jax.numpy as jnp
from jax import lax
from jax.experimental import pallas as pl
from jax.experimental.pallas import tpu as pltpu

HG_HEADS = 8
HG_D = 128
MLA_HEADS = 8
MLA_NOPE = 128
MLA_ROPE = 64
MLA_V = 128
MLA_QK = MLA_NOPE + MLA_ROPE
MLA_QK_PAD = 256
MLA_V_PAD = 256
LOG2E = 1.4426950408889634
MEM_HEADS = 4
MEM_HD = 256
N_BRANCH = 3
ROPE_THETA = 10000.0
LN_EPS = 1e-5
RMS_EPS = 1e-6
DEPTH = 1
ALPHA = (2.0 * DEPTH) ** 0.25

LANES = 128
SUBLANES = 8
VMEM_LIMIT = 56 * 1024 * 1024

HG_CHUNK = 64
HG_SUB = SUBLANES

bf16 = jnp.bfloat16
f32 = jnp.float32


def _cparams(sem):
    return pltpu.CompilerParams(dimension_semantics=sem, vmem_limit_bytes=VMEM_LIMIT)


def _dot(a, b):
    return jnp.dot(a, b, preferred_element_type=f32)


def _dot_nt(a, b):
    return lax.dot_general(a, b, (((1,), (1,)), ((), ())), preferred_element_type=f32)


def _dot_tn(a, b):
    return lax.dot_general(a, b, (((0,), (0,)), ((), ())), preferred_element_type=f32)


def _ln_rows(x, g, b):
    mu = jnp.mean(x, axis=-1, keepdims=True)
    xc = x - mu
    var = jnp.mean(xc * xc, axis=-1, keepdims=True)
    return xc * lax.rsqrt(var + LN_EPS) * g + b


def _ln_kernel(x_ref, g_ref, b_ref, o32_ref, o16_ref):
    y = _ln_rows(x_ref[...], g_ref[...], b_ref[...])
    o32_ref[...] = y
    o16_ref[...] = y.astype(bf16)


def _layernorm(x, g, b, tm=512):
    m, d = x.shape
    return pl.pallas_call(
        _ln_kernel,
        grid=(m // tm,),
        in_specs=[pl.BlockSpec((tm, d), lambda i: (i, 0)),
                  pl.BlockSpec((1, d), lambda i: (0, 0)),
                  pl.BlockSpec((1, d), lambda i: (0, 0))],
        out_specs=[pl.BlockSpec((tm, d), lambda i: (i, 0)),
                   pl.BlockSpec((tm, d), lambda i: (i, 0))],
        out_shape=[jax.ShapeDtypeStruct((m, d), f32), jax.ShapeDtypeStruct((m, d), bf16)],
        compiler_params=_cparams(("parallel",)),
        name="ln_embed",
    )(x, g.reshape(1, d), b.reshape(1, d))


def _mm_kernel(a_ref, w_ref, *rest, epilogue):
    o_ref = rest[-1]
    acc = _dot(a_ref[...], w_ref[...])
    o_ref[...] = epilogue(acc, *[r[...] for r in rest[:-1]]).astype(o_ref.dtype)


def _matmul(a, w, *, tm, tn, out_dtype, epilogue=None, row_extras=(), col_extras=(), name):
    m, k = a.shape
    n = w.shape[1]
    if epilogue is None:
        epilogue = lambda acc: acc
    in_specs = [pl.BlockSpec((tm, k), lambda i, j: (i, 0)),
                pl.BlockSpec((k, tn), lambda i, j: (0, j))]
    for e in row_extras:
        in_specs.append(pl.BlockSpec((tm, e.shape[1]), lambda i, j: (i, 0)))
    for e in col_extras:
        in_specs.append(pl.BlockSpec((e.shape[0], tn), lambda i, j: (0, j)))
    return pl.pallas_call(
        functools.partial(_mm_kernel, epilogue=epilogue),
        grid=(m // tm, n // tn),
        in_specs=in_specs,
        out_specs=pl.BlockSpec((tm, tn), lambda i, j: (i, j)),
        out_shape=jax.ShapeDtypeStruct((m, n), out_dtype),
        compiler_params=_cparams(("parallel", "parallel")),
        name=name,
    )(a, w, *row_extras, *col_extras)


def _rms_epilogue(acc, g):
    ms = jnp.mean(acc * acc, axis=-1, keepdims=True)
    return acc * lax.rsqrt(ms + RMS_EPS) * g


def _rope_block(blk, c, s1, s2):
    half = MLA_ROPE // 2
    return blk * c + pltpu.roll(blk, LANES - half, 1) * s1 + pltpu.roll(blk, half, 1) * s2


def _krope_epilogue(acc, c, s1, s2):
    return _rope_block(acc, c, s1, s2)


def _mla_q_kernel(cq_ref, w_ref, c_ref, s1_ref, s2_ref, o_ref):
    q = _dot(cq_ref[...], w_ref[...]) * (MLA_QK ** -0.5 * LOG2E)
    c, s1, s2 = c_ref[...], s1_ref[...], s2_ref[...]
    for h in range(MLA_HEADS):
        base = h * MLA_QK_PAD
        o_ref[:, base:base + MLA_NOPE] = q[:, base:base + MLA_NOPE].astype(bf16)
        o_ref[:, base + MLA_NOPE:base + MLA_QK_PAD] = _rope_block(
            q[:, base + MLA_NOPE:base + MLA_QK_PAD], c, s1, s2).astype(bf16)


def _mla_q(cqn, w_q, rope_c, rope_s1, rope_s2, tm=512):
    m, r = cqn.shape
    n = w_q.shape[1]
    return pl.pallas_call(
        _mla_q_kernel,
        grid=(m // tm,),
        in_specs=[pl.BlockSpec((tm, r), lambda i: (i, 0)),
                  pl.BlockSpec((r, n), lambda i: (0, 0)),
                  pl.BlockSpec((tm, LANES), lambda i: (i, 0)),
                  pl.BlockSpec((tm, LANES), lambda i: (i, 0)),
                  pl.BlockSpec((tm, LANES), lambda i: (i, 0))],
        out_specs=pl.BlockSpec((tm, n), lambda i: (i, 0)),
        out_shape=jax.ShapeDtypeStruct((m, n), bf16),
        compiler_params=_cparams(("parallel",)),
        name="mla_q_up",
    )(cqn, w_q, rope_c, rope_s1, rope_s2)


def _mla_kv_kernel(ckv_ref, w_ref, kr_ref, k_ref, v_ref):
    kv = _dot(ckv_ref[...], w_ref[...])
    kr = kr_ref[...]
    nk = MLA_HEADS * MLA_NOPE
    for h in range(MLA_HEADS):
        base = h * MLA_QK_PAD
        k_ref[:, base:base + MLA_NOPE] = kv[:, h * MLA_NOPE:(h + 1) * MLA_NOPE].astype(bf16)
        k_ref[:, base + MLA_NOPE:base + MLA_QK_PAD] = kr
    lane = lax.broadcasted_iota(jnp.int32, (kv.shape[0], MLA_V_PAD - MLA_V), 1)
    ones_col = jnp.where(lane == 0, 1.0, 0.0).astype(bf16)
    for h in range(MLA_HEADS):
        base = h * MLA_V_PAD
        v_ref[:, base:base + MLA_V] = kv[:, nk + h * MLA_V:nk + (h + 1) * MLA_V].astype(bf16)
        v_ref[:, base + MLA_V:base + MLA_V_PAD] = ones_col


def _mla_kv(ckvn, w_kv, krope, tm=512):
    m, r = ckvn.shape
    n = w_kv.shape[1]
    nk = MLA_HEADS * MLA_QK_PAD
    nv = MLA_HEADS * MLA_V_PAD
    return pl.pallas_call(
        _mla_kv_kernel,
        grid=(m // tm,),
        in_specs=[pl.BlockSpec((tm, r), lambda i: (i, 0)),
                  pl.BlockSpec((r, n), lambda i: (0, 0)),
                  pl.BlockSpec((tm, LANES), lambda i: (i, 0))],
        out_specs=[pl.BlockSpec((tm, nk), lambda i: (i, 0)),
                   pl.BlockSpec((tm, nv), lambda i: (i, 0))],
        out_shape=[jax.ShapeDtypeStruct((m, nk), bf16), jax.ShapeDtypeStruct((m, nv), bf16)],
        compiler_params=_cparams(("parallel",)),
        name="mla_kv_up",
    )(ckvn, w_kv, krope)


def _mla_attn_kernel(q_ref, k_ref, v_ref, o_ref, s_ref, *, tk, unroll):
    q = q_ref[...]
    tq = q.shape[0]
    nkv = k_ref.shape[0] // tk
    assert nkv % 2 == 0

    def scores(c):
        off = pl.multiple_of(c * tk, tk)
        return _dot_nt(q, k_ref[pl.ds(off, tk), :])

    def update(s, c, m, acc):
        off = pl.multiple_of(c * tk, tk)
        m_new = jnp.maximum(m, jnp.max(s, axis=-1, keepdims=True))
        p = jnp.exp2(s - m_new)
        acc = jnp.exp2(m - m_new) * acc + _dot(p.astype(bf16), v_ref[pl.ds(off, tk), :])
        return m_new, acc

    s_ref[...] = scores(0)

    def body(i, carry):
        m, acc = carry
        c = 2 * i
        s_a = s_ref[...]
        s_b = scores(c + 1)
        m, acc = update(s_a, c, m, acc)
        s_ref[...] = scores(jnp.minimum(c + 2, nkv - 1))
        m, acc = update(s_b, c + 1, m, acc)
        return m, acc

    m0 = jnp.full((tq, 1), -jnp.inf, f32)
    a0 = jnp.zeros((tq, MLA_V_PAD), f32)
    _, acc = lax.fori_loop(0, nkv // 2, body, (m0, a0), unroll=unroll)
    o_ref[...] = (acc[:, :MLA_V] / acc[:, MLA_V:MLA_V + 1]).astype(o_ref.dtype)


def _mla_attention(q, k, v, tq=512, tk=512, unroll=2):
    s = q.shape[0]
    return pl.pallas_call(
        functools.partial(_mla_attn_kernel, tk=tk, unroll=unroll),
        scratch_shapes=[pltpu.VMEM((tq, tk), f32)],
        grid=(MLA_HEADS, s // tq),
        in_specs=[pl.BlockSpec((tq, MLA_QK_PAD), lambda h, i: (i, h)),
                  pl.BlockSpec((s, MLA_QK_PAD), lambda h, i: (0, h)),
                  pl.BlockSpec((s, MLA_V_PAD), lambda h, i: (0, h))],
        out_specs=pl.BlockSpec((tq, MLA_V), lambda h, i: (i, h)),
        out_shape=jax.ShapeDtypeStruct((s, MLA_HEADS * MLA_V), bf16),
        compiler_params=_cparams(("parallel", "parallel")),
        name="mla_attention",
    )(q, k, v)


def _mem_attn_kernel(q_ref, k_ref, v_ref, o_ref):
    for h in range(MEM_HEADS):
        sl = slice(h * MEM_HD, (h + 1) * MEM_HD)
        s = _dot_nt(q_ref[:, sl], k_ref[:, sl])
        m = jnp.max(s, axis=-1, keepdims=True)
        p = jnp.exp(s - m)
        l = jnp.sum(p, axis=-1, keepdims=True)
        o = _dot(p.astype(bf16), v_ref[:, sl])
        o_ref[:, sl] = (o / l).astype(o_ref.dtype)


def _mem_attention(q, k, v, tm=512):
    s, w = q.shape
    nm = k.shape[0]
    return pl.pallas_call(
        _mem_attn_kernel,
        grid=(s // tm,),
        in_specs=[pl.BlockSpec((tm, w), lambda i: (i, 0)),
                  pl.BlockSpec((nm, w), lambda i: (0, 0)),
                  pl.BlockSpec((nm, w), lambda i: (0, 0))],
        out_specs=pl.BlockSpec((tm, w), lambda i: (i, 0)),
        out_shape=jax.ShapeDtypeStruct((s, w), bf16),
        compiler_params=_cparams(("parallel",)),
        name="mem_attention",
    )(q, k, v)


def _split3(x):
    hi = x.astype(bf16)
    r1 = x - hi.astype(f32)
    mid = r1.astype(bf16)
    lo = (r1 - mid.astype(f32)).astype(bf16)
    return hi, mid, lo


def _hgrn_chunk(q, v16, fraw, lb, state, tri, kbuf, bbuf, *, rev):
    C = q.shape[0]
    nb = C // HG_SUB
    f = lb + (1.0 - lb) * jax.nn.sigmoid(fraw)
    k = 1.0 - f
    g = jnp.log(f)
    g_hi, g_mid, g_lo = _split3(g)
    b = _dot(tri, g_hi) + _dot(tri, g_mid) + _dot(tri, g_lo)
    kbuf[...] = k
    bbuf[...] = b
    edge = (C - 1) if not rev else 0
    b_edge = bbuf[edge:edge + 1, :]

    o = _dot_nt((q * jnp.exp(b)).astype(bf16), state.astype(bf16))
    kdec = (k * jnp.exp(b_edge - b)).astype(bf16)
    new_state = jnp.exp(b_edge) * state + _dot_tn(v16, kdec)

    lane = lax.broadcasted_iota(jnp.int32, (HG_SUB, C), 1)
    sub = lax.broadcasted_iota(jnp.int32, (HG_SUB, C), 0)
    rows = []
    for i in range(nb):
        r0 = i * HG_SUB
        q_i = q[r0:r0 + HG_SUB]
        b_i = b[r0:r0 + HG_SUB]
        a_blk = jnp.zeros((HG_SUB, C), f32)
        for s in range(HG_SUB):
            ks = kbuf[r0 + s:r0 + s + 1, :]
            bs = bbuf[r0 + s:r0 + s + 1, :]
            w = q_i * (ks * jnp.exp(b_i - bs))
            a_blk = jnp.where(lane == r0 + s, jnp.sum(w, axis=-1, keepdims=True), a_blk)
        keep = (sub >= lane - r0) if not rev else (sub <= lane - r0)
        a_blk = jnp.where(keep, a_blk, 0.0)
        if not rev and i > 0:
            piv = bbuf[r0 - 1:r0, :]
            qd = (q_i * jnp.exp(b_i - piv)).astype(bf16)
            kd = k[:r0] * jnp.exp(piv - b[:r0])
            kd = jnp.concatenate([kd, jnp.zeros((C - r0, kd.shape[1]), f32)], axis=0).astype(bf16)
            a_blk = a_blk + _dot_nt(qd, kd)
        if rev and i < nb - 1:
            r1 = r0 + HG_SUB
            piv = bbuf[r1:r1 + 1, :]
            qd = (q_i * jnp.exp(b_i - piv)).astype(bf16)
            kd = k[r1:] * jnp.exp(piv - b[r1:])
            kd = jnp.concatenate([jnp.zeros((r1, kd.shape[1]), f32), kd], axis=0).astype(bf16)
            a_blk = a_blk + _dot_nt(qd, kd)
        rows.append(a_blk)
    a = jnp.concatenate(rows, axis=0).astype(bf16)
    o = o + _dot(a, v16)
    return o, new_state


def _hgrn_kernel(qf_ref, vf_ref, ff_ref, qb_ref, vb_ref, fb_ref, lb_ref, of_ref, ob_ref,
                 sf_ref, sb_ref, kbuf_f, bbuf_f, kbuf_b, bbuf_b):
    n = pl.program_id(1)
    C = HG_CHUNK
    nchunk = qf_ref.shape[0] // C

    @pl.when(n == 0)
    def _():
        sf_ref[...] = jnp.zeros_like(sf_ref)
        sb_ref[...] = jnp.zeros_like(sb_ref)

    r = lax.broadcasted_iota(jnp.int32, (C, C), 0)
    c = lax.broadcasted_iota(jnp.int32, (C, C), 1)
    tri_f = (c <= r).astype(bf16)
    tri_b = (c >= r).astype(bf16)
    lb_f = lb_ref[0:1, :]
    lb_b = lb_ref[1:2, :]

    def body(j, carry):
        off_f = pl.multiple_of(j * C, C)
        off_b = pl.multiple_of((nchunk - 1 - j) * C, C)
        qf = jax.nn.silu(qf_ref[pl.ds(off_f, C), :])
        o, s_new = _hgrn_chunk(qf, vf_ref[pl.ds(off_f, C), :].astype(bf16), ff_ref[pl.ds(off_f, C), :],
                               lb_f, sf_ref[...], tri_f, kbuf_f, bbuf_f, rev=False)
        of_ref[pl.ds(off_f, C), :] = o
        sf_ref[...] = s_new
        qb = jax.nn.silu(qb_ref[pl.ds(off_b, C), :])
        o, s_new = _hgrn_chunk(qb, vb_ref[pl.ds(off_b, C), :].astype(bf16), fb_ref[pl.ds(off_b, C), :],
                               lb_b, sb_ref[...], tri_b, kbuf_b, bbuf_b, rev=True)
        ob_ref[pl.ds(off_b, C), :] = o
        sb_ref[...] = s_new
        return carry

    lax.fori_loop(0, nchunk, body, 0)


def _hgrn_scan(proj, lb, tb=512):
    s = proj.shape[0]
    w = HG_HEADS * HG_D
    nb = s // tb
    hb = w // HG_D
    def fw(seg):
        return pl.BlockSpec((tb, HG_D), lambda h, n: (n, seg * hb + h))

    def bw(seg):
        return pl.BlockSpec((tb, HG_D), lambda h, n: (nb - 1 - n, seg * hb + h))

    return pl.pallas_call(
        _hgrn_kernel,
        grid=(HG_HEADS, nb),
        in_specs=[fw(0), fw(1), fw(2), bw(0), bw(1), bw(3),
                  pl.BlockSpec((2, HG_D), lambda h, n: (0, h))],
        out_specs=[pl.BlockSpec((tb, HG_D), lambda h, n: (n, h)),
                   pl.BlockSpec((tb, HG_D), lambda h, n: (nb - 1 - n, h))],
        out_shape=[jax.ShapeDtypeStruct((s, w), f32), jax.ShapeDtypeStruct((s, w), f32)],
        scratch_shapes=[pltpu.VMEM((HG_D, HG_D), f32), pltpu.VMEM((HG_D, HG_D), f32)]
        + [pltpu.VMEM((HG_CHUNK, HG_D), f32)] * 4,
        compiler_params=_cparams(("parallel", "arbitrary")),
        name="hgrn_scan",
    )(proj, proj, proj, proj, proj, proj, lb)


def _hgrn_out_kernel(of_ref, ob_ref, g_ref, ng_ref, y_ref):
    ng = ng_ref[...]
    for h in range(HG_HEADS):
        sl = slice(h * HG_D, (h + 1) * HG_D)
        o = of_ref[:, sl] + ob_ref[:, sl]
        ms = jnp.mean(o * o, axis=-1, keepdims=True)
        o = o * lax.rsqrt(ms + RMS_EPS) * ng
        y_ref[:, sl] = (o * jax.nn.sigmoid(g_ref[:, sl])).astype(y_ref.dtype)


def _hgrn_out(o_f, o_b, proj, norm_g, tm=512):
    s, w = o_f.shape
    gate_blk = 4
    return pl.pallas_call(
        _hgrn_out_kernel,
        grid=(s // tm,),
        in_specs=[pl.BlockSpec((tm, w), lambda i: (i, 0)),
                  pl.BlockSpec((tm, w), lambda i: (i, 0)),
                  pl.BlockSpec((tm, w), lambda i: (i, gate_blk)),
                  pl.BlockSpec((1, HG_D), lambda i: (0, 0))],
        out_specs=pl.BlockSpec((tm, w), lambda i: (i, 0)),
        out_shape=jax.ShapeDtypeStruct((s, w), bf16),
        compiler_params=_cparams(("parallel",)),
        name="hgrn_out",
    )(o_f, o_b, proj, norm_g.reshape(1, HG_D))


def _lb_kernel(logit_ref, lb_ref, *, layer):
    x = logit_ref[...]
    e = jnp.exp(x - jnp.max(x, axis=1, keepdims=True))
    p = e / jnp.sum(e, axis=1, keepdims=True)
    acc = p[:, 0, :]
    for l in range(1, layer + 1):
        acc = acc + p[:, l, :]
    lb_ref[...] = acc


def _lower_bounds(logits, layer):
    z, _, w = logits.shape
    return pl.pallas_call(
        functools.partial(_lb_kernel, layer=layer),
        out_shape=jax.ShapeDtypeStruct((z, w), f32),
        name="hgrn_lower_bounds",
    )(logits)


def _merge_kernel(y0_ref, y1_ref, y2_ref, w_ref, g0_ref, g1_ref, g2_ref, o_ref):
    acc = g0_ref[...].astype(f32) * _dot(y0_ref[...], w_ref[0])
    acc = acc + g1_ref[...].astype(f32) * _dot(y1_ref[...], w_ref[1])
    acc = acc + g2_ref[...].astype(f32) * _dot(y2_ref[...], w_ref[2])
    o_ref[...] = acc.astype(o_ref.dtype)


def _merge(y0, y1, y2, w_branch, gsig, tm=1024, tn=512):
    m, kb = y0.shape
    d = w_branch.shape[2]
    nj = d // tn
    yspec = pl.BlockSpec((tm, kb), lambda i, j: (i, 0))
    def gspec(b):
        return pl.BlockSpec((tm, tn), lambda i, j: (i, b * nj + j))

    return pl.pallas_call(
        _merge_kernel,
        grid=(m // tm, nj),
        in_specs=[yspec, yspec, yspec,
                  pl.BlockSpec((N_BRANCH, kb, tn), lambda i, j: (0, 0, j)),
                  gspec(0), gspec(1), gspec(2)],
        out_specs=pl.BlockSpec((tm, tn), lambda i, j: (i, j)),
        out_shape=jax.ShapeDtypeStruct((m, d), bf16),
        compiler_params=_cparams(("parallel", "parallel")),
        name="branch_merge",
    )(y0, y1, y2, w_branch, gsig, gsig, gsig)


def _proj_ln_kernel(a_ref, w_ref, res_ref, g_ref, b_ref, o32_ref, o16_ref):
    x = ALPHA * res_ref[...] + _dot(a_ref[...], w_ref[...])
    y = _ln_rows(x, g_ref[...], b_ref[...])
    o32_ref[...] = y
    o16_ref[...] = y.astype(bf16)


def _proj_residual_ln(a, w, res, g, b, tm=512):
    m, k = a.shape
    d = w.shape[1]
    return pl.pallas_call(
        _proj_ln_kernel,
        grid=(m // tm,),
        in_specs=[pl.BlockSpec((tm, k), lambda i: (i, 0)),
                  pl.BlockSpec((k, d), lambda i: (0, 0)),
                  pl.BlockSpec((tm, d), lambda i: (i, 0)),
                  pl.BlockSpec((1, d), lambda i: (0, 0)),
                  pl.BlockSpec((1, d), lambda i: (0, 0))],
        out_specs=[pl.BlockSpec((tm, d), lambda i: (i, 0)),
                   pl.BlockSpec((tm, d), lambda i: (i, 0))],
        out_shape=[jax.ShapeDtypeStruct((m, d), f32), jax.ShapeDtypeStruct((m, d), bf16)],
        compiler_params=_cparams(("parallel",)),
        name="out_proj_ln",
    )(a, w, res, g.reshape(1, d), b.reshape(1, d))


def _ffn_kernel(x_ref, wg_ref, wu_ref, wd_ref, res_ref, g_ref, b_ref, o_ref, acc_ref):
    j = pl.program_id(1)

    @pl.when(j == 0)
    def _():
        acc_ref[...] = jnp.zeros_like(acc_ref)

    x = x_ref[...]
    hid = jax.nn.silu(_dot(x, wg_ref[...])) * _dot(x, wu_ref[...])
    acc_ref[...] += _dot(hid.astype(bf16), wd_ref[...])

    @pl.when(j == pl.num_programs(1) - 1)
    def _():
        o_ref[...] = _ln_rows(ALPHA * res_ref[...] + acc_ref[...], g_ref[...], b_ref[...])


def _ffn_residual_ln(x16, x32, wg, wu, wd, g, b, tm=512, tf=512):
    m, d = x16.shape
    ff = wg.shape[1]
    return pl.pallas_call(
        _ffn_kernel,
        grid=(m // tm, ff // tf),
        in_specs=[pl.BlockSpec((tm, d), lambda i, j: (i, 0)),
                  pl.BlockSpec((d, tf), lambda i, j: (0, j)),
                  pl.BlockSpec((d, tf), lambda i, j: (0, j)),
                  pl.BlockSpec((tf, d), lambda i, j: (j, 0)),
                  pl.BlockSpec((tm, d), lambda i, j: (i, 0)),
                  pl.BlockSpec((1, d), lambda i, j: (0, 0)),
                  pl.BlockSpec((1, d), lambda i, j: (0, 0))],
        out_specs=pl.BlockSpec((tm, d), lambda i, j: (i, 0)),
        out_shape=jax.ShapeDtypeStruct((m, d), f32),
        scratch_shapes=[pltpu.VMEM((tm, d), f32)],
        compiler_params=_cparams(("parallel", "arbitrary")),
        name="ffn_ln",
    )(x16, wg, wu, wd, x32, g.reshape(1, d), b.reshape(1, d))


def _rope_tables(positions):
    half = MLA_ROPE // 2
    inv_freq = jnp.power(ROPE_THETA, -jnp.arange(half, dtype=f32) / half)
    ang = positions.astype(f32)[..., None] * inv_freq
    cos, sin = jnp.cos(ang), jnp.sin(ang)
    z = jnp.zeros_like(cos)
    pad = jnp.zeros((ang.shape[0], LANES - MLA_ROPE), f32)
    c = jnp.concatenate([cos, cos, pad], axis=-1)
    s1 = jnp.concatenate([-sin, z, pad], axis=-1)
    s2 = jnp.concatenate([z, sin, pad], axis=-1)
    return c, s1, s2


def _layer(h32, h16, mem16, rope, lb, w_in, hgrn_norm_g, g_cq, g_ckv, w_uq, w_ukv, w_memkv, w_branch, w_o,
           ln1_g, ln1_b, w_gate, w_up, w_down, ln2_g, ln2_b):
    s, d = h32.shape
    hgw = HG_HEADS * HG_D
    q_rank = g_cq.shape[0]
    kv_rank = g_ckv.shape[0]
    memw = MEM_HEADS * MEM_HD
    o_cq = 5 * hgw
    o_ckv = o_cq + q_rank
    o_kr = o_ckv + kv_rank
    o_qm = o_kr + MLA_ROPE
    o_gt = o_qm + memw
    rope_c, rope_s1, rope_s2 = rope

    w_hg = w_in[:, :o_cq].astype(bf16)
    w_cq = w_in[:, o_cq:o_ckv].astype(bf16)
    w_ckv = w_in[:, o_ckv:o_kr].astype(bf16)
    w_kr = jnp.pad(w_in[:, o_kr:o_qm], ((0, 0), (0, LANES - MLA_ROPE))).astype(bf16)
    w_qm = w_in[:, o_qm:o_gt].astype(bf16)
    w_gt = w_in[:, o_gt:].astype(bf16)

    proj_hg = _matmul(h16, w_hg, tm=1024, tn=512, out_dtype=f32, name="in_proj_hgrn")
    cqn = _matmul(h16, w_cq, tm=1024, tn=q_rank, out_dtype=bf16, epilogue=_rms_epilogue,
                  col_extras=(g_cq.reshape(1, -1),), name="in_proj_cq")
    ckvn = _matmul(h16, w_ckv, tm=1024, tn=kv_rank, out_dtype=bf16, epilogue=_rms_epilogue,
                   col_extras=(g_ckv.reshape(1, -1),), name="in_proj_ckv")
    krope = _matmul(h16, w_kr, tm=1024, tn=LANES, out_dtype=bf16, epilogue=_krope_epilogue,
                    row_extras=(rope_c, rope_s1, rope_s2), name="in_proj_krope")
    q_mem = _matmul(h16, w_qm, tm=1024, tn=512, out_dtype=bf16,
                    epilogue=lambda acc: acc * (MEM_HD ** -0.5), name="in_proj_qmem")
    gsig = _matmul(h16, w_gt, tm=1024, tn=512, out_dtype=bf16, epilogue=jax.nn.sigmoid, name="in_proj_gates")

    o_f, o_b = _hgrn_scan(proj_hg, lb)
    y_hg = _hgrn_out(o_f, o_b, proj_hg, hgrn_norm_g)

    zq = jnp.zeros((q_rank, MLA_HEADS, MLA_QK_PAD - MLA_QK), f32)
    w_q = jnp.concatenate([w_uq.reshape(q_rank, MLA_HEADS, MLA_QK), zq], axis=-1)
    w_q = w_q.reshape(q_rank, MLA_HEADS * MLA_QK_PAD).astype(bf16)
    w_kv3 = w_ukv.reshape(kv_rank, MLA_HEADS, MLA_NOPE + MLA_V)
    w_kv = jnp.concatenate([w_kv3[:, :, :MLA_NOPE].reshape(kv_rank, -1),
                            w_kv3[:, :, MLA_NOPE:].reshape(kv_rank, -1)], axis=-1).astype(bf16)
    q = _mla_q(cqn, w_q, rope_c, rope_s1, rope_s2)
    k, v = _mla_kv(ckvn, w_kv, krope)
    y_mla = _mla_attention(q, k, v)

    kv_mem = _matmul(mem16, w_memkv.astype(bf16), tm=mem16.shape[0], tn=512, out_dtype=bf16, name="mem_kv_proj")
    y_mem = _mem_attention(q_mem, kv_mem[:, :memw], kv_mem[:, memw:])

    merged = _merge(y_hg, y_mla, y_mem, w_branch.astype(bf16), gsig)
    x32, x16 = _proj_residual_ln(merged, w_o.astype(bf16), h32, ln1_g, ln1_b)
    out32 = _ffn_residual_ln(x16, x32, w_gate.astype(bf16), w_up.astype(bf16), w_down.astype(bf16), ln2_g, ln2_b)
    return out32


def kernel(x, mem, positions, ln_emb_g, ln_emb_b, hgrn_lb_logits, w_in, hgrn_norm_g, mla_g_cq, mla_g_ckv,
           mla_w_uq, mla_w_ukv, mem_w_kv, w_branch, w_o, ln1_g, ln1_b, w_ffn_gate, w_ffn_up, w_ffn_down,
           ln2_g, ln2_b):
    bsz, s, d = x.shape
    depth = w_in.shape[0]
    outs = []
    for bi in range(bsz):
        rope = _rope_tables(positions[bi])
        mem16 = mem[bi].astype(bf16)
        h32, h16 = _layernorm(x[bi], ln_emb_g, ln_emb_b)
        for l in range(depth):
            lb = _lower_bounds(hgrn_lb_logits, l)
            h32 = _layer(h32, h16, mem16, rope, lb, w_in[l], hgrn_norm_g[l], mla_g_cq[l], mla_g_ckv[l],
                         mla_w_uq[l], mla_w_ukv[l], mem_w_kv[l], w_branch[l], w_o[l], ln1_g[l], ln1_b[l],
                         w_ffn_gate[l], w_ffn_up[l], w_ffn_down[l], ln2_g[l], ln2_b[l])
            if l + 1 < depth:
                h16 = h32.astype(bf16)
        outs.append(h32)
    return jnp.stack(outs)
```

```python
import functools

import jax
import jax.numpy as jnp
from jax import lax
from jax.experimental import pallas as pl
from jax.experimental.pallas import tpu as pltpu

HG_HEADS = 8
HG_D = 128
MLA_HEADS = 8
MLA_NOPE = 128
MLA_ROPE = 64
MLA_V = 128
MLA_QK = MLA_NOPE + MLA_ROPE
MLA_QK_PAD = 256
MLA_V_PAD = 256
LOG2E = 1.4426950408889634
MEM_HEADS = 4
MEM_HD = 256
N_BRANCH = 3
ROPE_THETA = 10000.0
LN_EPS = 1e-5
RMS_EPS = 1e-6
DEPTH = 1
ALPHA = (2.0 * DEPTH) ** 0.25

LANES = 128
SUBLANES = 8
VMEM_LIMIT = 56 * 1024 * 1024

HG_CHUNK = 64
HG_SUB = SUBLANES
HG_UNROLL = 2

bf16 = jnp.bfloat16
f32 = jnp.float32


def _cparams(sem):
    return pltpu.CompilerParams(dimension_semantics=sem, vmem_limit_bytes=VMEM_LIMIT)


def _dot(a, b):
    return jnp.dot(a, b, preferred_element_type=f32)


def _dot_nt(a, b):
    return lax.dot_general(a, b, (((1,), (1,)), ((), ())), preferred_element_type=f32)


def _dot_tn(a, b):
    return lax.dot_general(a, b, (((0,), (0,)), ((), ())), preferred_element_type=f32)


def _ln_rows(x, g, b):
    mu = jnp.mean(x, axis=-1, keepdims=True)
    xc = x - mu
    var = jnp.mean(xc * xc, axis=-1, keepdims=True)
    return xc * lax.rsqrt(var + LN_EPS) * g + b


def _ln_kernel(x_ref, g_ref, b_ref, o32_ref, o16_ref):
    y = _ln_rows(x_ref[...], g_ref[...], b_ref[...])
    o32_ref[...] = y
    o16_ref[...] = y.astype(bf16)


def _layernorm(x, g, b, tm=512):
    m, d = x.shape
    return pl.pallas_call(
        _ln_kernel,
        grid=(m // tm,),
        in_specs=[pl.BlockSpec((tm, d), lambda i: (i, 0)),
                  pl.BlockSpec((1, d), lambda i: (0, 0)),
                  pl.BlockSpec((1, d), lambda i: (0, 0))],
        out_specs=[pl.BlockSpec((tm, d), lambda i: (i, 0)),
                   pl.BlockSpec((tm, d), lambda i: (i, 0))],
        out_shape=[jax.ShapeDtypeStruct((m, d), f32), jax.ShapeDtypeStruct((m, d), bf16)],
        compiler_params=_cparams(("parallel",)),
        name="ln_embed",
    )(x, g.reshape(1, d), b.reshape(1, d))


def _mm_kernel(a_ref, w_ref, *rest, epilogue):
    o_ref = rest[-1]
    acc = _dot(a_ref[...], w_ref[...])
    o_ref[...] = epilogue(acc, *[r[...] for r in rest[:-1]]).astype(o_ref.dtype)


def _matmul(a, w, *, tm, tn, out_dtype, epilogue=None, row_extras=(), col_extras=(), name):
    m, k = a.shape
    n = w.shape[1]
    if epilogue is None:
        epilogue = lambda acc: acc
    in_specs = [pl.BlockSpec((tm, k), lambda i, j: (i, 0)),
                pl.BlockSpec((k, tn), lambda i, j: (0, j))]
    for e in row_extras:
        in_specs.append(pl.BlockSpec((tm, e.shape[1]), lambda i, j: (i, 0)))
    for e in col_extras:
        in_specs.append(pl.BlockSpec((e.shape[0], tn), lambda i, j: (0, j)))
    return pl.pallas_call(
        functools.partial(_mm_kernel, epilogue=epilogue),
        grid=(m // tm, n // tn),
        in_specs=in_specs,
        out_specs=pl.BlockSpec((tm, tn), lambda i, j: (i, j)),
        out_shape=jax.ShapeDtypeStruct((m, n), out_dtype),
        compiler_params=_cparams(("parallel", "parallel")),
        name=name,
    )(a, w, *row_extras, *col_extras)


def _rms_epilogue(acc, g):
    ms = jnp.mean(acc * acc, axis=-1, keepdims=True)
    return acc * lax.rsqrt(ms + RMS_EPS) * g


def _rope_block(blk, c, s1, s2):
    half = MLA_ROPE // 2
    return blk * c + pltpu.roll(blk, LANES - half, 1) * s1 + pltpu.roll(blk, half, 1) * s2


def _krope_epilogue(acc, c, s1, s2):
    return _rope_block(acc, c, s1, s2)


def _mla_q_kernel(cq_ref, w_ref, c_ref, s1_ref, s2_ref, o_ref):
    q = _dot(cq_ref[...], w_ref[...]) * (MLA_QK ** -0.5 * LOG2E)
    c, s1, s2 = c_ref[...], s1_ref[...], s2_ref[...]
    for h in range(MLA_HEADS):
        base = h * MLA_QK_PAD
        o_ref[:, base:base + MLA_NOPE] = q[:, base:base + MLA_NOPE].astype(bf16)
        o_ref[:, base + MLA_NOPE:base + MLA_QK_PAD] = _rope_block(
            q[:, base + MLA_NOPE:base + MLA_QK_PAD], c, s1, s2).astype(bf16)


def _mla_q(cqn, w_q, rope_c, rope_s1, rope_s2, tm=512):
    m, r = cqn.shape
    n = w_q.shape[1]
    return pl.pallas_call(
        _mla_q_kernel,
        grid=(m // tm,),
        in_specs=[pl.BlockSpec((tm, r), lambda i: (i, 0)),
                  pl.BlockSpec((r, n), lambda i: (0, 0)),
                  pl.BlockSpec((tm, LANES), lambda i: (i, 0)),
                  pl.BlockSpec((tm, LANES), lambda i: (i, 0)),
                  pl.BlockSpec((tm, LANES), lambda i: (i, 0))],
        out_specs=pl.BlockSpec((tm, n), lambda i: (i, 0)),
        out_shape=jax.ShapeDtypeStruct((m, n), bf16),
        compiler_params=_cparams(("parallel",)),
        name="mla_q_up",
    )(cqn, w_q, rope_c, rope_s1, rope_s2)


def _mla_kv_kernel(ckv_ref, w_ref, kr_ref, k_ref, v_ref):
    kv = _dot(ckv_ref[...], w_ref[...])
    kr = kr_ref[...]
    nk = MLA_HEADS * MLA_NOPE
    for h in range(MLA_HEADS):
        base = h * MLA_QK_PAD
        k_ref[:, base:base + MLA_NOPE] = kv[:, h * MLA_NOPE:(h + 1) * MLA_NOPE].astype(bf16)
        k_ref[:, base + MLA_NOPE:base + MLA_QK_PAD] = kr
    lane = lax.broadcasted_iota(jnp.int32, (kv.shape[0], MLA_V_PAD - MLA_V), 1)
    ones_col = jnp.where(lane == 0, 1.0, 0.0).astype(bf16)
    for h in range(MLA_HEADS):
        base = h * MLA_V_PAD
        v_ref[:, base:base + MLA_V] = kv[:, nk + h * MLA_V:nk + (h + 1) * MLA_V].astype(bf16)
        v_ref[:, base + MLA_V:base + MLA_V_PAD] = ones_col


def _mla_kv(ckvn, w_kv, krope, tm=512):
    m, r = ckvn.shape
    n = w_kv.shape[1]
    nk = MLA_HEADS * MLA_QK_PAD
    nv = MLA_HEADS * MLA_V_PAD
    return pl.pallas_call(
        _mla_kv_kernel,
        grid=(m // tm,),
        in_specs=[pl.BlockSpec((tm, r), lambda i: (i, 0)),
                  pl.BlockSpec((r, n), lambda i: (0, 0)),
                  pl.BlockSpec((tm, LANES), lambda i: (i, 0))],
        out_specs=[pl.BlockSpec((tm, nk), lambda i: (i, 0)),
                   pl.BlockSpec((tm, nv), lambda i: (i, 0))],
        out_shape=[jax.ShapeDtypeStruct((m, nk), bf16), jax.ShapeDtypeStruct((m, nv), bf16)],
        compiler_params=_cparams(("parallel",)),
        name="mla_kv_up",
    )(ckvn, w_kv, krope)


def _mla_attn_kernel(q_ref, k_ref, v_ref, o_ref, *, tk):
    q = q_ref[...]
    tq = q.shape[0]
    nkv = k_ref.shape[0] // tk

    def scores(c):
        return _dot_nt(q, k_ref[c * tk:(c + 1) * tk, :])

    def update(s, c, m, acc):
        m_new = jnp.maximum(m, jnp.max(s, axis=-1, keepdims=True))
        p = jnp.exp2(s - m_new)
        acc = jnp.exp2(m - m_new) * acc + _dot(p.astype(bf16), v_ref[c * tk:(c + 1) * tk, :])
        return m_new, acc

    m = jnp.full((tq, 1), -jnp.inf, f32)
    acc = jnp.zeros((tq, MLA_V_PAD), f32)
    s_next = scores(0)
    for c in range(nkv):
        s_cur = s_next
        if c + 1 < nkv:
            s_next = scores(c + 1)
        m, acc = update(s_cur, c, m, acc)
    o_ref[...] = (acc[:, :MLA_V] / acc[:, MLA_V:MLA_V + 1]).astype(o_ref.dtype)


def _mla_attention(q, k, v, tq=512, tk=512):
    s = q.shape[0]
    return pl.pallas_call(
        functools.partial(_mla_attn_kernel, tk=tk),
        grid=(MLA_HEADS, s // tq),
        in_specs=[pl.BlockSpec((tq, MLA_QK_PAD), lambda h, i: (i, h)),
                  pl.BlockSpec((s, MLA_QK_PAD), lambda h, i: (0, h)),
                  pl.BlockSpec((s, MLA_V_PAD), lambda h, i: (0, h))],
        out_specs=pl.BlockSpec((tq, MLA_V), lambda h, i: (i, h)),
        out_shape=jax.ShapeDtypeStruct((s, MLA_HEADS * MLA_V), bf16),
        compiler_params=_cparams(("parallel", "parallel")),
        name="mla_attention",
    )(q, k, v)


def _mem_attn_kernel(q_ref, k_ref, v_ref, o_ref):
    for h in range(MEM_HEADS):
        sl = slice(h * MEM_HD, (h + 1) * MEM_HD)
        s = _dot_nt(q_ref[:, sl], k_ref[:, sl])
        m = jnp.max(s, axis=-1, keepdims=True)
        p = jnp.exp(s - m)
        l = jnp.sum(p, axis=-1, keepdims=True)
        o = _dot(p.astype(bf16), v_ref[:, sl])
        o_ref[:, sl] = (o / l).astype(o_ref.dtype)


def _mem_attention(q, k, v, tm=512):
    s, w = q.shape
    nm = k.shape[0]
    return pl.pallas_call(
        _mem_attn_kernel,
        grid=(s // tm,),
        in_specs=[pl.BlockSpec((tm, w), lambda i: (i, 0)),
                  pl.BlockSpec((nm, w), lambda i: (0, 0)),
                  pl.BlockSpec((nm, w), lambda i: (0, 0))],
        out_specs=pl.BlockSpec((tm, w), lambda i: (i, 0)),
        out_shape=jax.ShapeDtypeStruct((s, w), bf16),
        compiler_params=_cparams(("parallel",)),
        name="mem_attention",
    )(q, k, v)


def _split3(x):
    hi = x.astype(bf16)
    r1 = x - hi.astype(f32)
    mid = r1.astype(bf16)
    lo = (r1 - mid.astype(f32)).astype(bf16)
    return hi, mid, lo


def _hgrn_chunk(q, v16, fraw, lb, state, tri, bbuf, cbuf, *, rev):
    C = q.shape[0]
    nb = C // HG_SUB
    f = lb + (1.0 - lb) * jax.nn.sigmoid(fraw)
    g_hi, g_mid, g_lo = _split3(jnp.log(f) * LOG2E)
    b = _dot(tri, g_hi) + _dot(tri, g_mid) + _dot(tri, g_lo)
    c = jnp.log(1.0 - f) * LOG2E - b
    bbuf[...] = b
    cbuf[...] = c
    edge = (C - 1) if not rev else 0
    b_edge = bbuf[edge:edge + 1, :]

    o = _dot_nt((q * jnp.exp2(b)).astype(bf16), state.astype(bf16))
    kdec = jnp.exp2(c + b_edge).astype(bf16)
    new_state = jnp.exp2(b_edge) * state + _dot_tn(v16, kdec)

    lane = lax.broadcasted_iota(jnp.int32, (HG_SUB, C), 1)
    sub = lax.broadcasted_iota(jnp.int32, (HG_SUB, C), 0)
    rows = []
    for i in range(nb):
        r0 = i * HG_SUB
        q_i = q[r0:r0 + HG_SUB]
        b_i = b[r0:r0 + HG_SUB]
        a_blk = jnp.zeros((HG_SUB, C), f32)
        for s in range(HG_SUB):
            w = q_i * jnp.exp2(b_i + cbuf[r0 + s:r0 + s + 1, :])
            a_blk = jnp.where(lane == r0 + s, jnp.sum(w, axis=-1, keepdims=True), a_blk)
        keep = (sub >= lane - r0) if not rev else (sub <= lane - r0)
        a_blk = jnp.where(keep, a_blk, 0.0)
        if not rev and i > 0:
            piv = bbuf[r0 - 1:r0, :]
            qd = (q_i * jnp.exp2(b_i - piv)).astype(bf16)
            kd = jnp.exp2(c[:r0] + piv)
            kd = jnp.concatenate([kd, jnp.zeros((C - r0, kd.shape[1]), f32)], axis=0).astype(bf16)
            a_blk = a_blk + _dot_nt(qd, kd)
        if rev and i < nb - 1:
            r1 = r0 + HG_SUB
            piv = bbuf[r1:r1 + 1, :]
            qd = (q_i * jnp.exp2(b_i - piv)).astype(bf16)
            kd = jnp.exp2(c[r1:] + piv)
            kd = jnp.concatenate([jnp.zeros((r1, kd.shape[1]), f32), kd], axis=0).astype(bf16)
            a_blk = a_blk + _dot_nt(qd, kd)
        rows.append(a_blk)
    a = jnp.concatenate(rows, axis=0).astype(bf16)
    o = o + _dot(a, v16)
    return o, new_state


def _hgrn_kernel(qf_ref, vf_ref, ff_ref, qb_ref, vb_ref, fb_ref, lb_ref, of_ref, ob_ref,
                 sf_ref, sb_ref, bbuf, cbuf):
    n = pl.program_id(1)
    C = HG_CHUNK
    nchunk = qf_ref.shape[0] // C
    assert nchunk % HG_UNROLL == 0

    @pl.when(n == 0)
    def _():
        sf_ref[...] = jnp.zeros_like(sf_ref)
        sb_ref[...] = jnp.zeros_like(sb_ref)

    r = lax.broadcasted_iota(jnp.int32, (C, C), 0)
    c = lax.broadcasted_iota(jnp.int32, (C, C), 1)
    tri_f = (c <= r).astype(bf16)
    tri_b = (c >= r).astype(bf16)
    lb_f = lb_ref[0:1, :]
    lb_b = lb_ref[1:2, :]

    def body(j, carry):
        s_f, s_b = carry
        for u in range(HG_UNROLL):
            jj = j * HG_UNROLL + u
            off_f = pl.multiple_of(jj * C, C)
            off_b = pl.multiple_of((nchunk - 1 - jj) * C, C)
            qf = jax.nn.silu(qf_ref[pl.ds(off_f, C), :])
            o, s_f = _hgrn_chunk(qf, vf_ref[pl.ds(off_f, C), :].astype(bf16), ff_ref[pl.ds(off_f, C), :],
                                 lb_f, s_f, tri_f, bbuf.at[2 * u], cbuf.at[2 * u], rev=False)
            of_ref[pl.ds(off_f, C), :] = o
            qb = jax.nn.silu(qb_ref[pl.ds(off_b, C), :])
            o, s_b = _hgrn_chunk(qb, vb_ref[pl.ds(off_b, C), :].astype(bf16), fb_ref[pl.ds(off_b, C), :],
                                 lb_b, s_b, tri_b, bbuf.at[2 * u + 1], cbuf.at[2 * u + 1], rev=True)
            ob_ref[pl.ds(off_b, C), :] = o
        return s_f, s_b

    s_f, s_b = lax.fori_loop(0, nchunk // HG_UNROLL, body, (sf_ref[...], sb_ref[...]))
    sf_ref[...] = s_f
    sb_ref[...] = s_b


def _hgrn_scan(proj, lb, tb=512):
    s = proj.shape[0]
    w = HG_HEADS * HG_D
    nb = s // tb
    hb = w // HG_D
    def fw(seg):
        return pl.BlockSpec((tb, HG_D), lambda h, n: (n, seg * hb + h))

    def bw(seg):
        return pl.BlockSpec((tb, HG_D), lambda h, n: (nb - 1 - n, seg * hb + h))

    return pl.pallas_call(
        _hgrn_kernel,
        grid=(HG_HEADS, nb),
        in_specs=[fw(0), fw(1), fw(2), bw(0), bw(1), bw(3),
                  pl.BlockSpec((2, HG_D), lambda h, n: (0, h))],
        out_specs=[pl.BlockSpec((tb, HG_D), lambda h, n: (n, h)),
                   pl.BlockSpec((tb, HG_D), lambda h, n: (nb - 1 - n, h))],
        out_shape=[jax.ShapeDtypeStruct((s, w), f32), jax.ShapeDtypeStruct((s, w), f32)],
        scratch_shapes=[pltpu.VMEM((HG_D, HG_D), f32), pltpu.VMEM((HG_D, HG_D), f32),
                        pltpu.VMEM((2 * HG_UNROLL, HG_CHUNK, HG_D), f32),
                        pltpu.VMEM((2 * HG_UNROLL, HG_CHUNK, HG_D), f32)],
        compiler_params=_cparams(("parallel", "arbitrary")),
        name="hgrn_scan",
    )(proj, proj, proj, proj, proj, proj, lb)


def _hgrn_out_kernel(of_ref, ob_ref, g_ref, ng_ref, y_ref):
    ng = ng_ref[...]
    for h in range(HG_HEADS):
        sl = slice(h * HG_D, (h + 1) * HG_D)
        o = of_ref[:, sl] + ob_ref[:, sl]
        ms = jnp.mean(o * o, axis=-1, keepdims=True)
        o = o * lax.rsqrt(ms + RMS_EPS) * ng
        y_ref[:, sl] = (o * jax.nn.sigmoid(g_ref[:, sl])).astype(y_ref.dtype)


def _hgrn_out(o_f, o_b, proj, norm_g, tm=512):
    s, w = o_f.shape
    gate_blk = 4
    return pl.pallas_call(
        _hgrn_out_kernel,
        grid=(s // tm,),
        in_specs=[pl.BlockSpec((tm, w), lambda i: (i, 0)),
                  pl.BlockSpec((tm, w), lambda i: (i, 0)),
                  pl.BlockSpec((tm, w), lambda i: (i, gate_blk)),
                  pl.BlockSpec((1, HG_D), lambda i: (0, 0))],
        out_specs=pl.BlockSpec((tm, w), lambda i: (i, 0)),
        out_shape=jax.ShapeDtypeStruct((s, w), bf16),
        compiler_params=_cparams(("parallel",)),
        name="hgrn_out",
    )(o_f, o_b, proj, norm_g.reshape(1, HG_D))


def _lb_kernel(logit_ref, lb_ref, *, layer):
    x = logit_ref[...]
    e = jnp.exp(x - jnp.max(x, axis=1, keepdims=True))
    p = e / jnp.sum(e, axis=1, keepdims=True)
    acc = p[:, 0, :]
    for l in range(1, layer + 1):
        acc = acc + p[:, l, :]
    lb_ref[...] = acc


def _lower_bounds(logits, layer):
    z, _, w = logits.shape
    return pl.pallas_call(
        functools.partial(_lb_kernel, layer=layer),
        out_shape=jax.ShapeDtypeStruct((z, w), f32),
        name="hgrn_lower_bounds",
    )(logits)


def _merge_kernel(y0_ref, y1_ref, y2_ref, w_ref, g0_ref, g1_ref, g2_ref, o_ref):
    acc = g0_ref[...].astype(f32) * _dot(y0_ref[...], w_ref[0])
    acc = acc + g1_ref[...].astype(f32) * _dot(y1_ref[...], w_ref[1])
    acc = acc + g2_ref[...].astype(f32) * _dot(y2_ref[...], w_ref[2])
    o_ref[...] = acc.astype(o_ref.dtype)


def _merge(y0, y1, y2, w_branch, gsig, tm=1024, tn=512):
    m, kb = y0.shape
    d = w_branch.shape[2]
    nj = d // tn
    yspec = pl.BlockSpec((tm, kb), lambda i, j: (i, 0))
    def gspec(b):
        return pl.BlockSpec((tm, tn), lambda i, j: (i, b * nj + j))

    return pl.pallas_call(
        _merge_kernel,
        grid=(m // tm, nj),
        in_specs=[yspec, yspec, yspec,
                  pl.BlockSpec((N_BRANCH, kb, tn), lambda i, j: (0, 0, j)),
                  gspec(0), gspec(1), gspec(2)],
        out_specs=pl.BlockSpec((tm, tn), lambda i, j: (i, j)),
        out_shape=jax.ShapeDtypeStruct((m, d), bf16),
        compiler_params=_cparams(("parallel", "parallel")),
        name="branch_merge",
    )(y0, y1, y2, w_branch, gsig, gsig, gsig)


def _proj_ln_kernel(a_ref, w_ref, res_ref, g_ref, b_ref, o32_ref, o16_ref):
    x = ALPHA * res_ref[...] + _dot(a_ref[...], w_ref[...])
    y = _ln_rows(x, g_ref[...], b_ref[...])
    o32_ref[...] = y
    o16_ref[...] = y.astype(bf16)


def _proj_residual_ln(a, w, res, g, b, tm=512):
    m, k = a.shape
    d = w.shape[1]
    return pl.pallas_call(
        _proj_ln_kernel,
        grid=(m // tm,),
        in_specs=[pl.BlockSpec((tm, k), lambda i: (i, 0)),
                  pl.BlockSpec((k, d), lambda i: (0, 0)),
                  pl.BlockSpec((tm, d), lambda i: (i, 0)),
                  pl.BlockSpec((1, d), lambda i: (0, 0)),
                  pl.BlockSpec((1, d), lambda i: (0, 0))],
        out_specs=[pl.BlockSpec((tm, d), lambda i: (i, 0)),
                   pl.BlockSpec((tm, d), lambda i: (i, 0))],
        out_shape=[jax.ShapeDtypeStruct((m, d), f32), jax.ShapeDtypeStruct((m, d), bf16)],
        compiler_params=_cparams(("parallel",)),
        name="out_proj_ln",
    )(a, w, res, g.reshape(1, d), b.reshape(1, d))


def _ffn_kernel(x_ref, wg_ref, wu_ref, wd_ref, res_ref, g_ref, b_ref, o_ref, acc_ref):
    j = pl.program_id(1)

    @pl.when(j == 0)
    def _():
        acc_ref[...] = jnp.zeros_like(acc_ref)

    x = x_ref[...]
    hid = jax.nn.silu(_dot(x, wg_ref[...])) * _dot(x, wu_ref[...])
    acc_ref[...] += _dot(hid.astype(bf16), wd_ref[...])

    @pl.when(j == pl.num_programs(1) - 1)
    def _():
        o_ref[...] = _ln_rows(ALPHA * res_ref[...] + acc_ref[...], g_ref[...], b_ref[...])


def _ffn_residual_ln(x16, x32, wg, wu, wd, g, b, tm=512, tf=512):
    m, d = x16.shape
    ff = wg.shape[1]
    return pl.pallas_call(
        _ffn_kernel,
        grid=(m // tm, ff // tf),
        in_specs=[pl.BlockSpec((tm, d), lambda i, j: (i, 0)),
                  pl.BlockSpec((d, tf), lambda i, j: (0, j)),
                  pl.BlockSpec((d, tf), lambda i, j: (0, j)),
                  pl.BlockSpec((tf, d), lambda i, j: (j, 0)),
                  pl.BlockSpec((tm, d), lambda i, j: (i, 0)),
                  pl.BlockSpec((1, d), lambda i, j: (0, 0)),
                  pl.BlockSpec((1, d), lambda i, j: (0, 0))],
        out_specs=pl.BlockSpec((tm, d), lambda i, j: (i, 0)),
        out_shape=jax.ShapeDtypeStruct((m, d), f32),
        scratch_shapes=[pltpu.VMEM((tm, d), f32)],
        compiler_params=_cparams(("parallel", "arbitrary")),
        name="ffn_ln",
    )(x16, wg, wu, wd, x32, g.reshape(1, d), b.reshape(1, d))


def _rope_tables(positions):
    half = MLA_ROPE // 2
    inv_freq = jnp.power(ROPE_THETA, -jnp.arange(half, dtype=f32) / half)
    ang = positions.astype(f32)[..., None] * inv_freq
    cos, sin = jnp.cos(ang), jnp.sin(ang)
    z = jnp.zeros_like(cos)
    pad = jnp.zeros((ang.shape[0], LANES - MLA_ROPE), f32)
    c = jnp.concatenate([cos, cos, pad], axis=-1)
    s1 = jnp.concatenate([-sin, z, pad], axis=-1)
    s2 = jnp.concatenate([z, sin, pad], axis=-1)
    return c, s1, s2


def _layer(h32, h16, mem16, rope, lb, w_in, hgrn_norm_g, g_cq, g_ckv, w_uq, w_ukv, w_memkv, w_branch, w_o,
           ln1_g, ln1_b, w_gate, w_up, w_down, ln2_g, ln2_b):
    s, d = h32.shape
    hgw = HG_HEADS * HG_D
    q_rank = g_cq.shape[0]
    kv_rank = g_ckv.shape[0]
    memw = MEM_HEADS * MEM_HD
    o_cq = 5 * hgw
    o_ckv = o_cq + q_rank
    o_kr = o_ckv + kv_rank
    o_qm = o_kr + MLA_ROPE
    o_gt = o_qm + memw
    rope_c, rope_s1, rope_s2 = rope

    w_hg = w_in[:, :o_cq].astype(bf16)
    w_cq = w_in[:, o_cq:o_ckv].astype(bf16)
    w_ckv = w_in[:, o_ckv:o_kr].astype(bf16)
    w_kr = jnp.pad(w_in[:, o_kr:o_qm], ((0, 0), (0, LANES - MLA_ROPE))).astype(bf16)
    w_qm = w_in[:, o_qm:o_gt].astype(bf16)
    w_gt = w_in[:, o_gt:].astype(bf16)

    proj_hg = _matmul(h16, w_hg, tm=1024, tn=512, out_dtype=f32, name="in_proj_hgrn")
    cqn = _matmul(h16, w_cq, tm=1024, tn=q_rank, out_dtype=bf16, epilogue=_rms_epilogue,
                  col_extras=(g_cq.reshape(1, -1),), name="in_proj_cq")
    ckvn = _matmul(h16, w_ckv, tm=1024, tn=kv_rank, out_dtype=bf16, epilogue=_rms_epilogue,
                   col_extras=(g_ckv.reshape(1, -1),), name="in_proj_ckv")
    krope = _matmul(h16, w_kr, tm=1024, tn=LANES, out_dtype=bf16, epilogue=_krope_epilogue,
                    row_extras=(rope_c, rope_s1, rope_s2), name="in_proj_krope")
    q_mem = _matmul(h16, w_qm, tm=1024, tn=512, out_dtype=bf16,
                    epilogue=lambda acc: acc * (MEM_HD ** -0.5), name="in_proj_qmem")
    gsig = _matmul(h16, w_gt, tm=1024, tn=512, out_dtype=bf16, epilogue=jax.nn.sigmoid, name="in_proj_gates")

    o_f, o_b = _hgrn_scan(proj_hg, lb)
    y_hg = _hgrn_out(o_f, o_b, proj_hg, hgrn_norm_g)

    zq = jnp.zeros((q_rank, MLA_HEADS, MLA_QK_PAD - MLA_QK), f32)
    w_q = jnp.concatenate([w_uq.reshape(q_rank, MLA_HEADS, MLA_QK), zq], axis=-1)
    w_q = w_q.reshape(q_rank, MLA_HEADS * MLA_QK_PAD).astype(bf16)
    w_kv3 = w_ukv.reshape(kv_rank, MLA_HEADS, MLA_NOPE + MLA_V)
    w_kv = jnp.concatenate([w_kv3[:, :, :MLA_NOPE].reshape(kv_rank, -1),
                            w_kv3[:, :, MLA_NOPE:].reshape(kv_rank, -1)], axis=-1).astype(bf16)
    q = _mla_q(cqn, w_q, rope_c, rope_s1, rope_s2)
    k, v = _mla_kv(ckvn, w_kv, krope)
    y_mla = _mla_attention(q, k, v)

    kv_mem = _matmul(mem16, w_memkv.astype(bf16), tm=mem16.shape[0], tn=512, out_dtype=bf16, name="mem_kv_proj")
    y_mem = _mem_attention(q_mem, kv_mem[:, :memw], kv_mem[:, memw:])

    merged = _merge(y_hg, y_mla, y_mem, w_branch.astype(bf16), gsig)
    x32, x16 = _proj_residual_ln(merged, w_o.astype(bf16), h32, ln1_g, ln1_b)
    out32 = _ffn_residual_ln(x16, x32, w_gate.astype(bf16), w_up.astype(bf16), w_down.astype(bf16), ln2_g, ln2_b)
    return out32


def kernel(x, mem, positions, ln_emb_g, ln_emb_b, hgrn_lb_logits, w_in, hgrn_norm_g, mla_g_cq, mla_g_ckv,
           mla_w_uq, mla_w_ukv, mem_w_kv, w_branch, w_o, ln1_g, ln1_b, w_ffn_gate, w_ffn_up, w_ffn_down,
           ln2_g, ln2_b):
    bsz, s, d = x.shape
    depth = w_in.shape[0]
    outs = []
    for bi in range(bsz):
        rope = _rope_tables(positions[bi])
        mem16 = mem[bi].astype(bf16)
        h32, h16 = _layernorm(x[bi], ln_emb_g, ln_emb_b)
        for l in range(depth):
            lb = _lower_bounds(hgrn_lb_logits, l)
            h32 = _layer(h32, h16, mem16, rope, lb, w_in[l], hgrn_norm_g[l], mla_g_cq[l], mla_g_ckv[l],
                         mla_w_uq[l], mla_w_ukv[l], mem_w_kv[l], w_branch[l], w_o[l], ln1_g[l], ln1_b[l],
                         w_ffn_gate[l], w_ffn_up[l], w_ffn_down[l], ln2_g[l], ln2_b[l])
            if l + 1 < depth:
                h16 = h32.astype(bf16)
        outs.append(h32)
    return jnp.stack(outs)
```

```python
import functools

import jax
import jax.numpy as jnp
from jax import lax
from jax.experimental import pallas as pl
from jax.experimental.pallas import tpu as pltpu

HG_HEADS = 8
HG_D = 128
MLA_HEADS = 8
MLA_NOPE = 128
MLA_ROPE = 64
MLA_V = 128
MLA_QK = MLA_NOPE + MLA_ROPE
MLA_QK_PAD = 256
MLA_V_PAD = 256
LOG2E = 1.4426950408889634
MEM_HEADS = 4
MEM_HD = 256
N_BRANCH = 3
ROPE_THETA = 10000.0
LN_EPS = 1e-5
RMS_EPS = 1e-6
DEPTH = 1
ALPHA = (2.0 * DEPTH) ** 0.25

LANES = 128
SUBLANES = 8
VMEM_LIMIT = 56 * 1024 * 1024

HG_CHUNK = 64
HG_SUB = SUBLANES

bf16 = jnp.bfloat16
f32 = jnp.float32


def _cparams(sem):
    return pltpu.CompilerParams(dimension_semantics=sem, vmem_limit_bytes=VMEM_LIMIT)


def _dot(a, b):
    return jnp.dot(a, b, preferred_element_type=f32)


def _dot_nt(a, b):
    return lax.dot_general(a, b, (((1,), (1,)), ((), ())), preferred_element_type=f32)


def _dot_tn(a, b):
    return lax.dot_general(a, b, (((0,), (0,)), ((), ())), preferred_element_type=f32)


def _ln_rows(x, g, b):
    mu = jnp.mean(x, axis=-1, keepdims=True)
    xc = x - mu
    var = jnp.mean(xc * xc, axis=-1, keepdims=True)
    return xc * lax.rsqrt(var + LN_EPS) * g + b


def _ln_kernel(x_ref, g_ref, b_ref, o32_ref, o16_ref):
    y = _ln_rows(x_ref[...], g_ref[...], b_ref[...])
    o32_ref[...] = y
    o16_ref[...] = y.astype(bf16)


def _layernorm(x, g, b, tm=512):
    m, d = x.shape
    return pl.pallas_call(
        _ln_kernel,
        grid=(m // tm,),
        in_specs=[pl.BlockSpec((tm, d), lambda i: (i, 0)),
                  pl.BlockSpec((1, d), lambda i: (0, 0)),
                  pl.BlockSpec((1, d), lambda i: (0, 0))],
        out_specs=[pl.BlockSpec((tm, d), lambda i: (i, 0)),
                   pl.BlockSpec((tm, d), lambda i: (i, 0))],
        out_shape=[jax.ShapeDtypeStruct((m, d), f32), jax.ShapeDtypeStruct((m, d), bf16)],
        compiler_params=_cparams(("parallel",)),
        name="ln_embed",
    )(x, g.reshape(1, d), b.reshape(1, d))


def _mm_kernel(a_ref, w_ref, *rest, epilogue):
    o_ref = rest[-1]
    acc = _dot(a_ref[...], w_ref[...].astype(bf16))
    o_ref[...] = epilogue(acc, *[r[...] for r in rest[:-1]]).astype(o_ref.dtype)


def _matmul(a, w, *, tm, tn, out_dtype, epilogue=None, row_extras=(), col_extras=(), name, col0=0, n=None):
    m, k = a.shape
    n = w.shape[1] if n is None else n
    if epilogue is None:
        epilogue = lambda acc: acc
    in_specs = [pl.BlockSpec((tm, k), lambda i, j: (i, 0)),
                pl.BlockSpec((k, tn), lambda i, j: (0, col0 + j))]
    for e in row_extras:
        in_specs.append(pl.BlockSpec((tm, e.shape[1]), lambda i, j: (i, 0)))
    for e in col_extras:
        in_specs.append(pl.BlockSpec((e.shape[0], tn), lambda i, j: (0, j)))
    return pl.pallas_call(
        functools.partial(_mm_kernel, epilogue=epilogue),
        grid=(m // tm, n // tn),
        in_specs=in_specs,
        out_specs=pl.BlockSpec((tm, tn), lambda i, j: (i, j)),
        out_shape=jax.ShapeDtypeStruct((m, n), out_dtype),
        compiler_params=_cparams(("parallel", "parallel")),
        name=name,
    )(a, w, *row_extras, *col_extras)


def _rms_epilogue(acc, g):
    ms = jnp.mean(acc * acc, axis=-1, keepdims=True)
    return acc * lax.rsqrt(ms + RMS_EPS) * g


def _rope_block(blk, c, s1, s2):
    half = MLA_ROPE // 2
    return blk * c + pltpu.roll(blk, LANES - half, 1) * s1 + pltpu.roll(blk, half, 1) * s2


def _krope_epilogue(acc, c, s1, s2):
    return _rope_block(acc, c, s1, s2)


def _mla_q_kernel(cq_ref, w_ref, c_ref, s1_ref, s2_ref, o_ref):
    q = _dot(cq_ref[...], w_ref[...]) * (MLA_QK ** -0.5 * LOG2E)
    c, s1, s2 = c_ref[...], s1_ref[...], s2_ref[...]
    for h in range(MLA_HEADS):
        base = h * MLA_QK_PAD
        o_ref[:, base:base + MLA_NOPE] = q[:, base:base + MLA_NOPE].astype(bf16)
        o_ref[:, base + MLA_NOPE:base + MLA_QK_PAD] = _rope_block(
            q[:, base + MLA_NOPE:base + MLA_QK_PAD], c, s1, s2).astype(bf16)


def _mla_q(cqn, w_q, rope_c, rope_s1, rope_s2, tm=512):
    m, r = cqn.shape
    n = w_q.shape[1]
    return pl.pallas_call(
        _mla_q_kernel,
        grid=(m // tm,),
        in_specs=[pl.BlockSpec((tm, r), lambda i: (i, 0)),
                  pl.BlockSpec((r, n), lambda i: (0, 0)),
                  pl.BlockSpec((tm, LANES), lambda i: (i, 0)),
                  pl.BlockSpec((tm, LANES), lambda i: (i, 0)),
                  pl.BlockSpec((tm, LANES), lambda i: (i, 0))],
        out_specs=pl.BlockSpec((tm, n), lambda i: (i, 0)),
        out_shape=jax.ShapeDtypeStruct((m, n), bf16),
        compiler_params=_cparams(("parallel",)),
        name="mla_q_up",
    )(cqn, w_q, rope_c, rope_s1, rope_s2)


def _mla_kv_kernel(ckv_ref, w_ref, kr_ref, k_ref, v_ref):
    kv = _dot(ckv_ref[...], w_ref[...])
    kr = kr_ref[...]
    nk = MLA_HEADS * MLA_NOPE
    for h in range(MLA_HEADS):
        base = h * MLA_QK_PAD
        k_ref[:, base:base + MLA_NOPE] = kv[:, h * MLA_NOPE:(h + 1) * MLA_NOPE].astype(bf16)
        k_ref[:, base + MLA_NOPE:base + MLA_QK_PAD] = kr
    lane = lax.broadcasted_iota(jnp.int32, (kv.shape[0], MLA_V_PAD - MLA_V), 1)
    ones_col = jnp.where(lane == 0, 1.0, 0.0).astype(bf16)
    for h in range(MLA_HEADS):
        base = h * MLA_V_PAD
        v_ref[:, base:base + MLA_V] = kv[:, nk + h * MLA_V:nk + (h + 1) * MLA_V].astype(bf16)
        v_ref[:, base + MLA_V:base + MLA_V_PAD] = ones_col


def _mla_kv(ckvn, w_kv, krope, tm=512):
    m, r = ckvn.shape
    n = w_kv.shape[1]
    nk = MLA_HEADS * MLA_QK_PAD
    nv = MLA_HEADS * MLA_V_PAD
    return pl.pallas_call(
        _mla_kv_kernel,
        grid=(m // tm,),
        in_specs=[pl.BlockSpec((tm, r), lambda i: (i, 0)),
                  pl.BlockSpec((r, n), lambda i: (0, 0)),
                  pl.BlockSpec((tm, LANES), lambda i: (i, 0))],
        out_specs=[pl.BlockSpec((tm, nk), lambda i: (i, 0)),
                   pl.BlockSpec((tm, nv), lambda i: (i, 0))],
        out_shape=[jax.ShapeDtypeStruct((m, nk), bf16), jax.ShapeDtypeStruct((m, nv), bf16)],
        compiler_params=_cparams(("parallel",)),
        name="mla_kv_up",
    )(ckvn, w_kv, krope)


def _mla_attn_kernel(q_ref, k_ref, v_ref, o_ref, *, tk):
    q = q_ref[...]
    tq = q.shape[0]
    nkv = k_ref.shape[0] // tk

    def scores(c):
        return _dot_nt(q, k_ref[c * tk:(c + 1) * tk, :])

    def update(s, c, m, acc):
        m_new = jnp.maximum(m, jnp.max(s, axis=-1, keepdims=True))
        p = jnp.exp2(s - m_new)
        acc = jnp.exp2(m - m_new) * acc + _dot(p.astype(bf16), v_ref[c * tk:(c + 1) * tk, :])
        return m_new, acc

    m = jnp.full((tq, 1), -jnp.inf, f32)
    acc = jnp.zeros((tq, MLA_V_PAD), f32)
    s_next = scores(0)
    for c in range(nkv):
        s_cur = s_next
        if c + 1 < nkv:
            s_next = scores(c + 1)
        m, acc = update(s_cur, c, m, acc)
    o_ref[...] = (acc[:, :MLA_V] / acc[:, MLA_V:MLA_V + 1]).astype(o_ref.dtype)


def _mla_attention(q, k, v, tq=512, tk=512):
    s = q.shape[0]
    return pl.pallas_call(
        functools.partial(_mla_attn_kernel, tk=tk),
        grid=(MLA_HEADS, s // tq),
        in_specs=[pl.BlockSpec((tq, MLA_QK_PAD), lambda h, i: (i, h)),
                  pl.BlockSpec((s, MLA_QK_PAD), lambda h, i: (0, h)),
                  pl.BlockSpec((s, MLA_V_PAD), lambda h, i: (0, h))],
        out_specs=pl.BlockSpec((tq, MLA_V), lambda h, i: (i, h)),
        out_shape=jax.ShapeDtypeStruct((s, MLA_HEADS * MLA_V), bf16),
        compiler_params=_cparams(("parallel", "parallel")),
        name="mla_attention",
    )(q, k, v)


def _mem_attn_kernel(q_ref, k_ref, v_ref, o_ref):
    for h in range(MEM_HEADS):
        sl = slice(h * MEM_HD, (h + 1) * MEM_HD)
        s = _dot_nt(q_ref[:, sl], k_ref[:, sl])
        m = jnp.max(s, axis=-1, keepdims=True)
        p = jnp.exp(s - m)
        l = jnp.sum(p, axis=-1, keepdims=True)
        o = _dot(p.astype(bf16), v_ref[:, sl])
        o_ref[:, sl] = (o / l).astype(o_ref.dtype)


def _mem_attention(q, k, v, tm=512):
    s, w = q.shape
    nm = k.shape[0]
    return pl.pallas_call(
        _mem_attn_kernel,
        grid=(s // tm,),
        in_specs=[pl.BlockSpec((tm, w), lambda i: (i, 0)),
                  pl.BlockSpec((nm, w), lambda i: (0, 0)),
                  pl.BlockSpec((nm, w), lambda i: (0, 0))],
        out_specs=pl.BlockSpec((tm, w), lambda i: (i, 0)),
        out_shape=jax.ShapeDtypeStruct((s, w), bf16),
        compiler_params=_cparams(("parallel",)),
        name="mem_attention",
    )(q, k, v)


def _hgrn_levels(rev):
    levels = []
    sz = HG_CHUNK // 2
    while sz >= HG_SUB:
        blocks = []
        for base in range(0, HG_CHUNK, 2 * sz):
            if not rev:
                blocks.append((base + sz, base, sz, base + sz - 1))
            else:
                blocks.append((base, base + sz, sz, base + sz))
        levels.append(blocks)
        sz //= 2
    return levels


def _hgrn_intra(q_c, b_c, c_c, bbuf, cbuf, row0, *, rev):
    C = HG_CHUNK
    nb = C // HG_SUB
    lane = lax.broadcasted_iota(jnp.int32, (HG_SUB, C), 1)
    sub = lax.broadcasted_iota(jnp.int32, (HG_SUB, C), 0)
    blocks = []
    for i in range(nb):
        r0 = i * HG_SUB
        q_i = q_c[r0:r0 + HG_SUB]
        b_i = b_c[r0:r0 + HG_SUB]
        a_blk = jnp.zeros((HG_SUB, C), f32)
        for s in range(HG_SUB):
            w = q_i * jnp.exp2(b_i + cbuf[row0 + r0 + s:row0 + r0 + s + 1, :])
            a_blk = jnp.where(lane == r0 + s, jnp.sum(w, axis=-1, keepdims=True), a_blk)
        keep = (sub >= lane - r0) if not rev else (sub <= lane - r0)
        blocks.append(jnp.where(keep, a_blk, 0.0))
    for level in _hgrn_levels(rev):
        sz = level[0][2]
        qd, kd, valid = [], [], None
        k_pos = 0
        rq = lax.broadcasted_iota(jnp.int32, (len(level) * sz, C), 0)
        ck = lax.broadcasted_iota(jnp.int32, (len(level) * sz, C), 1)
        for j, (q0, k0, _, piv_row) in enumerate(level):
            piv = bbuf[row0 + piv_row:row0 + piv_row + 1, :]
            qd.append(q_c[q0:q0 + sz] * jnp.exp2(b_c[q0:q0 + sz] - piv))
            if k0 > k_pos:
                kd.append(jnp.zeros((k0 - k_pos, q_c.shape[1]), f32))
            kd.append(jnp.exp2(c_c[k0:k0 + sz] + piv))
            k_pos = k0 + sz
            ok = (rq >= j * sz) & (rq < (j + 1) * sz) & (ck >= k0) & (ck < k0 + sz)
            valid = ok if valid is None else (valid | ok)
        if k_pos < C:
            kd.append(jnp.zeros((C - k_pos, q_c.shape[1]), f32))
        r = _dot_nt(jnp.concatenate(qd, axis=0).astype(bf16), jnp.concatenate(kd, axis=0).astype(bf16))
        if len(level) > 1:
            r = jnp.where(valid, r, 0.0)
        for j, (q0, _, _, _) in enumerate(level):
            for t in range(sz // HG_SUB):
                i = q0 // HG_SUB + t
                blocks[i] = blocks[i] + r[j * sz + t * HG_SUB:j * sz + (t + 1) * HG_SUB]
    return jnp.concatenate(blocks, axis=0)


def _hgrn_block_kernel(qf_ref, vf_ref, ff_ref, qb_ref, vb_ref, fb_ref, lb_ref, of_ref, ob_ref,
                       sf_ref, sb_ref, bbuf, cbuf):
    n = pl.program_id(1)
    C = HG_CHUNK
    T = qf_ref.shape[0]
    nchunk = T // C

    @pl.when(n == 0)
    def _():
        sf_ref[...] = jnp.zeros_like(sf_ref)
        sb_ref[...] = jnp.zeros_like(sb_ref)

    r = lax.broadcasted_iota(jnp.int32, (C, C), 0)
    c = lax.broadcasted_iota(jnp.int32, (C, C), 1)
    dirs = []
    for d, (q_ref, v_ref, f_ref, rev) in enumerate(((qf_ref, vf_ref, ff_ref, False), (qb_ref, vb_ref, fb_ref, True))):
        tri = ((c <= r) if not rev else (c >= r)).astype(bf16)
        lb = lb_ref[d:d + 1, :]
        q = jax.nn.silu(q_ref[...])
        v16 = v_ref[...].astype(bf16)
        f = lb + (1.0 - lb) * jax.nn.sigmoid(f_ref[...])
        g = jnp.log(f) * LOG2E
        g_hi = g.astype(bf16)
        g_lo = (g - g_hi.astype(f32)).astype(bf16)
        b = jnp.concatenate([_dot(tri, g_hi[j * C:(j + 1) * C]) + _dot(tri, g_lo[j * C:(j + 1) * C])
                             for j in range(nchunk)], axis=0)
        cc = jnp.log(1.0 - f) * LOG2E - b
        bbuf[d] = b
        cbuf[d] = cc
        dirs.append((q, v16, b, cc, rev))

    intra = []
    for d, (q, v16, b, cc, rev) in enumerate(dirs):
        intra.append([_hgrn_intra(q[j * C:(j + 1) * C], b[j * C:(j + 1) * C], cc[j * C:(j + 1) * C],
                                  bbuf.at[d], cbuf.at[d], j * C, rev=rev).astype(bf16) for j in range(nchunk)])

    for d, (q, v16, b, cc, rev) in enumerate(dirs):
        s_ref, o_ref = (sf_ref, of_ref) if not rev else (sb_ref, ob_ref)
        qe = (q * jnp.exp2(b)).astype(bf16)
        order = range(nchunk) if not rev else range(nchunk - 1, -1, -1)
        edge = (C - 1) if not rev else 0
        upd = {}
        for j in order:
            b_edge = bbuf[d, j * C + edge:j * C + edge + 1, :]
            kdec = jnp.exp2(cc[j * C:(j + 1) * C] + b_edge).astype(bf16)
            upd[j] = (jnp.exp2(b_edge), _dot_tn(v16[j * C:(j + 1) * C], kdec))
        state = s_ref[...]
        for j in order:
            rows = slice(j * C, (j + 1) * C)
            o_ref[rows, :] = _dot_nt(qe[rows], state.astype(bf16)) + _dot(intra[d][j], v16[rows])
            state = upd[j][0] * state + upd[j][1]
        s_ref[...] = state


def _hgrn_scan(proj, lb, tb=512):
    s = proj.shape[0]
    w = HG_HEADS * HG_D
    nb = s // tb
    hb = w // HG_D
    def fw(seg):
        return pl.BlockSpec((tb, HG_D), lambda h, n: (n, seg * hb + h))

    def bw(seg):
        return pl.BlockSpec((tb, HG_D), lambda h, n: (nb - 1 - n, seg * hb + h))

    return pl.pallas_call(
        _hgrn_block_kernel,
        grid=(HG_HEADS, nb),
        in_specs=[fw(0), fw(1), fw(2), bw(0), bw(1), bw(3),
                  pl.BlockSpec((2, HG_D), lambda h, n: (0, h))],
        out_specs=[pl.BlockSpec((tb, HG_D), lambda h, n: (n, h)),
                   pl.BlockSpec((tb, HG_D), lambda h, n: (nb - 1 - n, h))],
        out_shape=[jax.ShapeDtypeStruct((s, w), f32), jax.ShapeDtypeStruct((s, w), f32)],
        scratch_shapes=[pltpu.VMEM((HG_D, HG_D), f32), pltpu.VMEM((HG_D, HG_D), f32),
                        pltpu.VMEM((2, tb, HG_D), f32), pltpu.VMEM((2, tb, HG_D), f32)],
        compiler_params=_cparams(("parallel", "arbitrary")),
        name="hgrn_scan",
    )(proj, proj, proj, proj, proj, proj, lb)


def _hgrn_out_kernel(of_ref, ob_ref, g_ref, ng_ref, y_ref):
    ng = ng_ref[...]
    for h in range(HG_HEADS):
        sl = slice(h * HG_D, (h + 1) * HG_D)
        o = of_ref[:, sl] + ob_ref[:, sl]
        ms = jnp.mean(o * o, axis=-1, keepdims=True)
        o = o * lax.rsqrt(ms + RMS_EPS) * ng
        y_ref[:, sl] = (o * jax.nn.sigmoid(g_ref[:, sl])).astype(y_ref.dtype)


def _hgrn_out(o_f, o_b, proj, norm_g, tm=512):
    s, w = o_f.shape
    gate_blk = 4
    return pl.pallas_call(
        _hgrn_out_kernel,
        grid=(s // tm,),
        in_specs=[pl.BlockSpec((tm, w), lambda i: (i, 0)),
                  pl.BlockSpec((tm, w), lambda i: (i, 0)),
                  pl.BlockSpec((tm, w), lambda i: (i, gate_blk)),
                  pl.BlockSpec((1, HG_D), lambda i: (0, 0))],
        out_specs=pl.BlockSpec((tm, w), lambda i: (i, 0)),
        out_shape=jax.ShapeDtypeStruct((s, w), bf16),
        compiler_params=_cparams(("parallel",)),
        name="hgrn_out",
    )(o_f, o_b, proj, norm_g.reshape(1, HG_D))


def _lb_kernel(logit_ref, lb_ref, *, layer):
    x = logit_ref[...]
    e = jnp.exp(x - jnp.max(x, axis=1, keepdims=True))
    p = e / jnp.sum(e, axis=1, keepdims=True)
    acc = p[:, 0, :]
    for l in range(1, layer + 1):
        acc = acc + p[:, l, :]
    lb_ref[...] = acc


def _lower_bounds(logits, layer):
    z, _, w = logits.shape
    return pl.pallas_call(
        functools.partial(_lb_kernel, layer=layer),
        out_shape=jax.ShapeDtypeStruct((z, w), f32),
        name="hgrn_lower_bounds",
    )(logits)


def _merge_kernel(y0_ref, y1_ref, y2_ref, w_ref, g0_ref, g1_ref, g2_ref, o_ref):
    acc = g0_ref[...].astype(f32) * _dot(y0_ref[...], w_ref[0])
    acc = acc + g1_ref[...].astype(f32) * _dot(y1_ref[...], w_ref[1])
    acc = acc + g2_ref[...].astype(f32) * _dot(y2_ref[...], w_ref[2])
    o_ref[...] = acc.astype(o_ref.dtype)


def _merge(y0, y1, y2, w_branch, gsig, tm=1024, tn=512):
    m, kb = y0.shape
    d = w_branch.shape[2]
    nj = d // tn
    yspec = pl.BlockSpec((tm, kb), lambda i, j: (i, 0))
    def gspec(b):
        return pl.BlockSpec((tm, tn), lambda i, j: (i, b * nj + j))

    return pl.pallas_call(
        _merge_kernel,
        grid=(m // tm, nj),
        in_specs=[yspec, yspec, yspec,
                  pl.BlockSpec((N_BRANCH, kb, tn), lambda i, j: (0, 0, j)),
                  gspec(0), gspec(1), gspec(2)],
        out_specs=pl.BlockSpec((tm, tn), lambda i, j: (i, j)),
        out_shape=jax.ShapeDtypeStruct((m, d), bf16),
        compiler_params=_cparams(("parallel", "parallel")),
        name="branch_merge",
    )(y0, y1, y2, w_branch, gsig, gsig, gsig)


def _proj_ln_kernel(a_ref, w_ref, res_ref, g_ref, b_ref, o32_ref, o16_ref):
    x = ALPHA * res_ref[...] + _dot(a_ref[...], w_ref[...])
    y = _ln_rows(x, g_ref[...], b_ref[...])
    o32_ref[...] = y
    o16_ref[...] = y.astype(bf16)


def _proj_residual_ln(a, w, res, g, b, tm=512):
    m, k = a.shape
    d = w.shape[1]
    return pl.pallas_call(
        _proj_ln_kernel,
        grid=(m // tm,),
        in_specs=[pl.BlockSpec((tm, k), lambda i: (i, 0)),
                  pl.BlockSpec((k, d), lambda i: (0, 0)),
                  pl.BlockSpec((tm, d), lambda i: (i, 0)),
                  pl.BlockSpec((1, d), lambda i: (0, 0)),
                  pl.BlockSpec((1, d), lambda i: (0, 0))],
        out_specs=[pl.BlockSpec((tm, d), lambda i: (i, 0)),
                   pl.BlockSpec((tm, d), lambda i: (i, 0))],
        out_shape=[jax.ShapeDtypeStruct((m, d), f32), jax.ShapeDtypeStruct((m, d), bf16)],
        compiler_params=_cparams(("parallel",)),
        name="out_proj_ln",
    )(a, w, res, g.reshape(1, d), b.reshape(1, d))


def _ffn_kernel(x_ref, wg_ref, wu_ref, wd_ref, res_ref, g_ref, b_ref, o_ref, acc_ref):
    j = pl.program_id(1)

    @pl.when(j == 0)
    def _():
        acc_ref[...] = jnp.zeros_like(acc_ref)

    x = x_ref[...]
    hid = jax.nn.silu(_dot(x, wg_ref[...])) * _dot(x, wu_ref[...])
    acc_ref[...] += _dot(hid.astype(bf16), wd_ref[...])

    @pl.when(j == pl.num_programs(1) - 1)
    def _():
        o_ref[...] = _ln_rows(ALPHA * res_ref[...] + acc_ref[...], g_ref[...], b_ref[...])


def _ffn_residual_ln(x16, x32, wg, wu, wd, g, b, tm=512, tf=512):
    m, d = x16.shape
    ff = wg.shape[1]
    return pl.pallas_call(
        _ffn_kernel,
        grid=(m // tm, ff // tf),
        in_specs=[pl.BlockSpec((tm, d), lambda i, j: (i, 0)),
                  pl.BlockSpec((d, tf), lambda i, j: (0, j)),
                  pl.BlockSpec((d, tf), lambda i, j: (0, j)),
                  pl.BlockSpec((tf, d), lambda i, j: (j, 0)),
                  pl.BlockSpec((tm, d), lambda i, j: (i, 0)),
                  pl.BlockSpec((1, d), lambda i, j: (0, 0)),
                  pl.BlockSpec((1, d), lambda i, j: (0, 0))],
        out_specs=pl.BlockSpec((tm, d), lambda i, j: (i, 0)),
        out_shape=jax.ShapeDtypeStruct((m, d), f32),
        scratch_shapes=[pltpu.VMEM((tm, d), f32)],
        compiler_params=_cparams(("parallel", "arbitrary")),
        name="ffn_ln",
    )(x16, wg, wu, wd, x32, g.reshape(1, d), b.reshape(1, d))


def _rope_tables(positions):
    half = MLA_ROPE // 2
    inv_freq = jnp.power(ROPE_THETA, -jnp.arange(half, dtype=f32) / half)
    ang = positions.astype(f32)[..., None] * inv_freq
    cos, sin = jnp.cos(ang), jnp.sin(ang)
    z = jnp.zeros_like(cos)
    pad = jnp.zeros((ang.shape[0], LANES - MLA_ROPE), f32)
    c = jnp.concatenate([cos, cos, pad], axis=-1)
    s1 = jnp.concatenate([-sin, z, pad], axis=-1)
    s2 = jnp.concatenate([z, sin, pad], axis=-1)
    return c, s1, s2


def _layer(h32, h16, mem16, rope, lb, w_in, hgrn_norm_g, g_cq, g_ckv, w_uq, w_ukv, w_memkv, w_branch, w_o,
           ln1_g, ln1_b, w_gate, w_up, w_down, ln2_g, ln2_b):
    s, d = h32.shape
    hgw = HG_HEADS * HG_D
    q_rank = g_cq.shape[0]
    kv_rank = g_ckv.shape[0]
    memw = MEM_HEADS * MEM_HD
    o_cq = 5 * hgw
    o_ckv = o_cq + q_rank
    o_kr = o_ckv + kv_rank
    o_qm = o_kr + MLA_ROPE
    o_gt = o_qm + memw
    rope_c, rope_s1, rope_s2 = rope

    assert o_cq % 512 == 0 and o_ckv % q_rank == 0 and o_kr % kv_rank == 0 and o_kr % LANES == 0
    w_qm = w_in[:, o_qm:o_gt].astype(bf16)
    w_gt = w_in[:, o_gt:].astype(bf16)

    proj_hg = _matmul(h16, w_in, tm=1024, tn=512, n=o_cq, out_dtype=f32, name="in_proj_hgrn")
    cqn = _matmul(h16, w_in, tm=1024, tn=q_rank, col0=o_cq // q_rank, n=q_rank, out_dtype=bf16,
                  epilogue=_rms_epilogue, col_extras=(g_cq.reshape(1, -1),), name="in_proj_cq")
    ckvn = _matmul(h16, w_in, tm=1024, tn=kv_rank, col0=o_ckv // kv_rank, n=kv_rank, out_dtype=bf16,
                   epilogue=_rms_epilogue, col_extras=(g_ckv.reshape(1, -1),), name="in_proj_ckv")
    krope = _matmul(h16, w_in, tm=1024, tn=LANES, col0=o_kr // LANES, n=LANES, out_dtype=bf16,
                    epilogue=_krope_epilogue, row_extras=(rope_c, rope_s1, rope_s2), name="in_proj_krope")
    q_mem = _matmul(h16, w_qm, tm=1024, tn=512, out_dtype=bf16,
                    epilogue=lambda acc: acc * (MEM_HD ** -0.5), name="in_proj_qmem")
    gsig = _matmul(h16, w_gt, tm=1024, tn=512, out_dtype=bf16, epilogue=jax.nn.sigmoid, name="in_proj_gates")

    o_f, o_b = _hgrn_scan(proj_hg, lb)
    y_hg = _hgrn_out(o_f, o_b, proj_hg, hgrn_norm_g)

    zq = jnp.zeros((q_rank, MLA_HEADS, MLA_QK_PAD - MLA_QK), f32)
    w_q = jnp.concatenate([w_uq.reshape(q_rank, MLA_HEADS, MLA_QK), zq], axis=-1)
    w_q = w_q.reshape(q_rank, MLA_HEADS * MLA_QK_PAD).astype(bf16)
    w_kv3 = w_ukv.reshape(kv_rank, MLA_HEADS, MLA_NOPE + MLA_V)
    w_kv = jnp.concatenate([w_kv3[:, :, :MLA_NOPE].reshape(kv_rank, -1),
                            w_kv3[:, :, MLA_NOPE:].reshape(kv_rank, -1)], axis=-1).astype(bf16)
    q = _mla_q(cqn, w_q, rope_c, rope_s1, rope_s2)
    k, v = _mla_kv(ckvn, w_kv, krope)
    y_mla = _mla_attention(q, k, v)

    kv_mem = _matmul(mem16, w_memkv.astype(bf16), tm=mem16.shape[0], tn=512, out_dtype=bf16, name="mem_kv_proj")
    y_mem = _mem_attention(q_mem, kv_mem[:, :memw], kv_mem[:, memw:])

    merged = _merge(y_hg, y_mla, y_mem, w_branch.astype(bf16), gsig)
    x32, x16 = _proj_residual_ln(merged, w_o.astype(bf16), h32, ln1_g, ln1_b)
    out32 = _ffn_residual_ln(x16, x32, w_gate.astype(bf16), w_up.astype(bf16), w_down.astype(bf16), ln2_g, ln2_b)
    return out32


def kernel(x, mem, positions, ln_emb_g, ln_emb_b, hgrn_lb_logits, w_in, hgrn_norm_g, mla_g_cq, mla_g_ckv,
           mla_w_uq, mla_w_ukv, mem_w_kv, w_branch, w_o, ln1_g, ln1_b, w_ffn_gate, w_ffn_up, w_ffn_down,
           ln2_g, ln2_b):
    bsz, s, d = x.shape
    depth = w_in.shape[0]
    outs = []
    for bi in range(bsz):
        rope = _rope_tables(positions[bi])
        mem16 = mem[bi].astype(bf16)
        h32, h16 = _layernorm(x[bi], ln_emb_g, ln_emb_b)
        for l in range(depth):
            lb = _lower_bounds(hgrn_lb_logits, l)
            h32 = _layer(h32, h16, mem16, rope, lb, w_in[l], hgrn_norm_g[l], mla_g_cq[l], mla_g_ckv[l],
                         mla_w_uq[l], mla_w_ukv[l], mem_w_kv[l], w_branch[l], w_o[l], ln1_g[l], ln1_b[l],
                         w_ffn_gate[l], w_ffn_up[l], w_ffn_down[l], ln2_g[l], ln2_b[l])
            if l + 1 < depth:
                h16 = h32.astype(bf16)
        outs.append(h32)
    return jnp.stack(outs)
```

```python
import functools

import jax
import jax.numpy as jnp
from jax import lax
from jax.experimental import pallas as pl
from jax.experimental.pallas import tpu as pltpu

HG_HEADS = 8
HG_D = 128
MLA_HEADS = 8
MLA_NOPE = 128
MLA_ROPE = 64
MLA_V = 128
MLA_QK = MLA_NOPE + MLA_ROPE
MLA_QK_PAD = 256
MLA_V_PAD = 256
LOG2E = 1.4426950408889634
MEM_HEADS = 4
MEM_HD = 256
N_BRANCH = 3
ROPE_THETA = 10000.0
LN_EPS = 1e-5
RMS_EPS = 1e-6
DEPTH = 1
ALPHA = (2.0 * DEPTH) ** 0.25

LANES = 128
SUBLANES = 8
VMEM_LIMIT = 56 * 1024 * 1024

HG_CHUNK = 64
HG_SUB = SUBLANES

bf16 = jnp.bfloat16
f32 = jnp.float32


def _cparams(sem):
    return pltpu.CompilerParams(dimension_semantics=sem, vmem_limit_bytes=VMEM_LIMIT)


def _dot(a, b):
    return jnp.dot(a, b, preferred_element_type=f32)


def _dot_nt(a, b):
    return lax.dot_general(a, b, (((1,), (1,)), ((), ())), preferred_element_type=f32)


def _dot_tn(a, b):
    return lax.dot_general(a, b, (((0,), (0,)), ((), ())), preferred_element_type=f32)


def _ln_rows(x, g, b):
    mu = jnp.mean(x, axis=-1, keepdims=True)
    xc = x - mu
    var = jnp.mean(xc * xc, axis=-1, keepdims=True)
    return xc * lax.rsqrt(var + LN_EPS) * g + b


def _ln_kernel(x_ref, g_ref, b_ref, o32_ref, o16_ref):
    y = _ln_rows(x_ref[...], g_ref[...], b_ref[...])
    o32_ref[...] = y
    o16_ref[...] = y.astype(bf16)


def _layernorm(x, g, b, tm=512):
    m, d = x.shape
    return pl.pallas_call(
        _ln_kernel,
        grid=(m // tm,),
        in_specs=[pl.BlockSpec((tm, d), lambda i: (i, 0)),
                  pl.BlockSpec((1, d), lambda i: (0, 0)),
                  pl.BlockSpec((1, d), lambda i: (0, 0))],
        out_specs=[pl.BlockSpec((tm, d), lambda i: (i, 0)),
                   pl.BlockSpec((tm, d), lambda i: (i, 0))],
        out_shape=[jax.ShapeDtypeStruct((m, d), f32), jax.ShapeDtypeStruct((m, d), bf16)],
        compiler_params=_cparams(("parallel",)),
        name="ln_embed",
    )(x, g.reshape(1, d), b.reshape(1, d))


def _mm_kernel(a_ref, w_ref, *rest, epilogue):
    o_ref = rest[-1]
    acc = _dot(a_ref[...], w_ref[...].astype(bf16))
    o_ref[...] = epilogue(acc, *[r[...] for r in rest[:-1]]).astype(o_ref.dtype)


def _matmul(a, w, *, tm, tn, out_dtype, epilogue=None, row_extras=(), col_extras=(), name, col0=0, n=None):
    m, k = a.shape
    n = w.shape[1] if n is None else n
    if epilogue is None:
        epilogue = lambda acc: acc
    in_specs = [pl.BlockSpec((tm, k), lambda i, j: (i, 0)),
                pl.BlockSpec((k, tn), lambda i, j: (0, col0 + j))]
    for e in row_extras:
        in_specs.append(pl.BlockSpec((tm, e.shape[1]), lambda i, j: (i, 0)))
    for e in col_extras:
        in_specs.append(pl.BlockSpec((e.shape[0], tn), lambda i, j: (0, j)))
    return pl.pallas_call(
        functools.partial(_mm_kernel, epilogue=epilogue),
        grid=(m // tm, n // tn),
        in_specs=in_specs,
        out_specs=pl.BlockSpec((tm, tn), lambda i, j: (i, j)),
        out_shape=jax.ShapeDtypeStruct((m, n), out_dtype),
        compiler_params=_cparams(("parallel", "parallel")),
        name=name,
    )(a, w, *row_extras, *col_extras)


def _rms_epilogue(acc, g):
    ms = jnp.mean(acc * acc, axis=-1, keepdims=True)
    return acc * lax.rsqrt(ms + RMS_EPS) * g


def _rope_block(blk, c, s1, s2):
    half = MLA_ROPE // 2
    return blk * c + pltpu.roll(blk, LANES - half, 1) * s1 + pltpu.roll(blk, half, 1) * s2


def _krope_epilogue(acc, c, s1, s2):
    return _rope_block(acc, c, s1, s2)


def _mla_q_kernel(cq_ref, w_ref, c_ref, s1_ref, s2_ref, o_ref):
    q = _dot(cq_ref[...], w_ref[...]) * (MLA_QK ** -0.5 * LOG2E)
    c, s1, s2 = c_ref[...], s1_ref[...], s2_ref[...]
    for h in range(MLA_HEADS):
        base = h * MLA_QK_PAD
        o_ref[:, base:base + MLA_NOPE] = q[:, base:base + MLA_NOPE].astype(bf16)
        o_ref[:, base + MLA_NOPE:base + MLA_QK_PAD] = _rope_block(
            q[:, base + MLA_NOPE:base + MLA_QK_PAD], c, s1, s2).astype(bf16)


def _mla_q(cqn, w_q, rope_c, rope_s1, rope_s2, tm=512):
    m, r = cqn.shape
    n = w_q.shape[1]
    return pl.pallas_call(
        _mla_q_kernel,
        grid=(m // tm,),
        in_specs=[pl.BlockSpec((tm, r), lambda i: (i, 0)),
                  pl.BlockSpec((r, n), lambda i: (0, 0)),
                  pl.BlockSpec((tm, LANES), lambda i: (i, 0)),
                  pl.BlockSpec((tm, LANES), lambda i: (i, 0)),
                  pl.BlockSpec((tm, LANES), lambda i: (i, 0))],
        out_specs=pl.BlockSpec((tm, n), lambda i: (i, 0)),
        out_shape=jax.ShapeDtypeStruct((m, n), bf16),
        compiler_params=_cparams(("parallel",)),
        name="mla_q_up",
    )(cqn, w_q, rope_c, rope_s1, rope_s2)


def _mla_kv_kernel(ckv_ref, w_ref, kr_ref, k_ref, v_ref):
    kv = _dot(ckv_ref[...], w_ref[...])
    kr = kr_ref[...]
    nk = MLA_HEADS * MLA_NOPE
    for h in range(MLA_HEADS):
        base = h * MLA_QK_PAD
        k_ref[:, base:base + MLA_NOPE] = kv[:, h * MLA_NOPE:(h + 1) * MLA_NOPE].astype(bf16)
        k_ref[:, base + MLA_NOPE:base + MLA_QK_PAD] = kr
    lane = lax.broadcasted_iota(jnp.int32, (kv.shape[0], MLA_V_PAD - MLA_V), 1)
    ones_col = jnp.where(lane == 0, 1.0, 0.0).astype(bf16)
    for h in range(MLA_HEADS):
        base = h * MLA_V_PAD
        v_ref[:, base:base + MLA_V] = kv[:, nk + h * MLA_V:nk + (h + 1) * MLA_V].astype(bf16)
        v_ref[:, base + MLA_V:base + MLA_V_PAD] = ones_col


def _mla_kv(ckvn, w_kv, krope, tm=512):
    m, r = ckvn.shape
    n = w_kv.shape[1]
    nk = MLA_HEADS * MLA_QK_PAD
    nv = MLA_HEADS * MLA_V_PAD
    return pl.pallas_call(
        _mla_kv_kernel,
        grid=(m // tm,),
        in_specs=[pl.BlockSpec((tm, r), lambda i: (i, 0)),
                  pl.BlockSpec((r, n), lambda i: (0, 0)),
                  pl.BlockSpec((tm, LANES), lambda i: (i, 0))],
        out_specs=[pl.BlockSpec((tm, nk), lambda i: (i, 0)),
                   pl.BlockSpec((tm, nv), lambda i: (i, 0))],
        out_shape=[jax.ShapeDtypeStruct((m, nk), bf16), jax.ShapeDtypeStruct((m, nv), bf16)],
        compiler_params=_cparams(("parallel",)),
        name="mla_kv_up",
    )(ckvn, w_kv, krope)


def _mla_attn_kernel(q_ref, k_ref, v_ref, o_ref, *, tk):
    q = q_ref[...]
    tq = q.shape[0]
    nkv = k_ref.shape[0] // tk

    def scores(c):
        return _dot_nt(q, k_ref[c * tk:(c + 1) * tk, :])

    def update(s, c, m, acc):
        m_new = jnp.maximum(m, jnp.max(s, axis=-1, keepdims=True))
        p = jnp.exp2(s - m_new)
        acc = jnp.exp2(m - m_new) * acc + _dot(p.astype(bf16), v_ref[c * tk:(c + 1) * tk, :])
        return m_new, acc

    m = jnp.full((tq, 1), -jnp.inf, f32)
    acc = jnp.zeros((tq, MLA_V_PAD), f32)
    s_next = scores(0)
    for c in range(nkv):
        s_cur = s_next
        if c + 1 < nkv:
            s_next = scores(c + 1)
        m, acc = update(s_cur, c, m, acc)
    o_ref[...] = (acc[:, :MLA_V] / acc[:, MLA_V:MLA_V + 1]).astype(o_ref.dtype)


def _mla_attention(q, k, v, tq=512, tk=512):
    s = q.shape[0]
    return pl.pallas_call(
        functools.partial(_mla_attn_kernel, tk=tk),
        grid=(MLA_HEADS, s // tq),
        in_specs=[pl.BlockSpec((tq, MLA_QK_PAD), lambda h, i: (i, h)),
                  pl.BlockSpec((s, MLA_QK_PAD), lambda h, i: (0, h)),
                  pl.BlockSpec((s, MLA_V_PAD), lambda h, i: (0, h))],
        out_specs=pl.BlockSpec((tq, MLA_V), lambda h, i: (i, h)),
        out_shape=jax.ShapeDtypeStruct((s, MLA_HEADS * MLA_V), bf16),
        compiler_params=_cparams(("parallel", "parallel")),
        name="mla_attention",
    )(q, k, v)


def _mem_attn_kernel(q_ref, k_ref, v_ref, o_ref):
    for h in range(MEM_HEADS):
        sl = slice(h * MEM_HD, (h + 1) * MEM_HD)
        s = _dot_nt(q_ref[:, sl], k_ref[:, sl])
        m = jnp.max(s, axis=-1, keepdims=True)
        p = jnp.exp(s - m)
        l = jnp.sum(p, axis=-1, keepdims=True)
        o = _dot(p.astype(bf16), v_ref[:, sl])
        o_ref[:, sl] = (o / l).astype(o_ref.dtype)


def _mem_attention(q, k, v, tm=512):
    s, w = q.shape
    nm = k.shape[0]
    return pl.pallas_call(
        _mem_attn_kernel,
        grid=(s // tm,),
        in_specs=[pl.BlockSpec((tm, w), lambda i: (i, 0)),
                  pl.BlockSpec((nm, w), lambda i: (0, 0)),
                  pl.BlockSpec((nm, w), lambda i: (0, 0))],
        out_specs=pl.BlockSpec((tm, w), lambda i: (i, 0)),
        out_shape=jax.ShapeDtypeStruct((s, w), bf16),
        compiler_params=_cparams(("parallel",)),
        name="mem_attention",
    )(q, k, v)


def _hgrn_levels(rev):
    levels = []
    sz = HG_CHUNK // 2
    while sz >= HG_SUB:
        blocks = []
        for base in range(0, HG_CHUNK, 2 * sz):
            if not rev:
                blocks.append((base + sz, base, sz, base + sz - 1))
            else:
                blocks.append((base, base + sz, sz, base + sz))
        levels.append(blocks)
        sz //= 2
    return levels


def _hgrn_intra(q_c, b_c, c_c, bbuf, cbuf, row0, *, rev):
    C = HG_CHUNK
    nb = C // HG_SUB
    lane = lax.broadcasted_iota(jnp.int32, (HG_SUB, C), 1)
    sub = lax.broadcasted_iota(jnp.int32, (HG_SUB, C), 0)
    blocks = []
    for i in range(nb):
        r0 = i * HG_SUB
        q_i = q_c[r0:r0 + HG_SUB]
        b_i = b_c[r0:r0 + HG_SUB]
        a_blk = jnp.zeros((HG_SUB, C), f32)
        for s in range(HG_SUB):
            w = q_i * jnp.exp2(b_i + cbuf[row0 + r0 + s:row0 + r0 + s + 1, :])
            a_blk = jnp.where(lane == r0 + s, jnp.sum(w, axis=-1, keepdims=True), a_blk)
        keep = (sub >= lane - r0) if not rev else (sub <= lane - r0)
        blocks.append(jnp.where(keep, a_blk, 0.0))
    for level in _hgrn_levels(rev):
        sz = level[0][2]
        qd, kd, valid = [], [], None
        k_pos = 0
        rq = lax.broadcasted_iota(jnp.int32, (len(level) * sz, C), 0)
        ck = lax.broadcasted_iota(jnp.int32, (len(level) * sz, C), 1)
        for j, (q0, k0, _, piv_row) in enumerate(level):
            piv = bbuf[row0 + piv_row:row0 + piv_row + 1, :]
            qd.append(q_c[q0:q0 + sz] * jnp.exp2(b_c[q0:q0 + sz] - piv))
            if k0 > k_pos:
                kd.append(jnp.zeros((k0 - k_pos, q_c.shape[1]), f32))
            kd.append(jnp.exp2(c_c[k0:k0 + sz] + piv))
            k_pos = k0 + sz
            ok = (rq >= j * sz) & (rq < (j + 1) * sz) & (ck >= k0) & (ck < k0 + sz)
            valid = ok if valid is None else (valid | ok)
        if k_pos < C:
            kd.append(jnp.zeros((C - k_pos, q_c.shape[1]), f32))
        r = _dot_nt(jnp.concatenate(qd, axis=0).astype(bf16), jnp.concatenate(kd, axis=0).astype(bf16))
        if len(level) > 1:
            r = jnp.where(valid, r, 0.0)
        for j, (q0, _, _, _) in enumerate(level):
            for t in range(sz // HG_SUB):
                i = q0 // HG_SUB + t
                blocks[i] = blocks[i] + r[j * sz + t * HG_SUB:j * sz + (t + 1) * HG_SUB]
    return jnp.concatenate(blocks, axis=0)


def _hgrn_block_kernel(qf_ref, vf_ref, ff_ref, qb_ref, vb_ref, fb_ref, lb_ref, of_ref, ob_ref,
                       sf_ref, sb_ref, bbuf, cbuf):
    n = pl.program_id(1)
    C = HG_CHUNK
    T = qf_ref.shape[0]
    nchunk = T // C

    @pl.when(n == 0)
    def _():
        sf_ref[...] = jnp.zeros_like(sf_ref)
        sb_ref[...] = jnp.zeros_like(sb_ref)

    r = lax.broadcasted_iota(jnp.int32, (C, C), 0)
    c = lax.broadcasted_iota(jnp.int32, (C, C), 1)
    dirs = []
    for d, (q_ref, v_ref, f_ref, rev) in enumerate(((qf_ref, vf_ref, ff_ref, False), (qb_ref, vb_ref, fb_ref, True))):
        tri = ((c <= r) if not rev else (c >= r)).astype(bf16)
        lb = lb_ref[d:d + 1, :]
        q = jax.nn.silu(q_ref[...])
        v16 = v_ref[...].astype(bf16)
        f = lb + (1.0 - lb) * jax.nn.sigmoid(f_ref[...])
        g = jnp.log(f) * LOG2E
        g_hi = g.astype(bf16)
        g_lo = (g - g_hi.astype(f32)).astype(bf16)
        b = jnp.concatenate([_dot(tri, g_hi[j * C:(j + 1) * C]) + _dot(tri, g_lo[j * C:(j + 1) * C])
                             for j in range(nchunk)], axis=0)
        cc = jnp.log(1.0 - f) * LOG2E - b
        bbuf[d] = b
        cbuf[d] = cc
        dirs.append((q, v16, b, cc, rev))

    intra = []
    for d, (q, v16, b, cc, rev) in enumerate(dirs):
        intra.append([_hgrn_intra(q[j * C:(j + 1) * C], b[j * C:(j + 1) * C], cc[j * C:(j + 1) * C],
                                  bbuf.at[d], cbuf.at[d], j * C, rev=rev).astype(bf16) for j in range(nchunk)])

    for d, (q, v16, b, cc, rev) in enumerate(dirs):
        s_ref, o_ref = (sf_ref, of_ref) if not rev else (sb_ref, ob_ref)
        qe = (q * jnp.exp2(b)).astype(bf16)
        order = range(nchunk) if not rev else range(nchunk - 1, -1, -1)
        edge = (C - 1) if not rev else 0
        upd = {}
        for j in order:
            b_edge = bbuf[d, j * C + edge:j * C + edge + 1, :]
            kdec = jnp.exp2(cc[j * C:(j + 1) * C] + b_edge).astype(bf16)
            upd[j] = (jnp.exp2(b_edge), _dot_tn(v16[j * C:(j + 1) * C], kdec))
        state = s_ref[...]
        for j in order:
            rows = slice(j * C, (j + 1) * C)
            o_ref[rows, :] = _dot_nt(qe[rows], state.astype(bf16)) + _dot(intra[d][j], v16[rows])
            state = upd[j][0] * state + upd[j][1]
        s_ref[...] = state


def _hgrn_scan(proj, lb, tb=512):
    s = proj.shape[0]
    w = HG_HEADS * HG_D
    nb = s // tb
    hb = w // HG_D
    def fw(seg):
        return pl.BlockSpec((tb, HG_D), lambda h, n: (n, seg * hb + h))

    def bw(seg):
        return pl.BlockSpec((tb, HG_D), lambda h, n: (nb - 1 - n, seg * hb + h))

    return pl.pallas_call(
        _hgrn_block_kernel,
        grid=(HG_HEADS, nb),
        in_specs=[fw(0), fw(1), fw(2), bw(0), bw(1), bw(3),
                  pl.BlockSpec((2, HG_D), lambda h, n: (0, h))],
        out_specs=[pl.BlockSpec((tb, HG_D), lambda h, n: (n, h)),
                   pl.BlockSpec((tb, HG_D), lambda h, n: (nb - 1 - n, h))],
        out_shape=[jax.ShapeDtypeStruct((s, w), f32), jax.ShapeDtypeStruct((s, w), f32)],
        scratch_shapes=[pltpu.VMEM((HG_D, HG_D), f32), pltpu.VMEM((HG_D, HG_D), f32),
                        pltpu.VMEM((2, tb, HG_D), f32), pltpu.VMEM((2, tb, HG_D), f32)],
        compiler_params=_cparams(("parallel", "arbitrary")),
        name="hgrn_scan",
    )(proj, proj, proj, proj, proj, proj, lb)


def _hgrn_out_kernel(of_ref, ob_ref, g_ref, ng_ref, y_ref):
    ng = ng_ref[...]
    for h in range(HG_HEADS):
        sl = slice(h * HG_D, (h + 1) * HG_D)
        o = of_ref[:, sl] + ob_ref[:, sl]
        ms = jnp.mean(o * o, axis=-1, keepdims=True)
        o = o * lax.rsqrt(ms + RMS_EPS) * ng
        y_ref[:, sl] = (o * jax.nn.sigmoid(g_ref[:, sl])).astype(y_ref.dtype)


def _hgrn_out(o_f, o_b, proj, norm_g, tm=512):
    s, w = o_f.shape
    gate_blk = 4
    return pl.pallas_call(
        _hgrn_out_kernel,
        grid=(s // tm,),
        in_specs=[pl.BlockSpec((tm, w), lambda i: (i, 0)),
                  pl.BlockSpec((tm, w), lambda i: (i, 0)),
                  pl.BlockSpec((tm, w), lambda i: (i, gate_blk)),
                  pl.BlockSpec((1, HG_D), lambda i: (0, 0))],
        out_specs=pl.BlockSpec((tm, w), lambda i: (i, 0)),
        out_shape=jax.ShapeDtypeStruct((s, w), bf16),
        compiler_params=_cparams(("parallel",)),
        name="hgrn_out",
    )(o_f, o_b, proj, norm_g.reshape(1, HG_D))


def _lb_kernel(logit_ref, lb_ref, *, layer):
    x = logit_ref[...]
    e = jnp.exp(x - jnp.max(x, axis=1, keepdims=True))
    p = e / jnp.sum(e, axis=1, keepdims=True)
    acc = p[:, 0, :]
    for l in range(1, layer + 1):
        acc = acc + p[:, l, :]
    lb_ref[...] = acc


def _lower_bounds(logits, layer):
    z, _, w = logits.shape
    return pl.pallas_call(
        functools.partial(_lb_kernel, layer=layer),
        out_shape=jax.ShapeDtypeStruct((z, w), f32),
        name="hgrn_lower_bounds",
    )(logits)


def _merge_kernel(y0_ref, y1_ref, y2_ref, w_ref, g0_ref, g1_ref, g2_ref, o_ref):
    acc = g0_ref[...].astype(f32) * _dot(y0_ref[...], w_ref[0])
    acc = acc + g1_ref[...].astype(f32) * _dot(y1_ref[...], w_ref[1])
    acc = acc + g2_ref[...].astype(f32) * _dot(y2_ref[...], w_ref[2])
    o_ref[...] = acc.astype(o_ref.dtype)


def _merge(y0, y1, y2, w_branch, gsig, tm=1024, tn=512):
    m, kb = y0.shape
    d = w_branch.shape[2]
    nj = d // tn
    yspec = pl.BlockSpec((tm, kb), lambda i, j: (i, 0))
    def gspec(b):
        return pl.BlockSpec((tm, tn), lambda i, j: (i, b * nj + j))

    return pl.pallas_call(
        _merge_kernel,
        grid=(m // tm, nj),
        in_specs=[yspec, yspec, yspec,
                  pl.BlockSpec((N_BRANCH, kb, tn), lambda i, j: (0, 0, j)),
                  gspec(0), gspec(1), gspec(2)],
        out_specs=pl.BlockSpec((tm, tn), lambda i, j: (i, j)),
        out_shape=jax.ShapeDtypeStruct((m, d), bf16),
        compiler_params=_cparams(("parallel", "parallel")),
        name="branch_merge",
    )(y0, y1, y2, w_branch, gsig, gsig, gsig)


def _proj_ln_kernel(a_ref, w_ref, res_ref, g_ref, b_ref, o32_ref, o16_ref):
    x = ALPHA * res_ref[...] + _dot(a_ref[...], w_ref[...])
    y = _ln_rows(x, g_ref[...], b_ref[...])
    o32_ref[...] = y
    o16_ref[...] = y.astype(bf16)


def _proj_residual_ln(a, w, res, g, b, tm=512):
    m, k = a.shape
    d = w.shape[1]
    return pl.pallas_call(
        _proj_ln_kernel,
        grid=(m // tm,),
        in_specs=[pl.BlockSpec((tm, k), lambda i: (i, 0)),
                  pl.BlockSpec((k, d), lambda i: (0, 0)),
                  pl.BlockSpec((tm, d), lambda i: (i, 0)),
                  pl.BlockSpec((1, d), lambda i: (0, 0)),
                  pl.BlockSpec((1, d), lambda i: (0, 0))],
        out_specs=[pl.BlockSpec((tm, d), lambda i: (i, 0)),
                   pl.BlockSpec((tm, d), lambda i: (i, 0))],
        out_shape=[jax.ShapeDtypeStruct((m, d), f32), jax.ShapeDtypeStruct((m, d), bf16)],
        compiler_params=_cparams(("parallel",)),
        name="out_proj_ln",
    )(a, w, res, g.reshape(1, d), b.reshape(1, d))


def _ffn_kernel(x_ref, wg_ref, wu_ref, wd_ref, res_ref, g_ref, b_ref, o_ref, acc_ref):
    j = pl.program_id(1)

    @pl.when(j == 0)
    def _():
        acc_ref[...] = jnp.zeros_like(acc_ref)

    x = x_ref[...]
    hid = jax.nn.silu(_dot(x, wg_ref[...])) * _dot(x, wu_ref[...])
    acc_ref[...] += _dot(hid.astype(bf16), wd_ref[...])

    @pl.when(j == pl.num_programs(1) - 1)
    def _():
        o_ref[...] = _ln_rows(ALPHA * res_ref[...] + acc_ref[...], g_ref[...], b_ref[...])


def _ffn_residual_ln(x16, x32, wg, wu, wd, g, b, tm=512, tf=512):
    m, d = x16.shape
    ff = wg.shape[1]
    return pl.pallas_call(
        _ffn_kernel,
        grid=(m // tm, ff // tf),
        in_specs=[pl.BlockSpec((tm, d), lambda i, j: (i, 0)),
                  pl.BlockSpec((d, tf), lambda i, j: (0, j)),
                  pl.BlockSpec((d, tf), lambda i, j: (0, j)),
                  pl.BlockSpec((tf, d), lambda i, j: (j, 0)),
                  pl.BlockSpec((tm, d), lambda i, j: (i, 0)),
                  pl.BlockSpec((1, d), lambda i, j: (0, 0)),
                  pl.BlockSpec((1, d), lambda i, j: (0, 0))],
        out_specs=pl.BlockSpec((tm, d), lambda i, j: (i, 0)),
        out_shape=jax.ShapeDtypeStruct((m, d), f32),
        scratch_shapes=[pltpu.VMEM((tm, d), f32)],
        compiler_params=_cparams(("parallel", "arbitrary")),
        name="ffn_ln",
    )(x16, wg, wu, wd, x32, g.reshape(1, d), b.reshape(1, d))


REALIGN_TN = 512


def _realign_kernel(w_ref, o_ref, prev_ref, *, shift):
    j = pl.program_id(0)
    tn = w_ref.shape[1]
    rot = pltpu.roll(w_ref[...], tn - shift, 1)
    lane = lax.broadcasted_iota(jnp.int32, rot.shape, 1)

    @pl.when(j > 0)
    def _():
        o_ref[...] = jnp.where(lane < tn - shift, prev_ref[...], rot).astype(o_ref.dtype)

    prev_ref[...] = rot


def _realign_columns(w, start):
    k, total = w.shape
    tn = REALIGN_TN
    base = start // tn
    shift = start - base * tn
    n_out = (total - start) // tn
    assert (total - start) % tn == 0 and 0 < shift < tn
    return pl.pallas_call(
        functools.partial(_realign_kernel, shift=shift),
        grid=(n_out + 1,),
        in_specs=[pl.BlockSpec((k, tn), lambda j: (0, base + j))],
        out_specs=pl.BlockSpec((k, tn), lambda j: (0, jnp.maximum(j - 1, 0))),
        out_shape=jax.ShapeDtypeStruct((k, n_out * tn), bf16),
        scratch_shapes=[pltpu.VMEM((k, tn), f32)],
        compiler_params=_cparams(("arbitrary",)),
        name="realign_w_in_tail",
    )(w)


def _rope_tables(positions):
    half = MLA_ROPE // 2
    inv_freq = jnp.power(ROPE_THETA, -jnp.arange(half, dtype=f32) / half)
    ang = positions.astype(f32)[..., None] * inv_freq
    cos, sin = jnp.cos(ang), jnp.sin(ang)
    z = jnp.zeros_like(cos)
    pad = jnp.zeros((ang.shape[0], LANES - MLA_ROPE), f32)
    c = jnp.concatenate([cos, cos, pad], axis=-1)
    s1 = jnp.concatenate([-sin, z, pad], axis=-1)
    s2 = jnp.concatenate([z, sin, pad], axis=-1)
    return c, s1, s2


def _layer(h32, h16, mem16, rope, lb, w_in, hgrn_norm_g, g_cq, g_ckv, w_uq, w_ukv, w_memkv, w_branch, w_o,
           ln1_g, ln1_b, w_gate, w_up, w_down, ln2_g, ln2_b):
    s, d = h32.shape
    hgw = HG_HEADS * HG_D
    q_rank = g_cq.shape[0]
    kv_rank = g_ckv.shape[0]
    memw = MEM_HEADS * MEM_HD
    o_cq = 5 * hgw
    o_ckv = o_cq + q_rank
    o_kr = o_ckv + kv_rank
    o_qm = o_kr + MLA_ROPE
    o_gt = o_qm + memw
    rope_c, rope_s1, rope_s2 = rope

    assert o_cq % 512 == 0 and o_ckv % q_rank == 0 and o_kr % kv_rank == 0 and o_kr % LANES == 0
    w_tail = _realign_columns(w_in, o_qm)

    proj_hg = _matmul(h16, w_in, tm=2048, tn=512, n=o_cq, out_dtype=f32, name="in_proj_hgrn")
    cqn = _matmul(h16, w_in, tm=1024, tn=q_rank, col0=o_cq // q_rank, n=q_rank, out_dtype=bf16,
                  epilogue=_rms_epilogue, col_extras=(g_cq.reshape(1, -1),), name="in_proj_cq")
    ckvn = _matmul(h16, w_in, tm=1024, tn=kv_rank, col0=o_ckv // kv_rank, n=kv_rank, out_dtype=bf16,
                   epilogue=_rms_epilogue, col_extras=(g_ckv.reshape(1, -1),), name="in_proj_ckv")
    krope = _matmul(h16, w_in, tm=1024, tn=LANES, col0=o_kr // LANES, n=LANES, out_dtype=bf16,
                    epilogue=_krope_epilogue, row_extras=(rope_c, rope_s1, rope_s2), name="in_proj_krope")
    q_mem = _matmul(h16, w_tail, tm=1024, tn=512, n=memw, out_dtype=bf16,
                    epilogue=lambda acc: acc * (MEM_HD ** -0.5), name="in_proj_qmem")
    gsig = _matmul(h16, w_tail, tm=1024, tn=512, col0=memw // 512, n=N_BRANCH * d, out_dtype=bf16,
                   epilogue=jax.nn.sigmoid, name="in_proj_gates")

    o_f, o_b = _hgrn_scan(proj_hg, lb)
    y_hg = _hgrn_out(o_f, o_b, proj_hg, hgrn_norm_g)

    zq = jnp.zeros((q_rank, MLA_HEADS, MLA_QK_PAD - MLA_QK), f32)
    w_q = jnp.concatenate([w_uq.reshape(q_rank, MLA_HEADS, MLA_QK), zq], axis=-1)
    w_q = w_q.reshape(q_rank, MLA_HEADS * MLA_QK_PAD).astype(bf16)
    w_kv3 = w_ukv.reshape(kv_rank, MLA_HEADS, MLA_NOPE + MLA_V)
    w_kv = jnp.concatenate([w_kv3[:, :, :MLA_NOPE].reshape(kv_rank, -1),
                            w_kv3[:, :, MLA_NOPE:].reshape(kv_rank, -1)], axis=-1).astype(bf16)
    q = _mla_q(cqn, w_q, rope_c, rope_s1, rope_s2)
    k, v = _mla_kv(ckvn, w_kv, krope)
    y_mla = _mla_attention(q, k, v)

    kv_mem = _matmul(mem16, w_memkv.astype(bf16), tm=mem16.shape[0], tn=512, out_dtype=bf16, name="mem_kv_proj")
    y_mem = _mem_attention(q_mem, kv_mem[:, :memw], kv_mem[:, memw:])

    merged = _merge(y_hg, y_mla, y_mem, w_branch.astype(bf16), gsig)
    x32, x16 = _proj_residual_ln(merged, w_o.astype(bf16), h32, ln1_g, ln1_b)
    out32 = _ffn_residual_ln(x16, x32, w_gate.astype(bf16), w_up.astype(bf16), w_down.astype(bf16), ln2_g, ln2_b)
    return out32


def kernel(x, mem, positions, ln_emb_g, ln_emb_b, hgrn_lb_logits, w_in, hgrn_norm_g, mla_g_cq, mla_g_ckv,
           mla_w_uq, mla_w_ukv, mem_w_kv, w_branch, w_o, ln1_g, ln1_b, w_ffn_gate, w_ffn_up, w_ffn_down,
           ln2_g, ln2_b):
    bsz, s, d = x.shape
    depth = w_in.shape[0]
    outs = []
    for bi in range(bsz):
        rope = _rope_tables(positions[bi])
        mem16 = mem[bi].astype(bf16)
        h32, h16 = _layernorm(x[bi], ln_emb_g, ln_emb_b)
        for l in range(depth):
            lb = _lower_bounds(hgrn_lb_logits, l)
            h32 = _layer(h32, h16, mem16, rope, lb, w_in[l], hgrn_norm_g[l], mla_g_cq[l], mla_g_ckv[l],
                         mla_w_uq[l], mla_w_ukv[l], mem_w_kv[l], w_branch[l], w_o[l], ln1_g[l], ln1_b[l],
                         w_ffn_gate[l], w_ffn_up[l], w_ffn_down[l], ln2_g[l], ln2_b[l])
            if l + 1 < depth:
                h16 = h32.astype(bf16)
        outs.append(h32)
    return jnp.stack(outs)
```

```python
import functools

import jax
import jax.numpy as jnp
from jax import lax
from jax.experimental import pallas as pl
from jax.experimental.pallas import tpu as pltpu

HG_HEADS = 8
HG_D = 128
MLA_HEADS = 8
MLA_NOPE = 128
MLA_ROPE = 64
MLA_V = 128
MLA_QK = MLA_NOPE + MLA_ROPE
MLA_QK_PAD = 256
MLA_V_PAD = 256
LOG2E = 1.4426950408889634
MEM_HEADS = 4
MEM_HD = 256
N_BRANCH = 3
ROPE_THETA = 10000.0
LN_EPS = 1e-5
RMS_EPS = 1e-6
DEPTH = 1
ALPHA = (2.0 * DEPTH) ** 0.25

LANES = 128
SUBLANES = 8
VMEM_LIMIT = 56 * 1024 * 1024

HG_CHUNK = 64
HG_SUB = SUBLANES

bf16 = jnp.bfloat16
f32 = jnp.float32


def _cparams(sem):
    return pltpu.CompilerParams(dimension_semantics=sem, vmem_limit_bytes=VMEM_LIMIT)


def _dot(a, b):
    return jnp.dot(a, b, preferred_element_type=f32)


def _dot_nt(a, b):
    return lax.dot_general(a, b, (((1,), (1,)), ((), ())), preferred_element_type=f32)


def _dot_tn(a, b):
    return lax.dot_general(a, b, (((0,), (0,)), ((), ())), preferred_element_type=f32)


def _ln_rows(x, g, b):
    mu = jnp.mean(x, axis=-1, keepdims=True)
    xc = x - mu
    var = jnp.mean(xc * xc, axis=-1, keepdims=True)
    return xc * lax.rsqrt(var + LN_EPS) * g + b


def _ln_kernel(x_ref, g_ref, b_ref, o32_ref, o16_ref):
    y = _ln_rows(x_ref[...], g_ref[...], b_ref[...])
    o32_ref[...] = y
    o16_ref[...] = y.astype(bf16)


def _layernorm(x, g, b, tm=512):
    m, d = x.shape
    return pl.pallas_call(
        _ln_kernel,
        grid=(m // tm,),
        in_specs=[pl.BlockSpec((tm, d), lambda i: (i, 0)),
                  pl.BlockSpec((1, d), lambda i: (0, 0)),
                  pl.BlockSpec((1, d), lambda i: (0, 0))],
        out_specs=[pl.BlockSpec((tm, d), lambda i: (i, 0)),
                   pl.BlockSpec((tm, d), lambda i: (i, 0))],
        out_shape=[jax.ShapeDtypeStruct((m, d), f32), jax.ShapeDtypeStruct((m, d), bf16)],
        compiler_params=_cparams(("parallel",)),
        name="ln_embed",
    )(x, g.reshape(1, d), b.reshape(1, d))


def _mm_kernel(a_ref, w_ref, *rest, epilogue, w_transposed):
    o_ref = rest[-1]
    w = w_ref[...].astype(bf16)
    acc = _dot_nt(a_ref[...], w) if w_transposed else _dot(a_ref[...], w)
    o_ref[...] = epilogue(acc, *[r[...] for r in rest[:-1]]).astype(o_ref.dtype)


def _matmul(a, w, *, tm, tn, out_dtype, epilogue=None, row_extras=(), col_extras=(), name, wt_rows=None):
    m, k = a.shape
    if epilogue is None:
        epilogue = lambda acc: acc
    if wt_rows is None:
        n = w.shape[1]
        w_spec = pl.BlockSpec((k, tn), lambda i, j: (0, j))
    else:
        row0, n = wt_rows
        assert row0 % SUBLANES == 0 and w.shape[1] == k
        w_spec = pl.BlockSpec((pl.Element(tn), pl.Element(k)),
                              lambda i, j: ((row0 // SUBLANES + j * (tn // SUBLANES)) * SUBLANES, 0))
    in_specs = [pl.BlockSpec((tm, k), lambda i, j: (i, 0)), w_spec]
    for e in row_extras:
        in_specs.append(pl.BlockSpec((tm, e.shape[1]), lambda i, j: (i, 0)))
    for e in col_extras:
        in_specs.append(pl.BlockSpec((e.shape[0], tn), lambda i, j: (0, j)))
    return pl.pallas_call(
        functools.partial(_mm_kernel, epilogue=epilogue, w_transposed=wt_rows is not None),
        grid=(m // tm, n // tn),
        in_specs=in_specs,
        out_specs=pl.BlockSpec((tm, tn), lambda i, j: (i, j)),
        out_shape=jax.ShapeDtypeStruct((m, n), out_dtype),
        compiler_params=_cparams(("parallel", "parallel")),
        name=name,
    )(a, w, *row_extras, *col_extras)


def _rms_epilogue(acc, g):
    ms = jnp.mean(acc * acc, axis=-1, keepdims=True)
    return acc * lax.rsqrt(ms + RMS_EPS) * g


def _rope_block(blk, c, s1, s2):
    half = MLA_ROPE // 2
    return blk * c + pltpu.roll(blk, LANES - half, 1) * s1 + pltpu.roll(blk, half, 1) * s2


def _krope_epilogue(acc, c, s1, s2):
    return _rope_block(acc, c, s1, s2)


def _mla_q_kernel(cq_ref, w_ref, c_ref, s1_ref, s2_ref, o_ref):
    q = _dot(cq_ref[...], w_ref[...]) * (MLA_QK ** -0.5 * LOG2E)
    c, s1, s2 = c_ref[...], s1_ref[...], s2_ref[...]
    for h in range(MLA_HEADS):
        base = h * MLA_QK_PAD
        o_ref[:, base:base + MLA_NOPE] = q[:, base:base + MLA_NOPE].astype(bf16)
        o_ref[:, base + MLA_NOPE:base + MLA_QK_PAD] = _rope_block(
            q[:, base + MLA_NOPE:base + MLA_QK_PAD], c, s1, s2).astype(bf16)


def _mla_q(cqn, w_q, rope_c, rope_s1, rope_s2, tm=512):
    m, r = cqn.shape
    n = w_q.shape[1]
    return pl.pallas_call(
        _mla_q_kernel,
        grid=(m // tm,),
        in_specs=[pl.BlockSpec((tm, r), lambda i: (i, 0)),
                  pl.BlockSpec((r, n), lambda i: (0, 0)),
                  pl.BlockSpec((tm, LANES), lambda i: (i, 0)),
                  pl.BlockSpec((tm, LANES), lambda i: (i, 0)),
                  pl.BlockSpec((tm, LANES), lambda i: (i, 0))],
        out_specs=pl.BlockSpec((tm, n), lambda i: (i, 0)),
        out_shape=jax.ShapeDtypeStruct((m, n), bf16),
        compiler_params=_cparams(("parallel",)),
        name="mla_q_up",
    )(cqn, w_q, rope_c, rope_s1, rope_s2)


def _mla_kv_kernel(ckv_ref, w_ref, kr_ref, k_ref, v_ref):
    kv = _dot(ckv_ref[...], w_ref[...])
    kr = kr_ref[...]
    nk = MLA_HEADS * MLA_NOPE
    for h in range(MLA_HEADS):
        base = h * MLA_QK_PAD
        k_ref[:, base:base + MLA_NOPE] = kv[:, h * MLA_NOPE:(h + 1) * MLA_NOPE].astype(bf16)
        k_ref[:, base + MLA_NOPE:base + MLA_QK_PAD] = kr
    lane = lax.broadcasted_iota(jnp.int32, (kv.shape[0], MLA_V_PAD - MLA_V), 1)
    ones_col = jnp.where(lane == 0, 1.0, 0.0).astype(bf16)
    for h in range(MLA_HEADS):
        base = h * MLA_V_PAD
        v_ref[:, base:base + MLA_V] = kv[:, nk + h * MLA_V:nk + (h + 1) * MLA_V].astype(bf16)
        v_ref[:, base + MLA_V:base + MLA_V_PAD] = ones_col


def _mla_kv(ckvn, w_kv, krope, tm=512):
    m, r = ckvn.shape
    n = w_kv.shape[1]
    nk = MLA_HEADS * MLA_QK_PAD
    nv = MLA_HEADS * MLA_V_PAD
    return pl.pallas_call(
        _mla_kv_kernel,
        grid=(m // tm,),
        in_specs=[pl.BlockSpec((tm, r), lambda i: (i, 0)),
                  pl.BlockSpec((r, n), lambda i: (0, 0)),
                  pl.BlockSpec((tm, LANES), lambda i: (i, 0))],
        out_specs=[pl.BlockSpec((tm, nk), lambda i: (i, 0)),
                   pl.BlockSpec((tm, nv), lambda i: (i, 0))],
        out_shape=[jax.ShapeDtypeStruct((m, nk), bf16), jax.ShapeDtypeStruct((m, nv), bf16)],
        compiler_params=_cparams(("parallel",)),
        name="mla_kv_up",
    )(ckvn, w_kv, krope)


def _mla_attn_kernel(q_ref, k_ref, v_ref, o_ref, *, tk):
    q = q_ref[...]
    tq = q.shape[0]
    nkv = k_ref.shape[0] // tk

    def scores(c):
        return _dot_nt(q, k_ref[c * tk:(c + 1) * tk, :])

    def update(s, c, m, acc):
        m_new = jnp.maximum(m, jnp.max(s, axis=-1, keepdims=True))
        p = jnp.exp2(s - m_new)
        acc = jnp.exp2(m - m_new) * acc + _dot(p.astype(bf16), v_ref[c * tk:(c + 1) * tk, :])
        return m_new, acc

    m = jnp.full((tq, 1), -jnp.inf, f32)
    acc = jnp.zeros((tq, MLA_V_PAD), f32)
    s_next = scores(0)
    for c in range(nkv):
        s_cur = s_next
        if c + 1 < nkv:
            s_next = scores(c + 1)
        m, acc = update(s_cur, c, m, acc)
    o_ref[...] = (acc[:, :MLA_V] / acc[:, MLA_V:MLA_V + 1]).astype(o_ref.dtype)


def _mla_attention(q, k, v, tq=512, tk=512):
    s = q.shape[0]
    return pl.pallas_call(
        functools.partial(_mla_attn_kernel, tk=tk),
        grid=(MLA_HEADS, s // tq),
        in_specs=[pl.BlockSpec((tq, MLA_QK_PAD), lambda h, i: (i, h)),
                  pl.BlockSpec((s, MLA_QK_PAD), lambda h, i: (0, h)),
                  pl.BlockSpec((s, MLA_V_PAD), lambda h, i: (0, h))],
        out_specs=pl.BlockSpec((tq, MLA_V), lambda h, i: (i, h)),
        out_shape=jax.ShapeDtypeStruct((s, MLA_HEADS * MLA_V), bf16),
        compiler_params=_cparams(("parallel", "parallel")),
        name="mla_attention",
    )(q, k, v)


def _mem_attn_kernel(q_ref, k_ref, v_ref, o_ref):
    for h in range(MEM_HEADS):
        sl = slice(h * MEM_HD, (h + 1) * MEM_HD)
        s = _dot_nt(q_ref[:, sl], k_ref[:, sl])
        m = jnp.max(s, axis=-1, keepdims=True)
        p = jnp.exp(s - m)
        l = jnp.sum(p, axis=-1, keepdims=True)
        o = _dot(p.astype(bf16), v_ref[:, sl])
        o_ref[:, sl] = (o / l).astype(o_ref.dtype)


def _mem_attention(q, k, v, tm=512):
    s, w = q.shape
    nm = k.shape[0]
    return pl.pallas_call(
        _mem_attn_kernel,
        grid=(s // tm,),
        in_specs=[pl.BlockSpec((tm, w), lambda i: (i, 0)),
                  pl.BlockSpec((nm, w), lambda i: (0, 0)),
                  pl.BlockSpec((nm, w), lambda i: (0, 0))],
        out_specs=pl.BlockSpec((tm, w), lambda i: (i, 0)),
        out_shape=jax.ShapeDtypeStruct((s, w), bf16),
        compiler_params=_cparams(("parallel",)),
        name="mem_attention",
    )(q, k, v)


def _hgrn_levels(rev):
    levels = []
    sz = HG_CHUNK // 2
    while sz >= HG_SUB:
        blocks = []
        for base in range(0, HG_CHUNK, 2 * sz):
            if not rev:
                blocks.append((base + sz, base, sz, base + sz - 1))
            else:
                blocks.append((base, base + sz, sz, base + sz))
        levels.append(blocks)
        sz //= 2
    return levels


def _hgrn_intra(q_c, b_c, c_c, bbuf, cbuf, row0, *, rev):
    C = HG_CHUNK
    nb = C // HG_SUB
    lane = lax.broadcasted_iota(jnp.int32, (HG_SUB, C), 1)
    sub = lax.broadcasted_iota(jnp.int32, (HG_SUB, C), 0)
    blocks = []
    for i in range(nb):
        r0 = i * HG_SUB
        q_i = q_c[r0:r0 + HG_SUB]
        b_i = b_c[r0:r0 + HG_SUB]
        a_blk = jnp.zeros((HG_SUB, C), f32)
        for s in range(HG_SUB):
            w = q_i * jnp.exp2(b_i + cbuf[row0 + r0 + s:row0 + r0 + s + 1, :])
            a_blk = jnp.where(lane == r0 + s, jnp.sum(w, axis=-1, keepdims=True), a_blk)
        keep = (sub >= lane - r0) if not rev else (sub <= lane - r0)
        blocks.append(jnp.where(keep, a_blk, 0.0))
    for level in _hgrn_levels(rev):
        sz = level[0][2]
        qd, kd, valid = [], [], None
        k_pos = 0
        rq = lax.broadcasted_iota(jnp.int32, (len(level) * sz, C), 0)
        ck = lax.broadcasted_iota(jnp.int32, (len(level) * sz, C), 1)
        for j, (q0, k0, _, piv_row) in enumerate(level):
            piv = bbuf[row0 + piv_row:row0 + piv_row + 1, :]
            qd.append(q_c[q0:q0 + sz] * jnp.exp2(b_c[q0:q0 + sz] - piv))
            if k0 > k_pos:
                kd.append(jnp.zeros((k0 - k_pos, q_c.shape[1]), f32))
            kd.append(jnp.exp2(c_c[k0:k0 + sz] + piv))
            k_pos = k0 + sz
            ok = (rq >= j * sz) & (rq < (j + 1) * sz) & (ck >= k0) & (ck < k0 + sz)
            valid = ok if valid is None else (valid | ok)
        if k_pos < C:
            kd.append(jnp.zeros((C - k_pos, q_c.shape[1]), f32))
        r = _dot_nt(jnp.concatenate(qd, axis=0).astype(bf16), jnp.concatenate(kd, axis=0).astype(bf16))
        if len(level) > 1:
            r = jnp.where(valid, r, 0.0)
        for j, (q0, _, _, _) in enumerate(level):
            for t in range(sz // HG_SUB):
                i = q0 // HG_SUB + t
                blocks[i] = blocks[i] + r[j * sz + t * HG_SUB:j * sz + (t + 1) * HG_SUB]
    return jnp.concatenate(blocks, axis=0)


def _hgrn_block_kernel(qf_ref, vf_ref, ff_ref, qb_ref, vb_ref, fb_ref, lb_ref, of_ref, ob_ref,
                       sf_ref, sb_ref, bbuf, cbuf):
    n = pl.program_id(1)
    C = HG_CHUNK
    T = qf_ref.shape[0]
    nchunk = T // C

    @pl.when(n == 0)
    def _():
        sf_ref[...] = jnp.zeros_like(sf_ref)
        sb_ref[...] = jnp.zeros_like(sb_ref)

    r = lax.broadcasted_iota(jnp.int32, (C, C), 0)
    c = lax.broadcasted_iota(jnp.int32, (C, C), 1)
    dirs = []
    for d, (q_ref, v_ref, f_ref, rev) in enumerate(((qf_ref, vf_ref, ff_ref, False), (qb_ref, vb_ref, fb_ref, True))):
        tri = ((c <= r) if not rev else (c >= r)).astype(bf16)
        lb = lb_ref[d:d + 1, :]
        q = jax.nn.silu(q_ref[...])
        v16 = v_ref[...].astype(bf16)
        f = lb + (1.0 - lb) * jax.nn.sigmoid(f_ref[...])
        g = jnp.log(f) * LOG2E
        g_hi = g.astype(bf16)
        g_lo = (g - g_hi.astype(f32)).astype(bf16)
        b = jnp.concatenate([_dot(tri, g_hi[j * C:(j + 1) * C]) + _dot(tri, g_lo[j * C:(j + 1) * C])
                             for j in range(nchunk)], axis=0)
        cc = jnp.log(1.0 - f) * LOG2E - b
        bbuf[d] = b
        cbuf[d] = cc
        dirs.append((q, v16, b, cc, rev))

    intra = []
    for d, (q, v16, b, cc, rev) in enumerate(dirs):
        intra.append([_hgrn_intra(q[j * C:(j + 1) * C], b[j * C:(j + 1) * C], cc[j * C:(j + 1) * C],
                                  bbuf.at[d], cbuf.at[d], j * C, rev=rev).astype(bf16) for j in range(nchunk)])

    for d, (q, v16, b, cc, rev) in enumerate(dirs):
        s_ref, o_ref = (sf_ref, of_ref) if not rev else (sb_ref, ob_ref)
        qe = (q * jnp.exp2(b)).astype(bf16)
        order = range(nchunk) if not rev else range(nchunk - 1, -1, -1)
        edge = (C - 1) if not rev else 0
        upd = {}
        for j in order:
            b_edge = bbuf[d, j * C + edge:j * C + edge + 1, :]
            kdec = jnp.exp2(cc[j * C:(j + 1) * C] + b_edge).astype(bf16)
            upd[j] = (jnp.exp2(b_edge), _dot_tn(v16[j * C:(j + 1) * C], kdec))
        state = s_ref[...]
        for j in order:
            rows = slice(j * C, (j + 1) * C)
            o_ref[rows, :] = _dot_nt(qe[rows], state.astype(bf16)) + _dot(intra[d][j], v16[rows])
            state = upd[j][0] * state + upd[j][1]
        s_ref[...] = state


def _hgrn_scan(proj, lb, tb=512):
    s = proj.shape[0]
    w = HG_HEADS * HG_D
    nb = s // tb
    hb = w // HG_D
    def fw(seg):
        return pl.BlockSpec((tb, HG_D), lambda h, n: (n, seg * hb + h))

    def bw(seg):
        return pl.BlockSpec((tb, HG_D), lambda h, n: (nb - 1 - n, seg * hb + h))

    return pl.pallas_call(
        _hgrn_block_kernel,
        grid=(HG_HEADS, nb),
        in_specs=[fw(0), fw(1), fw(2), bw(0), bw(1), bw(3),
                  pl.BlockSpec((2, HG_D), lambda h, n: (0, h))],
        out_specs=[pl.BlockSpec((tb, HG_D), lambda h, n: (n, h)),
                   pl.BlockSpec((tb, HG_D), lambda h, n: (nb - 1 - n, h))],
        out_shape=[jax.ShapeDtypeStruct((s, w), f32), jax.ShapeDtypeStruct((s, w), f32)],
        scratch_shapes=[pltpu.VMEM((HG_D, HG_D), f32), pltpu.VMEM((HG_D, HG_D), f32),
                        pltpu.VMEM((2, tb, HG_D), f32), pltpu.VMEM((2, tb, HG_D), f32)],
        compiler_params=_cparams(("parallel", "arbitrary")),
        name="hgrn_scan",
    )(proj, proj, proj, proj, proj, proj, lb)


def _hgrn_out_kernel(of_ref, ob_ref, g_ref, ng_ref, y_ref):
    ng = ng_ref[...]
    for h in range(HG_HEADS):
        sl = slice(h * HG_D, (h + 1) * HG_D)
        o = of_ref[:, sl] + ob_ref[:, sl]
        ms = jnp.mean(o * o, axis=-1, keepdims=True)
        o = o * lax.rsqrt(ms + RMS_EPS) * ng
        y_ref[:, sl] = (o * jax.nn.sigmoid(g_ref[:, sl])).astype(y_ref.dtype)


def _hgrn_out(o_f, o_b, proj, norm_g, tm=512):
    s, w = o_f.shape
    gate_blk = 4
    return pl.pallas_call(
        _hgrn_out_kernel,
        grid=(s // tm,),
        in_specs=[pl.BlockSpec((tm, w), lambda i: (i, 0)),
                  pl.BlockSpec((tm, w), lambda i: (i, 0)),
                  pl.BlockSpec((tm, w), lambda i: (i, gate_blk)),
                  pl.BlockSpec((1, HG_D), lambda i: (0, 0))],
        out_specs=pl.BlockSpec((tm, w), lambda i: (i, 0)),
        out_shape=jax.ShapeDtypeStruct((s, w), bf16),
        compiler_params=_cparams(("parallel",)),
        name="hgrn_out",
    )(o_f, o_b, proj, norm_g.reshape(1, HG_D))


def _lb_kernel(logit_ref, lb_ref, *, layer):
    x = logit_ref[...]
    e = jnp.exp(x - jnp.max(x, axis=1, keepdims=True))
    p = e / jnp.sum(e, axis=1, keepdims=True)
    acc = p[:, 0, :]
    for l in range(1, layer + 1):
        acc = acc + p[:, l, :]
    lb_ref[...] = acc


def _lower_bounds(logits, layer):
    z, _, w = logits.shape
    return pl.pallas_call(
        functools.partial(_lb_kernel, layer=layer),
        out_shape=jax.ShapeDtypeStruct((z, w), f32),
        name="hgrn_lower_bounds",
    )(logits)


def _merge_kernel(y0_ref, y1_ref, y2_ref, w_ref, g0_ref, g1_ref, g2_ref, o_ref):
    acc = g0_ref[...].astype(f32) * _dot(y0_ref[...], w_ref[0])
    acc = acc + g1_ref[...].astype(f32) * _dot(y1_ref[...], w_ref[1])
    acc = acc + g2_ref[...].astype(f32) * _dot(y2_ref[...], w_ref[2])
    o_ref[...] = acc.astype(o_ref.dtype)


def _merge(y0, y1, y2, w_branch, gsig, tm=1024, tn=512):
    m, kb = y0.shape
    d = w_branch.shape[2]
    nj = d // tn
    yspec = pl.BlockSpec((tm, kb), lambda i, j: (i, 0))
    def gspec(b):
        return pl.BlockSpec((tm, tn), lambda i, j: (i, b * nj + j))

    return pl.pallas_call(
        _merge_kernel,
        grid=(m // tm, nj),
        in_specs=[yspec, yspec, yspec,
                  pl.BlockSpec((N_BRANCH, kb, tn), lambda i, j: (0, 0, j)),
                  gspec(0), gspec(1), gspec(2)],
        out_specs=pl.BlockSpec((tm, tn), lambda i, j: (i, j)),
        out_shape=jax.ShapeDtypeStruct((m, d), bf16),
        compiler_params=_cparams(("parallel", "parallel")),
        name="branch_merge",
    )(y0, y1, y2, w_branch, gsig, gsig, gsig)


def _proj_ln_kernel(a_ref, w_ref, res_ref, g_ref, b_ref, o32_ref, o16_ref):
    x = ALPHA * res_ref[...] + _dot(a_ref[...], w_ref[...])
    y = _ln_rows(x, g_ref[...], b_ref[...])
    o32_ref[...] = y
    o16_ref[...] = y.astype(bf16)


def _proj_residual_ln(a, w, res, g, b, tm=512):
    m, k = a.shape
    d = w.shape[1]
    return pl.pallas_call(
        _proj_ln_kernel,
        grid=(m // tm,),
        in_specs=[pl.BlockSpec((tm, k), lambda i: (i, 0)),
                  pl.BlockSpec((k, d), lambda i: (0, 0)),
                  pl.BlockSpec((tm, d), lambda i: (i, 0)),
                  pl.BlockSpec((1, d), lambda i: (0, 0)),
                  pl.BlockSpec((1, d), lambda i: (0, 0))],
        out_specs=[pl.BlockSpec((tm, d), lambda i: (i, 0)),
                   pl.BlockSpec((tm, d), lambda i: (i, 0))],
        out_shape=[jax.ShapeDtypeStruct((m, d), f32), jax.ShapeDtypeStruct((m, d), bf16)],
        compiler_params=_cparams(("parallel",)),
        name="out_proj_ln",
    )(a, w, res, g.reshape(1, d), b.reshape(1, d))


def _ffn_kernel(x_ref, wg_ref, wu_ref, wd_ref, res_ref, g_ref, b_ref, o_ref, acc_ref):
    j = pl.program_id(1)

    @pl.when(j == 0)
    def _():
        acc_ref[...] = jnp.zeros_like(acc_ref)

    x = x_ref[...]
    hid = jax.nn.silu(_dot(x, wg_ref[...])) * _dot(x, wu_ref[...])
    acc_ref[...] += _dot(hid.astype(bf16), wd_ref[...])

    @pl.when(j == pl.num_programs(1) - 1)
    def _():
        o_ref[...] = _ln_rows(ALPHA * res_ref[...] + acc_ref[...], g_ref[...], b_ref[...])


def _ffn_residual_ln(x16, x32, wg, wu, wd, g, b, tm=512, tf=512):
    m, d = x16.shape
    ff = wg.shape[1]
    return pl.pallas_call(
        _ffn_kernel,
        grid=(m // tm, ff // tf),
        in_specs=[pl.BlockSpec((tm, d), lambda i, j: (i, 0)),
                  pl.BlockSpec((d, tf), lambda i, j: (0, j)),
                  pl.BlockSpec((d, tf), lambda i, j: (0, j)),
                  pl.BlockSpec((tf, d), lambda i, j: (j, 0)),
                  pl.BlockSpec((tm, d), lambda i, j: (i, 0)),
                  pl.BlockSpec((1, d), lambda i, j: (0, 0)),
                  pl.BlockSpec((1, d), lambda i, j: (0, 0))],
        out_specs=pl.BlockSpec((tm, d), lambda i, j: (i, 0)),
        out_shape=jax.ShapeDtypeStruct((m, d), f32),
        scratch_shapes=[pltpu.VMEM((tm, d), f32)],
        compiler_params=_cparams(("parallel", "arbitrary")),
        name="ffn_ln",
    )(x16, wg, wu, wd, x32, g.reshape(1, d), b.reshape(1, d))


def _rope_tables(positions):
    half = MLA_ROPE // 2
    inv_freq = jnp.power(ROPE_THETA, -jnp.arange(half, dtype=f32) / half)
    ang = positions.astype(f32)[..., None] * inv_freq
    cos, sin = jnp.cos(ang), jnp.sin(ang)
    z = jnp.zeros_like(cos)
    pad = jnp.zeros((ang.shape[0], LANES - MLA_ROPE), f32)
    c = jnp.concatenate([cos, cos, pad], axis=-1)
    s1 = jnp.concatenate([-sin, z, pad], axis=-1)
    s2 = jnp.concatenate([z, sin, pad], axis=-1)
    return c, s1, s2


def _layer(h32, h16, mem16, rope, lb, w_in, hgrn_norm_g, g_cq, g_ckv, w_uq, w_ukv, w_memkv, w_branch, w_o,
           ln1_g, ln1_b, w_gate, w_up, w_down, ln2_g, ln2_b):
    s, d = h32.shape
    hgw = HG_HEADS * HG_D
    q_rank = g_cq.shape[0]
    kv_rank = g_ckv.shape[0]
    memw = MEM_HEADS * MEM_HD
    o_cq = 5 * hgw
    o_ckv = o_cq + q_rank
    o_kr = o_ckv + kv_rank
    o_qm = o_kr + MLA_ROPE
    o_gt = o_qm + memw
    rope_c, rope_s1, rope_s2 = rope

    wt = jnp.swapaxes(w_in, 0, 1)

    proj_hg = _matmul(h16, wt, wt_rows=(0, o_cq), tm=2048, tn=512, out_dtype=f32, name="in_proj_hgrn")
    cqn = _matmul(h16, wt, wt_rows=(o_cq, q_rank), tm=1024, tn=q_rank, out_dtype=bf16,
                  epilogue=_rms_epilogue, col_extras=(g_cq.reshape(1, -1),), name="in_proj_cq")
    ckvn = _matmul(h16, wt, wt_rows=(o_ckv, kv_rank), tm=1024, tn=kv_rank, out_dtype=bf16,
                   epilogue=_rms_epilogue, col_extras=(g_ckv.reshape(1, -1),), name="in_proj_ckv")
    krope = _matmul(h16, wt, wt_rows=(o_kr, LANES), tm=1024, tn=LANES, out_dtype=bf16,
                    epilogue=_krope_epilogue, row_extras=(rope_c, rope_s1, rope_s2), name="in_proj_krope")
    q_mem = _matmul(h16, wt, wt_rows=(o_qm, memw), tm=1024, tn=512, out_dtype=bf16,
                    epilogue=lambda acc: acc * (MEM_HD ** -0.5), name="in_proj_qmem")
    gsig = _matmul(h16, wt, wt_rows=(o_gt, N_BRANCH * d), tm=2048, tn=512, out_dtype=bf16,
                   epilogue=jax.nn.sigmoid, name="in_proj_gates")

    o_f, o_b = _hgrn_scan(proj_hg, lb)
    y_hg = _hgrn_out(o_f, o_b, proj_hg, hgrn_norm_g)

    zq = jnp.zeros((q_rank, MLA_HEADS, MLA_QK_PAD - MLA_QK), f32)
    w_q = jnp.concatenate([w_uq.reshape(q_rank, MLA_HEADS, MLA_QK), zq], axis=-1)
    w_q = w_q.reshape(q_rank, MLA_HEADS * MLA_QK_PAD).astype(bf16)
    w_kv3 = w_ukv.reshape(kv_rank, MLA_HEADS, MLA_NOPE + MLA_V)
    w_kv = jnp.concatenate([w_kv3[:, :, :MLA_NOPE].reshape(kv_rank, -1),
                            w_kv3[:, :, MLA_NOPE:].reshape(kv_rank, -1)], axis=-1).astype(bf16)
    q = _mla_q(cqn, w_q, rope_c, rope_s1, rope_s2)
    k, v = _mla_kv(ckvn, w_kv, krope)
    y_mla = _mla_attention(q, k, v)

    kv_mem = _matmul(mem16, w_memkv.astype(bf16), tm=mem16.shape[0], tn=512, out_dtype=bf16, name="mem_kv_proj")
    y_mem = _mem_attention(q_mem, kv_mem[:, :memw], kv_mem[:, memw:])

    merged = _merge(y_hg, y_mla, y_mem, w_branch.astype(bf16), gsig)
    x32, x16 = _proj_residual_ln(merged, w_o.astype(bf16), h32, ln1_g, ln1_b)
    out32 = _ffn_residual_ln(x16, x32, w_gate.astype(bf16), w_up.astype(bf16), w_down.astype(bf16), ln2_g, ln2_b)
    return out32


def kernel(x, mem, positions, ln_emb_g, ln_emb_b, hgrn_lb_logits, w_in, hgrn_norm_g, mla_g_cq, mla_g_ckv,
           mla_w_uq, mla_w_ukv, mem_w_kv, w_branch, w_o, ln1_g, ln1_b, w_ffn_gate, w_ffn_up, w_ffn_down,
           ln2_g, ln2_b):
    bsz, s, d = x.shape
    depth = w_in.shape[0]
    outs = []
    for bi in range(bsz):
        rope = _rope_tables(positions[bi])
        mem16 = mem[bi].astype(bf16)
        h32, h16 = _layernorm(x[bi], ln_emb_g, ln_emb_b)
        for l in range(depth):
            lb = _lower_bounds(hgrn_lb_logits, l)
            h32 = _layer(h32, h16, mem16, rope, lb, w_in[l], hgrn_norm_g[l], mla_g_cq[l], mla_g_ckv[l],
                         mla_w_uq[l], mla_w_ukv[l], mem_w_kv[l], w_branch[l], w_o[l], ln1_g[l], ln1_b[l],
                         w_ffn_gate[l], w_ffn_up[l], w_ffn_down[l], ln2_g[l], ln2_b[l])
            if l + 1 < depth:
                h16 = h32.astype(bf16)
        outs.append(h32)
    return jnp.stack(outs)
```

```python
import functools

import jax
import jax.numpy as jnp
from jax import lax
from jax.experimental import pallas as pl
from jax.experimental.pallas import tpu as pltpu

HG_HEADS = 8
HG_D = 128
MLA_HEADS = 8
MLA_NOPE = 128
MLA_ROPE = 64
MLA_V = 128
MLA_QK = MLA_NOPE + MLA_ROPE
MLA_QK_PAD = 256
MLA_V_PAD = 256
MLA_QBLOCKS_PER_STEP = 1
LOG2E = 1.4426950408889634
MEM_HEADS = 4
MEM_HD = 256
N_BRANCH = 3
ROPE_THETA = 10000.0
LN_EPS = 1e-5
RMS_EPS = 1e-6
DEPTH = 1
ALPHA = (2.0 * DEPTH) ** 0.25

LANES = 128
SUBLANES = 8
VMEM_LIMIT = 56 * 1024 * 1024

HG_CHUNK = 64
HG_SUB = SUBLANES

bf16 = jnp.bfloat16
f32 = jnp.float32


def _cparams(sem):
    return pltpu.CompilerParams(dimension_semantics=sem, vmem_limit_bytes=VMEM_LIMIT)


def _dot(a, b):
    return jnp.dot(a, b, preferred_element_type=f32)


def _dot_nt(a, b):
    return lax.dot_general(a, b, (((1,), (1,)), ((), ())), preferred_element_type=f32)


def _dot_tn(a, b):
    return lax.dot_general(a, b, (((0,), (0,)), ((), ())), preferred_element_type=f32)


def _ln_rows(x, g, b):
    mu = jnp.mean(x, axis=-1, keepdims=True)
    xc = x - mu
    var = jnp.mean(xc * xc, axis=-1, keepdims=True)
    return xc * lax.rsqrt(var + LN_EPS) * g + b


def _ln_kernel(x_ref, g_ref, b_ref, o16_ref):
    o16_ref[...] = _ln_rows(x_ref[...], g_ref[...], b_ref[...]).astype(bf16)


def _layernorm_bf16(x, g, b, tm=512):
    m, d = x.shape
    return pl.pallas_call(
        _ln_kernel,
        grid=(m // tm,),
        in_specs=[pl.BlockSpec((tm, d), lambda i: (i, 0)),
                  pl.BlockSpec((1, d), lambda i: (0, 0)),
                  pl.BlockSpec((1, d), lambda i: (0, 0))],
        out_specs=pl.BlockSpec((tm, d), lambda i: (i, 0)),
        out_shape=jax.ShapeDtypeStruct((m, d), bf16),
        compiler_params=_cparams(("parallel",)),
        name="ln_embed",
    )(x, g.reshape(1, d), b.reshape(1, d))


def _mm_kernel(a_ref, w_ref, *rest, epilogue, w_transposed):
    o_ref = rest[-1]
    w = w_ref[...].astype(bf16)
    acc = _dot_nt(a_ref[...], w) if w_transposed else _dot(a_ref[...], w)
    o_ref[...] = epilogue(acc, *[r[...] for r in rest[:-1]]).astype(o_ref.dtype)


def _matmul(a, w, *, tm, tn, out_dtype, epilogue=None, row_extras=(), col_extras=(), name, wt_rows=None):
    m, k = a.shape
    if epilogue is None:
        epilogue = lambda acc: acc
    if wt_rows is None:
        n = w.shape[1]
        w_spec = pl.BlockSpec((k, tn), lambda i, j: (0, j))
    else:
        row0, n = wt_rows
        assert row0 % SUBLANES == 0 and w.shape[1] == k
        w_spec = pl.BlockSpec((pl.Element(tn), pl.Element(k)),
                              lambda i, j: ((row0 // SUBLANES + j * (tn // SUBLANES)) * SUBLANES, 0))
    in_specs = [pl.BlockSpec((tm, k), lambda i, j: (i, 0)), w_spec]
    for e in row_extras:
        in_specs.append(pl.BlockSpec((tm, e.shape[1]), lambda i, j: (i, 0)))
    for e in col_extras:
        in_specs.append(pl.BlockSpec((e.shape[0], tn), lambda i, j: (0, j)))
    return pl.pallas_call(
        functools.partial(_mm_kernel, epilogue=epilogue, w_transposed=wt_rows is not None),
        grid=(m // tm, n // tn),
        in_specs=in_specs,
        out_specs=pl.BlockSpec((tm, tn), lambda i, j: (i, j)),
        out_shape=jax.ShapeDtypeStruct((m, n), out_dtype),
        compiler_params=_cparams(("parallel", "parallel")),
        name=name,
    )(a, w, *row_extras, *col_extras)


def _rms_epilogue(acc, g):
    ms = jnp.mean(acc * acc, axis=-1, keepdims=True)
    return acc * lax.rsqrt(ms + RMS_EPS) * g


def _rope_block(blk, c, s1, s2):
    half = MLA_ROPE // 2
    return blk * c + pltpu.roll(blk, LANES - half, 1) * s1 + pltpu.roll(blk, half, 1) * s2


def _krope_epilogue(acc, c, s1, s2):
    return _rope_block(acc, c, s1, s2)


def _mla_q_kernel(cq_ref, w_ref, c_ref, s1_ref, s2_ref, o_ref):
    q = _dot(cq_ref[...], w_ref[...]) * (MLA_QK ** -0.5 * LOG2E)
    c, s1, s2 = c_ref[...], s1_ref[...], s2_ref[...]
    for h in range(MLA_HEADS):
        base = h * MLA_QK_PAD
        o_ref[:, base:base + MLA_NOPE] = q[:, base:base + MLA_NOPE].astype(bf16)
        o_ref[:, base + MLA_NOPE:base + MLA_QK_PAD] = _rope_block(
            q[:, base + MLA_NOPE:base + MLA_QK_PAD], c, s1, s2).astype(bf16)


def _mla_q(cqn, w_q, rope_c, rope_s1, rope_s2, tm=512):
    m, r = cqn.shape
    n = w_q.shape[1]
    return pl.pallas_call(
        _mla_q_kernel,
        grid=(m // tm,),
        in_specs=[pl.BlockSpec((tm, r), lambda i: (i, 0)),
                  pl.BlockSpec((r, n), lambda i: (0, 0)),
                  pl.BlockSpec((tm, LANES), lambda i: (i, 0)),
                  pl.BlockSpec((tm, LANES), lambda i: (i, 0)),
                  pl.BlockSpec((tm, LANES), lambda i: (i, 0))],
        out_specs=pl.BlockSpec((tm, n), lambda i: (i, 0)),
        out_shape=jax.ShapeDtypeStruct((m, n), bf16),
        compiler_params=_cparams(("parallel",)),
        name="mla_q_up",
    )(cqn, w_q, rope_c, rope_s1, rope_s2)


def _mla_kv_kernel(ckv_ref, w_ref, kr_ref, k_ref, v_ref):
    kv = _dot(ckv_ref[...], w_ref[...])
    kr = kr_ref[...]
    nk = MLA_HEADS * MLA_NOPE
    for h in range(MLA_HEADS):
        base = h * MLA_QK_PAD
        k_ref[:, base:base + MLA_NOPE] = kv[:, h * MLA_NOPE:(h + 1) * MLA_NOPE].astype(bf16)
        k_ref[:, base + MLA_NOPE:base + MLA_QK_PAD] = kr
    lane = lax.broadcasted_iota(jnp.int32, (kv.shape[0], MLA_V_PAD - MLA_V), 1)
    ones_col = jnp.where(lane == 0, 1.0, 0.0).astype(bf16)
    for h in range(MLA_HEADS):
        base = h * MLA_V_PAD
        v_ref[:, base:base + MLA_V] = kv[:, nk + h * MLA_V:nk + (h + 1) * MLA_V].astype(bf16)
        v_ref[:, base + MLA_V:base + MLA_V_PAD] = ones_col


def _mla_kv(ckvn, w_kv, krope, tm=512):
    m, r = ckvn.shape
    n = w_kv.shape[1]
    nk = MLA_HEADS * MLA_QK_PAD
    nv = MLA_HEADS * MLA_V_PAD
    return pl.pallas_call(
        _mla_kv_kernel,
        grid=(m // tm,),
        in_specs=[pl.BlockSpec((tm, r), lambda i: (i, 0)),
                  pl.BlockSpec((r, n), lambda i: (0, 0)),
                  pl.BlockSpec((tm, LANES), lambda i: (i, 0))],
        out_specs=[pl.BlockSpec((tm, nk), lambda i: (i, 0)),
                   pl.BlockSpec((tm, nv), lambda i: (i, 0))],
        out_shape=[jax.ShapeDtypeStruct((m, nk), bf16), jax.ShapeDtypeStruct((m, nv), bf16)],
        compiler_params=_cparams(("parallel",)),
        name="mla_kv_up",
    )(ckvn, w_kv, krope)


def _mla_attn_kernel(q_ref, k_ref, v_ref, o_ref, s_ref, *, tq, tk):
    nq = q_ref.shape[0] // tq
    nkv = k_ref.shape[0] // tk

    def scores(q, c):
        return _dot_nt(q, k_ref[c * tk:(c + 1) * tk, :])

    def update(s, c, m, acc):
        m_new = jnp.maximum(m, jnp.max(s, axis=-1, keepdims=True))
        p = jnp.exp2(s - m_new)
        acc = jnp.exp2(m - m_new) * acc + _dot(p.astype(bf16), v_ref[c * tk:(c + 1) * tk, :])
        return m_new, acc

    s_ref[...] = scores(q_ref[0:tq, :], 0)
    assert nq % MLA_QBLOCKS_PER_STEP == 0

    def body(i, carry):
        s_next = s_ref[...]
        for u in range(MLA_QBLOCKS_PER_STEP):
            blk = i * MLA_QBLOCKS_PER_STEP + u
            off = pl.multiple_of(blk * tq, tq)
            q = q_ref[pl.ds(off, tq), :]
            m = jnp.full((tq, 1), -jnp.inf, f32)
            acc = jnp.zeros((tq, MLA_V_PAD), f32)
            for c in range(nkv):
                s_cur = s_next
                if c + 1 < nkv:
                    s_next = scores(q, c + 1)
                else:
                    nxt = pl.multiple_of(jnp.minimum(blk + 1, nq - 1) * tq, tq)
                    s_next = scores(q_ref[pl.ds(nxt, tq), :], 0)
                m, acc = update(s_cur, c, m, acc)
            o_ref[pl.ds(off, tq), :] = (acc[:, :MLA_V] / acc[:, MLA_V:MLA_V + 1]).astype(o_ref.dtype)
        s_ref[...] = s_next
        return carry

    lax.fori_loop(0, nq // MLA_QBLOCKS_PER_STEP, body, 0)


def _mla_attention(q, k, v, tq=512, tk=512):
    s = q.shape[0]
    return pl.pallas_call(
        functools.partial(_mla_attn_kernel, tq=tq, tk=tk),
        grid=(MLA_HEADS,),
        in_specs=[pl.BlockSpec((s, MLA_QK_PAD), lambda h: (0, h)),
                  pl.BlockSpec((s, MLA_QK_PAD), lambda h: (0, h)),
                  pl.BlockSpec((s, MLA_V_PAD), lambda h: (0, h))],
        out_specs=pl.BlockSpec((s, MLA_V), lambda h: (0, h)),
        out_shape=jax.ShapeDtypeStruct((s, MLA_HEADS * MLA_V), bf16),
        scratch_shapes=[pltpu.VMEM((tq, tk), f32)],
        compiler_params=_cparams(("parallel",)),
        name="mla_attention",
    )(q, k, v)


def _mem_attn_kernel(q_ref, k_ref, v_ref, o_ref):
    for h in range(MEM_HEADS):
        sl = slice(h * MEM_HD, (h + 1) * MEM_HD)
        s = _dot_nt(q_ref[:, sl], k_ref[:, sl])
        m = jnp.max(s, axis=-1, keepdims=True)
        p = jnp.exp(s - m)
        l = jnp.sum(p, axis=-1, keepdims=True)
        o = _dot(p.astype(bf16), v_ref[:, sl])
        o_ref[:, sl] = (o / l).astype(o_ref.dtype)


def _mem_attention(q, k, v, tm=512):
    s, w = q.shape
    nm = k.shape[0]
    return pl.pallas_call(
        _mem_attn_kernel,
        grid=(s // tm,),
        in_specs=[pl.BlockSpec((tm, w), lambda i: (i, 0)),
                  pl.BlockSpec((nm, w), lambda i: (0, 0)),
                  pl.BlockSpec((nm, w), lambda i: (0, 0))],
        out_specs=pl.BlockSpec((tm, w), lambda i: (i, 0)),
        out_shape=jax.ShapeDtypeStruct((s, w), bf16),
        compiler_params=_cparams(("parallel",)),
        name="mem_attention",
    )(q, k, v)


def _hgrn_levels(rev):
    levels = []
    sz = HG_CHUNK // 2
    while sz >= HG_SUB:
        blocks = []
        for base in range(0, HG_CHUNK, 2 * sz):
            if not rev:
                blocks.append((base + sz, base, sz, base + sz - 1))
            else:
                blocks.append((base, base + sz, sz, base + sz))
        levels.append(blocks)
        sz //= 2
    return levels


def _hgrn_intra(q_c, b_c, c_c, bbuf, cbuf, row0, *, rev):
    C = HG_CHUNK
    nb = C // HG_SUB
    lane = lax.broadcasted_iota(jnp.int32, (HG_SUB, C), 1)
    sub = lax.broadcasted_iota(jnp.int32, (HG_SUB, C), 0)
    blocks = []
    for i in range(nb):
        r0 = i * HG_SUB
        q_i = q_c[r0:r0 + HG_SUB]
        b_i = b_c[r0:r0 + HG_SUB]
        a_blk = jnp.zeros((HG_SUB, C), f32)
        for s in range(HG_SUB):
            w = q_i * jnp.exp2(b_i + cbuf[row0 + r0 + s:row0 + r0 + s + 1, :])
            a_blk = jnp.where(lane == r0 + s, jnp.sum(w, axis=-1, keepdims=True), a_blk)
        keep = (sub >= lane - r0) if not rev else (sub <= lane - r0)
        blocks.append(jnp.where(keep, a_blk, 0.0))
    for level in _hgrn_levels(rev):
        sz = level[0][2]
        qd, kd, valid = [], [], None
        k_pos = 0
        rq = lax.broadcasted_iota(jnp.int32, (len(level) * sz, C), 0)
        ck = lax.broadcasted_iota(jnp.int32, (len(level) * sz, C), 1)
        for j, (q0, k0, _, piv_row) in enumerate(level):
            piv = bbuf[row0 + piv_row:row0 + piv_row + 1, :]
            qd.append(q_c[q0:q0 + sz] * jnp.exp2(b_c[q0:q0 + sz] - piv))
            if k0 > k_pos:
                kd.append(jnp.zeros((k0 - k_pos, q_c.shape[1]), f32))
            kd.append(jnp.exp2(c_c[k0:k0 + sz] + piv))
            k_pos = k0 + sz
            ok = (rq >= j * sz) & (rq < (j + 1) * sz) & (ck >= k0) & (ck < k0 + sz)
            valid = ok if valid is None else (valid | ok)
        if k_pos < C:
            kd.append(jnp.zeros((C - k_pos, q_c.shape[1]), f32))
        r = _dot_nt(jnp.concatenate(qd, axis=0).astype(bf16), jnp.concatenate(kd, axis=0).astype(bf16))
        if len(level) > 1:
            r = jnp.where(valid, r, 0.0)
        for j, (q0, _, _, _) in enumerate(level):
            for t in range(sz // HG_SUB):
                i = q0 // HG_SUB + t
                blocks[i] = blocks[i] + r[j * sz + t * HG_SUB:j * sz + (t + 1) * HG_SUB]
    return jnp.concatenate(blocks, axis=0)


def _hgrn_block_kernel(qf_ref, vf_ref, ff_ref, qb_ref, vb_ref, fb_ref, lb_ref, of_ref, ob_ref,
                       sf_ref, sb_ref, bbuf, cbuf):
    n = pl.program_id(1)
    C = HG_CHUNK
    T = qf_ref.shape[0]
    nchunk = T // C

    @pl.when(n == 0)
    def _():
        sf_ref[...] = jnp.zeros_like(sf_ref)
        sb_ref[...] = jnp.zeros_like(sb_ref)

    r = lax.broadcasted_iota(jnp.int32, (C, C), 0)
    c = lax.broadcasted_iota(jnp.int32, (C, C), 1)
    dirs = []
    for d, (q_ref, v_ref, f_ref, rev) in enumerate(((qf_ref, vf_ref, ff_ref, False), (qb_ref, vb_ref, fb_ref, True))):
        tri = ((c <= r) if not rev else (c >= r)).astype(bf16)
        lb = lb_ref[d:d + 1, :]
        q = jax.nn.silu(q_ref[...])
        v16 = v_ref[...].astype(bf16)
        f = lb + (1.0 - lb) * jax.nn.sigmoid(f_ref[...])
        g = jnp.log(f) * LOG2E
        g_hi = g.astype(bf16)
        g_lo = (g - g_hi.astype(f32)).astype(bf16)
        b = jnp.concatenate([_dot(tri, g_hi[j * C:(j + 1) * C]) + _dot(tri, g_lo[j * C:(j + 1) * C])
                             for j in range(nchunk)], axis=0)
        cc = jnp.log(1.0 - f) * LOG2E - b
        bbuf[d] = b
        cbuf[d] = cc
        dirs.append((q, v16, b, cc, rev))

    intra = []
    for d, (q, v16, b, cc, rev) in enumerate(dirs):
        intra.append([_hgrn_intra(q[j * C:(j + 1) * C], b[j * C:(j + 1) * C], cc[j * C:(j + 1) * C],
                                  bbuf.at[d], cbuf.at[d], j * C, rev=rev).astype(bf16) for j in range(nchunk)])

    for d, (q, v16, b, cc, rev) in enumerate(dirs):
        s_ref, o_ref = (sf_ref, of_ref) if not rev else (sb_ref, ob_ref)
        qe = (q * jnp.exp2(b)).astype(bf16)
        order = range(nchunk) if not rev else range(nchunk - 1, -1, -1)
        edge = (C - 1) if not rev else 0
        upd = {}
        for j in order:
            b_edge = bbuf[d, j * C + edge:j * C + edge + 1, :]
            kdec = jnp.exp2(cc[j * C:(j + 1) * C] + b_edge).astype(bf16)
            upd[j] = (jnp.exp2(b_edge), _dot_tn(v16[j * C:(j + 1) * C], kdec))
        state = s_ref[...]
        for j in order:
            rows = slice(j * C, (j + 1) * C)
            o = _dot_nt(qe[rows], state.astype(bf16)) + _dot(intra[d][j], v16[rows])
            o_ref[rows, :] = o.astype(o_ref.dtype)
            state = upd[j][0] * state + upd[j][1]
        s_ref[...] = state


def _hgrn_scan(proj, lb, tb=512):
    s = proj.shape[0]
    w = HG_HEADS * HG_D
    nb = s // tb
    hb = w // HG_D
    def fw(seg):
        return pl.BlockSpec((tb, HG_D), lambda h, n: (n, seg * hb + h))

    def bw(seg):
        return pl.BlockSpec((tb, HG_D), lambda h, n: (nb - 1 - n, seg * hb + h))

    return pl.pallas_call(
        _hgrn_block_kernel,
        grid=(HG_HEADS, nb),
        in_specs=[fw(0), fw(1), fw(2), bw(0), bw(1), bw(3),
                  pl.BlockSpec((2, HG_D), lambda h, n: (0, h))],
        out_specs=[pl.BlockSpec((tb, HG_D), lambda h, n: (n, h)),
                   pl.BlockSpec((tb, HG_D), lambda h, n: (nb - 1 - n, h))],
        out_shape=[jax.ShapeDtypeStruct((s, w), bf16), jax.ShapeDtypeStruct((s, w), bf16)],
        scratch_shapes=[pltpu.VMEM((HG_D, HG_D), f32), pltpu.VMEM((HG_D, HG_D), f32),
                        pltpu.VMEM((2, tb, HG_D), f32), pltpu.VMEM((2, tb, HG_D), f32)],
        compiler_params=_cparams(("parallel", "arbitrary")),
        name="hgrn_scan",
    )(proj, proj, proj, proj, proj, proj, lb)


def _lb_kernel(logit_ref, lb_ref, *, layer):
    x = logit_ref[...]
    e = jnp.exp(x - jnp.max(x, axis=1, keepdims=True))
    p = e / jnp.sum(e, axis=1, keepdims=True)
    acc = p[:, 0, :]
    for l in range(1, layer + 1):
        acc = acc + p[:, l, :]
    lb_ref[...] = acc


def _lower_bounds(logits, layer):
    z, _, w = logits.shape
    return pl.pallas_call(
        functools.partial(_lb_kernel, layer=layer),
        out_shape=jax.ShapeDtypeStruct((z, w), f32),
        name="hgrn_lower_bounds",
    )(logits)


def _merge_kernel(of_ref, ob_ref, hg_ref, ng_ref, y1_ref, y2_ref, w_ref, g0_ref, g1_ref, g2_ref, o_ref, y0_ref):
    @pl.when(pl.program_id(1) == 0)
    def _():
        ng = ng_ref[...]
        for h in range(HG_HEADS):
            sl = slice(h * HG_D, (h + 1) * HG_D)
            o = of_ref[:, sl].astype(f32) + ob_ref[:, sl].astype(f32)
            ms = jnp.mean(o * o, axis=-1, keepdims=True)
            o = o * lax.rsqrt(ms + RMS_EPS) * ng
            y0_ref[:, sl] = (o * jax.nn.sigmoid(hg_ref[:, sl])).astype(y0_ref.dtype)

    acc = g0_ref[...].astype(f32) * _dot(y0_ref[...], w_ref[0])
    acc = acc + g1_ref[...].astype(f32) * _dot(y1_ref[...], w_ref[1])
    acc = acc + g2_ref[...].astype(f32) * _dot(y2_ref[...], w_ref[2])
    o_ref[...] = acc.astype(o_ref.dtype)


def _merge(o_f, o_b, proj_hg, norm_g, y1, y2, w_branch, gsig, tm=1024, tn=512):
    m, kb = y1.shape
    d = w_branch.shape[2]
    nj = d // tn
    hg_gate_blk = 4
    yspec = pl.BlockSpec((tm, kb), lambda i, j: (i, 0))
    def gspec(b):
        return pl.BlockSpec((tm, tn), lambda i, j: (i, b * nj + j))

    return pl.pallas_call(
        _merge_kernel,
        grid=(m // tm, nj),
        in_specs=[yspec, yspec,
                  pl.BlockSpec((tm, kb), lambda i, j: (i, hg_gate_blk)),
                  pl.BlockSpec((1, HG_D), lambda i, j: (0, 0)),
                  yspec, yspec,
                  pl.BlockSpec((N_BRANCH, kb, tn), lambda i, j: (0, 0, j)),
                  gspec(0), gspec(1), gspec(2)],
        out_specs=pl.BlockSpec((tm, tn), lambda i, j: (i, j)),
        out_shape=jax.ShapeDtypeStruct((m, d), bf16),
        scratch_shapes=[pltpu.VMEM((tm, kb), bf16)],
        compiler_params=_cparams(("parallel", "arbitrary")),
        name="branch_merge",
    )(o_f, o_b, proj_hg, norm_g.reshape(1, HG_D), y1, y2, w_branch, gsig, gsig, gsig)


def _proj_ln_kernel(a_ref, w_ref, res_ref, rg_ref, rb_ref, g_ref, b_ref, o32_ref, o16_ref, *, res_is_raw):
    res = res_ref[...]
    if res_is_raw:
        res = _ln_rows(res, rg_ref[...], rb_ref[...])
    x = ALPHA * res + _dot(a_ref[...], w_ref[...])
    y = _ln_rows(x, g_ref[...], b_ref[...])
    o32_ref[...] = y
    o16_ref[...] = y.astype(bf16)


def _proj_residual_ln(a, w, res, res_ln, g, b, tm=512):
    m, k = a.shape
    d = w.shape[1]
    rg, rb = res_ln if res_ln is not None else (g, b)
    vec = pl.BlockSpec((1, d), lambda i: (0, 0))
    return pl.pallas_call(
        functools.partial(_proj_ln_kernel, res_is_raw=res_ln is not None),
        grid=(m // tm,),
        in_specs=[pl.BlockSpec((tm, k), lambda i: (i, 0)),
                  pl.BlockSpec((k, d), lambda i: (0, 0)),
                  pl.BlockSpec((tm, d), lambda i: (i, 0)),
                  vec, vec, vec, vec],
        out_specs=[pl.BlockSpec((tm, d), lambda i: (i, 0)),
                   pl.BlockSpec((tm, d), lambda i: (i, 0))],
        out_shape=[jax.ShapeDtypeStruct((m, d), f32), jax.ShapeDtypeStruct((m, d), bf16)],
        compiler_params=_cparams(("parallel",)),
        name="out_proj_ln",
    )(a, w, res, rg.reshape(1, d), rb.reshape(1, d), g.reshape(1, d), b.reshape(1, d))


def _ffn_kernel(x_ref, wg_ref, wu_ref, wd_ref, res_ref, g_ref, b_ref, o_ref, acc_ref):
    j = pl.program_id(1)

    @pl.when(j == 0)
    def _():
        acc_ref[...] = jnp.zeros_like(acc_ref)

    x = x_ref[...]
    hid = jax.nn.silu(_dot(x, wg_ref[...])) * _dot(x, wu_ref[...])
    acc_ref[...] += _dot(hid.astype(bf16), wd_ref[...])

    @pl.when(j == pl.num_programs(1) - 1)
    def _():
        o_ref[...] = _ln_rows(ALPHA * res_ref[...] + acc_ref[...], g_ref[...], b_ref[...])


def _ffn_residual_ln(x16, x32, wg, wu, wd, g, b, tm=512, tf=512):
    m, d = x16.shape
    ff = wg.shape[1]
    return pl.pallas_call(
        _ffn_kernel,
        grid=(m // tm, ff // tf),
        in_specs=[pl.BlockSpec((tm, d), lambda i, j: (i, 0)),
                  pl.BlockSpec((d, tf), lambda i, j: (0, j)),
                  pl.BlockSpec((d, tf), lambda i, j: (0, j)),
                  pl.BlockSpec((tf, d), lambda i, j: (j, 0)),
                  pl.BlockSpec((tm, d), lambda i, j: (i, 0)),
                  pl.BlockSpec((1, d), lambda i, j: (0, 0)),
                  pl.BlockSpec((1, d), lambda i, j: (0, 0))],
        out_specs=pl.BlockSpec((tm, d), lambda i, j: (i, 0)),
        out_shape=jax.ShapeDtypeStruct((m, d), f32),
        scratch_shapes=[pltpu.VMEM((tm, d), f32)],
        compiler_params=_cparams(("parallel", "arbitrary")),
        name="ffn_ln",
    )(x16, wg, wu, wd, x32, g.reshape(1, d), b.reshape(1, d))


def _rope_tables(positions):
    half = MLA_ROPE // 2
    inv_freq = jnp.power(ROPE_THETA, -jnp.arange(half, dtype=f32) / half)
    ang = positions.astype(f32)[..., None] * inv_freq
    cos, sin = jnp.cos(ang), jnp.sin(ang)
    z = jnp.zeros_like(cos)
    pad = jnp.zeros((ang.shape[0], LANES - MLA_ROPE), f32)
    c = jnp.concatenate([cos, cos, pad], axis=-1)
    s1 = jnp.concatenate([-sin, z, pad], axis=-1)
    s2 = jnp.concatenate([z, sin, pad], axis=-1)
    return c, s1, s2


def _layer(res, res_ln, h16, mem16, rope, lb, w_in, hgrn_norm_g, g_cq, g_ckv, w_uq, w_ukv, w_memkv, w_branch, w_o,
           ln1_g, ln1_b, w_gate, w_up, w_down, ln2_g, ln2_b):
    s, d = res.shape
    hgw = HG_HEADS * HG_D
    q_rank = g_cq.shape[0]
    kv_rank = g_ckv.shape[0]
    memw = MEM_HEADS * MEM_HD
    o_cq = 5 * hgw
    o_ckv = o_cq + q_rank
    o_kr = o_ckv + kv_rank
    o_qm = o_kr + MLA_ROPE
    o_gt = o_qm + memw
    rope_c, rope_s1, rope_s2 = rope

    wt = jnp.swapaxes(w_in, 0, 1)

    proj_hg = _matmul(h16, wt, wt_rows=(0, o_cq), tm=2048, tn=512, out_dtype=f32, name="in_proj_hgrn")
    cqn = _matmul(h16, wt, wt_rows=(o_cq, q_rank), tm=1024, tn=q_rank, out_dtype=bf16,
                  epilogue=_rms_epilogue, col_extras=(g_cq.reshape(1, -1),), name="in_proj_cq")
    ckvn = _matmul(h16, wt, wt_rows=(o_ckv, kv_rank), tm=1024, tn=kv_rank, out_dtype=bf16,
                   epilogue=_rms_epilogue, col_extras=(g_ckv.reshape(1, -1),), name="in_proj_ckv")
    krope = _matmul(h16, wt, wt_rows=(o_kr, LANES), tm=1024, tn=LANES, out_dtype=bf16,
                    epilogue=_krope_epilogue, row_extras=(rope_c, rope_s1, rope_s2), name="in_proj_krope")
    q_mem = _matmul(h16, wt, wt_rows=(o_qm, memw), tm=1024, tn=512, out_dtype=bf16,
                    epilogue=lambda acc: acc * (MEM_HD ** -0.5), name="in_proj_qmem")
    gsig = _matmul(h16, wt, wt_rows=(o_gt, N_BRANCH * d), tm=2048, tn=512, out_dtype=bf16,
                   epilogue=jax.nn.sigmoid, name="in_proj_gates")

    o_f, o_b = _hgrn_scan(proj_hg, lb)

    zq = jnp.zeros((q_rank, MLA_HEADS, MLA_QK_PAD - MLA_QK), f32)
    w_q = jnp.concatenate([w_uq.reshape(q_rank, MLA_HEADS, MLA_QK), zq], axis=-1)
    w_q = w_q.reshape(q_rank, MLA_HEADS * MLA_QK_PAD).astype(bf16)
    w_kv3 = w_ukv.reshape(kv_rank, MLA_HEADS, MLA_NOPE + MLA_V)
    w_kv = jnp.concatenate([w_kv3[:, :, :MLA_NOPE].reshape(kv_rank, -1),
                            w_kv3[:, :, MLA_NOPE:].reshape(kv_rank, -1)], axis=-1).astype(bf16)
    q = _mla_q(cqn, w_q, rope_c, rope_s1, rope_s2)
    k, v = _mla_kv(ckvn, w_kv, krope)
    y_mla = _mla_attention(q, k, v)

    kv_mem = _matmul(mem16, w_memkv.astype(bf16), tm=mem16.shape[0], tn=512, out_dtype=bf16, name="mem_kv_proj")
    y_mem = _mem_attention(q_mem, kv_mem[:, :memw], kv_mem[:, memw:])

    merged = _merge(o_f, o_b, proj_hg, hgrn_norm_g, y_mla, y_mem, w_branch.astype(bf16), gsig)
    x32, x16 = _proj_residual_ln(merged, w_o.astype(bf16), res, res_ln, ln1_g, ln1_b)
    out32 = _ffn_residual_ln(x16, x32, w_gate.astype(bf16), w_up.astype(bf16), w_down.astype(bf16), ln2_g, ln2_b)
    return out32


def kernel(x, mem, positions, ln_emb_g, ln_emb_b, hgrn_lb_logits, w_in, hgrn_norm_g, mla_g_cq, mla_g_ckv,
           mla_w_uq, mla_w_ukv, mem_w_kv, w_branch, w_o, ln1_g, ln1_b, w_ffn_gate, w_ffn_up, w_ffn_down,
           ln2_g, ln2_b):
    bsz, s, d = x.shape
    depth = w_in.shape[0]
    outs = []
    for bi in range(bsz):
        rope = _rope_tables(positions[bi])
        mem16 = mem[bi].astype(bf16)
        h16 = _layernorm_bf16(x[bi], ln_emb_g, ln_emb_b)
        res, res_ln = x[bi], (ln_emb_g, ln_emb_b)
        for l in range(depth):
            lb = _lower_bounds(hgrn_lb_logits, l)
            res = _layer(res, res_ln, h16, mem16, rope, lb, w_in[l], hgrn_norm_g[l], mla_g_cq[l], mla_g_ckv[l],
                         mla_w_uq[l], mla_w_ukv[l], mem_w_kv[l], w_branch[l], w_o[l], ln1_g[l], ln1_b[l],
                         w_ffn_gate[l], w_ffn_up[l], w_ffn_down[l], ln2_g[l], ln2_b[l])
            res_ln = None
            if l + 1 < depth:
                h16 = res.astype(bf16)
        outs.append(res)
    return jnp.stack(outs)
```

```python
import functools

import jax
import jax.numpy as jnp
from jax import lax
from jax.experimental import pallas as pl
from jax.experimental.pallas import tpu as pltpu

HG_HEADS = 8
HG_D = 128
MLA_HEADS = 8
MLA_NOPE = 128
MLA_ROPE = 64
MLA_V = 128
MLA_QK = MLA_NOPE + MLA_ROPE
MLA_QK_PAD = 256
MLA_V_PAD = 256
MLA_QBLOCKS_PER_STEP = 1
LOG2E = 1.4426950408889634
MEM_HEADS = 4
MEM_HD = 256
N_BRANCH = 3
ROPE_THETA = 10000.0
LN_EPS = 1e-5
RMS_EPS = 1e-6
DEPTH = 1
ALPHA = (2.0 * DEPTH) ** 0.25

LANES = 128
SUBLANES = 8
VMEM_LIMIT = 56 * 1024 * 1024

HG_CHUNK = 64
HG_SUB = SUBLANES

bf16 = jnp.bfloat16
f32 = jnp.float32


def _cparams(sem):
    return pltpu.CompilerParams(dimension_semantics=sem, vmem_limit_bytes=VMEM_LIMIT)


def _dot(a, b):
    return jnp.dot(a, b, preferred_element_type=f32)


def _dot_nt(a, b):
    return lax.dot_general(a, b, (((1,), (1,)), ((), ())), preferred_element_type=f32)


def _dot_tn(a, b):
    return lax.dot_general(a, b, (((0,), (0,)), ((), ())), preferred_element_type=f32)


def _ln_rows(x, g, b):
    mu = jnp.mean(x, axis=-1, keepdims=True)
    xc = x - mu
    var = jnp.mean(xc * xc, axis=-1, keepdims=True)
    return xc * lax.rsqrt(var + LN_EPS) * g + b


def _ln_kernel(x_ref, g_ref, b_ref, o16_ref):
    o16_ref[...] = _ln_rows(x_ref[...], g_ref[...], b_ref[...]).astype(bf16)


def _layernorm_bf16(x, g, b, tm=512):
    m, d = x.shape
    return pl.pallas_call(
        _ln_kernel,
        grid=(m // tm,),
        in_specs=[pl.BlockSpec((tm, d), lambda i: (i, 0)),
                  pl.BlockSpec((1, d), lambda i: (0, 0)),
                  pl.BlockSpec((1, d), lambda i: (0, 0))],
        out_specs=pl.BlockSpec((tm, d), lambda i: (i, 0)),
        out_shape=jax.ShapeDtypeStruct((m, d), bf16),
        compiler_params=_cparams(("parallel",)),
        name="ln_embed",
    )(x, g.reshape(1, d), b.reshape(1, d))


def _mm_kernel(a_ref, w_ref, *rest, epilogue, w_transposed):
    o_ref = rest[-1]
    w = w_ref[...].astype(bf16)
    acc = _dot_nt(a_ref[...], w) if w_transposed else _dot(a_ref[...], w)
    o_ref[...] = epilogue(acc, *[r[...] for r in rest[:-1]]).astype(o_ref.dtype)


def _matmul(a, w, *, tm, tn, out_dtype, epilogue=None, row_extras=(), col_extras=(), name, wt_rows=None):
    m, k = a.shape
    if epilogue is None:
        epilogue = lambda acc: acc
    if wt_rows is None:
        n = w.shape[1]
        w_spec = pl.BlockSpec((k, tn), lambda i, j: (0, j))
    else:
        row0, n = wt_rows
        assert row0 % SUBLANES == 0 and w.shape[1] == k
        w_spec = pl.BlockSpec((pl.Element(tn), pl.Element(k)),
                              lambda i, j: ((row0 // SUBLANES + j * (tn // SUBLANES)) * SUBLANES, 0))
    in_specs = [pl.BlockSpec((tm, k), lambda i, j: (i, 0)), w_spec]
    for e in row_extras:
        in_specs.append(pl.BlockSpec((tm, e.shape[1]), lambda i, j: (i, 0)))
    for e in col_extras:
        in_specs.append(pl.BlockSpec((e.shape[0], tn), lambda i, j: (0, j)))
    return pl.pallas_call(
        functools.partial(_mm_kernel, epilogue=epilogue, w_transposed=wt_rows is not None),
        grid=(m // tm, n // tn),
        in_specs=in_specs,
        out_specs=pl.BlockSpec((tm, tn), lambda i, j: (i, j)),
        out_shape=jax.ShapeDtypeStruct((m, n), out_dtype),
        compiler_params=_cparams(("parallel", "parallel")),
        name=name,
    )(a, w, *row_extras, *col_extras)


def _rms_epilogue(acc, g):
    ms = jnp.mean(acc * acc, axis=-1, keepdims=True)
    return acc * lax.rsqrt(ms + RMS_EPS) * g


def _rope_block(blk, c, s1, s2):
    half = MLA_ROPE // 2
    return blk * c + pltpu.roll(blk, LANES - half, 1) * s1 + pltpu.roll(blk, half, 1) * s2


def _krope_epilogue(acc, c, s1, s2):
    return _rope_block(acc, c, s1, s2)


def _mla_q_kernel(cq_ref, w_ref, c_ref, s1_ref, s2_ref, o_ref):
    q = _dot(cq_ref[...], w_ref[...]) * (MLA_QK ** -0.5 * LOG2E)
    c, s1, s2 = c_ref[...], s1_ref[...], s2_ref[...]
    for h in range(MLA_HEADS):
        base = h * MLA_QK_PAD
        o_ref[:, base:base + MLA_NOPE] = q[:, base:base + MLA_NOPE].astype(bf16)
        o_ref[:, base + MLA_NOPE:base + MLA_QK_PAD] = _rope_block(
            q[:, base + MLA_NOPE:base + MLA_QK_PAD], c, s1, s2).astype(bf16)


def _mla_q(cqn, w_q, rope_c, rope_s1, rope_s2, tm=512):
    m, r = cqn.shape
    n = w_q.shape[1]
    return pl.pallas_call(
        _mla_q_kernel,
        grid=(m // tm,),
        in_specs=[pl.BlockSpec((tm, r), lambda i: (i, 0)),
                  pl.BlockSpec((r, n), lambda i: (0, 0)),
                  pl.BlockSpec((tm, LANES), lambda i: (i, 0)),
                  pl.BlockSpec((tm, LANES), lambda i: (i, 0)),
                  pl.BlockSpec((tm, LANES), lambda i: (i, 0))],
        out_specs=pl.BlockSpec((tm, n), lambda i: (i, 0)),
        out_shape=jax.ShapeDtypeStruct((m, n), bf16),
        compiler_params=_cparams(("parallel",)),
        name="mla_q_up",
    )(cqn, w_q, rope_c, rope_s1, rope_s2)


def _mla_kv_kernel(ckv_ref, w_ref, kr_ref, k_ref, v_ref):
    kv = _dot(ckv_ref[...], w_ref[...])
    kr = kr_ref[...]
    nk = MLA_HEADS * MLA_NOPE
    for h in range(MLA_HEADS):
        base = h * MLA_QK_PAD
        k_ref[:, base:base + MLA_NOPE] = kv[:, h * MLA_NOPE:(h + 1) * MLA_NOPE].astype(bf16)
        k_ref[:, base + MLA_NOPE:base + MLA_QK_PAD] = kr
    lane = lax.broadcasted_iota(jnp.int32, (kv.shape[0], MLA_V_PAD - MLA_V), 1)
    ones_col = jnp.where(lane == 0, 1.0, 0.0).astype(bf16)
    for h in range(MLA_HEADS):
        base = h * MLA_V_PAD
        v_ref[:, base:base + MLA_V] = kv[:, nk + h * MLA_V:nk + (h + 1) * MLA_V].astype(bf16)
        v_ref[:, base + MLA_V:base + MLA_V_PAD] = ones_col


def _mla_kv(ckvn, w_kv, krope, tm=512):
    m, r = ckvn.shape
    n = w_kv.shape[1]
    nk = MLA_HEADS * MLA_QK_PAD
    nv = MLA_HEADS * MLA_V_PAD
    return pl.pallas_call(
        _mla_kv_kernel,
        grid=(m // tm,),
        in_specs=[pl.BlockSpec((tm, r), lambda i: (i, 0)),
                  pl.BlockSpec((r, n), lambda i: (0, 0)),
                  pl.BlockSpec((tm, LANES), lambda i: (i, 0))],
        out_specs=[pl.BlockSpec((tm, nk), lambda i: (i, 0)),
                   pl.BlockSpec((tm, nv), lambda i: (i, 0))],
        out_shape=[jax.ShapeDtypeStruct((m, nk), bf16), jax.ShapeDtypeStruct((m, nv), bf16)],
        compiler_params=_cparams(("parallel",)),
        name="mla_kv_up",
    )(ckvn, w_kv, krope)


def _mla_attn_kernel(q_ref, k_ref, v_ref, o_ref, s_ref, *, tq, tk):
    nq = q_ref.shape[0] // tq
    nkv = k_ref.shape[0] // tk

    def scores(q, c):
        return _dot_nt(q, k_ref[c * tk:(c + 1) * tk, :])

    def update(s, c, m, acc):
        m_new = jnp.maximum(m, jnp.max(s, axis=-1, keepdims=True))
        p = jnp.exp2(s - m_new)
        acc = jnp.exp2(m - m_new) * acc + _dot(p.astype(bf16), v_ref[c * tk:(c + 1) * tk, :])
        return m_new, acc

    s_ref[...] = scores(q_ref[0:tq, :], 0)
    assert nq % MLA_QBLOCKS_PER_STEP == 0

    def body(i, carry):
        s_next = s_ref[...]
        for u in range(MLA_QBLOCKS_PER_STEP):
            blk = i * MLA_QBLOCKS_PER_STEP + u
            off = pl.multiple_of(blk * tq, tq)
            q = q_ref[pl.ds(off, tq), :]
            m = jnp.full((tq, 1), -jnp.inf, f32)
            acc = jnp.zeros((tq, MLA_V_PAD), f32)
            for c in range(nkv):
                s_cur = s_next
                if c + 1 < nkv:
                    s_next = scores(q, c + 1)
                else:
                    nxt = pl.multiple_of(jnp.minimum(blk + 1, nq - 1) * tq, tq)
                    s_next = scores(q_ref[pl.ds(nxt, tq), :], 0)
                m, acc = update(s_cur, c, m, acc)
            o_ref[pl.ds(off, tq), :] = (acc[:, :MLA_V] / acc[:, MLA_V:MLA_V + 1]).astype(o_ref.dtype)
        s_ref[...] = s_next
        return carry

    lax.fori_loop(0, nq // MLA_QBLOCKS_PER_STEP, body, 0)


def _mla_attention(q, k, v, tq=512, tk=512):
    s = q.shape[0]
    return pl.pallas_call(
        functools.partial(_mla_attn_kernel, tq=tq, tk=tk),
        grid=(MLA_HEADS,),
        in_specs=[pl.BlockSpec((s, MLA_QK_PAD), lambda h: (0, h)),
                  pl.BlockSpec((s, MLA_QK_PAD), lambda h: (0, h)),
                  pl.BlockSpec((s, MLA_V_PAD), lambda h: (0, h))],
        out_specs=pl.BlockSpec((s, MLA_V), lambda h: (0, h)),
        out_shape=jax.ShapeDtypeStruct((s, MLA_HEADS * MLA_V), bf16),
        scratch_shapes=[pltpu.VMEM((tq, tk), f32)],
        compiler_params=_cparams(("parallel",)),
        name="mla_attention",
    )(q, k, v)


def _mem_attn_kernel(q_ref, k_ref, v_ref, o_ref):
    for h in range(MEM_HEADS):
        sl = slice(h * MEM_HD, (h + 1) * MEM_HD)
        s = _dot_nt(q_ref[:, sl], k_ref[:, sl])
        m = jnp.max(s, axis=-1, keepdims=True)
        p = jnp.exp(s - m)
        l = jnp.sum(p, axis=-1, keepdims=True)
        o = _dot(p.astype(bf16), v_ref[:, sl])
        o_ref[:, sl] = (o / l).astype(o_ref.dtype)


def _mem_attention(q, k, v, tm=512):
    s, w = q.shape
    nm = k.shape[0]
    return pl.pallas_call(
        _mem_attn_kernel,
        grid=(s // tm,),
        in_specs=[pl.BlockSpec((tm, w), lambda i: (i, 0)),
                  pl.BlockSpec((nm, w), lambda i: (0, 0)),
                  pl.BlockSpec((nm, w), lambda i: (0, 0))],
        out_specs=pl.BlockSpec((tm, w), lambda i: (i, 0)),
        out_shape=jax.ShapeDtypeStruct((s, w), bf16),
        compiler_params=_cparams(("parallel",)),
        name="mem_attention",
    )(q, k, v)


def _hgrn_levels(rev):
    levels = []
    sz = HG_CHUNK // 2
    while sz >= HG_SUB:
        blocks = []
        for base in range(0, HG_CHUNK, 2 * sz):
            if not rev:
                blocks.append((base + sz, base, sz, base + sz - 1))
            else:
                blocks.append((base, base + sz, sz, base + sz))
        levels.append(blocks)
        sz //= 2
    return levels


def _hgrn_intra(q_c, b_c, c_c, bbuf, cbuf, row0, *, rev):
    C = HG_CHUNK
    nb = C // HG_SUB
    lane = lax.broadcasted_iota(jnp.int32, (HG_SUB, C), 1)
    sub = lax.broadcasted_iota(jnp.int32, (HG_SUB, C), 0)
    blocks = []
    for i in range(nb):
        r0 = i * HG_SUB
        q_i = q_c[r0:r0 + HG_SUB]
        b_i = b_c[r0:r0 + HG_SUB]
        a_blk = jnp.zeros((HG_SUB, C), f32)
        for s in range(HG_SUB):
            w = q_i * jnp.exp2(b_i + cbuf[row0 + r0 + s:row0 + r0 + s + 1, :])
            a_blk = jnp.where(lane == r0 + s, jnp.sum(w, axis=-1, keepdims=True), a_blk)
        keep = (sub >= lane - r0) if not rev else (sub <= lane - r0)
        blocks.append(jnp.where(keep, a_blk, 0.0))
    for level in _hgrn_levels(rev):
        sz = level[0][2]
        qd, kd, valid = [], [], None
        k_pos = 0
        rq = lax.broadcasted_iota(jnp.int32, (len(level) * sz, C), 0)
        ck = lax.broadcasted_iota(jnp.int32, (len(level) * sz, C), 1)
        for j, (q0, k0, _, piv_row) in enumerate(level):
            piv = bbuf[row0 + piv_row:row0 + piv_row + 1, :]
            qd.append(q_c[q0:q0 + sz] * jnp.exp2(b_c[q0:q0 + sz] - piv))
            if k0 > k_pos:
                kd.append(jnp.zeros((k0 - k_pos, q_c.shape[1]), f32))
            kd.append(jnp.exp2(c_c[k0:k0 + sz] + piv))
            k_pos = k0 + sz
            ok = (rq >= j * sz) & (rq < (j + 1) * sz) & (ck >= k0) & (ck < k0 + sz)
            valid = ok if valid is None else (valid | ok)
        if k_pos < C:
            kd.append(jnp.zeros((C - k_pos, q_c.shape[1]), f32))
        r = _dot_nt(jnp.concatenate(qd, axis=0).astype(bf16), jnp.concatenate(kd, axis=0).astype(bf16))
        if len(level) > 1:
            r = jnp.where(valid, r, 0.0)
        for j, (q0, _, _, _) in enumerate(level):
            for t in range(sz // HG_SUB):
                i = q0 // HG_SUB + t
                blocks[i] = blocks[i] + r[j * sz + t * HG_SUB:j * sz + (t + 1) * HG_SUB]
    return jnp.concatenate(blocks, axis=0)


def _hgrn_block_kernel(qf_ref, vf_ref, ff_ref, qb_ref, vb_ref, fb_ref, lb_ref, of_ref, ob_ref,
                       sf_ref, sb_ref, bbuf, cbuf):
    n = pl.program_id(1)
    C = HG_CHUNK
    T = qf_ref.shape[0]
    nchunk = T // C

    @pl.when(n == 0)
    def _():
        sf_ref[...] = jnp.zeros_like(sf_ref)
        sb_ref[...] = jnp.zeros_like(sb_ref)

    r = lax.broadcasted_iota(jnp.int32, (C, C), 0)
    c = lax.broadcasted_iota(jnp.int32, (C, C), 1)
    dirs = []
    for d, (q_ref, v_ref, f_ref, rev) in enumerate(((qf_ref, vf_ref, ff_ref, False), (qb_ref, vb_ref, fb_ref, True))):
        tri = ((c <= r) if not rev else (c >= r)).astype(bf16)
        lb = lb_ref[d:d + 1, :]
        q = jax.nn.silu(q_ref[...])
        v16 = v_ref[...].astype(bf16)
        f = lb + (1.0 - lb) * jax.nn.sigmoid(f_ref[...])
        g = jnp.log(f) * LOG2E
        g_hi = g.astype(bf16)
        g_lo = (g - g_hi.astype(f32)).astype(bf16)
        b = jnp.concatenate([_dot(tri, g_hi[j * C:(j + 1) * C]) + _dot(tri, g_lo[j * C:(j + 1) * C])
                             for j in range(nchunk)], axis=0)
        cc = jnp.log(1.0 - f) * LOG2E - b
        bbuf[d] = b
        cbuf[d] = cc
        dirs.append((q, v16, b, cc, rev))

    intra = []
    for d, (q, v16, b, cc, rev) in enumerate(dirs):
        intra.append([_hgrn_intra(q[j * C:(j + 1) * C], b[j * C:(j + 1) * C], cc[j * C:(j + 1) * C],
                                  bbuf.at[d], cbuf.at[d], j * C, rev=rev).astype(bf16) for j in range(nchunk)])

    for d, (q, v16, b, cc, rev) in enumerate(dirs):
        s_ref, o_ref = (sf_ref, of_ref) if not rev else (sb_ref, ob_ref)
        qe = (q * jnp.exp2(b)).astype(bf16)
        order = range(nchunk) if not rev else range(nchunk - 1, -1, -1)
        edge = (C - 1) if not rev else 0
        upd = {}
        for j in order:
            b_edge = bbuf[d, j * C + edge:j * C + edge + 1, :]
            kdec = jnp.exp2(cc[j * C:(j + 1) * C] + b_edge).astype(bf16)
            upd[j] = (jnp.exp2(b_edge), _dot_tn(v16[j * C:(j + 1) * C], kdec))
        state = s_ref[...]
        for j in order:
            rows = slice(j * C, (j + 1) * C)
            o = _dot_nt(qe[rows], state.astype(bf16)) + _dot(intra[d][j], v16[rows])
            o_ref[rows, :] = o.astype(o_ref.dtype)
            state = upd[j][0] * state + upd[j][1]
        s_ref[...] = state


def _hgrn_scan(proj, lb, tb=512):
    s = proj.shape[0]
    w = HG_HEADS * HG_D
    nb = s // tb
    hb = w // HG_D
    def fw(seg):
        return pl.BlockSpec((tb, HG_D), lambda h, n: (n, seg * hb + h))

    def bw(seg):
        return pl.BlockSpec((tb, HG_D), lambda h, n: (nb - 1 - n, seg * hb + h))

    return pl.pallas_call(
        _hgrn_block_kernel,
        grid=(HG_HEADS, nb),
        in_specs=[fw(0), fw(1), fw(2), bw(0), bw(1), bw(3),
                  pl.BlockSpec((2, HG_D), lambda h, n: (0, h))],
        out_specs=[pl.BlockSpec((tb, HG_D), lambda h, n: (n, h)),
                   pl.BlockSpec((tb, HG_D), lambda h, n: (nb - 1 - n, h))],
        out_shape=[jax.ShapeDtypeStruct((s, w), bf16), jax.ShapeDtypeStruct((s, w), bf16)],
        scratch_shapes=[pltpu.VMEM((HG_D, HG_D), f32), pltpu.VMEM((HG_D, HG_D), f32),
                        pltpu.VMEM((2, tb, HG_D), f32), pltpu.VMEM((2, tb, HG_D), f32)],
        compiler_params=_cparams(("parallel", "arbitrary")),
        name="hgrn_scan",
    )(proj, proj, proj, proj, proj, proj, lb)


def _lb_kernel(logit_ref, lb_ref, *, layer):
    x = logit_ref[...]
    e = jnp.exp(x - jnp.max(x, axis=1, keepdims=True))
    p = e / jnp.sum(e, axis=1, keepdims=True)
    acc = p[:, 0, :]
    for l in range(1, layer + 1):
        acc = acc + p[:, l, :]
    lb_ref[...] = acc


def _lower_bounds(logits, layer):
    z, _, w = logits.shape
    return pl.pallas_call(
        functools.partial(_lb_kernel, layer=layer),
        out_shape=jax.ShapeDtypeStruct((z, w), f32),
        name="hgrn_lower_bounds",
    )(logits)


def _merge_kernel(of_ref, ob_ref, hg_ref, ng_ref, y1_ref, y2_ref, w_ref, g0_ref, g1_ref, g2_ref, o_ref):
    acc = g1_ref[...].astype(f32) * _dot(y1_ref[...], w_ref[1])
    acc = acc + g2_ref[...].astype(f32) * _dot(y2_ref[...], w_ref[2])
    ng = ng_ref[...]
    y0 = []
    for h in range(HG_HEADS):
        sl = slice(h * HG_D, (h + 1) * HG_D)
        o = of_ref[:, sl].astype(f32) + ob_ref[:, sl].astype(f32)
        ms = jnp.mean(o * o, axis=-1, keepdims=True)
        y0.append((o * lax.rsqrt(ms + RMS_EPS) * ng * jax.nn.sigmoid(hg_ref[:, sl])).astype(bf16))
    acc = acc + g0_ref[...].astype(f32) * _dot(jnp.concatenate(y0, axis=1), w_ref[0])
    o_ref[...] = acc.astype(o_ref.dtype)


def _merge(o_f, o_b, proj_hg, norm_g, y1, y2, w_branch, gsig, tm=512):
    m, kb = y1.shape
    d = w_branch.shape[2]
    hg_gate_blk = 4
    yspec = pl.BlockSpec((tm, kb), lambda i: (i, 0))
    def gspec(b):
        return pl.BlockSpec((tm, d), lambda i: (i, b))

    return pl.pallas_call(
        _merge_kernel,
        grid=(m // tm,),
        in_specs=[yspec, yspec,
                  pl.BlockSpec((tm, kb), lambda i: (i, hg_gate_blk)),
                  pl.BlockSpec((1, HG_D), lambda i: (0, 0)),
                  yspec, yspec,
                  pl.BlockSpec((N_BRANCH, kb, d), lambda i: (0, 0, 0), pipeline_mode=pl.Buffered(1)),
                  gspec(0), gspec(1), gspec(2)],
        out_specs=pl.BlockSpec((tm, d), lambda i: (i, 0)),
        out_shape=jax.ShapeDtypeStruct((m, d), bf16),
        compiler_params=_cparams(("parallel",)),
        name="branch_merge",
    )(o_f, o_b, proj_hg, norm_g.reshape(1, HG_D), y1, y2, w_branch, gsig, gsig, gsig)


def _proj_ln_kernel(a_ref, w_ref, res_ref, rg_ref, rb_ref, g_ref, b_ref, o32_ref, o16_ref, *, res_is_raw):
    res = res_ref[...]
    if res_is_raw:
        res = _ln_rows(res, rg_ref[...], rb_ref[...])
    x = ALPHA * res + _dot(a_ref[...], w_ref[...])
    y = _ln_rows(x, g_ref[...], b_ref[...])
    o32_ref[...] = y
    o16_ref[...] = y.astype(bf16)


def _proj_residual_ln(a, w, res, res_ln, g, b, tm=512):
    m, k = a.shape
    d = w.shape[1]
    rg, rb = res_ln if res_ln is not None else (g, b)
    vec = pl.BlockSpec((1, d), lambda i: (0, 0))
    return pl.pallas_call(
        functools.partial(_proj_ln_kernel, res_is_raw=res_ln is not None),
        grid=(m // tm,),
        in_specs=[pl.BlockSpec((tm, k), lambda i: (i, 0)),
                  pl.BlockSpec((k, d), lambda i: (0, 0)),
                  pl.BlockSpec((tm, d), lambda i: (i, 0)),
                  vec, vec, vec, vec],
        out_specs=[pl.BlockSpec((tm, d), lambda i: (i, 0)),
                   pl.BlockSpec((tm, d), lambda i: (i, 0))],
        out_shape=[jax.ShapeDtypeStruct((m, d), f32), jax.ShapeDtypeStruct((m, d), bf16)],
        compiler_params=_cparams(("parallel",)),
        name="out_proj_ln",
    )(a, w, res, rg.reshape(1, d), rb.reshape(1, d), g.reshape(1, d), b.reshape(1, d))


def _ffn_kernel(x_ref, wg_ref, wu_ref, wd_ref, res_ref, g_ref, b_ref, o_ref):
    j = pl.program_id(1)

    @pl.when(j == 0)
    def _():
        o_ref[...] = jnp.zeros_like(o_ref)

    x = x_ref[...]
    hid = jax.nn.silu(_dot(x, wg_ref[...].astype(bf16))) * _dot(x, wu_ref[...].astype(bf16))
    o_ref[...] += _dot(hid.astype(bf16), wd_ref[...].astype(bf16))

    @pl.when(j == pl.num_programs(1) - 1)
    def _():
        o_ref[...] = _ln_rows(ALPHA * res_ref[...] + o_ref[...], g_ref[...], b_ref[...])


def _ffn_residual_ln(x16, x32, wg, wu, wd, g, b, tm=1024, tf=256):
    m, d = x16.shape
    ff = wg.shape[1]
    once = pl.Buffered(1)
    return pl.pallas_call(
        _ffn_kernel,
        grid=(m // tm, ff // tf),
        in_specs=[pl.BlockSpec((tm, d), lambda i, j: (i, 0)),
                  pl.BlockSpec((d, tf), lambda i, j: (0, j)),
                  pl.BlockSpec((d, tf), lambda i, j: (0, j)),
                  pl.BlockSpec((tf, d), lambda i, j: (j, 0)),
                  pl.BlockSpec((tm, d), lambda i, j: (i, 0), pipeline_mode=once),
                  pl.BlockSpec((1, d), lambda i, j: (0, 0)),
                  pl.BlockSpec((1, d), lambda i, j: (0, 0))],
        out_specs=pl.BlockSpec((tm, d), lambda i, j: (i, 0)),
        out_shape=jax.ShapeDtypeStruct((m, d), f32),
        compiler_params=_cparams(("parallel", "arbitrary")),
        name="ffn_ln",
    )(x16, wg, wu, wd, x32, g.reshape(1, d), b.reshape(1, d))


def _rope_tables(positions):
    half = MLA_ROPE // 2
    inv_freq = jnp.power(ROPE_THETA, -jnp.arange(half, dtype=f32) / half)
    ang = positions.astype(f32)[..., None] * inv_freq
    cos, sin = jnp.cos(ang), jnp.sin(ang)
    z = jnp.zeros_like(cos)
    pad = jnp.zeros((ang.shape[0], LANES - MLA_ROPE), f32)
    c = jnp.concatenate([cos, cos, pad], axis=-1)
    s1 = jnp.concatenate([-sin, z, pad], axis=-1)
    s2 = jnp.concatenate([z, sin, pad], axis=-1)
    return c, s1, s2


def _layer(res, res_ln, h16, mem16, rope, lb, w_in, hgrn_norm_g, g_cq, g_ckv, w_uq, w_ukv, w_memkv, w_branch, w_o,
           ln1_g, ln1_b, w_gate, w_up, w_down, ln2_g, ln2_b):
    s, d = res.shape
    hgw = HG_HEADS * HG_D
    q_rank = g_cq.shape[0]
    kv_rank = g_ckv.shape[0]
    memw = MEM_HEADS * MEM_HD
    o_cq = 5 * hgw
    o_ckv = o_cq + q_rank
    o_kr = o_ckv + kv_rank
    o_qm = o_kr + MLA_ROPE
    o_gt = o_qm + memw
    rope_c, rope_s1, rope_s2 = rope

    wt = jnp.swapaxes(w_in, 0, 1)

    proj_hg = _matmul(h16, wt, wt_rows=(0, o_cq), tm=2048, tn=512, out_dtype=f32, name="in_proj_hgrn")
    cqn = _matmul(h16, wt, wt_rows=(o_cq, q_rank), tm=1024, tn=q_rank, out_dtype=bf16,
                  epilogue=_rms_epilogue, col_extras=(g_cq.reshape(1, -1),), name="in_proj_cq")
    ckvn = _matmul(h16, wt, wt_rows=(o_ckv, kv_rank), tm=1024, tn=kv_rank, out_dtype=bf16,
                   epilogue=_rms_epilogue, col_extras=(g_ckv.reshape(1, -1),), name="in_proj_ckv")
    krope = _matmul(h16, wt, wt_rows=(o_kr, LANES), tm=1024, tn=LANES, out_dtype=bf16,
                    epilogue=_krope_epilogue, row_extras=(rope_c, rope_s1, rope_s2), name="in_proj_krope")
    q_mem = _matmul(h16, wt, wt_rows=(o_qm, memw), tm=1024, tn=512, out_dtype=bf16,
                    epilogue=lambda acc: acc * (MEM_HD ** -0.5), name="in_proj_qmem")
    gsig = _matmul(h16, wt, wt_rows=(o_gt, N_BRANCH * d), tm=2048, tn=512, out_dtype=bf16,
                   epilogue=jax.nn.sigmoid, name="in_proj_gates")

    o_f, o_b = _hgrn_scan(proj_hg, lb)

    zq = jnp.zeros((q_rank, MLA_HEADS, MLA_QK_PAD - MLA_QK), f32)
    w_q = jnp.concatenate([w_uq.reshape(q_rank, MLA_HEADS, MLA_QK), zq], axis=-1)
    w_q = w_q.reshape(q_rank, MLA_HEADS * MLA_QK_PAD).astype(bf16)
    w_kv3 = w_ukv.reshape(kv_rank, MLA_HEADS, MLA_NOPE + MLA_V)
    w_kv = jnp.concatenate([w_kv3[:, :, :MLA_NOPE].reshape(kv_rank, -1),
                            w_kv3[:, :, MLA_NOPE:].reshape(kv_rank, -1)], axis=-1).astype(bf16)
    q = _mla_q(cqn, w_q, rope_c, rope_s1, rope_s2)
    k, v = _mla_kv(ckvn, w_kv, krope)
    y_mla = _mla_attention(q, k, v)

    kv_mem = _matmul(mem16, w_memkv.astype(bf16), tm=mem16.shape[0], tn=512, out_dtype=bf16, name="mem_kv_proj")
    y_mem = _mem_attention(q_mem, kv_mem[:, :memw], kv_mem[:, memw:])

    merged = _merge(o_f, o_b, proj_hg, hgrn_norm_g, y_mla, y_mem, w_branch.astype(bf16), gsig)
    x32, x16 = _proj_residual_ln(merged, w_o.astype(bf16), res, res_ln, ln1_g, ln1_b)
    out32 = _ffn_residual_ln(x16, x32, w_gate, w_up, w_down, ln2_g, ln2_b)
    return out32


def kernel(x, mem, positions, ln_emb_g, ln_emb_b, hgrn_lb_logits, w_in, hgrn_norm_g, mla_g_cq, mla_g_ckv,
           mla_w_uq, mla_w_ukv, mem_w_kv, w_branch, w_o, ln1_g, ln1_b, w_ffn_gate, w_ffn_up, w_ffn_down,
           ln2_g, ln2_b):
    bsz, s, d = x.shape
    depth = w_in.shape[0]
    outs = []
    for bi in range(bsz):
        rope = _rope_tables(positions[bi])
        mem16 = mem[bi].astype(bf16)
        h16 = _layernorm_bf16(x[bi], ln_emb_g, ln_emb_b)
        res, res_ln = x[bi], (ln_emb_g, ln_emb_b)
        for l in range(depth):
            lb = _lower_bounds(hgrn_lb_logits, l)
            res = _layer(res, res_ln, h16, mem16, rope, lb, w_in[l], hgrn_norm_g[l], mla_g_cq[l], mla_g_ckv[l],
                         mla_w_uq[l], mla_w_ukv[l], mem_w_kv[l], w_branch[l], w_o[l], ln1_g[l], ln1_b[l],
                         w_ffn_gate[l], w_ffn_up[l], w_ffn_down[l], ln2_g[l], ln2_b[l])
            res_ln = None
            if l + 1 < depth:
                h16 = res.astype(bf16)
        outs.append(res)
    return jnp.stack(outs)
```

```python
import functools

import jax
import jax.numpy as jnp
from jax import lax
from jax.experimental import pallas as pl
from jax.experimental.pallas import tpu as pltpu

HG_HEADS = 8
HG_D = 128
MLA_HEADS = 8
MLA_NOPE = 128
MLA_ROPE = 64
MLA_V = 128
MLA_QK = MLA_NOPE + MLA_ROPE
MLA_QK_PAD = 256
MLA_V_PAD = 256
MLA_QBLOCKS_PER_STEP = 1
LOG2E = 1.4426950408889634
MEM_HEADS = 4
MEM_HD = 256
N_BRANCH = 3
ROPE_THETA = 10000.0
LN_EPS = 1e-5
RMS_EPS = 1e-6
DEPTH = 1
ALPHA = (2.0 * DEPTH) ** 0.25

LANES = 128
SUBLANES = 8
VMEM_LIMIT = 56 * 1024 * 1024

HG_CHUNK = 64
HG_SUB = SUBLANES

bf16 = jnp.bfloat16
f32 = jnp.float32


def _cparams(sem):
    return pltpu.CompilerParams(dimension_semantics=sem, vmem_limit_bytes=VMEM_LIMIT)


def _dot(a, b):
    return jnp.dot(a, b, preferred_element_type=f32)


def _dot_nt(a, b):
    return lax.dot_general(a, b, (((1,), (1,)), ((), ())), preferred_element_type=f32)


def _dot_tn(a, b):
    return lax.dot_general(a, b, (((0,), (0,)), ((), ())), preferred_element_type=f32)


def _ln_rows(x, g, b):
    mu = jnp.mean(x, axis=-1, keepdims=True)
    xc = x - mu
    var = jnp.mean(xc * xc, axis=-1, keepdims=True)
    return xc * lax.rsqrt(var + LN_EPS) * g + b


def _ln_kernel(x_ref, g_ref, b_ref, o16_ref):
    o16_ref[...] = _ln_rows(x_ref[...], g_ref[...], b_ref[...]).astype(bf16)


def _layernorm_bf16(x, g, b, tm=512):
    m, d = x.shape
    return pl.pallas_call(
        _ln_kernel,
        grid=(m // tm,),
        in_specs=[pl.BlockSpec((tm, d), lambda i: (i, 0)),
                  pl.BlockSpec((1, d), lambda i: (0, 0)),
                  pl.BlockSpec((1, d), lambda i: (0, 0))],
        out_specs=pl.BlockSpec((tm, d), lambda i: (i, 0)),
        out_shape=jax.ShapeDtypeStruct((m, d), bf16),
        compiler_params=_cparams(("parallel",)),
        name="ln_embed",
    )(x, g.reshape(1, d), b.reshape(1, d))


def _mm_kernel(a_ref, w_ref, *rest, epilogue, w_transposed):
    o_ref = rest[-1]
    w = w_ref[...].astype(bf16)
    acc = _dot_nt(a_ref[...], w) if w_transposed else _dot(a_ref[...], w)
    o_ref[...] = epilogue(acc, *[r[...] for r in rest[:-1]]).astype(o_ref.dtype)


def _matmul(a, w, *, tm, tn, out_dtype, epilogue=None, row_extras=(), col_extras=(), name, wt_rows=None):
    m, k = a.shape
    if epilogue is None:
        epilogue = lambda acc: acc
    if wt_rows is None:
        n = w.shape[1]
        w_spec = pl.BlockSpec((k, tn), lambda i, j: (0, j))
    else:
        row0, n = wt_rows
        assert row0 % SUBLANES == 0 and w.shape[1] == k
        w_spec = pl.BlockSpec((pl.Element(tn), pl.Element(k)),
                              lambda i, j: ((row0 // SUBLANES + j * (tn // SUBLANES)) * SUBLANES, 0))
    in_specs = [pl.BlockSpec((tm, k), lambda i, j: (i, 0)), w_spec]
    for e in row_extras:
        in_specs.append(pl.BlockSpec((tm, e.shape[1]), lambda i, j: (i, 0)))
    for e in col_extras:
        in_specs.append(pl.BlockSpec((e.shape[0], tn), lambda i, j: (0, j)))
    return pl.pallas_call(
        functools.partial(_mm_kernel, epilogue=epilogue, w_transposed=wt_rows is not None),
        grid=(m // tm, n // tn),
        in_specs=in_specs,
        out_specs=pl.BlockSpec((tm, tn), lambda i, j: (i, j)),
        out_shape=jax.ShapeDtypeStruct((m, n), out_dtype),
        compiler_params=_cparams(("parallel", "parallel")),
        name=name,
    )(a, w, *row_extras, *col_extras)


def _rms_epilogue(acc, g):
    ms = jnp.mean(acc * acc, axis=-1, keepdims=True)
    return acc * lax.rsqrt(ms + RMS_EPS) * g


def _rope_block(blk, c, s1, s2):
    half = MLA_ROPE // 2
    return blk * c + pltpu.roll(blk, LANES - half, 1) * s1 + pltpu.roll(blk, half, 1) * s2


def _krope_epilogue(acc, c, s1, s2):
    return _rope_block(acc, c, s1, s2)


IN_TN = 512


def _in_proj_rest_kernel(a_ref, w_ref, gcq_ref, gckv_ref, c_ref, s1_ref, s2_ref, o_ref, *, n_gate, n_qmem):
    j = pl.program_id(1)
    acc = _dot_nt(a_ref[...], w_ref[...].astype(bf16))
    t_cq = n_gate + n_qmem

    @pl.when(j < n_gate)
    def _():
        o_ref[...] = jax.nn.sigmoid(acc).astype(o_ref.dtype)

    @pl.when((j >= n_gate) & (j < t_cq))
    def _():
        o_ref[...] = (acc * (MEM_HD ** -0.5)).astype(o_ref.dtype)

    @pl.when(j == t_cq)
    def _():
        o_ref[...] = _rms_epilogue(acc, gcq_ref[...]).astype(o_ref.dtype)

    @pl.when(j == t_cq + 1)
    def _():
        o_ref[...] = _rms_epilogue(acc, gckv_ref[...]).astype(o_ref.dtype)

    @pl.when(j == t_cq + 2)
    def _():
        o_ref[:, :LANES] = _rope_block(acc[:, :LANES], c_ref[...], s1_ref[...], s2_ref[...]).astype(o_ref.dtype)
        o_ref[:, LANES:] = jnp.zeros((acc.shape[0], acc.shape[1] - LANES), o_ref.dtype)


def _in_proj_rest(h16, wt, offs, g_cq, g_ckv, rope, tm=2048):
    m, k = h16.shape
    o_gt, o_qm, o_cq, o_ckv, o_kr = offs
    tn = IN_TN
    n_gate = (wt.shape[0] - o_gt) // tn
    n_qmem = (o_gt - o_qm) // tn
    assert g_cq.shape[0] == tn and g_ckv.shape[0] == tn and all(o % SUBLANES == 0 for o in offs)
    n_tiles = n_gate + n_qmem + 3
    t_cq = n_gate + n_qmem

    def w_row(i, j):
        r = jnp.where(j < n_gate, o_gt // SUBLANES + j * (tn // SUBLANES),
                      jnp.where(j < t_cq, o_qm // SUBLANES + (j - n_gate) * (tn // SUBLANES),
                                jnp.where(j == t_cq, o_cq // SUBLANES,
                                          jnp.where(j == t_cq + 1, o_ckv // SUBLANES, o_kr // SUBLANES))))
        return (r * SUBLANES, 0)

    vec = pl.BlockSpec((1, tn), lambda i, j: (0, 0))
    tab = pl.BlockSpec((tm, LANES), lambda i, j: (i, 0))
    return pl.pallas_call(
        functools.partial(_in_proj_rest_kernel, n_gate=n_gate, n_qmem=n_qmem),
        grid=(m // tm, n_tiles),
        in_specs=[pl.BlockSpec((tm, k), lambda i, j: (i, 0)),
                  pl.BlockSpec((pl.Element(tn), pl.Element(k)), w_row),
                  vec, vec, tab, tab, tab],
        out_specs=pl.BlockSpec((tm, tn), lambda i, j: (i, j)),
        out_shape=jax.ShapeDtypeStruct((m, n_tiles * tn), bf16),
        compiler_params=_cparams(("parallel", "arbitrary")),
        name="in_proj_rest",
    )(h16, wt, g_cq.reshape(1, tn), g_ckv.reshape(1, tn), *rope)


def _mla_q_kernel(cq_ref, w_ref, c_ref, s1_ref, s2_ref, o_ref):
    q = _dot(cq_ref[...], w_ref[...]) * (MLA_QK ** -0.5 * LOG2E)
    c, s1, s2 = c_ref[...], s1_ref[...], s2_ref[...]
    for h in range(MLA_HEADS):
        base = h * MLA_QK_PAD
        o_ref[:, base:base + MLA_NOPE] = q[:, base:base + MLA_NOPE].astype(bf16)
        o_ref[:, base + MLA_NOPE:base + MLA_QK_PAD] = _rope_block(
            q[:, base + MLA_NOPE:base + MLA_QK_PAD], c, s1, s2).astype(bf16)


def _mla_q(pr, cq_col, w_q, rope_c, rope_s1, rope_s2, tm=512):
    m = pr.shape[0]
    r, n = w_q.shape
    assert cq_col % r == 0
    cqn = pr
    return pl.pallas_call(
        _mla_q_kernel,
        grid=(m // tm,),
        in_specs=[pl.BlockSpec((tm, r), lambda i: (i, cq_col // r)),
                  pl.BlockSpec((r, n), lambda i: (0, 0)),
                  pl.BlockSpec((tm, LANES), lambda i: (i, 0)),
                  pl.BlockSpec((tm, LANES), lambda i: (i, 0)),
                  pl.BlockSpec((tm, LANES), lambda i: (i, 0))],
        out_specs=pl.BlockSpec((tm, n), lambda i: (i, 0)),
        out_shape=jax.ShapeDtypeStruct((m, n), bf16),
        compiler_params=_cparams(("parallel",)),
        name="mla_q_up",
    )(cqn, w_q, rope_c, rope_s1, rope_s2)


def _mla_kv_kernel(ckv_ref, w_ref, kr_ref, k_ref, v_ref):
    kv = _dot(ckv_ref[...], w_ref[...])
    kr = kr_ref[...]
    nk = MLA_HEADS * MLA_NOPE
    for h in range(MLA_HEADS):
        base = h * MLA_QK_PAD
        k_ref[:, base:base + MLA_NOPE] = kv[:, h * MLA_NOPE:(h + 1) * MLA_NOPE].astype(bf16)
        k_ref[:, base + MLA_NOPE:base + MLA_QK_PAD] = kr
    lane = lax.broadcasted_iota(jnp.int32, (kv.shape[0], MLA_V_PAD - MLA_V), 1)
    ones_col = jnp.where(lane == 0, 1.0, 0.0).astype(bf16)
    for h in range(MLA_HEADS):
        base = h * MLA_V_PAD
        v_ref[:, base:base + MLA_V] = kv[:, nk + h * MLA_V:nk + (h + 1) * MLA_V].astype(bf16)
        v_ref[:, base + MLA_V:base + MLA_V_PAD] = ones_col


def _mla_kv(pr, ckv_col, kr_col, w_kv, tm=512):
    m = pr.shape[0]
    r, n = w_kv.shape
    assert ckv_col % r == 0 and kr_col % LANES == 0
    ckvn = krope = pr
    nk = MLA_HEADS * MLA_QK_PAD
    nv = MLA_HEADS * MLA_V_PAD
    return pl.pallas_call(
        _mla_kv_kernel,
        grid=(m // tm,),
        in_specs=[pl.BlockSpec((tm, r), lambda i: (i, ckv_col // r)),
                  pl.BlockSpec((r, n), lambda i: (0, 0)),
                  pl.BlockSpec((tm, LANES), lambda i: (i, kr_col // LANES))],
        out_specs=[pl.BlockSpec((tm, nk), lambda i: (i, 0)),
                   pl.BlockSpec((tm, nv), lambda i: (i, 0))],
        out_shape=[jax.ShapeDtypeStruct((m, nk), bf16), jax.ShapeDtypeStruct((m, nv), bf16)],
        compiler_params=_cparams(("parallel",)),
        name="mla_kv_up",
    )(ckvn, w_kv, krope)


def _mla_attn_kernel(q_ref, k_ref, v_ref, o_ref, s_ref, *, tq, tk):
    nq = q_ref.shape[0] // tq
    nkv = k_ref.shape[0] // tk

    def scores(q, c):
        return _dot_nt(q, k_ref[c * tk:(c + 1) * tk, :])

    def update(s, c, m, acc):
        m_new = jnp.maximum(m, jnp.max(s, axis=-1, keepdims=True))
        p = jnp.exp2(s - m_new)
        acc = jnp.exp2(m - m_new) * acc + _dot(p.astype(bf16), v_ref[c * tk:(c + 1) * tk, :])
        return m_new, acc

    s_ref[...] = scores(q_ref[0:tq, :], 0)
    assert nq % MLA_QBLOCKS_PER_STEP == 0

    def body(i, carry):
        s_next = s_ref[...]
        for u in range(MLA_QBLOCKS_PER_STEP):
            blk = i * MLA_QBLOCKS_PER_STEP + u
            off = pl.multiple_of(blk * tq, tq)
            q = q_ref[pl.ds(off, tq), :]
            m = jnp.full((tq, 1), -jnp.inf, f32)
            acc = jnp.zeros((tq, MLA_V_PAD), f32)
            for c in range(nkv):
                s_cur = s_next
                if c + 1 < nkv:
                    s_next = scores(q, c + 1)
                else:
                    nxt = pl.multiple_of(jnp.minimum(blk + 1, nq - 1) * tq, tq)
                    s_next = scores(q_ref[pl.ds(nxt, tq), :], 0)
                m, acc = update(s_cur, c, m, acc)
            o_ref[pl.ds(off, tq), :] = (acc[:, :MLA_V] / acc[:, MLA_V:MLA_V + 1]).astype(o_ref.dtype)
        s_ref[...] = s_next
        return carry

    lax.fori_loop(0, nq // MLA_QBLOCKS_PER_STEP, body, 0)


def _mla_attention(q, k, v, tq=512, tk=512):
    s = q.shape[0]
    return pl.pallas_call(
        functools.partial(_mla_attn_kernel, tq=tq, tk=tk),
        grid=(MLA_HEADS,),
        in_specs=[pl.BlockSpec((s, MLA_QK_PAD), lambda h: (0, h)),
                  pl.BlockSpec((s, MLA_QK_PAD), lambda h: (0, h)),
                  pl.BlockSpec((s, MLA_V_PAD), lambda h: (0, h))],
        out_specs=pl.BlockSpec((s, MLA_V), lambda h: (0, h)),
        out_shape=jax.ShapeDtypeStruct((s, MLA_HEADS * MLA_V), bf16),
        scratch_shapes=[pltpu.VMEM((tq, tk), f32)],
        compiler_params=_cparams(("parallel",)),
        name="mla_attention",
    )(q, k, v)


def _mem_attn_kernel(q_ref, k_ref, v_ref, o_ref):
    for h in range(MEM_HEADS):
        sl = slice(h * MEM_HD, (h + 1) * MEM_HD)
        s = _dot_nt(q_ref[:, sl], k_ref[:, sl])
        m = jnp.max(s, axis=-1, keepdims=True)
        p = jnp.exp(s - m)
        l = jnp.sum(p, axis=-1, keepdims=True)
        o = _dot(p.astype(bf16), v_ref[:, sl])
        o_ref[:, sl] = (o / l).astype(o_ref.dtype)


def _mem_attention(pr, q_col, k, v, tm=512):
    s = pr.shape[0]
    nm, w = k.shape
    assert q_col % w == 0
    q = pr
    return pl.pallas_call(
        _mem_attn_kernel,
        grid=(s // tm,),
        in_specs=[pl.BlockSpec((tm, w), lambda i: (i, q_col // w)),
                  pl.BlockSpec((nm, w), lambda i: (0, 0)),
                  pl.BlockSpec((nm, w), lambda i: (0, 0))],
        out_specs=pl.BlockSpec((tm, w), lambda i: (i, 0)),
        out_shape=jax.ShapeDtypeStruct((s, w), bf16),
        compiler_params=_cparams(("parallel",)),
        name="mem_attention",
    )(q, k, v)


def _hgrn_levels(rev):
    levels = []
    sz = HG_CHUNK // 2
    while sz >= HG_SUB:
        blocks = []
        for base in range(0, HG_CHUNK, 2 * sz):
            if not rev:
                blocks.append((base + sz, base, sz, base + sz - 1))
            else:
                blocks.append((base, base + sz, sz, base + sz))
        levels.append(blocks)
        sz //= 2
    return levels


def _hgrn_intra(q_c, b_c, c_c, bbuf, cbuf, row0, *, rev):
    C = HG_CHUNK
    nb = C // HG_SUB
    lane = lax.broadcasted_iota(jnp.int32, (HG_SUB, C), 1)
    sub = lax.broadcasted_iota(jnp.int32, (HG_SUB, C), 0)
    blocks = []
    for i in range(nb):
        r0 = i * HG_SUB
        q_i = q_c[r0:r0 + HG_SUB]
        b_i = b_c[r0:r0 + HG_SUB]
        a_blk = jnp.zeros((HG_SUB, C), f32)
        for s in range(HG_SUB):
            w = q_i * jnp.exp2(b_i + cbuf[row0 + r0 + s:row0 + r0 + s + 1, :])
            a_blk = jnp.where(lane == r0 + s, jnp.sum(w, axis=-1, keepdims=True), a_blk)
        keep = (sub >= lane - r0) if not rev else (sub <= lane - r0)
        blocks.append(jnp.where(keep, a_blk, 0.0))
    for level in _hgrn_levels(rev):
        sz = level[0][2]
        qd, kd, valid = [], [], None
        k_pos = 0
        rq = lax.broadcasted_iota(jnp.int32, (len(level) * sz, C), 0)
        ck = lax.broadcasted_iota(jnp.int32, (len(level) * sz, C), 1)
        for j, (q0, k0, _, piv_row) in enumerate(level):
            piv = bbuf[row0 + piv_row:row0 + piv_row + 1, :]
            qd.append(q_c[q0:q0 + sz] * jnp.exp2(b_c[q0:q0 + sz] - piv))
            if k0 > k_pos:
                kd.append(jnp.zeros((k0 - k_pos, q_c.shape[1]), f32))
            kd.append(jnp.exp2(c_c[k0:k0 + sz] + piv))
            k_pos = k0 + sz
            ok = (rq >= j * sz) & (rq < (j + 1) * sz) & (ck >= k0) & (ck < k0 + sz)
            valid = ok if valid is None else (valid | ok)
        if k_pos < C:
            kd.append(jnp.zeros((C - k_pos, q_c.shape[1]), f32))
        r = _dot_nt(jnp.concatenate(qd, axis=0).astype(bf16), jnp.concatenate(kd, axis=0).astype(bf16))
        if len(level) > 1:
            r = jnp.where(valid, r, 0.0)
        for j, (q0, _, _, _) in enumerate(level):
            for t in range(sz // HG_SUB):
                i = q0 // HG_SUB + t
                blocks[i] = blocks[i] + r[j * sz + t * HG_SUB:j * sz + (t + 1) * HG_SUB]
    return jnp.concatenate(blocks, axis=0)


def _hgrn_block_kernel(qf_ref, vf_ref, ff_ref, qb_ref, vb_ref, fb_ref, lb_ref, of_ref, ob_ref,
                       sf_ref, sb_ref, bbuf, cbuf):
    n = pl.program_id(1)
    C = HG_CHUNK
    T = qf_ref.shape[0]
    nchunk = T // C

    @pl.when(n == 0)
    def _():
        sf_ref[...] = jnp.zeros_like(sf_ref)
        sb_ref[...] = jnp.zeros_like(sb_ref)

    r = lax.broadcasted_iota(jnp.int32, (C, C), 0)
    c = lax.broadcasted_iota(jnp.int32, (C, C), 1)
    dirs = []
    for d, (q_ref, v_ref, f_ref, rev) in enumerate(((qf_ref, vf_ref, ff_ref, False), (qb_ref, vb_ref, fb_ref, True))):
        tri = ((c <= r) if not rev else (c >= r)).astype(bf16)
        lb = lb_ref[d:d + 1, :]
        q = jax.nn.silu(q_ref[...])
        v16 = v_ref[...].astype(bf16)
        f = lb + (1.0 - lb) * jax.nn.sigmoid(f_ref[...])
        g = jnp.log(f) * LOG2E
        g_hi = g.astype(bf16)
        g_lo = (g - g_hi.astype(f32)).astype(bf16)
        b = jnp.concatenate([_dot(tri, g_hi[j * C:(j + 1) * C]) + _dot(tri, g_lo[j * C:(j + 1) * C])
                             for j in range(nchunk)], axis=0)
        cc = jnp.log(1.0 - f) * LOG2E - b
        bbuf[d] = b
        cbuf[d] = cc
        dirs.append((q, v16, b, cc, rev))

    intra = []
    for d, (q, v16, b, cc, rev) in enumerate(dirs):
        intra.append([_hgrn_intra(q[j * C:(j + 1) * C], b[j * C:(j + 1) * C], cc[j * C:(j + 1) * C],
                                  bbuf.at[d], cbuf.at[d], j * C, rev=rev).astype(bf16) for j in range(nchunk)])

    for d, (q, v16, b, cc, rev) in enumerate(dirs):
        s_ref, o_ref = (sf_ref, of_ref) if not rev else (sb_ref, ob_ref)
        qe = (q * jnp.exp2(b)).astype(bf16)
        order = range(nchunk) if not rev else range(nchunk - 1, -1, -1)
        edge = (C - 1) if not rev else 0
        upd = {}
        for j in order:
            b_edge = bbuf[d, j * C + edge:j * C + edge + 1, :]
            kdec = jnp.exp2(cc[j * C:(j + 1) * C] + b_edge).astype(bf16)
            upd[j] = (jnp.exp2(b_edge), _dot_tn(v16[j * C:(j + 1) * C], kdec))
        state = s_ref[...]
        for j in order:
            rows = slice(j * C, (j + 1) * C)
            o = _dot_nt(qe[rows], state.astype(bf16)) + _dot(intra[d][j], v16[rows])
            o_ref[rows, :] = o.astype(o_ref.dtype)
            state = upd[j][0] * state + upd[j][1]
        s_ref[...] = state


def _hgrn_scan(proj, lb, tb=512):
    s = proj.shape[0]
    w = HG_HEADS * HG_D
    nb = s // tb
    hb = w // HG_D
    def fw(seg):
        return pl.BlockSpec((tb, HG_D), lambda h, n: (n, seg * hb + h))

    def bw(seg):
        return pl.BlockSpec((tb, HG_D), lambda h, n: (nb - 1 - n, seg * hb + h))

    return pl.pallas_call(
        _hgrn_block_kernel,
        grid=(HG_HEADS, nb),
        in_specs=[fw(0), fw(1), fw(2), bw(0), bw(1), bw(3),
                  pl.BlockSpec((2, HG_D), lambda h, n: (0, h))],
        out_specs=[pl.BlockSpec((tb, HG_D), lambda h, n: (n, h)),
                   pl.BlockSpec((tb, HG_D), lambda h, n: (nb - 1 - n, h))],
        out_shape=[jax.ShapeDtypeStruct((s, w), bf16), jax.ShapeDtypeStruct((s, w), bf16)],
        scratch_shapes=[pltpu.VMEM((HG_D, HG_D), f32), pltpu.VMEM((HG_D, HG_D), f32),
                        pltpu.VMEM((2, tb, HG_D), f32), pltpu.VMEM((2, tb, HG_D), f32)],
        compiler_params=_cparams(("parallel", "arbitrary")),
        name="hgrn_scan",
    )(proj, proj, proj, proj, proj, proj, lb)


def _lb_kernel(logit_ref, lb_ref, *, layer):
    x = logit_ref[...]
    e = jnp.exp(x - jnp.max(x, axis=1, keepdims=True))
    p = e / jnp.sum(e, axis=1, keepdims=True)
    acc = p[:, 0, :]
    for l in range(1, layer + 1):
        acc = acc + p[:, l, :]
    lb_ref[...] = acc


def _lower_bounds(logits, layer):
    z, _, w = logits.shape
    return pl.pallas_call(
        functools.partial(_lb_kernel, layer=layer),
        out_shape=jax.ShapeDtypeStruct((z, w), f32),
        name="hgrn_lower_bounds",
    )(logits)


def _merge_kernel(of_ref, ob_ref, hg_ref, ng_ref, y1_ref, y2_ref, w_ref, g0_ref, g1_ref, g2_ref, o_ref):
    acc = g1_ref[...].astype(f32) * _dot(y1_ref[...], w_ref[1])
    acc = acc + g2_ref[...].astype(f32) * _dot(y2_ref[...], w_ref[2])
    ng = ng_ref[...]
    y0 = []
    for h in range(HG_HEADS):
        sl = slice(h * HG_D, (h + 1) * HG_D)
        o = of_ref[:, sl].astype(f32) + ob_ref[:, sl].astype(f32)
        ms = jnp.mean(o * o, axis=-1, keepdims=True)
        y0.append((o * lax.rsqrt(ms + RMS_EPS) * ng * jax.nn.sigmoid(hg_ref[:, sl])).astype(bf16))
    acc = acc + g0_ref[...].astype(f32) * _dot(jnp.concatenate(y0, axis=1), w_ref[0])
    o_ref[...] = acc.astype(o_ref.dtype)


def _merge(o_f, o_b, proj_hg, norm_g, y1, y2, w_branch, gsig, tm=512):
    m, kb = y1.shape
    d = w_branch.shape[2]
    hg_gate_blk = 4
    yspec = pl.BlockSpec((tm, kb), lambda i: (i, 0))
    def gspec(b):
        return pl.BlockSpec((tm, d), lambda i: (i, b))

    return pl.pallas_call(
        _merge_kernel,
        grid=(m // tm,),
        in_specs=[yspec, yspec,
                  pl.BlockSpec((tm, kb), lambda i: (i, hg_gate_blk)),
                  pl.BlockSpec((1, HG_D), lambda i: (0, 0)),
                  yspec, yspec,
                  pl.BlockSpec((N_BRANCH, kb, d), lambda i: (0, 0, 0), pipeline_mode=pl.Buffered(1)),
                  gspec(0), gspec(1), gspec(2)],
        out_specs=pl.BlockSpec((tm, d), lambda i: (i, 0)),
        out_shape=jax.ShapeDtypeStruct((m, d), bf16),
        compiler_params=_cparams(("parallel",)),
        name="branch_merge",
    )(o_f, o_b, proj_hg, norm_g.reshape(1, HG_D), y1, y2, w_branch, gsig, gsig, gsig)


def _proj_ln_kernel(a_ref, w_ref, res_ref, rg_ref, rb_ref, g_ref, b_ref, o32_ref, o16_ref, *, res_is_raw):
    half = a_ref.shape[0] // 2
    proj = [_dot(a_ref[r * half:(r + 1) * half, :], w_ref[...]) for r in range(2)]
    for r in range(2):
        rows = slice(r * half, (r + 1) * half)
        res = res_ref[rows, :]
        if res_is_raw:
            res = _ln_rows(res, rg_ref[...], rb_ref[...])
        y = _ln_rows(ALPHA * res + proj[r], g_ref[...], b_ref[...])
        o32_ref[rows, :] = y
        o16_ref[rows, :] = y.astype(bf16)


def _proj_residual_ln(a, w, res, res_ln, g, b, tm=512):
    m, k = a.shape
    d = w.shape[1]
    rg, rb = res_ln if res_ln is not None else (g, b)
    vec = pl.BlockSpec((1, d), lambda i: (0, 0))
    return pl.pallas_call(
        functools.partial(_proj_ln_kernel, res_is_raw=res_ln is not None),
        grid=(m // tm,),
        in_specs=[pl.BlockSpec((tm, k), lambda i: (i, 0)),
                  pl.BlockSpec((k, d), lambda i: (0, 0)),
                  pl.BlockSpec((tm, d), lambda i: (i, 0)),
                  vec, vec, vec, vec],
        out_specs=[pl.BlockSpec((tm, d), lambda i: (i, 0)),
                   pl.BlockSpec((tm, d), lambda i: (i, 0))],
        out_shape=[jax.ShapeDtypeStruct((m, d), f32), jax.ShapeDtypeStruct((m, d), bf16)],
        compiler_params=_cparams(("parallel",)),
        name="out_proj_ln",
    )(a, w, res, rg.reshape(1, d), rb.reshape(1, d), g.reshape(1, d), b.reshape(1, d))


def _ffn_kernel(x_ref, wg_ref, wu_ref, wd_ref, res_ref, g_ref, b_ref, o_ref):
    j = pl.program_id(1)

    @pl.when(j == 0)
    def _():
        o_ref[...] = jnp.zeros_like(o_ref)

    x = x_ref[...]
    hid = jax.nn.silu(_dot(x, wg_ref[...].astype(bf16))) * _dot(x, wu_ref[...].astype(bf16))
    o_ref[...] += _dot(hid.astype(bf16), wd_ref[...].astype(bf16))

    @pl.when(j == pl.num_programs(1) - 1)
    def _():
        o_ref[...] = _ln_rows(ALPHA * res_ref[...] + o_ref[...], g_ref[...], b_ref[...])


def _ffn_residual_ln(x16, x32, wg, wu, wd, g, b, tm=1024, tf=256):
    m, d = x16.shape
    ff = wg.shape[1]
    once = pl.Buffered(1)
    return pl.pallas_call(
        _ffn_kernel,
        grid=(m // tm, ff // tf),
        in_specs=[pl.BlockSpec((tm, d), lambda i, j: (i, 0)),
                  pl.BlockSpec((d, tf), lambda i, j: (0, j)),
                  pl.BlockSpec((d, tf), lambda i, j: (0, j)),
                  pl.BlockSpec((tf, d), lambda i, j: (j, 0)),
                  pl.BlockSpec((tm, d), lambda i, j: (i, 0), pipeline_mode=once),
                  pl.BlockSpec((1, d), lambda i, j: (0, 0)),
                  pl.BlockSpec((1, d), lambda i, j: (0, 0))],
        out_specs=pl.BlockSpec((tm, d), lambda i, j: (i, 0)),
        out_shape=jax.ShapeDtypeStruct((m, d), f32),
        compiler_params=_cparams(("parallel", "arbitrary")),
        name="ffn_ln",
    )(x16, wg, wu, wd, x32, g.reshape(1, d), b.reshape(1, d))


def _rope_tables(positions):
    half = MLA_ROPE // 2
    inv_freq = jnp.power(ROPE_THETA, -jnp.arange(half, dtype=f32) / half)
    ang = positions.astype(f32)[..., None] * inv_freq
    cos, sin = jnp.cos(ang), jnp.sin(ang)
    z = jnp.zeros_like(cos)
    pad = jnp.zeros((ang.shape[0], LANES - MLA_ROPE), f32)
    c = jnp.concatenate([cos, cos, pad], axis=-1)
    s1 = jnp.concatenate([-sin, z, pad], axis=-1)
    s2 = jnp.concatenate([z, sin, pad], axis=-1)
    return c, s1, s2


def _layer(res, res_ln, h16, mem16, rope, lb, w_in, hgrn_norm_g, g_cq, g_ckv, w_uq, w_ukv, w_memkv, w_branch, w_o,
           ln1_g, ln1_b, w_gate, w_up, w_down, ln2_g, ln2_b):
    s, d = res.shape
    hgw = HG_HEADS * HG_D
    q_rank = g_cq.shape[0]
    kv_rank = g_ckv.shape[0]
    memw = MEM_HEADS * MEM_HD
    o_cq = 5 * hgw
    o_ckv = o_cq + q_rank
    o_kr = o_ckv + kv_rank
    o_qm = o_kr + MLA_ROPE
    o_gt = o_qm + memw
    rope_c, rope_s1, rope_s2 = rope

    wt = jnp.swapaxes(w_in, 0, 1)

    proj_hg = _matmul(h16, wt, wt_rows=(0, o_cq), tm=2048, tn=512, out_dtype=f32, name="in_proj_hgrn")
    pr = _in_proj_rest(h16, wt, (o_gt, o_qm, o_cq, o_ckv, o_kr), g_cq, g_ckv, rope)
    c_qm = N_BRANCH * d
    c_cq = c_qm + memw
    c_ckv = c_cq + q_rank
    c_kr = c_ckv + kv_rank

    o_f, o_b = _hgrn_scan(proj_hg, lb)

    zq = jnp.zeros((q_rank, MLA_HEADS, MLA_QK_PAD - MLA_QK), f32)
    w_q = jnp.concatenate([w_uq.reshape(q_rank, MLA_HEADS, MLA_QK), zq], axis=-1)
    w_q = w_q.reshape(q_rank, MLA_HEADS * MLA_QK_PAD).astype(bf16)
    w_kv3 = w_ukv.reshape(kv_rank, MLA_HEADS, MLA_NOPE + MLA_V)
    w_kv = jnp.concatenate([w_kv3[:, :, :MLA_NOPE].reshape(kv_rank, -1),
                            w_kv3[:, :, MLA_NOPE:].reshape(kv_rank, -1)], axis=-1).astype(bf16)
    q = _mla_q(pr, c_cq, w_q, rope_c, rope_s1, rope_s2)
    k, v = _mla_kv(pr, c_ckv, c_kr, w_kv)
    y_mla = _mla_attention(q, k, v)

    kv_mem = _matmul(mem16, w_memkv.astype(bf16), tm=mem16.shape[0], tn=512, out_dtype=bf16, name="mem_kv_proj")
    y_mem = _mem_attention(pr, c_qm, kv_mem[:, :memw], kv_mem[:, memw:])

    merged = _merge(o_f, o_b, proj_hg, hgrn_norm_g, y_mla, y_mem, w_branch.astype(bf16), pr)
    x32, x16 = _proj_residual_ln(merged, w_o.astype(bf16), res, res_ln, ln1_g, ln1_b)
    out32 = _ffn_residual_ln(x16, x32, w_gate, w_up, w_down, ln2_g, ln2_b)
    return out32


def kernel(x, mem, positions, ln_emb_g, ln_emb_b, hgrn_lb_logits, w_in, hgrn_norm_g, mla_g_cq, mla_g_ckv,
           mla_w_uq, mla_w_ukv, mem_w_kv, w_branch, w_o, ln1_g, ln1_b, w_ffn_gate, w_ffn_up, w_ffn_down,
           ln2_g, ln2_b):
    bsz, s, d = x.shape
    depth = w_in.shape[0]
    outs = []
    for bi in range(bsz):
        rope = _rope_tables(positions[bi])
        mem16 = mem[bi].astype(bf16)
        h16 = _layernorm_bf16(x[bi], ln_emb_g, ln_emb_b)
        res, res_ln = x[bi], (ln_emb_g, ln_emb_b)
        for l in range(depth):
            lb = _lower_bounds(hgrn_lb_logits, l)
            res = _layer(res, res_ln, h16, mem16, rope, lb, w_in[l], hgrn_norm_g[l], mla_g_cq[l], mla_g_ckv[l],
                         mla_w_uq[l], mla_w_ukv[l], mem_w_kv[l], w_branch[l], w_o[l], ln1_g[l], ln1_b[l],
                         w_ffn_gate[l], w_ffn_up[l], w_ffn_down[l], ln2_g[l], ln2_b[l])
            res_ln = None
            if l + 1 < depth:
                h16 = res.astype(bf16)
        outs.append(res)
    return jnp.stack(outs)
```

```python
import functools

import jax
import jax.numpy as jnp
from jax import lax
from jax.experimental import pallas as pl
from jax.experimental.pallas import tpu as pltpu

HG_HEADS = 8
HG_D = 128
MLA_HEADS = 8
MLA_NOPE = 128
MLA_ROPE = 64
MLA_V = 128
MLA_QK = MLA_NOPE + MLA_ROPE
MLA_QK_PAD = 256
MLA_V_PAD = 256
MLA_QBLOCKS_PER_STEP = 1
LOG2E = 1.4426950408889634
MEM_HEADS = 4
MEM_HD = 256
N_BRANCH = 3
ROPE_THETA = 10000.0
LN_EPS = 1e-5
RMS_EPS = 1e-6
DEPTH = 1
ALPHA = (2.0 * DEPTH) ** 0.25

LANES = 128
SUBLANES = 8
VMEM_LIMIT = 56 * 1024 * 1024

HG_CHUNK = 64
HG_SUB = SUBLANES

bf16 = jnp.bfloat16
f32 = jnp.float32


def _cparams(sem):
    return pltpu.CompilerParams(dimension_semantics=sem, vmem_limit_bytes=VMEM_LIMIT)


def _dot(a, b):
    return jnp.dot(a, b, preferred_element_type=f32)


def _dot_nt(a, b):
    return lax.dot_general(a, b, (((1,), (1,)), ((), ())), preferred_element_type=f32)


def _dot_tn(a, b):
    return lax.dot_general(a, b, (((0,), (0,)), ((), ())), preferred_element_type=f32)


def _ln_rows(x, g, b):
    mu = jnp.mean(x, axis=-1, keepdims=True)
    xc = x - mu
    var = jnp.mean(xc * xc, axis=-1, keepdims=True)
    return xc * lax.rsqrt(var + LN_EPS) * g + b


def _ln_kernel(x_ref, g_ref, b_ref, o16_ref):
    o16_ref[...] = _ln_rows(x_ref[...], g_ref[...], b_ref[...]).astype(bf16)


def _layernorm_bf16(x, g, b, tm=512):
    m, d = x.shape
    return pl.pallas_call(
        _ln_kernel,
        grid=(m // tm,),
        in_specs=[pl.BlockSpec((tm, d), lambda i: (i, 0)),
                  pl.BlockSpec((1, d), lambda i: (0, 0)),
                  pl.BlockSpec((1, d), lambda i: (0, 0))],
        out_specs=pl.BlockSpec((tm, d), lambda i: (i, 0)),
        out_shape=jax.ShapeDtypeStruct((m, d), bf16),
        compiler_params=_cparams(("parallel",)),
        name="ln_embed",
    )(x, g.reshape(1, d), b.reshape(1, d))


def _mm_kernel(a_ref, w_ref, *rest, epilogue, w_transposed):
    o_ref = rest[-1]
    w = w_ref[...].astype(bf16)
    acc = _dot_nt(a_ref[...], w) if w_transposed else _dot(a_ref[...], w)
    o_ref[...] = epilogue(acc, *[r[...] for r in rest[:-1]]).astype(o_ref.dtype)


def _matmul(a, w, *, tm, tn, out_dtype, epilogue=None, row_extras=(), col_extras=(), name, wt_rows=None):
    m, k = a.shape
    if epilogue is None:
        epilogue = lambda acc: acc
    if wt_rows is None:
        n = w.shape[1]
        w_spec = pl.BlockSpec((k, tn), lambda i, j: (0, j))
    else:
        row0, n = wt_rows
        assert row0 % SUBLANES == 0 and w.shape[1] == k
        w_spec = pl.BlockSpec((pl.Element(tn), pl.Element(k)),
                              lambda i, j: ((row0 // SUBLANES + j * (tn // SUBLANES)) * SUBLANES, 0))
    in_specs = [pl.BlockSpec((tm, k), lambda i, j: (i, 0)), w_spec]
    for e in row_extras:
        in_specs.append(pl.BlockSpec((tm, e.shape[1]), lambda i, j: (i, 0)))
    for e in col_extras:
        in_specs.append(pl.BlockSpec((e.shape[0], tn), lambda i, j: (0, j)))
    return pl.pallas_call(
        functools.partial(_mm_kernel, epilogue=epilogue, w_transposed=wt_rows is not None),
        grid=(m // tm, n // tn),
        in_specs=in_specs,
        out_specs=pl.BlockSpec((tm, tn), lambda i, j: (i, j)),
        out_shape=jax.ShapeDtypeStruct((m, n), out_dtype),
        compiler_params=_cparams(("parallel", "parallel")),
        name=name,
    )(a, w, *row_extras, *col_extras)


def _sigmoid(x):
    return 0.5 * jnp.tanh(0.5 * x) + 0.5


def _silu(x):
    return x * _sigmoid(x)


def _rms_epilogue(acc, g):
    ms = jnp.mean(acc * acc, axis=-1, keepdims=True)
    return acc * lax.rsqrt(ms + RMS_EPS) * g


def _rope_block(blk, c, s1, s2):
    half = MLA_ROPE // 2
    return blk * c + pltpu.roll(blk, LANES - half, 1) * s1 + pltpu.roll(blk, half, 1) * s2


def _krope_epilogue(acc, c, s1, s2):
    return _rope_block(acc, c, s1, s2)


IN_TN = 512


def _in_proj_rest_kernel(a_ref, w_ref, gcq_ref, gckv_ref, c_ref, s1_ref, s2_ref, o_ref, *, n_gate, n_qmem):
    j = pl.program_id(1)
    acc = _dot_nt(a_ref[...], w_ref[...].astype(bf16))
    t_cq = n_gate + n_qmem

    @pl.when(j < n_gate)
    def _():
        o_ref[...] = _sigmoid(acc).astype(o_ref.dtype)

    @pl.when((j >= n_gate) & (j < t_cq))
    def _():
        o_ref[...] = (acc * (MEM_HD ** -0.5)).astype(o_ref.dtype)

    @pl.when(j == t_cq)
    def _():
        o_ref[...] = _rms_epilogue(acc, gcq_ref[...]).astype(o_ref.dtype)

    @pl.when(j == t_cq + 1)
    def _():
        o_ref[...] = _rms_epilogue(acc, gckv_ref[...]).astype(o_ref.dtype)

    @pl.when(j == t_cq + 2)
    def _():
        o_ref[:, :LANES] = _rope_block(acc[:, :LANES], c_ref[...], s1_ref[...], s2_ref[...]).astype(o_ref.dtype)
        o_ref[:, LANES:] = jnp.zeros((acc.shape[0], acc.shape[1] - LANES), o_ref.dtype)


def _in_proj_rest(h16, wt, offs, gate_end, g_cq, g_ckv, rope, tm=2048):
    m, k = h16.shape
    o_gt, o_qm, o_cq, o_ckv, o_kr = offs
    tn = IN_TN
    n_gate = (gate_end - o_gt) // tn
    n_qmem = (o_gt - o_qm) // tn
    assert g_cq.shape[0] == tn and g_ckv.shape[0] == tn and all(o % SUBLANES == 0 for o in offs)
    n_tiles = n_gate + n_qmem + 3
    t_cq = n_gate + n_qmem

    def w_row(i, j):
        r = jnp.where(j < n_gate, o_gt // SUBLANES + j * (tn // SUBLANES),
                      jnp.where(j < t_cq, o_qm // SUBLANES + (j - n_gate) * (tn // SUBLANES),
                                jnp.where(j == t_cq, o_cq // SUBLANES,
                                          jnp.where(j == t_cq + 1, o_ckv // SUBLANES, o_kr // SUBLANES))))
        return (r * SUBLANES, 0)

    vec = pl.BlockSpec((1, tn), lambda i, j: (0, 0))
    tab = pl.BlockSpec((tm, LANES), lambda i, j: (i, 0))
    return pl.pallas_call(
        functools.partial(_in_proj_rest_kernel, n_gate=n_gate, n_qmem=n_qmem),
        grid=(m // tm, n_tiles),
        in_specs=[pl.BlockSpec((tm, k), lambda i, j: (i, 0)),
                  pl.BlockSpec((pl.Element(tn), pl.Element(k)), w_row),
                  vec, vec, tab, tab, tab],
        out_specs=pl.BlockSpec((tm, tn), lambda i, j: (i, j)),
        out_shape=jax.ShapeDtypeStruct((m, n_tiles * tn), bf16),
        compiler_params=_cparams(("parallel", "arbitrary")),
        name="in_proj_rest",
    )(h16, wt, g_cq.reshape(1, tn), g_ckv.reshape(1, tn), *rope)


def _mla_q_kernel(cq_ref, w_ref, c_ref, s1_ref, s2_ref, o_ref):
    q = _dot(cq_ref[...], w_ref[...]) * (MLA_QK ** -0.5 * LOG2E)
    c, s1, s2 = c_ref[...], s1_ref[...], s2_ref[...]
    for h in range(MLA_HEADS):
        base = h * MLA_QK_PAD
        o_ref[:, base:base + MLA_NOPE] = q[:, base:base + MLA_NOPE].astype(bf16)
        o_ref[:, base + MLA_NOPE:base + MLA_QK_PAD] = _rope_block(
            q[:, base + MLA_NOPE:base + MLA_QK_PAD], c, s1, s2).astype(bf16)


def _mla_q(pr, cq_col, w_q, rope_c, rope_s1, rope_s2, tm=512):
    m = pr.shape[0]
    r, n = w_q.shape
    assert cq_col % r == 0
    cqn = pr
    return pl.pallas_call(
        _mla_q_kernel,
        grid=(m // tm,),
        in_specs=[pl.BlockSpec((tm, r), lambda i: (i, cq_col // r)),
                  pl.BlockSpec((r, n), lambda i: (0, 0)),
                  pl.BlockSpec((tm, LANES), lambda i: (i, 0)),
                  pl.BlockSpec((tm, LANES), lambda i: (i, 0)),
                  pl.BlockSpec((tm, LANES), lambda i: (i, 0))],
        out_specs=pl.BlockSpec((tm, n), lambda i: (i, 0)),
        out_shape=jax.ShapeDtypeStruct((m, n), bf16),
        compiler_params=_cparams(("parallel",)),
        name="mla_q_up",
    )(cqn, w_q, rope_c, rope_s1, rope_s2)


def _mla_kv_kernel(ckv_ref, w_ref, kr_ref, k_ref, v_ref):
    kv = _dot(ckv_ref[...], w_ref[...])
    kr = kr_ref[...]
    nk = MLA_HEADS * MLA_NOPE
    for h in range(MLA_HEADS):
        base = h * MLA_QK_PAD
        k_ref[:, base:base + MLA_NOPE] = kv[:, h * MLA_NOPE:(h + 1) * MLA_NOPE].astype(bf16)
        k_ref[:, base + MLA_NOPE:base + MLA_QK_PAD] = kr
    lane = lax.broadcasted_iota(jnp.int32, (kv.shape[0], MLA_V_PAD - MLA_V), 1)
    ones_col = jnp.where(lane == 0, 1.0, 0.0).astype(bf16)
    for h in range(MLA_HEADS):
        base = h * MLA_V_PAD
        v_ref[:, base:base + MLA_V] = kv[:, nk + h * MLA_V:nk + (h + 1) * MLA_V].astype(bf16)
        v_ref[:, base + MLA_V:base + MLA_V_PAD] = ones_col


def _mla_kv(pr, ckv_col, kr_col, w_kv, tm=512):
    m = pr.shape[0]
    r, n = w_kv.shape
    assert ckv_col % r == 0 and kr_col % LANES == 0
    ckvn = krope = pr
    nk = MLA_HEADS * MLA_QK_PAD
    nv = MLA_HEADS * MLA_V_PAD
    return pl.pallas_call(
        _mla_kv_kernel,
        grid=(m // tm,),
        in_specs=[pl.BlockSpec((tm, r), lambda i: (i, ckv_col // r)),
                  pl.BlockSpec((r, n), lambda i: (0, 0)),
                  pl.BlockSpec((tm, LANES), lambda i: (i, kr_col // LANES))],
        out_specs=[pl.BlockSpec((tm, nk), lambda i: (i, 0)),
                   pl.BlockSpec((tm, nv), lambda i: (i, 0))],
        out_shape=[jax.ShapeDtypeStruct((m, nk), bf16), jax.ShapeDtypeStruct((m, nv), bf16)],
        compiler_params=_cparams(("parallel",)),
        name="mla_kv_up",
    )(ckvn, w_kv, krope)


def _mla_attn_kernel(q_ref, k_ref, v_ref, o_ref, s_ref, *, tq, tk):
    nq = q_ref.shape[0] // tq
    nkv = k_ref.shape[0] // tk

    def scores(q, c):
        return _dot_nt(q, k_ref[c * tk:(c + 1) * tk, :])

    def update(s, c, m, acc):
        m_new = jnp.maximum(m, jnp.max(s, axis=-1, keepdims=True))
        p = jnp.exp2(s - m_new)
        acc = jnp.exp2(m - m_new) * acc + _dot(p.astype(bf16), v_ref[c * tk:(c + 1) * tk, :])
        return m_new, acc

    s_ref[...] = scores(q_ref[0:tq, :], 0)
    assert nq % MLA_QBLOCKS_PER_STEP == 0

    def body(i, carry):
        s_next = s_ref[...]
        for u in range(MLA_QBLOCKS_PER_STEP):
            blk = i * MLA_QBLOCKS_PER_STEP + u
            off = pl.multiple_of(blk * tq, tq)
            q = q_ref[pl.ds(off, tq), :]
            m = jnp.full((tq, 1), -jnp.inf, f32)
            acc = jnp.zeros((tq, MLA_V_PAD), f32)
            for c in range(nkv):
                s_cur = s_next
                if c + 1 < nkv:
                    s_next = scores(q, c + 1)
                else:
                    nxt = pl.multiple_of(jnp.minimum(blk + 1, nq - 1) * tq, tq)
                    s_next = scores(q_ref[pl.ds(nxt, tq), :], 0)
                m, acc = update(s_cur, c, m, acc)
            o_ref[pl.ds(off, tq), :] = (acc[:, :MLA_V] / acc[:, MLA_V:MLA_V + 1]).astype(o_ref.dtype)
        s_ref[...] = s_next
        return carry

    lax.fori_loop(0, nq // MLA_QBLOCKS_PER_STEP, body, 0)


def _mla_attention(q, k, v, tq=512, tk=512):
    s = q.shape[0]
    return pl.pallas_call(
        functools.partial(_mla_attn_kernel, tq=tq, tk=tk),
        grid=(MLA_HEADS,),
        in_specs=[pl.BlockSpec((s, MLA_QK_PAD), lambda h: (0, h)),
                  pl.BlockSpec((s, MLA_QK_PAD), lambda h: (0, h)),
                  pl.BlockSpec((s, MLA_V_PAD), lambda h: (0, h))],
        out_specs=pl.BlockSpec((s, MLA_V), lambda h: (0, h)),
        out_shape=jax.ShapeDtypeStruct((s, MLA_HEADS * MLA_V), bf16),
        scratch_shapes=[pltpu.VMEM((tq, tk), f32)],
        compiler_params=_cparams(("parallel",)),
        name="mla_attention",
    )(q, k, v)


def _mem_attn_kernel(q_ref, k_ref, v_ref, o_ref):
    for h in range(MEM_HEADS):
        sl = slice(h * MEM_HD, (h + 1) * MEM_HD)
        s = _dot_nt(q_ref[:, sl], k_ref[:, sl])
        m = jnp.max(s, axis=-1, keepdims=True)
        p = jnp.exp(s - m)
        l = jnp.sum(p, axis=-1, keepdims=True)
        o = _dot(p.astype(bf16), v_ref[:, sl])
        o_ref[:, sl] = (o / l).astype(o_ref.dtype)


def _mem_attention(pr, q_col, k, v, tm=512):
    s = pr.shape[0]
    nm, w = k.shape
    assert q_col % w == 0
    q = pr
    return pl.pallas_call(
        _mem_attn_kernel,
        grid=(s // tm,),
        in_specs=[pl.BlockSpec((tm, w), lambda i: (i, q_col // w)),
                  pl.BlockSpec((nm, w), lambda i: (0, 0)),
                  pl.BlockSpec((nm, w), lambda i: (0, 0))],
        out_specs=pl.BlockSpec((tm, w), lambda i: (i, 0)),
        out_shape=jax.ShapeDtypeStruct((s, w), bf16),
        compiler_params=_cparams(("parallel",)),
        name="mem_attention",
    )(q, k, v)


def _hgrn_levels(rev):
    levels = []
    sz = HG_CHUNK // 2
    while sz >= HG_SUB:
        blocks = []
        for base in range(0, HG_CHUNK, 2 * sz):
            if not rev:
                blocks.append((base + sz, base, sz, base + sz - 1))
            else:
                blocks.append((base, base + sz, sz, base + sz))
        levels.append(blocks)
        sz //= 2
    return levels


def _hgrn_intra(q_c, b_c, c_c, bbuf, cbuf, row0, *, rev):
    C = HG_CHUNK
    nb = C // HG_SUB
    lane = lax.broadcasted_iota(jnp.int32, (HG_SUB, C), 1)
    sub = lax.broadcasted_iota(jnp.int32, (HG_SUB, C), 0)
    blocks = []
    for i in range(nb):
        r0 = i * HG_SUB
        q_i = q_c[r0:r0 + HG_SUB]
        b_i = b_c[r0:r0 + HG_SUB]
        a_blk = jnp.zeros((HG_SUB, C), f32)
        for s in range(HG_SUB):
            w = q_i * jnp.exp2(b_i + cbuf[row0 + r0 + s:row0 + r0 + s + 1, :])
            a_blk = jnp.where(lane == r0 + s, jnp.sum(w, axis=-1, keepdims=True), a_blk)
        keep = (sub >= lane - r0) if not rev else (sub <= lane - r0)
        blocks.append(jnp.where(keep, a_blk, 0.0))
    for level in _hgrn_levels(rev):
        sz = level[0][2]
        qd, kd, valid = [], [], None
        k_pos = 0
        rq = lax.broadcasted_iota(jnp.int32, (len(level) * sz, C), 0)
        ck = lax.broadcasted_iota(jnp.int32, (len(level) * sz, C), 1)
        for j, (q0, k0, _, piv_row) in enumerate(level):
            piv = bbuf[row0 + piv_row:row0 + piv_row + 1, :]
            qd.append(q_c[q0:q0 + sz] * jnp.exp2(b_c[q0:q0 + sz] - piv))
            if k0 > k_pos:
                kd.append(jnp.zeros((k0 - k_pos, q_c.shape[1]), f32))
            kd.append(jnp.exp2(c_c[k0:k0 + sz] + piv))
            k_pos = k0 + sz
            ok = (rq >= j * sz) & (rq < (j + 1) * sz) & (ck >= k0) & (ck < k0 + sz)
            valid = ok if valid is None else (valid | ok)
        if k_pos < C:
            kd.append(jnp.zeros((C - k_pos, q_c.shape[1]), f32))
        r = _dot_nt(jnp.concatenate(qd, axis=0).astype(bf16), jnp.concatenate(kd, axis=0).astype(bf16))
        if len(level) > 1:
            r = jnp.where(valid, r, 0.0)
        for j, (q0, _, _, _) in enumerate(level):
            for t in range(sz // HG_SUB):
                i = q0 // HG_SUB + t
                blocks[i] = blocks[i] + r[j * sz + t * HG_SUB:j * sz + (t + 1) * HG_SUB]
    return jnp.concatenate(blocks, axis=0)


def _hgrn_block_kernel(qf_ref, vf_ref, ff_ref, qb_ref, vb_ref, fb_ref, lb_ref, of_ref, ob_ref,
                       sf_ref, sb_ref, bbuf, cbuf):
    n = pl.program_id(1)
    C = HG_CHUNK
    T = qf_ref.shape[0]
    nchunk = T // C

    @pl.when(n == 0)
    def _():
        sf_ref[...] = jnp.zeros_like(sf_ref)
        sb_ref[...] = jnp.zeros_like(sb_ref)

    r = lax.broadcasted_iota(jnp.int32, (C, C), 0)
    c = lax.broadcasted_iota(jnp.int32, (C, C), 1)
    dirs = []
    for d, (q_ref, v_ref, f_ref, rev) in enumerate(((qf_ref, vf_ref, ff_ref, False), (qb_ref, vb_ref, fb_ref, True))):
        tri = ((c <= r) if not rev else (c >= r)).astype(bf16)
        lb = lb_ref[d:d + 1, :]
        q = _silu(q_ref[...])
        v16 = v_ref[...].astype(bf16)
        f = lb + (1.0 - lb) * _sigmoid(f_ref[...])
        g = jnp.log(f) * LOG2E
        g_hi = g.astype(bf16)
        g_lo = (g - g_hi.astype(f32)).astype(bf16)
        b = jnp.concatenate([_dot(tri, g_hi[j * C:(j + 1) * C]) + _dot(tri, g_lo[j * C:(j + 1) * C])
                             for j in range(nchunk)], axis=0)
        cc = jnp.log(1.0 - f) * LOG2E - b
        bbuf[d] = b
        cbuf[d] = cc
        dirs.append((q, v16, b, cc, rev))

    intra = []
    for d, (q, v16, b, cc, rev) in enumerate(dirs):
        intra.append([_hgrn_intra(q[j * C:(j + 1) * C], b[j * C:(j + 1) * C], cc[j * C:(j + 1) * C],
                                  bbuf.at[d], cbuf.at[d], j * C, rev=rev).astype(bf16) for j in range(nchunk)])

    for d, (q, v16, b, cc, rev) in enumerate(dirs):
        s_ref, o_ref = (sf_ref, of_ref) if not rev else (sb_ref, ob_ref)
        qe = (q * jnp.exp2(b)).astype(bf16)
        order = range(nchunk) if not rev else range(nchunk - 1, -1, -1)
        edge = (C - 1) if not rev else 0
        upd = {}
        for j in order:
            b_edge = bbuf[d, j * C + edge:j * C + edge + 1, :]
            kdec = jnp.exp2(cc[j * C:(j + 1) * C] + b_edge).astype(bf16)
            upd[j] = (jnp.exp2(b_edge), _dot_tn(v16[j * C:(j + 1) * C], kdec))
        state = s_ref[...]
        for j in order:
            rows = slice(j * C, (j + 1) * C)
            o = _dot_nt(qe[rows], state.astype(bf16)) + _dot(intra[d][j], v16[rows])
            o_ref[rows, :] = o.astype(o_ref.dtype)
            state = upd[j][0] * state + upd[j][1]
        s_ref[...] = state


def _hgrn_scan(proj, lb, tb=512):
    s = proj.shape[0]
    w = HG_HEADS * HG_D
    nb = s // tb
    hb = w // HG_D
    def fw(seg):
        return pl.BlockSpec((tb, HG_D), lambda h, n: (n, seg * hb + h))

    def bw(seg):
        return pl.BlockSpec((tb, HG_D), lambda h, n: (nb - 1 - n, seg * hb + h))

    return pl.pallas_call(
        _hgrn_block_kernel,
        grid=(HG_HEADS, nb),
        in_specs=[fw(0), fw(1), fw(2), bw(0), bw(1), bw(3),
                  pl.BlockSpec((2, HG_D), lambda h, n: (0, h))],
        out_specs=[pl.BlockSpec((tb, HG_D), lambda h, n: (n, h)),
                   pl.BlockSpec((tb, HG_D), lambda h, n: (nb - 1 - n, h))],
        out_shape=[jax.ShapeDtypeStruct((s, w), bf16), jax.ShapeDtypeStruct((s, w), bf16)],
        scratch_shapes=[pltpu.VMEM((HG_D, HG_D), f32), pltpu.VMEM((HG_D, HG_D), f32),
                        pltpu.VMEM((2, tb, HG_D), f32), pltpu.VMEM((2, tb, HG_D), f32)],
        compiler_params=_cparams(("parallel", "arbitrary")),
        name="hgrn_scan",
    )(proj, proj, proj, proj, proj, proj, lb)


def _lb_kernel(logit_ref, lb_ref, *, layer):
    x = logit_ref[...]
    e = jnp.exp(x - jnp.max(x, axis=1, keepdims=True))
    p = e / jnp.sum(e, axis=1, keepdims=True)
    acc = p[:, 0, :]
    for l in range(1, layer + 1):
        acc = acc + p[:, l, :]
    lb_ref[...] = acc


def _lower_bounds(logits, layer):
    z, _, w = logits.shape
    return pl.pallas_call(
        functools.partial(_lb_kernel, layer=layer),
        out_shape=jax.ShapeDtypeStruct((z, w), f32),
        name="hgrn_lower_bounds",
    )(logits)


def _merge_kernel(of_ref, ob_ref, hg_ref, ng_ref, y1_ref, y2_ref, w_ref, g0_ref, g1_ref, g2_ref, o_ref):
    acc = g1_ref[...].astype(f32) * _dot(y1_ref[...], w_ref[1])
    acc = acc + g2_ref[...].astype(f32) * _dot(y2_ref[...], w_ref[2])
    ng = ng_ref[...]
    y0 = []
    for h in range(HG_HEADS):
        sl = slice(h * HG_D, (h + 1) * HG_D)
        o = of_ref[:, sl].astype(f32) + ob_ref[:, sl].astype(f32)
        ms = jnp.mean(o * o, axis=-1, keepdims=True)
        y0.append((o * lax.rsqrt(ms + RMS_EPS) * ng * _sigmoid(hg_ref[:, sl])).astype(bf16))
    acc = acc + g0_ref[...].astype(f32) * _dot(jnp.concatenate(y0, axis=1), w_ref[0])
    o_ref[...] = acc.astype(o_ref.dtype)


def _merge(o_f, o_b, proj_hg, norm_g, y1, y2, w_branch, gsig, tm=512):
    m, kb = y1.shape
    d = w_branch.shape[2]
    hg_gate_blk = 4
    yspec = pl.BlockSpec((tm, kb), lambda i: (i, 0))
    def gspec(b):
        return pl.BlockSpec((tm, d), lambda i: (i, b))

    return pl.pallas_call(
        _merge_kernel,
        grid=(m // tm,),
        in_specs=[yspec, yspec,
                  pl.BlockSpec((tm, kb), lambda i: (i, hg_gate_blk)),
                  pl.BlockSpec((1, HG_D), lambda i: (0, 0)),
                  yspec, yspec,
                  pl.BlockSpec((N_BRANCH, kb, d), lambda i: (0, 0, 0), pipeline_mode=pl.Buffered(1)),
                  gspec(0), gspec(1), gspec(2)],
        out_specs=pl.BlockSpec((tm, d), lambda i: (i, 0)),
        out_shape=jax.ShapeDtypeStruct((m, d), bf16),
        compiler_params=_cparams(("parallel",)),
        name="branch_merge",
    )(o_f, o_b, proj_hg, norm_g.reshape(1, HG_D), y1, y2, w_branch, gsig, gsig, gsig)


def _proj_ln_kernel(a_ref, w_ref, res_ref, rg_ref, rb_ref, g_ref, b_ref, o32_ref, o16_ref, *, res_is_raw):
    half = a_ref.shape[0] // 2
    proj = [_dot(a_ref[r * half:(r + 1) * half, :], w_ref[...]) for r in range(2)]
    for r in range(2):
        rows = slice(r * half, (r + 1) * half)
        res = res_ref[rows, :]
        if res_is_raw:
            res = _ln_rows(res, rg_ref[...], rb_ref[...])
        y = _ln_rows(ALPHA * res + proj[r], g_ref[...], b_ref[...])
        o32_ref[rows, :] = y
        o16_ref[rows, :] = y.astype(bf16)


def _proj_residual_ln(a, w, res, res_ln, g, b, tm=512):
    m, k = a.shape
    d = w.shape[1]
    rg, rb = res_ln if res_ln is not None else (g, b)
    vec = pl.BlockSpec((1, d), lambda i: (0, 0))
    return pl.pallas_call(
        functools.partial(_proj_ln_kernel, res_is_raw=res_ln is not None),
        grid=(m // tm,),
        in_specs=[pl.BlockSpec((tm, k), lambda i: (i, 0)),
                  pl.BlockSpec((k, d), lambda i: (0, 0)),
                  pl.BlockSpec((tm, d), lambda i: (i, 0)),
                  vec, vec, vec, vec],
        out_specs=[pl.BlockSpec((tm, d), lambda i: (i, 0)),
                   pl.BlockSpec((tm, d), lambda i: (i, 0))],
        out_shape=[jax.ShapeDtypeStruct((m, d), f32), jax.ShapeDtypeStruct((m, d), bf16)],
        compiler_params=_cparams(("parallel",)),
        name="out_proj_ln",
    )(a, w, res, rg.reshape(1, d), rb.reshape(1, d), g.reshape(1, d), b.reshape(1, d))


def _ffn_kernel(x_ref, wg_ref, wu_ref, wd_ref, res_ref, g_ref, b_ref, o_ref):
    j = pl.program_id(1)

    @pl.when(j == 0)
    def _():
        o_ref[...] = jnp.zeros_like(o_ref)

    x = x_ref[...]
    hid = _silu(_dot(x, wg_ref[...].astype(bf16))) * _dot(x, wu_ref[...].astype(bf16))
    o_ref[...] += _dot(hid.astype(bf16), wd_ref[...].astype(bf16))

    @pl.when(j == pl.num_programs(1) - 1)
    def _():
        o_ref[...] = _ln_rows(ALPHA * res_ref[...] + o_ref[...], g_ref[...], b_ref[...])


def _ffn_residual_ln(x16, x32, wg, wu, wd, g, b, tm=1024, tf=256):
    m, d = x16.shape
    ff = wg.shape[1]
    once = pl.Buffered(1)
    return pl.pallas_call(
        _ffn_kernel,
        grid=(m // tm, ff // tf),
        in_specs=[pl.BlockSpec((tm, d), lambda i, j: (i, 0)),
                  pl.BlockSpec((d, tf), lambda i, j: (0, j)),
                  pl.BlockSpec((d, tf), lambda i, j: (0, j)),
                  pl.BlockSpec((tf, d), lambda i, j: (j, 0)),
                  pl.BlockSpec((tm, d), lambda i, j: (i, 0), pipeline_mode=once),
                  pl.BlockSpec((1, d), lambda i, j: (0, 0)),
                  pl.BlockSpec((1, d), lambda i, j: (0, 0))],
        out_specs=pl.BlockSpec((tm, d), lambda i, j: (i, 0)),
        out_shape=jax.ShapeDtypeStruct((m, d), f32),
        compiler_params=_cparams(("parallel", "arbitrary")),
        name="ffn_ln",
    )(x16, wg, wu, wd, x32, g.reshape(1, d), b.reshape(1, d))


def _rope_tables(positions):
    half = MLA_ROPE // 2
    inv_freq = jnp.power(ROPE_THETA, -jnp.arange(half, dtype=f32) / half)
    ang = positions.astype(f32)[..., None] * inv_freq
    cos, sin = jnp.cos(ang), jnp.sin(ang)
    z = jnp.zeros_like(cos)
    pad = jnp.zeros((ang.shape[0], LANES - MLA_ROPE), f32)
    c = jnp.concatenate([cos, cos, pad], axis=-1)
    s1 = jnp.concatenate([-sin, z, pad], axis=-1)
    s2 = jnp.concatenate([z, sin, pad], axis=-1)
    return c, s1, s2


def _layer(res, res_ln, h16, mem16, rope, lb, w_in, hgrn_norm_g, g_cq, g_ckv, w_uq, w_ukv, w_memkv, w_branch, w_o,
           ln1_g, ln1_b, w_gate, w_up, w_down, ln2_g, ln2_b):
    s, d = res.shape
    hgw = HG_HEADS * HG_D
    q_rank = g_cq.shape[0]
    kv_rank = g_ckv.shape[0]
    memw = MEM_HEADS * MEM_HD
    o_cq = 5 * hgw
    o_ckv = o_cq + q_rank
    o_kr = o_ckv + kv_rank
    o_qm = o_kr + MLA_ROPE
    o_gt = o_qm + memw
    rope_c, rope_s1, rope_s2 = rope

    wt = jnp.swapaxes(w_in, 0, 1)

    proj_hg = _matmul(h16, wt, wt_rows=(0, o_cq), tm=2048, tn=512, out_dtype=f32, name="in_proj_hgrn")
    gsig = _matmul(h16, wt, wt_rows=(o_gt, N_BRANCH * d), tm=2048, tn=512, out_dtype=bf16,
                   epilogue=_sigmoid, name="in_proj_gates")
    pr = _in_proj_rest(h16, wt, (o_gt, o_qm, o_cq, o_ckv, o_kr), o_gt, g_cq, g_ckv, rope)
    c_qm = 0
    c_cq = c_qm + memw
    c_ckv = c_cq + q_rank
    c_kr = c_ckv + kv_rank

    o_f, o_b = _hgrn_scan(proj_hg, lb)

    zq = jnp.zeros((q_rank, MLA_HEADS, MLA_QK_PAD - MLA_QK), f32)
    w_q = jnp.concatenate([w_uq.reshape(q_rank, MLA_HEADS, MLA_QK), zq], axis=-1)
    w_q = w_q.reshape(q_rank, MLA_HEADS * MLA_QK_PAD).astype(bf16)
    w_kv3 = w_ukv.reshape(kv_rank, MLA_HEADS, MLA_NOPE + MLA_V)
    w_kv = jnp.concatenate([w_kv3[:, :, :MLA_NOPE].reshape(kv_rank, -1),
                            w_kv3[:, :, MLA_NOPE:].reshape(kv_rank, -1)], axis=-1).astype(bf16)
    q = _mla_q(pr, c_cq, w_q, rope_c, rope_s1, rope_s2)
    k, v = _mla_kv(pr, c_ckv, c_kr, w_kv)
    y_mla = _mla_attention(q, k, v)

    kv_mem = _matmul(mem16, w_memkv, tm=mem16.shape[0], tn=512, out_dtype=bf16, name="mem_kv_proj")
    y_mem = _mem_attention(pr, c_qm, kv_mem[:, :memw], kv_mem[:, memw:])

    merged = _merge(o_f, o_b, proj_hg, hgrn_norm_g, y_mla, y_mem, w_branch.astype(bf16), gsig)
    x32, x16 = _proj_residual_ln(merged, w_o.astype(bf16), res, res_ln, ln1_g, ln1_b)
    out32 = _ffn_residual_ln(x16, x32, w_gate, w_up, w_down, ln2_g, ln2_b)
    return out32


def kernel(x, mem, positions, ln_emb_g, ln_emb_b, hgrn_lb_logits, w_in, hgrn_norm_g, mla_g_cq, mla_g_ckv,
           mla_w_uq, mla_w_ukv, mem_w_kv, w_branch, w_o, ln1_g, ln1_b, w_ffn_gate, w_ffn_up, w_ffn_down,
           ln2_g, ln2_b):
    bsz, s, d = x.shape
    depth = w_in.shape[0]
    outs = []
    for bi in range(bsz):
        rope = _rope_tables(positions[bi])
        mem16 = mem[bi].astype(bf16)
        h16 = _layernorm_bf16(x[bi], ln_emb_g, ln_emb_b)
        res, res_ln = x[bi], (ln_emb_g, ln_emb_b)
        for l in range(depth):
            lb = _lower_bounds(hgrn_lb_logits, l)
            res = _layer(res, res_ln, h16, mem16, rope, lb, w_in[l], hgrn_norm_g[l], mla_g_cq[l], mla_g_ckv[l],
                         mla_w_uq[l], mla_w_ukv[l], mem_w_kv[l], w_branch[l], w_o[l], ln1_g[l], ln1_b[l],
                         w_ffn_gate[l], w_ffn_up[l], w_ffn_down[l], ln2_g[l], ln2_b[l])
            res_ln = None
            if l + 1 < depth:
                h16 = res.astype(bf16)
        outs.append(res)
    return jnp.stack(outs)
```

```python
import functools

import jax
import jax.numpy as jnp
from jax import lax
from jax.experimental import pallas as pl
from jax.experimental.pallas import tpu as pltpu

HG_HEADS = 8
HG_D = 128
MLA_HEADS = 8
MLA_NOPE = 128
MLA_ROPE = 64
MLA_V = 128
MLA_QK = MLA_NOPE + MLA_ROPE
MLA_QK_PAD = 256
MLA_V_PAD = 256
MLA_QBLOCKS_PER_STEP = 1
LOG2E = 1.4426950408889634
MEM_HEADS = 4
MEM_HD = 256
N_BRANCH = 3
ROPE_THETA = 10000.0
LN_EPS = 1e-5
RMS_EPS = 1e-6
DEPTH = 1
ALPHA = (2.0 * DEPTH) ** 0.25

LANES = 128
SUBLANES = 8
VMEM_LIMIT = 56 * 1024 * 1024

HG_CHUNK = 64
HG_SUB = SUBLANES

bf16 = jnp.bfloat16
f32 = jnp.float32


def _cparams(sem):
    return pltpu.CompilerParams(dimension_semantics=sem, vmem_limit_bytes=VMEM_LIMIT)


def _dot(a, b):
    return jnp.dot(a, b, preferred_element_type=f32)


def _dot_nt(a, b):
    return lax.dot_general(a, b, (((1,), (1,)), ((), ())), preferred_element_type=f32)


def _dot_tn(a, b):
    return lax.dot_general(a, b, (((0,), (0,)), ((), ())), preferred_element_type=f32)


def _ln_rows(x, g, b):
    mu = jnp.mean(x, axis=-1, keepdims=True)
    xc = x - mu
    var = jnp.mean(xc * xc, axis=-1, keepdims=True)
    return xc * lax.rsqrt(var + LN_EPS) * g + b


def _ln_kernel(x_ref, g_ref, b_ref, o16_ref):
    o16_ref[...] = _ln_rows(x_ref[...], g_ref[...], b_ref[...]).astype(bf16)


def _layernorm_bf16(x, g, b, tm=512):
    m, d = x.shape
    return pl.pallas_call(
        _ln_kernel,
        grid=(m // tm,),
        in_specs=[pl.BlockSpec((tm, d), lambda i: (i, 0)),
                  pl.BlockSpec((1, d), lambda i: (0, 0)),
                  pl.BlockSpec((1, d), lambda i: (0, 0))],
        out_specs=pl.BlockSpec((tm, d), lambda i: (i, 0)),
        out_shape=jax.ShapeDtypeStruct((m, d), bf16),
        compiler_params=_cparams(("parallel",)),
        name="ln_embed",
    )(x, g.reshape(1, d), b.reshape(1, d))


def _mm_kernel(a_ref, w_ref, *rest, epilogue, w_transposed):
    o_ref = rest[-1]
    w = w_ref[...].astype(bf16)
    acc = _dot_nt(a_ref[...], w) if w_transposed else _dot(a_ref[...], w)
    o_ref[...] = epilogue(acc, *[r[...] for r in rest[:-1]]).astype(o_ref.dtype)


def _matmul(a, w, *, tm, tn, out_dtype, epilogue=None, row_extras=(), col_extras=(), name, wt_rows=None):
    m, k = a.shape
    if epilogue is None:
        epilogue = lambda acc: acc
    if wt_rows is None:
        n = w.shape[1]
        w_spec = pl.BlockSpec((k, tn), lambda i, j: (0, j))
    else:
        row0, n = wt_rows
        assert row0 % SUBLANES == 0 and w.shape[1] == k
        w_spec = pl.BlockSpec((pl.Element(tn), pl.Element(k)),
                              lambda i, j: ((row0 // SUBLANES + j * (tn // SUBLANES)) * SUBLANES, 0))
    in_specs = [pl.BlockSpec((tm, k), lambda i, j: (i, 0)), w_spec]
    for e in row_extras:
        in_specs.append(pl.BlockSpec((tm, e.shape[1]), lambda i, j: (i, 0)))
    for e in col_extras:
        in_specs.append(pl.BlockSpec((e.shape[0], tn), lambda i, j: (0, j)))
    return pl.pallas_call(
        functools.partial(_mm_kernel, epilogue=epilogue, w_transposed=wt_rows is not None),
        grid=(m // tm, n // tn),
        in_specs=in_specs,
        out_specs=pl.BlockSpec((tm, tn), lambda i, j: (i, j)),
        out_shape=jax.ShapeDtypeStruct((m, n), out_dtype),
        compiler_params=_cparams(("parallel", "parallel")),
        name=name,
    )(a, w, *row_extras, *col_extras)


def _sigmoid(x):
    return 0.5 * jnp.tanh(0.5 * x) + 0.5


def _silu(x):
    return x * _sigmoid(x)


def _rms_epilogue(acc, g):
    ms = jnp.mean(acc * acc, axis=-1, keepdims=True)
    return acc * lax.rsqrt(ms + RMS_EPS) * g


def _rope_block(blk, c, s1, s2):
    half = MLA_ROPE // 2
    return blk * c + pltpu.roll(blk, LANES - half, 1) * s1 + pltpu.roll(blk, half, 1) * s2


def _krope_epilogue(acc, c, s1, s2):
    return _rope_block(acc, c, s1, s2)


IN_TN = 512


def _in_proj_rest_kernel(a_ref, w_ref, gcq_ref, gckv_ref, c_ref, s1_ref, s2_ref, o_ref, *, n_gate, n_qmem):
    j = pl.program_id(1)
    acc = _dot_nt(a_ref[...], w_ref[...].astype(bf16))
    t_cq = n_gate + n_qmem

    @pl.when(j < n_gate)
    def _():
        o_ref[...] = _sigmoid(acc).astype(o_ref.dtype)

    @pl.when((j >= n_gate) & (j < t_cq))
    def _():
        o_ref[...] = (acc * (MEM_HD ** -0.5)).astype(o_ref.dtype)

    @pl.when(j == t_cq)
    def _():
        o_ref[...] = _rms_epilogue(acc, gcq_ref[...]).astype(o_ref.dtype)

    @pl.when(j == t_cq + 1)
    def _():
        o_ref[...] = _rms_epilogue(acc, gckv_ref[...]).astype(o_ref.dtype)

    @pl.when(j == t_cq + 2)
    def _():
        o_ref[:, :LANES] = _rope_block(acc[:, :LANES], c_ref[...], s1_ref[...], s2_ref[...]).astype(o_ref.dtype)
        o_ref[:, LANES:] = jnp.zeros((acc.shape[0], acc.shape[1] - LANES), o_ref.dtype)


def _in_proj_rest(h16, wt, offs, gate_end, g_cq, g_ckv, rope, tm=2048):
    m, k = h16.shape
    o_gt, o_qm, o_cq, o_ckv, o_kr = offs
    tn = IN_TN
    n_gate = (gate_end - o_gt) // tn
    n_qmem = (o_gt - o_qm) // tn
    assert g_cq.shape[0] == tn and g_ckv.shape[0] == tn and all(o % SUBLANES == 0 for o in offs)
    n_tiles = n_gate + n_qmem + 3
    t_cq = n_gate + n_qmem

    def w_row(i, j):
        r = jnp.where(j < n_gate, o_gt // SUBLANES + j * (tn // SUBLANES),
                      jnp.where(j < t_cq, o_qm // SUBLANES + (j - n_gate) * (tn // SUBLANES),
                                jnp.where(j == t_cq, o_cq // SUBLANES,
                                          jnp.where(j == t_cq + 1, o_ckv // SUBLANES, o_kr // SUBLANES))))
        return (r * SUBLANES, 0)

    vec = pl.BlockSpec((1, tn), lambda i, j: (0, 0))
    tab = pl.BlockSpec((tm, LANES), lambda i, j: (i, 0))
    return pl.pallas_call(
        functools.partial(_in_proj_rest_kernel, n_gate=n_gate, n_qmem=n_qmem),
        grid=(m // tm, n_tiles),
        in_specs=[pl.BlockSpec((tm, k), lambda i, j: (i, 0)),
                  pl.BlockSpec((pl.Element(tn), pl.Element(k)), w_row),
                  vec, vec, tab, tab, tab],
        out_specs=pl.BlockSpec((tm, tn), lambda i, j: (i, j)),
        out_shape=jax.ShapeDtypeStruct((m, n_tiles * tn), bf16),
        compiler_params=_cparams(("parallel", "arbitrary")),
        name="in_proj_rest",
    )(h16, wt, g_cq.reshape(1, tn), g_ckv.reshape(1, tn), *rope)


def _mla_q_kernel(cq_ref, w_ref, c_ref, s1_ref, s2_ref, o_ref):
    q = _dot(cq_ref[...], w_ref[...]) * (MLA_QK ** -0.5 * LOG2E)
    c, s1, s2 = c_ref[...], s1_ref[...], s2_ref[...]
    for h in range(MLA_HEADS):
        base = h * MLA_QK_PAD
        o_ref[:, base:base + MLA_NOPE] = q[:, base:base + MLA_NOPE].astype(bf16)
        o_ref[:, base + MLA_NOPE:base + MLA_QK_PAD] = _rope_block(
            q[:, base + MLA_NOPE:base + MLA_QK_PAD], c, s1, s2).astype(bf16)


def _mla_q(pr, cq_col, w_q, rope_c, rope_s1, rope_s2, tm=512):
    m = pr.shape[0]
    r, n = w_q.shape
    assert cq_col % r == 0
    cqn = pr
    return pl.pallas_call(
        _mla_q_kernel,
        grid=(m // tm,),
        in_specs=[pl.BlockSpec((tm, r), lambda i: (i, cq_col // r)),
                  pl.BlockSpec((r, n), lambda i: (0, 0)),
                  pl.BlockSpec((tm, LANES), lambda i: (i, 0)),
                  pl.BlockSpec((tm, LANES), lambda i: (i, 0)),
                  pl.BlockSpec((tm, LANES), lambda i: (i, 0))],
        out_specs=pl.BlockSpec((tm, n), lambda i: (i, 0)),
        out_shape=jax.ShapeDtypeStruct((m, n), bf16),
        compiler_params=_cparams(("parallel",)),
        name="mla_q_up",
    )(cqn, w_q, rope_c, rope_s1, rope_s2)


def _mla_kv_kernel(ckv_ref, w_ref, kr_ref, k_ref, v_ref):
    kv = _dot(ckv_ref[...], w_ref[...])
    kr = kr_ref[...]
    nk = MLA_HEADS * MLA_NOPE
    for h in range(MLA_HEADS):
        base = h * MLA_QK_PAD
        k_ref[:, base:base + MLA_NOPE] = kv[:, h * MLA_NOPE:(h + 1) * MLA_NOPE].astype(bf16)
        k_ref[:, base + MLA_NOPE:base + MLA_QK_PAD] = kr
    lane = lax.broadcasted_iota(jnp.int32, (kv.shape[0], MLA_V_PAD - MLA_V), 1)
    ones_col = jnp.where(lane == 0, 1.0, 0.0).astype(bf16)
    for h in range(MLA_HEADS):
        base = h * MLA_V_PAD
        v_ref[:, base:base + MLA_V] = kv[:, nk + h * MLA_V:nk + (h + 1) * MLA_V].astype(bf16)
        v_ref[:, base + MLA_V:base + MLA_V_PAD] = ones_col


def _mla_kv(pr, ckv_col, kr_col, w_kv, tm=512):
    m = pr.shape[0]
    r, n = w_kv.shape
    assert ckv_col % r == 0 and kr_col % LANES == 0
    ckvn = krope = pr
    nk = MLA_HEADS * MLA_QK_PAD
    nv = MLA_HEADS * MLA_V_PAD
    return pl.pallas_call(
        _mla_kv_kernel,
        grid=(m // tm,),
        in_specs=[pl.BlockSpec((tm, r), lambda i: (i, ckv_col // r)),
                  pl.BlockSpec((r, n), lambda i: (0, 0)),
                  pl.BlockSpec((tm, LANES), lambda i: (i, kr_col // LANES))],
        out_specs=[pl.BlockSpec((tm, nk), lambda i: (i, 0)),
                   pl.BlockSpec((tm, nv), lambda i: (i, 0))],
        out_shape=[jax.ShapeDtypeStruct((m, nk), bf16), jax.ShapeDtypeStruct((m, nv), bf16)],
        compiler_params=_cparams(("parallel",)),
        name="mla_kv_up",
    )(ckvn, w_kv, krope)


def _mla_attn_kernel(q_ref, k_ref, v_ref, o_ref, s_ref, *, tq, tk):
    nq = q_ref.shape[0] // tq
    nkv = k_ref.shape[0] // tk

    def scores(q, c):
        return _dot_nt(q, k_ref[c * tk:(c + 1) * tk, :])

    def update(s, c, m, acc):
        m_new = jnp.maximum(m, jnp.max(s, axis=-1, keepdims=True))
        p = jnp.exp2(s - m_new)
        acc = jnp.exp2(m - m_new) * acc + _dot(p.astype(bf16), v_ref[c * tk:(c + 1) * tk, :])
        return m_new, acc

    s_ref[...] = scores(q_ref[0:tq, :], 0)
    assert nq % MLA_QBLOCKS_PER_STEP == 0

    def body(i, carry):
        s_next = s_ref[...]
        for u in range(MLA_QBLOCKS_PER_STEP):
            blk = i * MLA_QBLOCKS_PER_STEP + u
            off = pl.multiple_of(blk * tq, tq)
            q = q_ref[pl.ds(off, tq), :]
            m = jnp.full((tq, 1), -jnp.inf, f32)
            acc = jnp.zeros((tq, MLA_V_PAD), f32)
            for c in range(nkv):
                s_cur = s_next
                if c + 1 < nkv:
                    s_next = scores(q, c + 1)
                else:
                    nxt = pl.multiple_of(jnp.minimum(blk + 1, nq - 1) * tq, tq)
                    s_next = scores(q_ref[pl.ds(nxt, tq), :], 0)
                m, acc = update(s_cur, c, m, acc)
            o_ref[pl.ds(off, tq), :] = (acc[:, :MLA_V] / acc[:, MLA_V:MLA_V + 1]).astype(o_ref.dtype)
        s_ref[...] = s_next
        return carry

    lax.fori_loop(0, nq // MLA_QBLOCKS_PER_STEP, body, 0)


def _mla_attention(q, k, v, tq=512, tk=512):
    s = q.shape[0]
    return pl.pallas_call(
        functools.partial(_mla_attn_kernel, tq=tq, tk=tk),
        grid=(MLA_HEADS,),
        in_specs=[pl.BlockSpec((s, MLA_QK_PAD), lambda h: (0, h)),
                  pl.BlockSpec((s, MLA_QK_PAD), lambda h: (0, h)),
                  pl.BlockSpec((s, MLA_V_PAD), lambda h: (0, h))],
        out_specs=pl.BlockSpec((s, MLA_V), lambda h: (0, h)),
        out_shape=jax.ShapeDtypeStruct((s, MLA_HEADS * MLA_V), bf16),
        scratch_shapes=[pltpu.VMEM((tq, tk), f32)],
        compiler_params=_cparams(("parallel",)),
        name="mla_attention",
    )(q, k, v)


def _mem_attn_kernel(q_ref, k_ref, v_ref, o_ref):
    for h in range(MEM_HEADS):
        sl = slice(h * MEM_HD, (h + 1) * MEM_HD)
        s = _dot_nt(q_ref[:, sl], k_ref[:, sl])
        m = jnp.max(s, axis=-1, keepdims=True)
        p = jnp.exp(s - m)
        l = jnp.sum(p, axis=-1, keepdims=True)
        o = _dot(p.astype(bf16), v_ref[:, sl])
        o_ref[:, sl] = (o / l).astype(o_ref.dtype)


def _mem_attention(pr, q_col, k, v, tm=512):
    s = pr.shape[0]
    nm, w = k.shape
    assert q_col % w == 0
    q = pr
    return pl.pallas_call(
        _mem_attn_kernel,
        grid=(s // tm,),
        in_specs=[pl.BlockSpec((tm, w), lambda i: (i, q_col // w)),
                  pl.BlockSpec((nm, w), lambda i: (0, 0)),
                  pl.BlockSpec((nm, w), lambda i: (0, 0))],
        out_specs=pl.BlockSpec((tm, w), lambda i: (i, 0)),
        out_shape=jax.ShapeDtypeStruct((s, w), bf16),
        compiler_params=_cparams(("parallel",)),
        name="mem_attention",
    )(q, k, v)


def _hgrn_levels(rev):
    levels = []
    sz = HG_CHUNK // 2
    while sz >= HG_SUB:
        blocks = []
        for base in range(0, HG_CHUNK, 2 * sz):
            if not rev:
                blocks.append((base + sz, base, sz, base + sz - 1))
            else:
                blocks.append((base, base + sz, sz, base + sz))
        levels.append(blocks)
        sz //= 2
    return levels


def _hgrn_intra(q_c, b_c, c_c, bbuf, cbuf, row0, *, rev):
    C = HG_CHUNK
    nb = C // HG_SUB
    lane = lax.broadcasted_iota(jnp.int32, (HG_SUB, C), 1)
    sub = lax.broadcasted_iota(jnp.int32, (HG_SUB, C), 0)
    blocks = []
    for i in range(nb):
        r0 = i * HG_SUB
        q_i = q_c[r0:r0 + HG_SUB]
        b_i = b_c[r0:r0 + HG_SUB]
        a_blk = jnp.zeros((HG_SUB, C), f32)
        for s in range(HG_SUB):
            w = q_i * jnp.exp2(b_i + cbuf[row0 + r0 + s:row0 + r0 + s + 1, :])
            a_blk = jnp.where(lane == r0 + s, jnp.sum(w, axis=-1, keepdims=True), a_blk)
        keep = (sub >= lane - r0) if not rev else (sub <= lane - r0)
        blocks.append(jnp.where(keep, a_blk, 0.0))
    for level in _hgrn_levels(rev):
        sz = level[0][2]
        qd, kd, valid = [], [], None
        k_pos = 0
        rq = lax.broadcasted_iota(jnp.int32, (len(level) * sz, C), 0)
        ck = lax.broadcasted_iota(jnp.int32, (len(level) * sz, C), 1)
        for j, (q0, k0, _, piv_row) in enumerate(level):
            piv = bbuf[row0 + piv_row:row0 + piv_row + 1, :]
            qd.append(q_c[q0:q0 + sz] * jnp.exp2(b_c[q0:q0 + sz] - piv))
            if k0 > k_pos:
                kd.append(jnp.zeros((k0 - k_pos, q_c.shape[1]), f32))
            kd.append(jnp.exp2(c_c[k0:k0 + sz] + piv))
            k_pos = k0 + sz
            ok = (rq >= j * sz) & (rq < (j + 1) * sz) & (ck >= k0) & (ck < k0 + sz)
            valid = ok if valid is None else (valid | ok)
        if k_pos < C:
            kd.append(jnp.zeros((C - k_pos, q_c.shape[1]), f32))
        r = _dot_nt(jnp.concatenate(qd, axis=0).astype(bf16), jnp.concatenate(kd, axis=0).astype(bf16))
        if len(level) > 1:
            r = jnp.where(valid, r, 0.0)
        for j, (q0, _, _, _) in enumerate(level):
            for t in range(sz // HG_SUB):
                i = q0 // HG_SUB + t
                blocks[i] = blocks[i] + r[j * sz + t * HG_SUB:j * sz + (t + 1) * HG_SUB]
    return jnp.concatenate(blocks, axis=0)


def _hgrn_block_kernel(qf_ref, vf_ref, ff_ref, qb_ref, vb_ref, fb_ref, lb_ref, of_ref, ob_ref,
                       sf_ref, sb_ref, bbuf, cbuf):
    n = pl.program_id(1)
    C = HG_CHUNK
    T = qf_ref.shape[0]
    nchunk = T // C

    @pl.when(n == 0)
    def _():
        sf_ref[...] = jnp.zeros_like(sf_ref)
        sb_ref[...] = jnp.zeros_like(sb_ref)

    r = lax.broadcasted_iota(jnp.int32, (C, C), 0)
    c = lax.broadcasted_iota(jnp.int32, (C, C), 1)
    dirs = []
    for d, (q_ref, v_ref, f_ref, rev) in enumerate(((qf_ref, vf_ref, ff_ref, False), (qb_ref, vb_ref, fb_ref, True))):
        tri = ((c <= r) if not rev else (c >= r)).astype(bf16)
        lb = lb_ref[d:d + 1, :]
        q = _silu(q_ref[...])
        v16 = v_ref[...].astype(bf16)
        f = lb + (1.0 - lb) * _sigmoid(f_ref[...])
        g = jnp.log(f) * LOG2E
        g_hi = g.astype(bf16)
        g_lo = (g - g_hi.astype(f32)).astype(bf16)
        b = jnp.concatenate([_dot(tri, g_hi[j * C:(j + 1) * C]) + _dot(tri, g_lo[j * C:(j + 1) * C])
                             for j in range(nchunk)], axis=0)
        cc = jnp.log(1.0 - f) * LOG2E - b
        bbuf[d] = b
        cbuf[d] = cc
        dirs.append((q, v16, b, cc, rev))

    intra = []
    for d, (q, v16, b, cc, rev) in enumerate(dirs):
        intra.append([_hgrn_intra(q[j * C:(j + 1) * C], b[j * C:(j + 1) * C], cc[j * C:(j + 1) * C],
                                  bbuf.at[d], cbuf.at[d], j * C, rev=rev).astype(bf16) for j in range(nchunk)])

    for d, (q, v16, b, cc, rev) in enumerate(dirs):
        s_ref, o_ref = (sf_ref, of_ref) if not rev else (sb_ref, ob_ref)
        qe = (q * jnp.exp2(b)).astype(bf16)
        order = range(nchunk) if not rev else range(nchunk - 1, -1, -1)
        edge = (C - 1) if not rev else 0
        upd = {}
        for j in order:
            b_edge = bbuf[d, j * C + edge:j * C + edge + 1, :]
            kdec = jnp.exp2(cc[j * C:(j + 1) * C] + b_edge).astype(bf16)
            upd[j] = (jnp.exp2(b_edge), _dot_tn(v16[j * C:(j + 1) * C], kdec))
        state = s_ref[...]
        for j in order:
            rows = slice(j * C, (j + 1) * C)
            o = _dot_nt(qe[rows], state.astype(bf16)) + _dot(intra[d][j], v16[rows])
            o_ref[rows, :] = o.astype(o_ref.dtype)
            state = upd[j][0] * state + upd[j][1]
        s_ref[...] = state


def _hgrn_scan(proj, lb, tb=1024):
    s = proj.shape[0]
    w = HG_HEADS * HG_D
    nb = s // tb
    hb = w // HG_D
    def fw(seg):
        return pl.BlockSpec((tb, HG_D), lambda h, n: (n, seg * hb + h))

    def bw(seg):
        return pl.BlockSpec((tb, HG_D), lambda h, n: (nb - 1 - n, seg * hb + h))

    return pl.pallas_call(
        _hgrn_block_kernel,
        grid=(HG_HEADS, nb),
        in_specs=[fw(0), fw(1), fw(2), bw(0), bw(1), bw(3),
                  pl.BlockSpec((2, HG_D), lambda h, n: (0, h))],
        out_specs=[pl.BlockSpec((tb, HG_D), lambda h, n: (n, h)),
                   pl.BlockSpec((tb, HG_D), lambda h, n: (nb - 1 - n, h))],
        out_shape=[jax.ShapeDtypeStruct((s, w), bf16), jax.ShapeDtypeStruct((s, w), bf16)],
        scratch_shapes=[pltpu.VMEM((HG_D, HG_D), f32), pltpu.VMEM((HG_D, HG_D), f32),
                        pltpu.VMEM((2, tb, HG_D), f32), pltpu.VMEM((2, tb, HG_D), f32)],
        compiler_params=_cparams(("parallel", "arbitrary")),
        name="hgrn_scan",
    )(proj, proj, proj, proj, proj, proj, lb)


def _lb_kernel(logit_ref, lb_ref, *, layer):
    x = logit_ref[...]
    e = jnp.exp(x - jnp.max(x, axis=1, keepdims=True))
    p = e / jnp.sum(e, axis=1, keepdims=True)
    acc = p[:, 0, :]
    for l in range(1, layer + 1):
        acc = acc + p[:, l, :]
    lb_ref[...] = acc


def _lower_bounds(logits, layer):
    z, _, w = logits.shape
    return pl.pallas_call(
        functools.partial(_lb_kernel, layer=layer),
        out_shape=jax.ShapeDtypeStruct((z, w), f32),
        name="hgrn_lower_bounds",
    )(logits)


def _merge_kernel(of_ref, ob_ref, hg_ref, ng_ref, y1_ref, y2_ref, w_ref, g0_ref, g1_ref, g2_ref, o_ref):
    acc = g1_ref[...].astype(f32) * _dot(y1_ref[...], w_ref[1])
    acc = acc + g2_ref[...].astype(f32) * _dot(y2_ref[...], w_ref[2])
    ng = ng_ref[...]
    y0 = []
    for h in range(HG_HEADS):
        sl = slice(h * HG_D, (h + 1) * HG_D)
        o = of_ref[:, sl].astype(f32) + ob_ref[:, sl].astype(f32)
        ms = jnp.mean(o * o, axis=-1, keepdims=True)
        y0.append((o * lax.rsqrt(ms + RMS_EPS) * ng * _sigmoid(hg_ref[:, sl])).astype(bf16))
    acc = acc + g0_ref[...].astype(f32) * _dot(jnp.concatenate(y0, axis=1), w_ref[0])
    o_ref[...] = acc.astype(o_ref.dtype)


def _merge(o_f, o_b, proj_hg, norm_g, y1, y2, w_branch, gsig, tm=512):
    m, kb = y1.shape
    d = w_branch.shape[2]
    hg_gate_blk = 4
    yspec = pl.BlockSpec((tm, kb), lambda i: (i, 0))
    def gspec(b):
        return pl.BlockSpec((tm, d), lambda i: (i, b))

    return pl.pallas_call(
        _merge_kernel,
        grid=(m // tm,),
        in_specs=[yspec, yspec,
                  pl.BlockSpec((tm, kb), lambda i: (i, hg_gate_blk)),
                  pl.BlockSpec((1, HG_D), lambda i: (0, 0)),
                  yspec, yspec,
                  pl.BlockSpec((N_BRANCH, kb, d), lambda i: (0, 0, 0), pipeline_mode=pl.Buffered(1)),
                  gspec(0), gspec(1), gspec(2)],
        out_specs=pl.BlockSpec((tm, d), lambda i: (i, 0)),
        out_shape=jax.ShapeDtypeStruct((m, d), bf16),
        compiler_params=_cparams(("parallel",)),
        name="branch_merge",
    )(o_f, o_b, proj_hg, norm_g.reshape(1, HG_D), y1, y2, w_branch, gsig, gsig, gsig)


PROJ_LN_GROUPS = 2


def _proj_ln_kernel(a_ref, w_ref, res_ref, rg_ref, rb_ref, g_ref, b_ref, o32_ref, o16_ref, *, res_is_raw):
    half = a_ref.shape[0] // PROJ_LN_GROUPS
    proj = [_dot(a_ref[r * half:(r + 1) * half, :], w_ref[...]) for r in range(PROJ_LN_GROUPS)]
    for r in range(PROJ_LN_GROUPS):
        rows = slice(r * half, (r + 1) * half)
        res = res_ref[rows, :]
        if res_is_raw:
            res = _ln_rows(res, rg_ref[...], rb_ref[...])
        y = _ln_rows(ALPHA * res + proj[r], g_ref[...], b_ref[...])
        o32_ref[rows, :] = y
        o16_ref[rows, :] = y.astype(bf16)


def _proj_residual_ln(a, w, res, res_ln, g, b, tm=512):
    m, k = a.shape
    d = w.shape[1]
    rg, rb = res_ln if res_ln is not None else (g, b)
    vec = pl.BlockSpec((1, d), lambda i: (0, 0))
    return pl.pallas_call(
        functools.partial(_proj_ln_kernel, res_is_raw=res_ln is not None),
        grid=(m // tm,),
        in_specs=[pl.BlockSpec((tm, k), lambda i: (i, 0)),
                  pl.BlockSpec((k, d), lambda i: (0, 0)),
                  pl.BlockSpec((tm, d), lambda i: (i, 0)),
                  vec, vec, vec, vec],
        out_specs=[pl.BlockSpec((tm, d), lambda i: (i, 0)),
                   pl.BlockSpec((tm, d), lambda i: (i, 0))],
        out_shape=[jax.ShapeDtypeStruct((m, d), f32), jax.ShapeDtypeStruct((m, d), bf16)],
        compiler_params=_cparams(("parallel",)),
        name="out_proj_ln",
    )(a, w, res, rg.reshape(1, d), rb.reshape(1, d), g.reshape(1, d), b.reshape(1, d))


def _ffn_kernel(x_ref, wg_ref, wu_ref, wd_ref, res_ref, g_ref, b_ref, o_ref):
    j = pl.program_id(1)

    @pl.when(j == 0)
    def _():
        o_ref[...] = jnp.zeros_like(o_ref)

    x = x_ref[...]
    hid = _silu(_dot(x, wg_ref[...].astype(bf16))) * _dot(x, wu_ref[...].astype(bf16))
    o_ref[...] += _dot(hid.astype(bf16), wd_ref[...].astype(bf16))

    @pl.when(j == pl.num_programs(1) - 1)
    def _():
        o_ref[...] = _ln_rows(ALPHA * res_ref[...] + o_ref[...], g_ref[...], b_ref[...])


def _ffn_residual_ln(x16, x32, wg, wu, wd, g, b, tm=1024, tf=256):
    m, d = x16.shape
    ff = wg.shape[1]
    once = pl.Buffered(1)
    return pl.pallas_call(
        _ffn_kernel,
        grid=(m // tm, ff // tf),
        in_specs=[pl.BlockSpec((tm, d), lambda i, j: (i, 0)),
                  pl.BlockSpec((d, tf), lambda i, j: (0, j)),
                  pl.BlockSpec((d, tf), lambda i, j: (0, j)),
                  pl.BlockSpec((tf, d), lambda i, j: (j, 0)),
                  pl.BlockSpec((tm, d), lambda i, j: (i, 0), pipeline_mode=once),
                  pl.BlockSpec((1, d), lambda i, j: (0, 0)),
                  pl.BlockSpec((1, d), lambda i, j: (0, 0))],
        out_specs=pl.BlockSpec((tm, d), lambda i, j: (i, 0)),
        out_shape=jax.ShapeDtypeStruct((m, d), f32),
        compiler_params=_cparams(("parallel", "arbitrary")),
        name="ffn_ln",
    )(x16, wg, wu, wd, x32, g.reshape(1, d), b.reshape(1, d))


def _rope_tables(positions):
    half = MLA_ROPE // 2
    inv_freq = jnp.power(ROPE_THETA, -jnp.arange(half, dtype=f32) / half)
    ang = positions.astype(f32)[..., None] * inv_freq
    cos, sin = jnp.cos(ang), jnp.sin(ang)
    z = jnp.zeros_like(cos)
    pad = jnp.zeros((ang.shape[0], LANES - MLA_ROPE), f32)
    c = jnp.concatenate([cos, cos, pad], axis=-1)
    s1 = jnp.concatenate([-sin, z, pad], axis=-1)
    s2 = jnp.concatenate([z, sin, pad], axis=-1)
    return c, s1, s2


def _layer(res, res_ln, h16, mem16, rope, lb, w_in, hgrn_norm_g, g_cq, g_ckv, w_uq, w_ukv, w_memkv, w_branch, w_o,
           ln1_g, ln1_b, w_gate, w_up, w_down, ln2_g, ln2_b):
    s, d = res.shape
    hgw = HG_HEADS * HG_D
    q_rank = g_cq.shape[0]
    kv_rank = g_ckv.shape[0]
    memw = MEM_HEADS * MEM_HD
    o_cq = 5 * hgw
    o_ckv = o_cq + q_rank
    o_kr = o_ckv + kv_rank
    o_qm = o_kr + MLA_ROPE
    o_gt = o_qm + memw
    rope_c, rope_s1, rope_s2 = rope

    wt = jnp.swapaxes(w_in, 0, 1)

    proj_hg = _matmul(h16, wt, wt_rows=(0, o_cq), tm=2048, tn=512, out_dtype=f32, name="in_proj_hgrn")
    gsig = _matmul(h16, wt, wt_rows=(o_gt, N_BRANCH * d), tm=2048, tn=512, out_dtype=bf16,
                   epilogue=_sigmoid, name="in_proj_gates")
    pr = _in_proj_rest(h16, wt, (o_gt, o_qm, o_cq, o_ckv, o_kr), o_gt, g_cq, g_ckv, rope)
    c_qm = 0
    c_cq = c_qm + memw
    c_ckv = c_cq + q_rank
    c_kr = c_ckv + kv_rank

    o_f, o_b = _hgrn_scan(proj_hg, lb)

    zq = jnp.zeros((q_rank, MLA_HEADS, MLA_QK_PAD - MLA_QK), f32)
    w_q = jnp.concatenate([w_uq.reshape(q_rank, MLA_HEADS, MLA_QK), zq], axis=-1)
    w_q = w_q.reshape(q_rank, MLA_HEADS * MLA_QK_PAD).astype(bf16)
    w_kv3 = w_ukv.reshape(kv_rank, MLA_HEADS, MLA_NOPE + MLA_V)
    w_kv = jnp.concatenate([w_kv3[:, :, :MLA_NOPE].reshape(kv_rank, -1),
                            w_kv3[:, :, MLA_NOPE:].reshape(kv_rank, -1)], axis=-1).astype(bf16)
    q = _mla_q(pr, c_cq, w_q, rope_c, rope_s1, rope_s2)
    k, v = _mla_kv(pr, c_ckv, c_kr, w_kv)
    y_mla = _mla_attention(q, k, v)

    kv_mem = _matmul(mem16, w_memkv, tm=mem16.shape[0], tn=512, out_dtype=bf16, name="mem_kv_proj")
    y_mem = _mem_attention(pr, c_qm, kv_mem[:, :memw], kv_mem[:, memw:])

    merged = _merge(o_f, o_b, proj_hg, hgrn_norm_g, y_mla, y_mem, w_branch.astype(bf16), gsig)
    x32, x16 = _proj_residual_ln(merged, w_o.astype(bf16), res, res_ln, ln1_g, ln1_b)
    out32 = _ffn_residual_ln(x16, x32, w_gate, w_up, w_down, ln2_g, ln2_b)
    return out32


def kernel(x, mem, positions, ln_emb_g, ln_emb_b, hgrn_lb_logits, w_in, hgrn_norm_g, mla_g_cq, mla_g_ckv,
           mla_w_uq, mla_w_ukv, mem_w_kv, w_branch, w_o, ln1_g, ln1_b, w_ffn_gate, w_ffn_up, w_ffn_down,
           ln2_g, ln2_b):
    bsz, s, d = x.shape
    depth = w_in.shape[0]
    outs = []
    for bi in range(bsz):
        rope = _rope_tables(positions[bi])
        mem16 = mem[bi].astype(bf16)
        h16 = _layernorm_bf16(x[bi], ln_emb_g, ln_emb_b)
        res, res_ln = x[bi], (ln_emb_g, ln_emb_b)
        for l in range(depth):
            lb = _lower_bounds(hgrn_lb_logits, l)
            res = _layer(res, res_ln, h16, mem16, rope, lb, w_in[l], hgrn_norm_g[l], mla_g_cq[l], mla_g_ckv[l],
                         mla_w_uq[l], mla_w_ukv[l], mem_w_kv[l], w_branch[l], w_o[l], ln1_g[l], ln1_b[l],
                         w_ffn_gate[l], w_ffn_up[l], w_ffn_down[l], ln2_g[l], ln2_b[l])
            res_ln = None
            if l + 1 < depth:
                h16 = res.astype(bf16)
        outs.append(res)
    return jnp.stack(outs)
```

```python
import functools

import jax
import jax.numpy as jnp
from jax import lax
from jax.experimental import pallas as pl
from jax.experimental.pallas import tpu as pltpu

HG_HEADS = 8
HG_D = 128
MLA_HEADS = 8
MLA_NOPE = 128
MLA_ROPE = 64
MLA_V = 128
MLA_QK = MLA_NOPE + MLA_ROPE
MLA_QK_PAD = 256
MLA_V_PAD = 256
LOG2E = 1.4426950408889634
MEM_HEADS = 4
MEM_HD = 256
N_BRANCH = 3
ROPE_THETA = 10000.0
LN_EPS = 1e-5
RMS_EPS = 1e-6

LANES = 128
SUBLANES = 8
VMEM_LIMIT = 56 * 1024 * 1024

TM_IN_PROJ = 2048
TN_IN_PROJ = 512
TM_ROW = 512
TM_LIGHT = 1024
ATTN_TQ = 512
ATTN_TK = 512
FFN_TM = 1024
FFN_TF = 256
PROJ_LN_GROUPS = 2
HG_BLOCK = 1024
HG_CHUNK = 64
HG_SUB = SUBLANES

bf16 = jnp.bfloat16
f32 = jnp.float32


def _cparams(sem):
    return pltpu.CompilerParams(dimension_semantics=sem, vmem_limit_bytes=VMEM_LIMIT)


def _dot(a, b):
    return jnp.dot(a, b, preferred_element_type=f32)


def _dot_nt(a, b):
    return lax.dot_general(a, b, (((1,), (1,)), ((), ())), preferred_element_type=f32)


def _dot_tn(a, b):
    return lax.dot_general(a, b, (((0,), (0,)), ((), ())), preferred_element_type=f32)


def _ln_rows(x, g, b):
    mu = jnp.mean(x, axis=-1, keepdims=True)
    xc = x - mu
    var = jnp.mean(xc * xc, axis=-1, keepdims=True)
    return xc * lax.rsqrt(var + LN_EPS) * g + b


def _ln_kernel(x_ref, g_ref, b_ref, o16_ref):
    o16_ref[...] = _ln_rows(x_ref[...], g_ref[...], b_ref[...]).astype(bf16)


def _layernorm_bf16(x, g, b, tm=TM_LIGHT):
    m, d = x.shape
    return pl.pallas_call(
        _ln_kernel,
        grid=(m // tm,),
        in_specs=[pl.BlockSpec((tm, d), lambda i: (i, 0)),
                  pl.BlockSpec((1, d), lambda i: (0, 0)),
                  pl.BlockSpec((1, d), lambda i: (0, 0))],
        out_specs=pl.BlockSpec((tm, d), lambda i: (i, 0)),
        out_shape=jax.ShapeDtypeStruct((m, d), bf16),
        compiler_params=_cparams(("parallel",)),
        name="ln_embed",
    )(x, g.reshape(1, d), b.reshape(1, d))


def _mm_kernel(a_ref, w_ref, o_ref, *, epilogue, w_transposed):
    w = w_ref[...].astype(bf16)
    acc = _dot_nt(a_ref[...], w) if w_transposed else _dot(a_ref[...], w)
    o_ref[...] = epilogue(acc).astype(o_ref.dtype)


def _matmul(a, w, *, tm, tn, out_dtype, epilogue=lambda acc: acc, name, wt_rows=None):
    m, k = a.shape
    if wt_rows is None:
        n = w.shape[1]
        w_spec = pl.BlockSpec((k, tn), lambda i, j: (0, j))
    else:
        row0, n = wt_rows
        assert row0 % SUBLANES == 0 and w.shape[1] == k
        w_spec = pl.BlockSpec((pl.Element(tn), pl.Element(k)),
                              lambda i, j: ((row0 // SUBLANES + j * (tn // SUBLANES)) * SUBLANES, 0))
    return pl.pallas_call(
        functools.partial(_mm_kernel, epilogue=epilogue, w_transposed=wt_rows is not None),
        grid=(m // tm, n // tn),
        in_specs=[pl.BlockSpec((tm, k), lambda i, j: (i, 0)), w_spec],
        out_specs=pl.BlockSpec((tm, tn), lambda i, j: (i, j)),
        out_shape=jax.ShapeDtypeStruct((m, n), out_dtype),
        compiler_params=_cparams(("parallel", "parallel")),
        name=name,
    )(a, w)


def _sigmoid(x):
    return 0.5 * jnp.tanh(0.5 * x) + 0.5


def _silu(x):
    return x * _sigmoid(x)


def _rms_epilogue(acc, g):
    ms = jnp.mean(acc * acc, axis=-1, keepdims=True)
    return acc * lax.rsqrt(ms + RMS_EPS) * g


def _rope_block(blk, c, s1, s2):
    half = MLA_ROPE // 2
    return blk * c + pltpu.roll(blk, LANES - half, 1) * s1 + pltpu.roll(blk, half, 1) * s2


def _in_proj_rest_kernel(a_ref, w_ref, gcq_ref, gckv_ref, c_ref, s1_ref, s2_ref, o_ref, *, n_qmem):
    j = pl.program_id(1)
    acc = _dot_nt(a_ref[...], w_ref[...].astype(bf16))

    @pl.when(j < n_qmem)
    def _():
        o_ref[...] = (acc * (MEM_HD ** -0.5)).astype(o_ref.dtype)

    @pl.when(j == n_qmem)
    def _():
        o_ref[...] = _rms_epilogue(acc, gcq_ref[...]).astype(o_ref.dtype)

    @pl.when(j == n_qmem + 1)
    def _():
        o_ref[...] = _rms_epilogue(acc, gckv_ref[...]).astype(o_ref.dtype)

    @pl.when(j == n_qmem + 2)
    def _():
        o_ref[:, :LANES] = _rope_block(acc[:, :LANES], c_ref[...], s1_ref[...], s2_ref[...]).astype(o_ref.dtype)
        o_ref[:, LANES:] = jnp.zeros((acc.shape[0], acc.shape[1] - LANES), o_ref.dtype)


def _in_proj_rest(h16, wt, o_qm, n_qm, o_cq, o_ckv, o_kr, g_cq, g_ckv, rope, tm=TM_IN_PROJ, tn=TN_IN_PROJ):
    m, k = h16.shape
    n_qmem = n_qm // tn
    assert n_qm % tn == 0 and g_cq.shape[0] == tn and g_ckv.shape[0] == tn
    assert all(o % SUBLANES == 0 for o in (o_qm, o_cq, o_ckv, o_kr))
    n_tiles = n_qmem + 3

    def w_row(i, j):
        r = jnp.where(j < n_qmem, o_qm // SUBLANES + j * (tn // SUBLANES),
                      jnp.where(j == n_qmem, o_cq // SUBLANES,
                                jnp.where(j == n_qmem + 1, o_ckv // SUBLANES, o_kr // SUBLANES)))
        return (r * SUBLANES, 0)

    vec = pl.BlockSpec((1, tn), lambda i, j: (0, 0))
    tab = pl.BlockSpec((tm, LANES), lambda i, j: (i, 0))
    return pl.pallas_call(
        functools.partial(_in_proj_rest_kernel, n_qmem=n_qmem),
        grid=(m // tm, n_tiles),
        in_specs=[pl.BlockSpec((tm, k), lambda i, j: (i, 0)),
                  pl.BlockSpec((pl.Element(tn), pl.Element(k)), w_row),
                  vec, vec, tab, tab, tab],
        out_specs=pl.BlockSpec((tm, tn), lambda i, j: (i, j)),
        out_shape=jax.ShapeDtypeStruct((m, n_tiles * tn), bf16),
        compiler_params=_cparams(("parallel", "arbitrary")),
        name="in_proj_rest",
    )(h16, wt, g_cq.reshape(1, tn), g_ckv.reshape(1, tn), *rope)


def _mla_q_kernel(cq_ref, w_ref, c_ref, s1_ref, s2_ref, o_ref):
    q = _dot(cq_ref[...], w_ref[...]) * (MLA_QK ** -0.5 * LOG2E)
    c, s1, s2 = c_ref[...], s1_ref[...], s2_ref[...]
    for h in range(MLA_HEADS):
        base = h * MLA_QK_PAD
        o_ref[:, base:base + MLA_NOPE] = q[:, base:base + MLA_NOPE].astype(bf16)
        o_ref[:, base + MLA_NOPE:base + MLA_QK_PAD] = _rope_block(
            q[:, base + MLA_NOPE:base + MLA_QK_PAD], c, s1, s2).astype(bf16)


def _mla_q(pr, cq_col, w_q, rope_c, rope_s1, rope_s2, tm=TM_LIGHT):
    m = pr.shape[0]
    r, n = w_q.shape
    assert cq_col % r == 0
    cqn = pr
    return pl.pallas_call(
        _mla_q_kernel,
        grid=(m // tm,),
        in_specs=[pl.BlockSpec((tm, r), lambda i: (i, cq_col // r)),
                  pl.BlockSpec((r, n), lambda i: (0, 0)),
                  pl.BlockSpec((tm, LANES), lambda i: (i, 0)),
                  pl.BlockSpec((tm, LANES), lambda i: (i, 0)),
                  pl.BlockSpec((tm, LANES), lambda i: (i, 0))],
        out_specs=pl.BlockSpec((tm, n), lambda i: (i, 0)),
        out_shape=jax.ShapeDtypeStruct((m, n), bf16),
        compiler_params=_cparams(("parallel",)),
        name="mla_q_up",
    )(cqn, w_q, rope_c, rope_s1, rope_s2)


def _mla_kv_kernel(ckv_ref, w_ref, kr_ref, k_ref, v_ref):
    kv = _dot(ckv_ref[...], w_ref[...])
    kr = kr_ref[...]
    nk = MLA_HEADS * MLA_NOPE
    for h in range(MLA_HEADS):
        base = h * MLA_QK_PAD
        k_ref[:, base:base + MLA_NOPE] = kv[:, h * MLA_NOPE:(h + 1) * MLA_NOPE].astype(bf16)
        k_ref[:, base + MLA_NOPE:base + MLA_QK_PAD] = kr
    lane = lax.broadcasted_iota(jnp.int32, (kv.shape[0], MLA_V_PAD - MLA_V), 1)
    ones_col = jnp.where(lane == 0, 1.0, 0.0).astype(bf16)
    for h in range(MLA_HEADS):
        base = h * MLA_V_PAD
        v_ref[:, base:base + MLA_V] = kv[:, nk + h * MLA_V:nk + (h + 1) * MLA_V].astype(bf16)
        v_ref[:, base + MLA_V:base + MLA_V_PAD] = ones_col


def _mla_kv(pr, ckv_col, kr_col, w_kv, tm=TM_LIGHT):
    m = pr.shape[0]
    r, n = w_kv.shape
    assert ckv_col % r == 0 and kr_col % LANES == 0
    ckvn = krope = pr
    nk = MLA_HEADS * MLA_QK_PAD
    nv = MLA_HEADS * MLA_V_PAD
    return pl.pallas_call(
        _mla_kv_kernel,
        grid=(m // tm,),
        in_specs=[pl.BlockSpec((tm, r), lambda i: (i, ckv_col // r)),
                  pl.BlockSpec((r, n), lambda i: (0, 0)),
                  pl.BlockSpec((tm, LANES), lambda i: (i, kr_col // LANES))],
        out_specs=[pl.BlockSpec((tm, nk), lambda i: (i, 0)),
                   pl.BlockSpec((tm, nv), lambda i: (i, 0))],
        out_shape=[jax.ShapeDtypeStruct((m, nk), bf16), jax.ShapeDtypeStruct((m, nv), bf16)],
        compiler_params=_cparams(("parallel",)),
        name="mla_kv_up",
    )(ckvn, w_kv, krope)


def _mla_attn_kernel(q_ref, k_ref, v_ref, o_ref, s_ref, *, tq, tk):
    nq = q_ref.shape[0] // tq
    nkv = k_ref.shape[0] // tk

    def scores(q, c):
        return _dot_nt(q, k_ref[c * tk:(c + 1) * tk, :])

    def update(s, c, m, acc):
        m_new = jnp.maximum(m, jnp.max(s, axis=-1, keepdims=True))
        p = jnp.exp2(s - m_new)
        acc = jnp.exp2(m - m_new) * acc + _dot(p.astype(bf16), v_ref[c * tk:(c + 1) * tk, :])
        return m_new, acc

    s_ref[...] = scores(q_ref[0:tq, :], 0)

    def body(blk, carry):
        off = pl.multiple_of(blk * tq, tq)
        q = q_ref[pl.ds(off, tq), :]
        m = jnp.full((tq, 1), -jnp.inf, f32)
        acc = jnp.zeros((tq, MLA_V_PAD), f32)
        s_next = s_ref[...]
        for c in range(nkv):
            s_cur = s_next
            if c + 1 < nkv:
                s_next = scores(q, c + 1)
            else:
                nxt = pl.multiple_of(jnp.minimum(blk + 1, nq - 1) * tq, tq)
                s_ref[...] = scores(q_ref[pl.ds(nxt, tq), :], 0)
            m, acc = update(s_cur, c, m, acc)
        o_ref[pl.ds(off, tq), :] = (acc[:, :MLA_V] / acc[:, MLA_V:MLA_V + 1]).astype(o_ref.dtype)
        return carry

    lax.fori_loop(0, nq, body, 0)


def _mla_attention(q, k, v, tq=ATTN_TQ, tk=ATTN_TK):
    s = q.shape[0]
    return pl.pallas_call(
        functools.partial(_mla_attn_kernel, tq=tq, tk=tk),
        grid=(MLA_HEADS,),
        in_specs=[pl.BlockSpec((s, MLA_QK_PAD), lambda h: (0, h)),
                  pl.BlockSpec((s, MLA_QK_PAD), lambda h: (0, h)),
                  pl.BlockSpec((s, MLA_V_PAD), lambda h: (0, h))],
        out_specs=pl.BlockSpec((s, MLA_V), lambda h: (0, h)),
        out_shape=jax.ShapeDtypeStruct((s, MLA_HEADS * MLA_V), bf16),
        scratch_shapes=[pltpu.VMEM((tq, tk), f32)],
        compiler_params=_cparams(("parallel",)),
        name="mla_attention",
    )(q, k, v)


def _mem_attn_kernel(q_ref, k_ref, v_ref, o_ref):
    for h in range(MEM_HEADS):
        sl = slice(h * MEM_HD, (h + 1) * MEM_HD)
        s = _dot_nt(q_ref[:, sl], k_ref[:, sl])
        m = jnp.max(s, axis=-1, keepdims=True)
        p = jnp.exp(s - m)
        l = jnp.sum(p, axis=-1, keepdims=True)
        o = _dot(p.astype(bf16), v_ref[:, sl])
        o_ref[:, sl] = (o / l).astype(o_ref.dtype)


def _mem_attention(pr, q_col, k, v, tm=TM_LIGHT):
    s = pr.shape[0]
    nm, w = k.shape
    assert q_col % w == 0
    q = pr
    return pl.pallas_call(
        _mem_attn_kernel,
        grid=(s // tm,),
        in_specs=[pl.BlockSpec((tm, w), lambda i: (i, q_col // w)),
                  pl.BlockSpec((nm, w), lambda i: (0, 0)),
                  pl.BlockSpec((nm, w), lambda i: (0, 0))],
        out_specs=pl.BlockSpec((tm, w), lambda i: (i, 0)),
        out_shape=jax.ShapeDtypeStruct((s, w), bf16),
        compiler_params=_cparams(("parallel",)),
        name="mem_attention",
    )(q, k, v)


def _hgrn_levels(rev):
    levels = []
    sz = HG_CHUNK // 2
    while sz >= HG_SUB:
        blocks = []
        for base in range(0, HG_CHUNK, 2 * sz):
            if not rev:
                blocks.append((base + sz, base, sz, base + sz - 1))
            else:
                blocks.append((base, base + sz, sz, base + sz))
        levels.append(blocks)
        sz //= 2
    return levels


def _hgrn_intra(q_c, b_c, c_c, bbuf, cbuf, row0, *, rev):
    C = HG_CHUNK
    nb = C // HG_SUB
    lane = lax.broadcasted_iota(jnp.int32, (HG_SUB, C), 1)
    sub = lax.broadcasted_iota(jnp.int32, (HG_SUB, C), 0)
    blocks = []
    for i in range(nb):
        r0 = i * HG_SUB
        q_i = q_c[r0:r0 + HG_SUB]
        b_i = b_c[r0:r0 + HG_SUB]
        a_blk = jnp.zeros((HG_SUB, C), f32)
        for s in range(HG_SUB):
            w = q_i * jnp.exp2(b_i + cbuf[row0 + r0 + s:row0 + r0 + s + 1, :])
            a_blk = jnp.where(lane == r0 + s, jnp.sum(w, axis=-1, keepdims=True), a_blk)
        keep = (sub >= lane - r0) if not rev else (sub <= lane - r0)
        blocks.append(jnp.where(keep, a_blk, 0.0))
    for level in _hgrn_levels(rev):
        sz = level[0][2]
        qd, kd, valid = [], [], None
        k_pos = 0
        rq = lax.broadcasted_iota(jnp.int32, (len(level) * sz, C), 0)
        ck = lax.broadcasted_iota(jnp.int32, (len(level) * sz, C), 1)
        for j, (q0, k0, _, piv_row) in enumerate(level):
            piv = bbuf[row0 + piv_row:row0 + piv_row + 1, :]
            qd.append(q_c[q0:q0 + sz] * jnp.exp2(b_c[q0:q0 + sz] - piv))
            if k0 > k_pos:
                kd.append(jnp.zeros((k0 - k_pos, q_c.shape[1]), f32))
            kd.append(jnp.exp2(c_c[k0:k0 + sz] + piv))
            k_pos = k0 + sz
            ok = (rq >= j * sz) & (rq < (j + 1) * sz) & (ck >= k0) & (ck < k0 + sz)
            valid = ok if valid is None else (valid | ok)
        if k_pos < C:
            kd.append(jnp.zeros((C - k_pos, q_c.shape[1]), f32))
        r = _dot_nt(jnp.concatenate(qd, axis=0).astype(bf16), jnp.concatenate(kd, axis=0).astype(bf16))
        if len(level) > 1:
            r = jnp.where(valid, r, 0.0)
        for j, (q0, _, _, _) in enumerate(level):
            for t in range(sz // HG_SUB):
                i = q0 // HG_SUB + t
                blocks[i] = blocks[i] + r[j * sz + t * HG_SUB:j * sz + (t + 1) * HG_SUB]
    return jnp.concatenate(blocks, axis=0)


def _hgrn_block_kernel(qf_ref, vf_ref, ff_ref, qb_ref, vb_ref, fb_ref, lb_ref, of_ref, ob_ref,
                       sf_ref, sb_ref, bbuf, cbuf):
    n = pl.program_id(1)
    C = HG_CHUNK
    T = qf_ref.shape[0]
    nchunk = T // C

    @pl.when(n == 0)
    def _():
        sf_ref[...] = jnp.zeros_like(sf_ref)
        sb_ref[...] = jnp.zeros_like(sb_ref)

    r = lax.broadcasted_iota(jnp.int32, (C, C), 0)
    c = lax.broadcasted_iota(jnp.int32, (C, C), 1)
    dirs = []
    for d, (q_ref, v_ref, f_ref, rev) in enumerate(((qf_ref, vf_ref, ff_ref, False), (qb_ref, vb_ref, fb_ref, True))):
        tri = ((c <= r) if not rev else (c >= r)).astype(bf16)
        lb = lb_ref[d:d + 1, :]
        q = _silu(q_ref[...])
        v16 = v_ref[...].astype(bf16)
        f = lb + (1.0 - lb) * _sigmoid(f_ref[...])
        g = jnp.log(f) * LOG2E
        g_hi = g.astype(bf16)
        g_lo = (g - g_hi.astype(f32)).astype(bf16)
        b = jnp.concatenate([_dot(tri, g_hi[j * C:(j + 1) * C]) + _dot(tri, g_lo[j * C:(j + 1) * C])
                             for j in range(nchunk)], axis=0)
        cc = jnp.log(1.0 - f) * LOG2E - b
        bbuf[d] = b
        cbuf[d] = cc
        dirs.append((q, v16, b, cc, rev))

    intra = []
    for d, (q, v16, b, cc, rev) in enumerate(dirs):
        intra.append([_hgrn_intra(q[j * C:(j + 1) * C], b[j * C:(j + 1) * C], cc[j * C:(j + 1) * C],
                                  bbuf.at[d], cbuf.at[d], j * C, rev=rev).astype(bf16) for j in range(nchunk)])

    for d, (q, v16, b, cc, rev) in enumerate(dirs):
        s_ref, o_ref = (sf_ref, of_ref) if not rev else (sb_ref, ob_ref)
        qe = (q * jnp.exp2(b)).astype(bf16)
        order = range(nchunk) if not rev else range(nchunk - 1, -1, -1)
        edge = (C - 1) if not rev else 0
        upd = {}
        for j in order:
            b_edge = bbuf[d, j * C + edge:j * C + edge + 1, :]
            kdec = jnp.exp2(cc[j * C:(j + 1) * C] + b_edge).astype(bf16)
            upd[j] = (jnp.exp2(b_edge), _dot_tn(v16[j * C:(j + 1) * C], kdec))
        state = s_ref[...]
        for j in order:
            rows = slice(j * C, (j + 1) * C)
            o = _dot_nt(qe[rows], state.astype(bf16)) + _dot(intra[d][j], v16[rows])
            o_ref[rows, :] = o.astype(o_ref.dtype)
            state = upd[j][0] * state + upd[j][1]
        s_ref[...] = state


def _hgrn_scan(proj, lb, tb=HG_BLOCK):
    s = proj.shape[0]
    w = HG_HEADS * HG_D
    nb = s // tb
    hb = w // HG_D
    def fw(seg):
        return pl.BlockSpec((tb, HG_D), lambda h, n: (n, seg * hb + h))

    def bw(seg):
        return pl.BlockSpec((tb, HG_D), lambda h, n: (nb - 1 - n, seg * hb + h))

    return pl.pallas_call(
        _hgrn_block_kernel,
        grid=(HG_HEADS, nb),
        in_specs=[fw(0), fw(1), fw(2), bw(0), bw(1), bw(3),
                  pl.BlockSpec((2, HG_D), lambda h, n: (0, h))],
        out_specs=[pl.BlockSpec((tb, HG_D), lambda h, n: (n, h)),
                   pl.BlockSpec((tb, HG_D), lambda h, n: (nb - 1 - n, h))],
        out_shape=[jax.ShapeDtypeStruct((s, w), bf16), jax.ShapeDtypeStruct((s, w), bf16)],
        scratch_shapes=[pltpu.VMEM((HG_D, HG_D), f32), pltpu.VMEM((HG_D, HG_D), f32),
                        pltpu.VMEM((2, tb, HG_D), f32), pltpu.VMEM((2, tb, HG_D), f32)],
        compiler_params=_cparams(("parallel", "arbitrary")),
        name="hgrn_scan",
    )(proj, proj, proj, proj, proj, proj, lb)


def _lb_kernel(logit_ref, lb_ref, *, layer):
    x = logit_ref[...]
    e = jnp.exp(x - jnp.max(x, axis=1, keepdims=True))
    p = e / jnp.sum(e, axis=1, keepdims=True)
    acc = p[:, 0, :]
    for l in range(1, layer + 1):
        acc = acc + p[:, l, :]
    lb_ref[...] = acc


def _lower_bounds(logits, layer):
    z, _, w = logits.shape
    return pl.pallas_call(
        functools.partial(_lb_kernel, layer=layer),
        out_shape=jax.ShapeDtypeStruct((z, w), f32),
        name="hgrn_lower_bounds",
    )(logits)


def _merge_kernel(of_ref, ob_ref, hg_ref, ng_ref, y1_ref, y2_ref, w_ref, g0_ref, g1_ref, g2_ref, o_ref):
    acc = g1_ref[...].astype(f32) * _dot(y1_ref[...], w_ref[1])
    acc = acc + g2_ref[...].astype(f32) * _dot(y2_ref[...], w_ref[2])
    ng = ng_ref[...]
    y0 = []
    for h in range(HG_HEADS):
        sl = slice(h * HG_D, (h + 1) * HG_D)
        o = of_ref[:, sl].astype(f32) + ob_ref[:, sl].astype(f32)
        ms = jnp.mean(o * o, axis=-1, keepdims=True)
        y0.append((o * lax.rsqrt(ms + RMS_EPS) * ng * _sigmoid(hg_ref[:, sl])).astype(bf16))
    acc = acc + g0_ref[...].astype(f32) * _dot(jnp.concatenate(y0, axis=1), w_ref[0])
    o_ref[...] = acc.astype(o_ref.dtype)


def _merge(o_f, o_b, proj_hg, norm_g, y1, y2, w_branch, gsig, tm=TM_ROW):
    m, kb = y1.shape
    d = w_branch.shape[2]
    hg_gate_blk = 4
    yspec = pl.BlockSpec((tm, kb), lambda i: (i, 0))
    def gspec(b):
        return pl.BlockSpec((tm, d), lambda i: (i, b))

    return pl.pallas_call(
        _merge_kernel,
        grid=(m // tm,),
        in_specs=[yspec, yspec,
                  pl.BlockSpec((tm, kb), lambda i: (i, hg_gate_blk)),
                  pl.BlockSpec((1, HG_D), lambda i: (0, 0)),
                  yspec, yspec,
                  pl.BlockSpec((N_BRANCH, kb, d), lambda i: (0, 0, 0), pipeline_mode=pl.Buffered(1)),
                  gspec(0), gspec(1), gspec(2)],
        out_specs=pl.BlockSpec((tm, d), lambda i: (i, 0)),
        out_shape=jax.ShapeDtypeStruct((m, d), bf16),
        compiler_params=_cparams(("parallel",)),
        name="branch_merge",
    )(o_f, o_b, proj_hg, norm_g.reshape(1, HG_D), y1, y2, w_branch, gsig, gsig, gsig)


def _proj_ln_kernel(a_ref, w_ref, res_ref, rg_ref, rb_ref, g_ref, b_ref, o32_ref, o16_ref, *, alpha, res_is_raw):
    rows_per_group = a_ref.shape[0] // PROJ_LN_GROUPS
    groups = [slice(r * rows_per_group, (r + 1) * rows_per_group) for r in range(PROJ_LN_GROUPS)]
    proj = [_dot(a_ref[rows, :], w_ref[...]) for rows in groups]
    for rows, p in zip(groups, proj):
        res = res_ref[rows, :]
        if res_is_raw:
            res = _ln_rows(res, rg_ref[...], rb_ref[...])
        y = _ln_rows(alpha * res + p, g_ref[...], b_ref[...])
        o32_ref[rows, :] = y
        o16_ref[rows, :] = y.astype(bf16)


def _proj_residual_ln(a, w, res, res_ln, g, b, alpha, tm=TM_ROW):
    m, k = a.shape
    d = w.shape[1]
    rg, rb = res_ln if res_ln is not None else (g, b)
    vec = pl.BlockSpec((1, d), lambda i: (0, 0))
    return pl.pallas_call(
        functools.partial(_proj_ln_kernel, alpha=alpha, res_is_raw=res_ln is not None),
        grid=(m // tm,),
        in_specs=[pl.BlockSpec((tm, k), lambda i: (i, 0)),
                  pl.BlockSpec((k, d), lambda i: (0, 0)),
                  pl.BlockSpec((tm, d), lambda i: (i, 0)),
                  vec, vec, vec, vec],
        out_specs=[pl.BlockSpec((tm, d), lambda i: (i, 0)),
                   pl.BlockSpec((tm, d), lambda i: (i, 0))],
        out_shape=[jax.ShapeDtypeStruct((m, d), f32), jax.ShapeDtypeStruct((m, d), bf16)],
        compiler_params=_cparams(("parallel",)),
        name="out_proj_ln",
    )(a, w, res, rg.reshape(1, d), rb.reshape(1, d), g.reshape(1, d), b.reshape(1, d))


def _ffn_kernel(x_ref, wg_ref, wu_ref, wd_ref, res_ref, g_ref, b_ref, o_ref, *, alpha):
    j = pl.program_id(1)

    @pl.when(j == 0)
    def _():
        o_ref[...] = jnp.zeros_like(o_ref)

    x = x_ref[...]
    hid = _silu(_dot(x, wg_ref[...].astype(bf16))) * _dot(x, wu_ref[...].astype(bf16))
    o_ref[...] += _dot(hid.astype(bf16), wd_ref[...].astype(bf16))

    @pl.when(j == pl.num_programs(1) - 1)
    def _():
        o_ref[...] = _ln_rows(alpha * res_ref[...] + o_ref[...], g_ref[...], b_ref[...])


def _ffn_residual_ln(x16, x32, wg, wu, wd, g, b, alpha, tm=FFN_TM, tf=FFN_TF):
    m, d = x16.shape
    ff = wg.shape[1]
    once = pl.Buffered(1)
    return pl.pallas_call(
        functools.partial(_ffn_kernel, alpha=alpha),
        grid=(m // tm, ff // tf),
        in_specs=[pl.BlockSpec((tm, d), lambda i, j: (i, 0)),
                  pl.BlockSpec((d, tf), lambda i, j: (0, j)),
                  pl.BlockSpec((d, tf), lambda i, j: (0, j)),
                  pl.BlockSpec((tf, d), lambda i, j: (j, 0)),
                  pl.BlockSpec((tm, d), lambda i, j: (i, 0), pipeline_mode=once),
                  pl.BlockSpec((1, d), lambda i, j: (0, 0)),
                  pl.BlockSpec((1, d), lambda i, j: (0, 0))],
        out_specs=pl.BlockSpec((tm, d), lambda i, j: (i, 0)),
        out_shape=jax.ShapeDtypeStruct((m, d), f32),
        compiler_params=_cparams(("parallel", "arbitrary")),
        name="ffn_ln",
    )(x16, wg, wu, wd, x32, g.reshape(1, d), b.reshape(1, d))


def _rope_tables(positions):
    half = MLA_ROPE // 2
    inv_freq = jnp.power(ROPE_THETA, -jnp.arange(half, dtype=f32) / half)
    ang = positions.astype(f32)[..., None] * inv_freq
    cos, sin = jnp.cos(ang), jnp.sin(ang)
    z = jnp.zeros_like(cos)
    pad = jnp.zeros((ang.shape[0], LANES - MLA_ROPE), f32)
    c = jnp.concatenate([cos, cos, pad], axis=-1)
    s1 = jnp.concatenate([-sin, z, pad], axis=-1)
    s2 = jnp.concatenate([z, sin, pad], axis=-1)
    return c, s1, s2


def _layer(res, res_ln, h16, mem16, rope, lb, alpha, w_in, hgrn_norm_g, g_cq, g_ckv, w_uq, w_ukv, w_memkv,
           w_branch, w_o, ln1_g, ln1_b, w_gate, w_up, w_down, ln2_g, ln2_b):
    s, d = res.shape
    hgw = HG_HEADS * HG_D
    q_rank = g_cq.shape[0]
    kv_rank = g_ckv.shape[0]
    memw = MEM_HEADS * MEM_HD
    o_cq = 5 * hgw
    o_ckv = o_cq + q_rank
    o_kr = o_ckv + kv_rank
    o_qm = o_kr + MLA_ROPE
    o_gt = o_qm + memw
    rope_c, rope_s1, rope_s2 = rope

    wt = jnp.swapaxes(w_in, 0, 1)

    proj_hg = _matmul(h16, wt, wt_rows=(0, o_cq), tm=TM_IN_PROJ, tn=TN_IN_PROJ, out_dtype=f32,
                      name="in_proj_hgrn")
    gsig = _matmul(h16, wt, wt_rows=(o_gt, N_BRANCH * d), tm=TM_IN_PROJ, tn=TN_IN_PROJ, out_dtype=bf16,
                   epilogue=_sigmoid, name="in_proj_gates")
    pr = _in_proj_rest(h16, wt, o_qm, memw, o_cq, o_ckv, o_kr, g_cq, g_ckv, rope)
    c_qm = 0
    c_cq = c_qm + memw
    c_ckv = c_cq + q_rank
    c_kr = c_ckv + kv_rank

    o_f, o_b = _hgrn_scan(proj_hg, lb)

    zq = jnp.zeros((q_rank, MLA_HEADS, MLA_QK_PAD - MLA_QK), f32)
    w_q = jnp.concatenate([w_uq.reshape(q_rank, MLA_HEADS, MLA_QK), zq], axis=-1)
    w_q = w_q.reshape(q_rank, MLA_HEADS * MLA_QK_PAD).astype(bf16)
    w_kv3 = w_ukv.reshape(kv_rank, MLA_HEADS, MLA_NOPE + MLA_V)
    w_kv = jnp.concatenate([w_kv3[:, :, :MLA_NOPE].reshape(kv_rank, -1),
                            w_kv3[:, :, MLA_NOPE:].reshape(kv_rank, -1)], axis=-1).astype(bf16)
    q = _mla_q(pr, c_cq, w_q, rope_c, rope_s1, rope_s2)
    k, v = _mla_kv(pr, c_ckv, c_kr, w_kv)
    y_mla = _mla_attention(q, k, v)

    kv_mem = _matmul(mem16, w_memkv, tm=mem16.shape[0], tn=TN_IN_PROJ, out_dtype=bf16, name="mem_kv_proj")
    y_mem = _mem_attention(pr, c_qm, kv_mem[:, :memw], kv_mem[:, memw:])

    merged = _merge(o_f, o_b, proj_hg, hgrn_norm_g, y_mla, y_mem, w_branch.astype(bf16), gsig)
    x32, x16 = _proj_residual_ln(merged, w_o.astype(bf16), res, res_ln, ln1_g, ln1_b, alpha)
    return _ffn_residual_ln(x16, x32, w_gate, w_up, w_down, ln2_g, ln2_b, alpha)


def kernel(x, mem, positions, ln_emb_g, ln_emb_b, hgrn_lb_logits, w_in, hgrn_norm_g, mla_g_cq, mla_g_ckv,
           mla_w_uq, mla_w_ukv, mem_w_kv, w_branch, w_o, ln1_g, ln1_b, w_ffn_gate, w_ffn_up, w_ffn_down,
           ln2_g, ln2_b):
    bsz = x.shape[0]
    depth = w_in.shape[0]
    alpha = (2.0 * depth) ** 0.25
    outs = []
    for bi in range(bsz):
        rope = _rope_tables(positions[bi])
        mem16 = mem[bi].astype(bf16)
        h16 = _layernorm_bf16(x[bi], ln_emb_g, ln_emb_b)
        res, res_ln = x[bi], (ln_emb_g, ln_emb_b)
        for l in range(depth):
            lb = _lower_bounds(hgrn_lb_logits, l)
            res = _layer(res, res_ln, h16, mem16, rope, lb, alpha, w_in[l], hgrn_norm_g[l], mla_g_cq[l],
                         mla_g_ckv[l], mla_w_uq[l], mla_w_ukv[l], mem_w_kv[l], w_branch[l], w_o[l], ln1_g[l],
                         ln1_b[l], w_ffn_gate[l], w_ffn_up[l], w_ffn_down[l], ln2_g[l], ln2_b[l])
            res_ln = None
            if l + 1 < depth:
                h16 = res.astype(bf16)
        outs.append(res)
    return jnp.stack(outs)
```

```python
import functools

import jax
import jax.numpy as jnp
from jax import lax
from jax.experimental import pallas as pl
from jax.experimental.pallas import tpu as pltpu

HG_HEADS = 8
HG_D = 128
MLA_HEADS = 8
MLA_NOPE = 128
MLA_ROPE = 64
MLA_V = 128
MLA_QK = MLA_NOPE + MLA_ROPE
MLA_QK_PAD = 256
MLA_V_PAD = 256
LOG2E = 1.4426950408889634
MEM_HEADS = 4
MEM_HD = 256
N_BRANCH = 3
ROPE_THETA = 10000.0
LN_EPS = 1e-5
RMS_EPS = 1e-6

LANES = 128
SUBLANES = 8
VMEM_LIMIT = 56 * 1024 * 1024

TM_IN_PROJ = 2048
TN_IN_PROJ = 512
TM_ROW = 512
TM_LIGHT = 1024
ATTN_TQ = 512
ATTN_TK = 256
FFN_TM = 1024
FFN_TF = 256
PROJ_LN_GROUPS = 2
HG_BLOCK = 1024
HG_CHUNK = 64
HG_SUB = SUBLANES

bf16 = jnp.bfloat16
f32 = jnp.float32


def _cparams(sem):
    return pltpu.CompilerParams(dimension_semantics=sem, vmem_limit_bytes=VMEM_LIMIT)


def _dot(a, b):
    return jnp.dot(a, b, preferred_element_type=f32)


def _dot_nt(a, b):
    return lax.dot_general(a, b, (((1,), (1,)), ((), ())), preferred_element_type=f32)


def _dot_tn(a, b):
    return lax.dot_general(a, b, (((0,), (0,)), ((), ())), preferred_element_type=f32)


def _ln_rows(x, g, b):
    mu = jnp.mean(x, axis=-1, keepdims=True)
    xc = x - mu
    var = jnp.mean(xc * xc, axis=-1, keepdims=True)
    return xc * lax.rsqrt(var + LN_EPS) * g + b


def _ln_kernel(x_ref, g_ref, b_ref, o16_ref):
    o16_ref[...] = _ln_rows(x_ref[...], g_ref[...], b_ref[...]).astype(bf16)


def _layernorm_bf16(x, g, b, tm=TM_LIGHT):
    m, d = x.shape
    return pl.pallas_call(
        _ln_kernel,
        grid=(m // tm,),
        in_specs=[pl.BlockSpec((tm, d), lambda i: (i, 0)),
                  pl.BlockSpec((1, d), lambda i: (0, 0)),
                  pl.BlockSpec((1, d), lambda i: (0, 0))],
        out_specs=pl.BlockSpec((tm, d), lambda i: (i, 0)),
        out_shape=jax.ShapeDtypeStruct((m, d), bf16),
        compiler_params=_cparams(("parallel",)),
        name="ln_embed",
    )(x, g.reshape(1, d), b.reshape(1, d))


def _mm_kernel(a_ref, w_ref, o_ref, *, epilogue, w_transposed):
    w = w_ref[...].astype(bf16)
    acc = _dot_nt(a_ref[...], w) if w_transposed else _dot(a_ref[...], w)
    o_ref[...] = epilogue(acc).astype(o_ref.dtype)


def _matmul(a, w, *, tm, tn, out_dtype, epilogue=lambda acc: acc, name, wt_rows=None):
    m, k = a.shape
    if wt_rows is None:
        n = w.shape[1]
        w_spec = pl.BlockSpec((k, tn), lambda i, j: (0, j))
    else:
        row0, n = wt_rows
        assert row0 % SUBLANES == 0 and w.shape[1] == k
        w_spec = pl.BlockSpec((pl.Element(tn), pl.Element(k)),
                              lambda i, j: ((row0 // SUBLANES + j * (tn // SUBLANES)) * SUBLANES, 0))
    return pl.pallas_call(
        functools.partial(_mm_kernel, epilogue=epilogue, w_transposed=wt_rows is not None),
        grid=(m // tm, n // tn),
        in_specs=[pl.BlockSpec((tm, k), lambda i, j: (i, 0)), w_spec],
        out_specs=pl.BlockSpec((tm, tn), lambda i, j: (i, j)),
        out_shape=jax.ShapeDtypeStruct((m, n), out_dtype),
        compiler_params=_cparams(("parallel", "parallel")),
        name=name,
    )(a, w)


def _sigmoid(x):
    return 0.5 * jnp.tanh(0.5 * x) + 0.5


def _silu(x):
    return x * _sigmoid(x)


def _rms_epilogue(acc, g):
    ms = jnp.mean(acc * acc, axis=-1, keepdims=True)
    return acc * lax.rsqrt(ms + RMS_EPS) * g


def _rope_block(blk, c, s1, s2):
    half = MLA_ROPE // 2
    return blk * c + pltpu.roll(blk, LANES - half, 1) * s1 + pltpu.roll(blk, half, 1) * s2


def _in_proj_rest_kernel(a_ref, w_ref, gcq_ref, gckv_ref, c_ref, s1_ref, s2_ref, o_ref, *, n_qmem):
    j = pl.program_id(1)
    acc = _dot_nt(a_ref[...], w_ref[...].astype(bf16))

    @pl.when(j < n_qmem)
    def _():
        o_ref[...] = (acc * (MEM_HD ** -0.5)).astype(o_ref.dtype)

    @pl.when(j == n_qmem)
    def _():
        o_ref[...] = _rms_epilogue(acc, gcq_ref[...]).astype(o_ref.dtype)

    @pl.when(j == n_qmem + 1)
    def _():
        o_ref[...] = _rms_epilogue(acc, gckv_ref[...]).astype(o_ref.dtype)

    @pl.when(j == n_qmem + 2)
    def _():
        o_ref[:, :LANES] = _rope_block(acc[:, :LANES], c_ref[...], s1_ref[...], s2_ref[...]).astype(o_ref.dtype)
        o_ref[:, LANES:] = jnp.zeros((acc.shape[0], acc.shape[1] - LANES), o_ref.dtype)


def _in_proj_rest(h16, wt, o_qm, n_qm, o_cq, o_ckv, o_kr, g_cq, g_ckv, rope, tm=TM_IN_PROJ, tn=TN_IN_PROJ):
    m, k = h16.shape
    n_qmem = n_qm // tn
    assert n_qm % tn == 0 and g_cq.shape[0] == tn and g_ckv.shape[0] == tn
    assert all(o % SUBLANES == 0 for o in (o_qm, o_cq, o_ckv, o_kr))
    n_tiles = n_qmem + 3

    def w_row(i, j):
        r = jnp.where(j < n_qmem, o_qm // SUBLANES + j * (tn // SUBLANES),
                      jnp.where(j == n_qmem, o_cq // SUBLANES,
                                jnp.where(j == n_qmem + 1, o_ckv // SUBLANES, o_kr // SUBLANES)))
        return (r * SUBLANES, 0)

    vec = pl.BlockSpec((1, tn), lambda i, j: (0, 0))
    tab = pl.BlockSpec((tm, LANES), lambda i, j: (i, 0))
    return pl.pallas_call(
        functools.partial(_in_proj_rest_kernel, n_qmem=n_qmem),
        grid=(m // tm, n_tiles),
        in_specs=[pl.BlockSpec((tm, k), lambda i, j: (i, 0)),
                  pl.BlockSpec((pl.Element(tn), pl.Element(k)), w_row),
                  vec, vec, tab, tab, tab],
        out_specs=pl.BlockSpec((tm, tn), lambda i, j: (i, j)),
        out_shape=jax.ShapeDtypeStruct((m, n_tiles * tn), bf16),
        compiler_params=_cparams(("parallel", "arbitrary")),
        name="in_proj_rest",
    )(h16, wt, g_cq.reshape(1, tn), g_ckv.reshape(1, tn), *rope)


def _mla_q_kernel(cq_ref, w_ref, c_ref, s1_ref, s2_ref, o_ref):
    q = _dot(cq_ref[...], w_ref[...]) * (MLA_QK ** -0.5 * LOG2E)
    c, s1, s2 = c_ref[...], s1_ref[...], s2_ref[...]
    for h in range(MLA_HEADS):
        base = h * MLA_QK_PAD
        o_ref[:, base:base + MLA_NOPE] = q[:, base:base + MLA_NOPE].astype(bf16)
        o_ref[:, base + MLA_NOPE:base + MLA_QK_PAD] = _rope_block(
            q[:, base + MLA_NOPE:base + MLA_QK_PAD], c, s1, s2).astype(bf16)


def _mla_q(pr, cq_col, w_q, rope_c, rope_s1, rope_s2, tm=TM_LIGHT):
    m = pr.shape[0]
    r, n = w_q.shape
    assert cq_col % r == 0
    cqn = pr
    return pl.pallas_call(
        _mla_q_kernel,
        grid=(m // tm,),
        in_specs=[pl.BlockSpec((tm, r), lambda i: (i, cq_col // r)),
                  pl.BlockSpec((r, n), lambda i: (0, 0)),
                  pl.BlockSpec((tm, LANES), lambda i: (i, 0)),
                  pl.BlockSpec((tm, LANES), lambda i: (i, 0)),
                  pl.BlockSpec((tm, LANES), lambda i: (i, 0))],
        out_specs=pl.BlockSpec((tm, n), lambda i: (i, 0)),
        out_shape=jax.ShapeDtypeStruct((m, n), bf16),
        compiler_params=_cparams(("parallel",)),
        name="mla_q_up",
    )(cqn, w_q, rope_c, rope_s1, rope_s2)


def _mla_kv_kernel(ckv_ref, w_ref, kr_ref, k_ref, v_ref):
    kv = _dot(ckv_ref[...], w_ref[...])
    kr = kr_ref[...]
    nk = MLA_HEADS * MLA_NOPE
    for h in range(MLA_HEADS):
        base = h * MLA_QK_PAD
        k_ref[:, base:base + MLA_NOPE] = kv[:, h * MLA_NOPE:(h + 1) * MLA_NOPE].astype(bf16)
        k_ref[:, base + MLA_NOPE:base + MLA_QK_PAD] = kr
    lane = lax.broadcasted_iota(jnp.int32, (kv.shape[0], MLA_V_PAD - MLA_V), 1)
    ones_col = jnp.where(lane == 0, 1.0, 0.0).astype(bf16)
    for h in range(MLA_HEADS):
        base = h * MLA_V_PAD
        v_ref[:, base:base + MLA_V] = kv[:, nk + h * MLA_V:nk + (h + 1) * MLA_V].astype(bf16)
        v_ref[:, base + MLA_V:base + MLA_V_PAD] = ones_col


def _mla_kv(pr, ckv_col, kr_col, w_kv, tm=TM_LIGHT):
    m = pr.shape[0]
    r, n = w_kv.shape
    assert ckv_col % r == 0 and kr_col % LANES == 0
    ckvn = krope = pr
    nk = MLA_HEADS * MLA_QK_PAD
    nv = MLA_HEADS * MLA_V_PAD
    return pl.pallas_call(
        _mla_kv_kernel,
        grid=(m // tm,),
        in_specs=[pl.BlockSpec((tm, r), lambda i: (i, ckv_col // r)),
                  pl.BlockSpec((r, n), lambda i: (0, 0)),
                  pl.BlockSpec((tm, LANES), lambda i: (i, kr_col // LANES))],
        out_specs=[pl.BlockSpec((tm, nk), lambda i: (i, 0)),
                   pl.BlockSpec((tm, nv), lambda i: (i, 0))],
        out_shape=[jax.ShapeDtypeStruct((m, nk), bf16), jax.ShapeDtypeStruct((m, nv), bf16)],
        compiler_params=_cparams(("parallel",)),
        name="mla_kv_up",
    )(ckvn, w_kv, krope)


def _mla_attn_kernel(q_ref, k_ref, v_ref, o_ref, s_ref, *, tq, tk):
    nq = q_ref.shape[0] // tq
    nkv = k_ref.shape[0] // tk

    def scores(q, c):
        return _dot_nt(q, k_ref[c * tk:(c + 1) * tk, :])

    def update(s, c, m, acc):
        m_new = jnp.maximum(m, jnp.max(s, axis=-1, keepdims=True))
        p = jnp.exp2(s - m_new)
        acc = jnp.exp2(m - m_new) * acc + _dot(p.astype(bf16), v_ref[c * tk:(c + 1) * tk, :])
        return m_new, acc

    s_ref[...] = scores(q_ref[0:tq, :], 0)

    def body(blk, carry):
        off = pl.multiple_of(blk * tq, tq)
        q = q_ref[pl.ds(off, tq), :]
        m = jnp.full((tq, 1), -jnp.inf, f32)
        acc = jnp.zeros((tq, MLA_V_PAD), f32)
        s_next = s_ref[...]
        for c in range(nkv):
            s_cur = s_next
            if c + 1 < nkv:
                s_next = scores(q, c + 1)
            else:
                nxt = pl.multiple_of(jnp.minimum(blk + 1, nq - 1) * tq, tq)
                s_ref[...] = scores(q_ref[pl.ds(nxt, tq), :], 0)
            m, acc = update(s_cur, c, m, acc)
        o_ref[pl.ds(off, tq), :] = (acc[:, :MLA_V] / acc[:, MLA_V:MLA_V + 1]).astype(o_ref.dtype)
        return carry

    lax.fori_loop(0, nq, body, 0)


def _mla_attention(q, k, v, tq=ATTN_TQ, tk=ATTN_TK):
    s = q.shape[0]
    return pl.pallas_call(
        functools.partial(_mla_attn_kernel, tq=tq, tk=tk),
        grid=(MLA_HEADS,),
        in_specs=[pl.BlockSpec((s, MLA_QK_PAD), lambda h: (0, h)),
                  pl.BlockSpec((s, MLA_QK_PAD), lambda h: (0, h)),
                  pl.BlockSpec((s, MLA_V_PAD), lambda h: (0, h))],
        out_specs=pl.BlockSpec((s, MLA_V), lambda h: (0, h)),
        out_shape=jax.ShapeDtypeStruct((s, MLA_HEADS * MLA_V), bf16),
        scratch_shapes=[pltpu.VMEM((tq, tk), f32)],
        compiler_params=_cparams(("parallel",)),
        name="mla_attention",
    )(q, k, v)


def _mem_attn_kernel(q_ref, k_ref, v_ref, o_ref):
    for h in range(MEM_HEADS):
        sl = slice(h * MEM_HD, (h + 1) * MEM_HD)
        s = _dot_nt(q_ref[:, sl], k_ref[:, sl])
        m = jnp.max(s, axis=-1, keepdims=True)
        p = jnp.exp(s - m)
        l = jnp.sum(p, axis=-1, keepdims=True)
        o = _dot(p.astype(bf16), v_ref[:, sl])
        o_ref[:, sl] = (o / l).astype(o_ref.dtype)


def _mem_attention(pr, q_col, k, v, tm=TM_LIGHT):
    s = pr.shape[0]
    nm, w = k.shape
    assert q_col % w == 0
    q = pr
    return pl.pallas_call(
        _mem_attn_kernel,
        grid=(s // tm,),
        in_specs=[pl.BlockSpec((tm, w), lambda i: (i, q_col // w)),
                  pl.BlockSpec((nm, w), lambda i: (0, 0)),
                  pl.BlockSpec((nm, w), lambda i: (0, 0))],
        out_specs=pl.BlockSpec((tm, w), lambda i: (i, 0)),
        out_shape=jax.ShapeDtypeStruct((s, w), bf16),
        compiler_params=_cparams(("parallel",)),
        name="mem_attention",
    )(q, k, v)


def _hgrn_levels(rev):
    levels = []
    sz = HG_CHUNK // 2
    while sz >= HG_SUB:
        blocks = []
        for base in range(0, HG_CHUNK, 2 * sz):
            if not rev:
                blocks.append((base + sz, base, sz, base + sz - 1))
            else:
                blocks.append((base, base + sz, sz, base + sz))
        levels.append(blocks)
        sz //= 2
    return levels


def _hgrn_intra(q_c, b_c, c_c, bbuf, cbuf, row0, *, rev):
    C = HG_CHUNK
    nb = C // HG_SUB
    lane = lax.broadcasted_iota(jnp.int32, (HG_SUB, C), 1)
    sub = lax.broadcasted_iota(jnp.int32, (HG_SUB, C), 0)
    blocks = []
    for i in range(nb):
        r0 = i * HG_SUB
        q_i = q_c[r0:r0 + HG_SUB]
        b_i = b_c[r0:r0 + HG_SUB]
        a_blk = jnp.zeros((HG_SUB, C), f32)
        for s in range(HG_SUB):
            w = q_i * jnp.exp2(b_i + cbuf[row0 + r0 + s:row0 + r0 + s + 1, :])
            a_blk = jnp.where(lane == r0 + s, jnp.sum(w, axis=-1, keepdims=True), a_blk)
        keep = (sub >= lane - r0) if not rev else (sub <= lane - r0)
        blocks.append(jnp.where(keep, a_blk, 0.0))
    for level in _hgrn_levels(rev):
        sz = level[0][2]
        qd, kd, valid = [], [], None
        k_pos = 0
        rq = lax.broadcasted_iota(jnp.int32, (len(level) * sz, C), 0)
        ck = lax.broadcasted_iota(jnp.int32, (len(level) * sz, C), 1)
        for j, (q0, k0, _, piv_row) in enumerate(level):
            piv = bbuf[row0 + piv_row:row0 + piv_row + 1, :]
            qd.append(q_c[q0:q0 + sz] * jnp.exp2(b_c[q0:q0 + sz] - piv))
            if k0 > k_pos:
                kd.append(jnp.zeros((k0 - k_pos, q_c.shape[1]), f32))
            kd.append(jnp.exp2(c_c[k0:k0 + sz] + piv))
            k_pos = k0 + sz
            ok = (rq >= j * sz) & (rq < (j + 1) * sz) & (ck >= k0) & (ck < k0 + sz)
            valid = ok if valid is None else (valid | ok)
        if k_pos < C:
            kd.append(jnp.zeros((C - k_pos, q_c.shape[1]), f32))
        r = _dot_nt(jnp.concatenate(qd, axis=0).astype(bf16), jnp.concatenate(kd, axis=0).astype(bf16))
        if len(level) > 1:
            r = jnp.where(valid, r, 0.0)
        for j, (q0, _, _, _) in enumerate(level):
            for t in range(sz // HG_SUB):
                i = q0 // HG_SUB + t
                blocks[i] = blocks[i] + r[j * sz + t * HG_SUB:j * sz + (t + 1) * HG_SUB]
    return jnp.concatenate(blocks, axis=0)


def _hgrn_block_kernel(qf_ref, vf_ref, ff_ref, qb_ref, vb_ref, fb_ref, lb_ref, of_ref, ob_ref,
                       sf_ref, sb_ref, bbuf, cbuf):
    n = pl.program_id(1)
    C = HG_CHUNK
    T = qf_ref.shape[0]
    nchunk = T // C

    @pl.when(n == 0)
    def _():
        sf_ref[...] = jnp.zeros_like(sf_ref)
        sb_ref[...] = jnp.zeros_like(sb_ref)

    r = lax.broadcasted_iota(jnp.int32, (C, C), 0)
    c = lax.broadcasted_iota(jnp.int32, (C, C), 1)
    dirs = []
    for d, (q_ref, v_ref, f_ref, rev) in enumerate(((qf_ref, vf_ref, ff_ref, False), (qb_ref, vb_ref, fb_ref, True))):
        tri = ((c <= r) if not rev else (c >= r)).astype(bf16)
        lb = lb_ref[d:d + 1, :]
        q = _silu(q_ref[...])
        v16 = v_ref[...].astype(bf16)
        f = lb + (1.0 - lb) * _sigmoid(f_ref[...])
        g = jnp.log(f) * LOG2E
        g_hi = g.astype(bf16)
        g_lo = (g - g_hi.astype(f32)).astype(bf16)
        b = jnp.concatenate([_dot(tri, g_hi[j * C:(j + 1) * C]) + _dot(tri, g_lo[j * C:(j + 1) * C])
                             for j in range(nchunk)], axis=0)
        cc = jnp.log(1.0 - f) * LOG2E - b
        bbuf[d] = b
        cbuf[d] = cc
        dirs.append((q, v16, b, cc, rev))

    intra = []
    for d, (q, v16, b, cc, rev) in enumerate(dirs):
        intra.append([_hgrn_intra(q[j * C:(j + 1) * C], b[j * C:(j + 1) * C], cc[j * C:(j + 1) * C],
                                  bbuf.at[d], cbuf.at[d], j * C, rev=rev).astype(bf16) for j in range(nchunk)])

    for d, (q, v16, b, cc, rev) in enumerate(dirs):
        s_ref, o_ref = (sf_ref, of_ref) if not rev else (sb_ref, ob_ref)
        qe = (q * jnp.exp2(b)).astype(bf16)
        order = range(nchunk) if not rev else range(nchunk - 1, -1, -1)
        edge = (C - 1) if not rev else 0
        upd = {}
        for j in order:
            b_edge = bbuf[d, j * C + edge:j * C + edge + 1, :]
            kdec = jnp.exp2(cc[j * C:(j + 1) * C] + b_edge).astype(bf16)
            upd[j] = (jnp.exp2(b_edge), _dot_tn(v16[j * C:(j + 1) * C], kdec))
        state = s_ref[...]
        for j in order:
            rows = slice(j * C, (j + 1) * C)
            o = _dot_nt(qe[rows], state.astype(bf16)) + _dot(intra[d][j], v16[rows])
            o_ref[rows, :] = o.astype(o_ref.dtype)
            state = upd[j][0] * state + upd[j][1]
        s_ref[...] = state


def _hgrn_scan(proj, lb, tb=HG_BLOCK):
    s = proj.shape[0]
    w = HG_HEADS * HG_D
    nb = s // tb
    hb = w // HG_D
    def fw(seg):
        return pl.BlockSpec((tb, HG_D), lambda h, n: (n, seg * hb + h))

    def bw(seg):
        return pl.BlockSpec((tb, HG_D), lambda h, n: (nb - 1 - n, seg * hb + h))

    return pl.pallas_call(
        _hgrn_block_kernel,
        grid=(HG_HEADS, nb),
        in_specs=[fw(0), fw(1), fw(2), bw(0), bw(1), bw(3),
                  pl.BlockSpec((2, HG_D), lambda h, n: (0, h))],
        out_specs=[pl.BlockSpec((tb, HG_D), lambda h, n: (n, h)),
                   pl.BlockSpec((tb, HG_D), lambda h, n: (nb - 1 - n, h))],
        out_shape=[jax.ShapeDtypeStruct((s, w), bf16), jax.ShapeDtypeStruct((s, w), bf16)],
        scratch_shapes=[pltpu.VMEM((HG_D, HG_D), f32), pltpu.VMEM((HG_D, HG_D), f32),
                        pltpu.VMEM((2, tb, HG_D), f32), pltpu.VMEM((2, tb, HG_D), f32)],
        compiler_params=_cparams(("parallel", "arbitrary")),
        name="hgrn_scan",
    )(proj, proj, proj, proj, proj, proj, lb)


def _lb_kernel(logit_ref, lb_ref, *, layer):
    x = logit_ref[...]
    e = jnp.exp(x - jnp.max(x, axis=1, keepdims=True))
    p = e / jnp.sum(e, axis=1, keepdims=True)
    acc = p[:, 0, :]
    for l in range(1, layer + 1):
        acc = acc + p[:, l, :]
    lb_ref[...] = acc


def _lower_bounds(logits, layer):
    z, _, w = logits.shape
    return pl.pallas_call(
        functools.partial(_lb_kernel, layer=layer),
        out_shape=jax.ShapeDtypeStruct((z, w), f32),
        name="hgrn_lower_bounds",
    )(logits)


def _merge_kernel(of_ref, ob_ref, hg_ref, ng_ref, y1_ref, y2_ref, w_ref, g0_ref, g1_ref, g2_ref, o_ref):
    acc = g1_ref[...].astype(f32) * _dot(y1_ref[...], w_ref[1])
    acc = acc + g2_ref[...].astype(f32) * _dot(y2_ref[...], w_ref[2])
    ng = ng_ref[...]
    y0 = []
    for h in range(HG_HEADS):
        sl = slice(h * HG_D, (h + 1) * HG_D)
        o = of_ref[:, sl].astype(f32) + ob_ref[:, sl].astype(f32)
        ms = jnp.mean(o * o, axis=-1, keepdims=True)
        y0.append((o * lax.rsqrt(ms + RMS_EPS) * ng * _sigmoid(hg_ref[:, sl])).astype(bf16))
    acc = acc + g0_ref[...].astype(f32) * _dot(jnp.concatenate(y0, axis=1), w_ref[0])
    o_ref[...] = acc.astype(o_ref.dtype)


def _merge(o_f, o_b, proj_hg, norm_g, y1, y2, w_branch, gsig, tm=TM_ROW):
    m, kb = y1.shape
    d = w_branch.shape[2]
    hg_gate_blk = 4
    yspec = pl.BlockSpec((tm, kb), lambda i: (i, 0))
    def gspec(b):
        return pl.BlockSpec((tm, d), lambda i: (i, b))

    return pl.pallas_call(
        _merge_kernel,
        grid=(m // tm,),
        in_specs=[yspec, yspec,
                  pl.BlockSpec((tm, kb), lambda i: (i, hg_gate_blk)),
                  pl.BlockSpec((1, HG_D), lambda i: (0, 0)),
                  yspec, yspec,
                  pl.BlockSpec((N_BRANCH, kb, d), lambda i: (0, 0, 0), pipeline_mode=pl.Buffered(1)),
                  gspec(0), gspec(1), gspec(2)],
        out_specs=pl.BlockSpec((tm, d), lambda i: (i, 0)),
        out_shape=jax.ShapeDtypeStruct((m, d), bf16),
        compiler_params=_cparams(("parallel",)),
        name="branch_merge",
    )(o_f, o_b, proj_hg, norm_g.reshape(1, HG_D), y1, y2, w_branch, gsig, gsig, gsig)


def _proj_ln_kernel(a_ref, w_ref, res_ref, rg_ref, rb_ref, g_ref, b_ref, o32_ref, o16_ref, *, alpha, res_is_raw):
    rows_per_group = a_ref.shape[0] // PROJ_LN_GROUPS
    groups = [slice(r * rows_per_group, (r + 1) * rows_per_group) for r in range(PROJ_LN_GROUPS)]
    proj = [_dot(a_ref[rows, :], w_ref[...]) for rows in groups]
    for rows, p in zip(groups, proj):
        res = res_ref[rows, :]
        if res_is_raw:
            res = _ln_rows(res, rg_ref[...], rb_ref[...])
        y = _ln_rows(alpha * res + p, g_ref[...], b_ref[...])
        o32_ref[rows, :] = y
        o16_ref[rows, :] = y.astype(bf16)


def _proj_residual_ln(a, w, res, res_ln, g, b, alpha, tm=TM_ROW):
    m, k = a.shape
    d = w.shape[1]
    rg, rb = res_ln if res_ln is not None else (g, b)
    vec = pl.BlockSpec((1, d), lambda i: (0, 0))
    return pl.pallas_call(
        functools.partial(_proj_ln_kernel, alpha=alpha, res_is_raw=res_ln is not None),
        grid=(m // tm,),
        in_specs=[pl.BlockSpec((tm, k), lambda i: (i, 0)),
                  pl.BlockSpec((k, d), lambda i: (0, 0)),
                  pl.BlockSpec((tm, d), lambda i: (i, 0)),
                  vec, vec, vec, vec],
        out_specs=[pl.BlockSpec((tm, d), lambda i: (i, 0)),
                   pl.BlockSpec((tm, d), lambda i: (i, 0))],
        out_shape=[jax.ShapeDtypeStruct((m, d), f32), jax.ShapeDtypeStruct((m, d), bf16)],
        compiler_params=_cparams(("parallel",)),
        name="out_proj_ln",
    )(a, w, res, rg.reshape(1, d), rb.reshape(1, d), g.reshape(1, d), b.reshape(1, d))


def _ffn_kernel(x_ref, wg_ref, wu_ref, wd_ref, res_ref, g_ref, b_ref, o_ref, *, alpha):
    j = pl.program_id(1)

    @pl.when(j == 0)
    def _():
        o_ref[...] = jnp.zeros_like(o_ref)

    x = x_ref[...]
    hid = _silu(_dot(x, wg_ref[...].astype(bf16))) * _dot(x, wu_ref[...].astype(bf16))
    o_ref[...] += _dot(hid.astype(bf16), wd_ref[...].astype(bf16))

    @pl.when(j == pl.num_programs(1) - 1)
    def _():
        o_ref[...] = _ln_rows(alpha * res_ref[...] + o_ref[...], g_ref[...], b_ref[...])


def _ffn_residual_ln(x16, x32, wg, wu, wd, g, b, alpha, tm=FFN_TM, tf=FFN_TF):
    m, d = x16.shape
    ff = wg.shape[1]
    once = pl.Buffered(1)
    return pl.pallas_call(
        functools.partial(_ffn_kernel, alpha=alpha),
        grid=(m // tm, ff // tf),
        in_specs=[pl.BlockSpec((tm, d), lambda i, j: (i, 0)),
                  pl.BlockSpec((d, tf), lambda i, j: (0, j)),
                  pl.BlockSpec((d, tf), lambda i, j: (0, j)),
                  pl.BlockSpec((tf, d), lambda i, j: (j, 0)),
                  pl.BlockSpec((tm, d), lambda i, j: (i, 0), pipeline_mode=once),
                  pl.BlockSpec((1, d), lambda i, j: (0, 0)),
                  pl.BlockSpec((1, d), lambda i, j: (0, 0))],
        out_specs=pl.BlockSpec((tm, d), lambda i, j: (i, 0)),
        out_shape=jax.ShapeDtypeStruct((m, d), f32),
        compiler_params=_cparams(("parallel", "arbitrary")),
        name="ffn_ln",
    )(x16, wg, wu, wd, x32, g.reshape(1, d), b.reshape(1, d))


def _rope_tables(positions):
    half = MLA_ROPE // 2
    inv_freq = jnp.power(ROPE_THETA, -jnp.arange(half, dtype=f32) / half)
    ang = positions.astype(f32)[..., None] * inv_freq
    cos, sin = jnp.cos(ang), jnp.sin(ang)
    z = jnp.zeros_like(cos)
    pad = jnp.zeros((ang.shape[0], LANES - MLA_ROPE), f32)
    c = jnp.concatenate([cos, cos, pad], axis=-1)
    s1 = jnp.concatenate([-sin, z, pad], axis=-1)
    s2 = jnp.concatenate([z, sin, pad], axis=-1)
    return c, s1, s2


def _layer(res, res_ln, h16, mem16, rope, lb, alpha, w_in, hgrn_norm_g, g_cq, g_ckv, w_uq, w_ukv, w_memkv,
           w_branch, w_o, ln1_g, ln1_b, w_gate, w_up, w_down, ln2_g, ln2_b):
    s, d = res.shape
    hgw = HG_HEADS * HG_D
    q_rank = g_cq.shape[0]
    kv_rank = g_ckv.shape[0]
    memw = MEM_HEADS * MEM_HD
    o_cq = 5 * hgw
    o_ckv = o_cq + q_rank
    o_kr = o_ckv + kv_rank
    o_qm = o_kr + MLA_ROPE
    o_gt = o_qm + memw
    rope_c, rope_s1, rope_s2 = rope

    wt = jnp.swapaxes(w_in, 0, 1)

    proj_hg = _matmul(h16, wt, wt_rows=(0, o_cq), tm=TM_IN_PROJ, tn=TN_IN_PROJ, out_dtype=f32,
                      name="in_proj_hgrn")
    gsig = _matmul(h16, wt, wt_rows=(o_gt, N_BRANCH * d), tm=TM_IN_PROJ, tn=TN_IN_PROJ, out_dtype=bf16,
                   epilogue=_sigmoid, name="in_proj_gates")
    pr = _in_proj_rest(h16, wt, o_qm, memw, o_cq, o_ckv, o_kr, g_cq, g_ckv, rope)
    c_qm = 0
    c_cq = c_qm + memw
    c_ckv = c_cq + q_rank
    c_kr = c_ckv + kv_rank

    o_f, o_b = _hgrn_scan(proj_hg, lb)

    zq = jnp.zeros((q_rank, MLA_HEADS, MLA_QK_PAD - MLA_QK), f32)
    w_q = jnp.concatenate([w_uq.reshape(q_rank, MLA_HEADS, MLA_QK), zq], axis=-1)
    w_q = w_q.reshape(q_rank, MLA_HEADS * MLA_QK_PAD).astype(bf16)
    w_kv3 = w_ukv.reshape(kv_rank, MLA_HEADS, MLA_NOPE + MLA_V)
    w_kv = jnp.concatenate([w_kv3[:, :, :MLA_NOPE].reshape(kv_rank, -1),
                            w_kv3[:, :, MLA_NOPE:].reshape(kv_rank, -1)], axis=-1).astype(bf16)
    q = _mla_q(pr, c_cq, w_q, rope_c, rope_s1, rope_s2)
    k, v = _mla_kv(pr, c_ckv, c_kr, w_kv)
    y_mla = _mla_attention(q, k, v)

    kv_mem = _matmul(mem16, w_memkv, tm=mem16.shape[0], tn=TN_IN_PROJ, out_dtype=bf16, name="mem_kv_proj")
    y_mem = _mem_attention(pr, c_qm, kv_mem[:, :memw], kv_mem[:, memw:])

    merged = _merge(o_f, o_b, proj_hg, hgrn_norm_g, y_mla, y_mem, w_branch.astype(bf16), gsig)
    x32, x16 = _proj_residual_ln(merged, w_o.astype(bf16), res, res_ln, ln1_g, ln1_b, alpha)
    return _ffn_residual_ln(x16, x32, w_gate, w_up, w_down, ln2_g, ln2_b, alpha)


def kernel(x, mem, positions, ln_emb_g, ln_emb_b, hgrn_lb_logits, w_in, hgrn_norm_g, mla_g_cq, mla_g_ckv,
           mla_w_uq, mla_w_ukv, mem_w_kv, w_branch, w_o, ln1_g, ln1_b, w_ffn_gate, w_ffn_up, w_ffn_down,
           ln2_g, ln2_b):
    bsz = x.shape[0]
    depth = w_in.shape[0]
    alpha = (2.0 * depth) ** 0.25
    outs = []
    for bi in range(bsz):
        rope = _rope_tables(positions[bi])
        mem16 = mem[bi].astype(bf16)
        h16 = _layernorm_bf16(x[bi], ln_emb_g, ln_emb_b)
        res, res_ln = x[bi], (ln_emb_g, ln_emb_b)
        for l in range(depth):
            lb = _lower_bounds(hgrn_lb_logits, l)
            res = _layer(res, res_ln, h16, mem16, rope, lb, alpha, w_in[l], hgrn_norm_g[l], mla_g_cq[l],
                         mla_g_ckv[l], mla_w_uq[l], mla_w_ukv[l], mem_w_kv[l], w_branch[l], w_o[l], ln1_g[l],
                         ln1_b[l], w_ffn_gate[l], w_ffn_up[l], w_ffn_down[l], ln2_g[l], ln2_b[l])
            res_ln = None
            if l + 1 < depth:
                h16 = res.astype(bf16)
        outs.append(res)
    return jnp.stack(outs)
```

```python
import functools

import jax
import jax.numpy as jnp
from jax import lax
from jax.experimental import pallas as pl
from jax.experimental.pallas import tpu as pltpu

HG_HEADS = 8
HG_D = 128
MLA_HEADS = 8
MLA_NOPE = 128
MLA_ROPE = 64
MLA_V = 128
MLA_QK = MLA_NOPE + MLA_ROPE
MLA_QK_PAD = 256
LOG2E = 1.4426950408889634
MEM_HEADS = 4
MEM_HD = 256
N_BRANCH = 3
ROPE_THETA = 10000.0
LN_EPS = 1e-5
RMS_EPS = 1e-6

LANES = 128
SUBLANES = 8
VMEM_LIMIT = 56 * 1024 * 1024

TM_IN_PROJ = 2048
TN_IN_PROJ = 512
TM_ROW = 512
TM_LIGHT = 1024
ATTN_TQ = 512
ATTN_TK = 256
FFN_TM = 1024
FFN_TF = 256
PROJ_LN_GROUPS = 2
HG_BLOCK = 1024
HG_CHUNK = 64
HG_SUB = SUBLANES

bf16 = jnp.bfloat16
f32 = jnp.float32


def _cparams(sem):
    return pltpu.CompilerParams(dimension_semantics=sem, vmem_limit_bytes=VMEM_LIMIT)


def _dot(a, b):
    return jnp.dot(a, b, preferred_element_type=f32)


def _dot_nt(a, b):
    return lax.dot_general(a, b, (((1,), (1,)), ((), ())), preferred_element_type=f32)


def _dot_tn(a, b):
    return lax.dot_general(a, b, (((0,), (0,)), ((), ())), preferred_element_type=f32)


def _ln_rows(x, g, b):
    mu = jnp.mean(x, axis=-1, keepdims=True)
    xc = x - mu
    var = jnp.mean(xc * xc, axis=-1, keepdims=True)
    return xc * lax.rsqrt(var + LN_EPS) * g + b


def _ln_kernel(x_ref, g_ref, b_ref, o16_ref):
    o16_ref[...] = _ln_rows(x_ref[...], g_ref[...], b_ref[...]).astype(bf16)


def _layernorm_bf16(x, g, b, tm=TM_LIGHT):
    m, d = x.shape
    return pl.pallas_call(
        _ln_kernel,
        grid=(m // tm,),
        in_specs=[pl.BlockSpec((tm, d), lambda i: (i, 0)),
                  pl.BlockSpec((1, d), lambda i: (0, 0)),
                  pl.BlockSpec((1, d), lambda i: (0, 0))],
        out_specs=pl.BlockSpec((tm, d), lambda i: (i, 0)),
        out_shape=jax.ShapeDtypeStruct((m, d), bf16),
        compiler_params=_cparams(("parallel",)),
        name="ln_embed",
    )(x, g.reshape(1, d), b.reshape(1, d))


def _mm_kernel(a_ref, w_ref, o_ref, *, epilogue, w_transposed):
    w = w_ref[...].astype(bf16)
    acc = _dot_nt(a_ref[...], w) if w_transposed else _dot(a_ref[...], w)
    o_ref[...] = epilogue(acc).astype(o_ref.dtype)


def _matmul(a, w, *, tm, tn, out_dtype, epilogue=lambda acc: acc, name, wt_rows=None):
    m, k = a.shape
    if wt_rows is None:
        n = w.shape[1]
        w_spec = pl.BlockSpec((k, tn), lambda i, j: (0, j))
    else:
        row0, n = wt_rows
        assert row0 % SUBLANES == 0 and w.shape[1] == k
        w_spec = pl.BlockSpec((pl.Element(tn), pl.Element(k)),
                              lambda i, j: ((row0 // SUBLANES + j * (tn // SUBLANES)) * SUBLANES, 0))
    return pl.pallas_call(
        functools.partial(_mm_kernel, epilogue=epilogue, w_transposed=wt_rows is not None),
        grid=(m // tm, n // tn),
        in_specs=[pl.BlockSpec((tm, k), lambda i, j: (i, 0)), w_spec],
        out_specs=pl.BlockSpec((tm, tn), lambda i, j: (i, j)),
        out_shape=jax.ShapeDtypeStruct((m, n), out_dtype),
        compiler_params=_cparams(("parallel", "parallel")),
        name=name,
    )(a, w)


def _sigmoid(x):
    return 0.5 * jnp.tanh(0.5 * x) + 0.5


def _silu(x):
    return x * _sigmoid(x)


def _rms_epilogue(acc, g):
    ms = jnp.mean(acc * acc, axis=-1, keepdims=True)
    return acc * lax.rsqrt(ms + RMS_EPS) * g


def _rope_block(blk, c, s1, s2):
    half = MLA_ROPE // 2
    return blk * c + pltpu.roll(blk, LANES - half, 1) * s1 + pltpu.roll(blk, half, 1) * s2


def _in_proj_rest_kernel(a_ref, w_ref, gcq_ref, gckv_ref, c_ref, s1_ref, s2_ref, o_ref, *, n_qmem):
    j = pl.program_id(1)
    acc = _dot_nt(a_ref[...], w_ref[...].astype(bf16))

    @pl.when(j < n_qmem)
    def _():
        o_ref[...] = (acc * (MEM_HD ** -0.5)).astype(o_ref.dtype)

    @pl.when(j == n_qmem)
    def _():
        o_ref[...] = _rms_epilogue(acc, gcq_ref[...]).astype(o_ref.dtype)

    @pl.when(j == n_qmem + 1)
    def _():
        o_ref[...] = _rms_epilogue(acc, gckv_ref[...]).astype(o_ref.dtype)

    @pl.when(j == n_qmem + 2)
    def _():
        o_ref[:, :LANES] = _rope_block(acc[:, :LANES], c_ref[...], s1_ref[...], s2_ref[...]).astype(o_ref.dtype)
        o_ref[:, LANES:] = jnp.zeros((acc.shape[0], acc.shape[1] - LANES), o_ref.dtype)


def _in_proj_rest(h16, wt, o_qm, n_qm, o_cq, o_ckv, o_kr, g_cq, g_ckv, rope, tm=TM_IN_PROJ, tn=TN_IN_PROJ):
    m, k = h16.shape
    n_qmem = n_qm // tn
    assert n_qm % tn == 0 and g_cq.shape[0] == tn and g_ckv.shape[0] == tn
    assert all(o % SUBLANES == 0 for o in (o_qm, o_cq, o_ckv, o_kr))
    n_tiles = n_qmem + 3

    def w_row(i, j):
        r = jnp.where(j < n_qmem, o_qm // SUBLANES + j * (tn // SUBLANES),
                      jnp.where(j == n_qmem, o_cq // SUBLANES,
                                jnp.where(j == n_qmem + 1, o_ckv // SUBLANES, o_kr // SUBLANES)))
        return (r * SUBLANES, 0)

    vec = pl.BlockSpec((1, tn), lambda i, j: (0, 0))
    tab = pl.BlockSpec((tm, LANES), lambda i, j: (i, 0))
    return pl.pallas_call(
        functools.partial(_in_proj_rest_kernel, n_qmem=n_qmem),
        grid=(m // tm, n_tiles),
        in_specs=[pl.BlockSpec((tm, k), lambda i, j: (i, 0)),
                  pl.BlockSpec((pl.Element(tn), pl.Element(k)), w_row),
                  vec, vec, tab, tab, tab],
        out_specs=pl.BlockSpec((tm, tn), lambda i, j: (i, j)),
        out_shape=jax.ShapeDtypeStruct((m, n_tiles * tn), bf16),
        compiler_params=_cparams(("parallel", "arbitrary")),
        name="in_proj_rest",
    )(h16, wt, g_cq.reshape(1, tn), g_ckv.reshape(1, tn), *rope)


def _mla_q_kernel(cq_ref, w_ref, c_ref, s1_ref, s2_ref, o_ref):
    q = _dot(cq_ref[...], w_ref[...]) * (MLA_QK ** -0.5 * LOG2E)
    c, s1, s2 = c_ref[...], s1_ref[...], s2_ref[...]
    for h in range(MLA_HEADS):
        base = h * MLA_QK_PAD
        o_ref[:, base:base + MLA_NOPE] = q[:, base:base + MLA_NOPE].astype(bf16)
        o_ref[:, base + MLA_NOPE:base + MLA_QK_PAD] = _rope_block(
            q[:, base + MLA_NOPE:base + MLA_QK_PAD], c, s1, s2).astype(bf16)


def _mla_q(pr, cq_col, w_q, rope_c, rope_s1, rope_s2, tm=TM_LIGHT):
    m = pr.shape[0]
    r, n = w_q.shape
    assert cq_col % r == 0
    cqn = pr
    return pl.pallas_call(
        _mla_q_kernel,
        grid=(m // tm,),
        in_specs=[pl.BlockSpec((tm, r), lambda i: (i, cq_col // r)),
                  pl.BlockSpec((r, n), lambda i: (0, 0)),
                  pl.BlockSpec((tm, LANES), lambda i: (i, 0)),
                  pl.BlockSpec((tm, LANES), lambda i: (i, 0)),
                  pl.BlockSpec((tm, LANES), lambda i: (i, 0))],
        out_specs=pl.BlockSpec((tm, n), lambda i: (i, 0)),
        out_shape=jax.ShapeDtypeStruct((m, n), bf16),
        compiler_params=_cparams(("parallel",)),
        name="mla_q_up",
    )(cqn, w_q, rope_c, rope_s1, rope_s2)


def _mla_kv_kernel(ckv_ref, w_ref, kr_ref, k_ref, v_ref):
    kv = _dot(ckv_ref[...], w_ref[...])
    kr = kr_ref[...]
    nk = MLA_HEADS * MLA_NOPE
    for h in range(MLA_HEADS):
        base = h * MLA_QK_PAD
        k_ref[:, base:base + MLA_NOPE] = kv[:, h * MLA_NOPE:(h + 1) * MLA_NOPE].astype(bf16)
        k_ref[:, base + MLA_NOPE:base + MLA_QK_PAD] = kr
    v_ref[...] = kv[:, nk:].astype(bf16)


def _mla_kv(pr, ckv_col, kr_col, w_kv, tm=TM_LIGHT):
    m = pr.shape[0]
    r, n = w_kv.shape
    assert ckv_col % r == 0 and kr_col % LANES == 0
    ckvn = krope = pr
    nk = MLA_HEADS * MLA_QK_PAD
    nv = MLA_HEADS * MLA_V
    return pl.pallas_call(
        _mla_kv_kernel,
        grid=(m // tm,),
        in_specs=[pl.BlockSpec((tm, r), lambda i: (i, ckv_col // r)),
                  pl.BlockSpec((r, n), lambda i: (0, 0)),
                  pl.BlockSpec((tm, LANES), lambda i: (i, kr_col // LANES))],
        out_specs=[pl.BlockSpec((tm, nk), lambda i: (i, 0)),
                   pl.BlockSpec((tm, nv), lambda i: (i, 0))],
        out_shape=[jax.ShapeDtypeStruct((m, nk), bf16), jax.ShapeDtypeStruct((m, nv), bf16)],
        compiler_params=_cparams(("parallel",)),
        name="mla_kv_up",
    )(ckvn, w_kv, krope)


def _mla_attn_kernel(q_ref, k_ref, v_ref, o_ref, s_ref, *, tq, tk):
    nq = q_ref.shape[0] // tq
    nkv = k_ref.shape[0] // tk

    def scores(q, c):
        return _dot_nt(q, k_ref[c * tk:(c + 1) * tk, :])

    def update(s, c, m, l, acc):
        m_new = jnp.maximum(m, jnp.max(s, axis=-1, keepdims=True))
        p = jnp.exp2(s - m_new)
        a = jnp.exp2(m - m_new)
        psum = p[:, :LANES]
        for g in range(1, tk // LANES):
            psum = psum + p[:, g * LANES:(g + 1) * LANES]
        l = a * l + psum
        acc = a * acc + _dot(p.astype(bf16), v_ref[c * tk:(c + 1) * tk, :])
        return m_new, l, acc

    s_ref[...] = scores(q_ref[0:tq, :], 0)

    def body(blk, carry):
        off = pl.multiple_of(blk * tq, tq)
        q = q_ref[pl.ds(off, tq), :]
        m = jnp.full((tq, 1), -jnp.inf, f32)
        l = jnp.zeros((tq, LANES), f32)
        acc = jnp.zeros((tq, MLA_V), f32)
        s_next = s_ref[...]
        for c in range(nkv):
            s_cur = s_next
            if c + 1 < nkv:
                s_next = scores(q, c + 1)
            else:
                nxt = pl.multiple_of(jnp.minimum(blk + 1, nq - 1) * tq, tq)
                s_ref[...] = scores(q_ref[pl.ds(nxt, tq), :], 0)
            m, l, acc = update(s_cur, c, m, l, acc)
        o_ref[pl.ds(off, tq), :] = (acc / jnp.sum(l, axis=-1, keepdims=True)).astype(o_ref.dtype)
        return carry

    lax.fori_loop(0, nq, body, 0)


def _mla_attention(q, k, v, tq=ATTN_TQ, tk=ATTN_TK):
    s = q.shape[0]
    return pl.pallas_call(
        functools.partial(_mla_attn_kernel, tq=tq, tk=tk),
        grid=(MLA_HEADS,),
        in_specs=[pl.BlockSpec((s, MLA_QK_PAD), lambda h: (0, h)),
                  pl.BlockSpec((s, MLA_QK_PAD), lambda h: (0, h)),
                  pl.BlockSpec((s, MLA_V), lambda h: (0, h))],
        out_specs=pl.BlockSpec((s, MLA_V), lambda h: (0, h)),
        out_shape=jax.ShapeDtypeStruct((s, MLA_HEADS * MLA_V), bf16),
        scratch_shapes=[pltpu.VMEM((tq, tk), f32)],
        compiler_params=_cparams(("parallel",)),
        name="mla_attention",
    )(q, k, v)


def _mem_attn_kernel(q_ref, k_ref, v_ref, o_ref):
    for h in range(MEM_HEADS):
        sl = slice(h * MEM_HD, (h + 1) * MEM_HD)
        s = _dot_nt(q_ref[:, sl], k_ref[:, sl])
        m = jnp.max(s, axis=-1, keepdims=True)
        p = jnp.exp(s - m)
        l = jnp.sum(p, axis=-1, keepdims=True)
        o = _dot(p.astype(bf16), v_ref[:, sl])
        o_ref[:, sl] = (o / l).astype(o_ref.dtype)


def _mem_attention(pr, q_col, k, v, tm=TM_LIGHT):
    s = pr.shape[0]
    nm, w = k.shape
    assert q_col % w == 0
    q = pr
    return pl.pallas_call(
        _mem_attn_kernel,
        grid=(s // tm,),
        in_specs=[pl.BlockSpec((tm, w), lambda i: (i, q_col // w)),
                  pl.BlockSpec((nm, w), lambda i: (0, 0)),
                  pl.BlockSpec((nm, w), lambda i: (0, 0))],
        out_specs=pl.BlockSpec((tm, w), lambda i: (i, 0)),
        out_shape=jax.ShapeDtypeStruct((s, w), bf16),
        compiler_params=_cparams(("parallel",)),
        name="mem_attention",
    )(q, k, v)


def _hgrn_levels(rev):
    levels = []
    sz = HG_CHUNK // 2
    while sz >= HG_SUB:
        blocks = []
        for base in range(0, HG_CHUNK, 2 * sz):
            if not rev:
                blocks.append((base + sz, base, sz, base + sz - 1))
            else:
                blocks.append((base, base + sz, sz, base + sz))
        levels.append(blocks)
        sz //= 2
    return levels


def _hgrn_intra(q_c, b_c, c_c, bbuf, cbuf, row0, *, rev):
    C = HG_CHUNK
    nb = C // HG_SUB
    lane = lax.broadcasted_iota(jnp.int32, (HG_SUB, C), 1)
    sub = lax.broadcasted_iota(jnp.int32, (HG_SUB, C), 0)
    blocks = []
    for i in range(nb):
        r0 = i * HG_SUB
        q_i = q_c[r0:r0 + HG_SUB]
        b_i = b_c[r0:r0 + HG_SUB]
        a_blk = jnp.zeros((HG_SUB, C), f32)
        for s in range(HG_SUB):
            w = q_i * jnp.exp2(b_i + cbuf[row0 + r0 + s:row0 + r0 + s + 1, :])
            a_blk = jnp.where(lane == r0 + s, jnp.sum(w, axis=-1, keepdims=True), a_blk)
        keep = (sub >= lane - r0) if not rev else (sub <= lane - r0)
        blocks.append(jnp.where(keep, a_blk, 0.0))
    for level in _hgrn_levels(rev):
        sz = level[0][2]
        qd, kd, valid = [], [], None
        k_pos = 0
        rq = lax.broadcasted_iota(jnp.int32, (len(level) * sz, C), 0)
        ck = lax.broadcasted_iota(jnp.int32, (len(level) * sz, C), 1)
        for j, (q0, k0, _, piv_row) in enumerate(level):
            piv = bbuf[row0 + piv_row:row0 + piv_row + 1, :]
            qd.append(q_c[q0:q0 + sz] * jnp.exp2(b_c[q0:q0 + sz] - piv))
            if k0 > k_pos:
                kd.append(jnp.zeros((k0 - k_pos, q_c.shape[1]), f32))
            kd.append(jnp.exp2(c_c[k0:k0 + sz] + piv))
            k_pos = k0 + sz
            ok = (rq >= j * sz) & (rq < (j + 1) * sz) & (ck >= k0) & (ck < k0 + sz)
            valid = ok if valid is None else (valid | ok)
        if k_pos < C:
            kd.append(jnp.zeros((C - k_pos, q_c.shape[1]), f32))
        r = _dot_nt(jnp.concatenate(qd, axis=0).astype(bf16), jnp.concatenate(kd, axis=0).astype(bf16))
        if len(level) > 1:
            r = jnp.where(valid, r, 0.0)
        for j, (q0, _, _, _) in enumerate(level):
            for t in range(sz // HG_SUB):
                i = q0 // HG_SUB + t
                blocks[i] = blocks[i] + r[j * sz + t * HG_SUB:j * sz + (t + 1) * HG_SUB]
    return jnp.concatenate(blocks, axis=0)


def _hgrn_block_kernel(qf_ref, vf_ref, ff_ref, qb_ref, vb_ref, fb_ref, lb_ref, of_ref, ob_ref,
                       sf_ref, sb_ref, bbuf, cbuf):
    n = pl.program_id(1)
    C = HG_CHUNK
    T = qf_ref.shape[0]
    nchunk = T // C

    @pl.when(n == 0)
    def _():
        sf_ref[...] = jnp.zeros_like(sf_ref)
        sb_ref[...] = jnp.zeros_like(sb_ref)

    r = lax.broadcasted_iota(jnp.int32, (C, C), 0)
    c = lax.broadcasted_iota(jnp.int32, (C, C), 1)
    dirs = []
    for d, (q_ref, v_ref, f_ref, rev) in enumerate(((qf_ref, vf_ref, ff_ref, False), (qb_ref, vb_ref, fb_ref, True))):
        tri = ((c <= r) if not rev else (c >= r)).astype(bf16)
        lb = lb_ref[d:d + 1, :]
        q = _silu(q_ref[...])
        v16 = v_ref[...].astype(bf16)
        f = lb + (1.0 - lb) * _sigmoid(f_ref[...])
        g = jnp.log(f) * LOG2E
        g_hi = g.astype(bf16)
        g_lo = (g - g_hi.astype(f32)).astype(bf16)
        b = jnp.concatenate([_dot(tri, g_hi[j * C:(j + 1) * C]) + _dot(tri, g_lo[j * C:(j + 1) * C])
                             for j in range(nchunk)], axis=0)
        cc = jnp.log(1.0 - f) * LOG2E - b
        bbuf[d] = b
        cbuf[d] = cc
        dirs.append((q, v16, b, cc, rev))

    intra = []
    for d, (q, v16, b, cc, rev) in enumerate(dirs):
        intra.append([_hgrn_intra(q[j * C:(j + 1) * C], b[j * C:(j + 1) * C], cc[j * C:(j + 1) * C],
                                  bbuf.at[d], cbuf.at[d], j * C, rev=rev).astype(bf16) for j in range(nchunk)])

    for d, (q, v16, b, cc, rev) in enumerate(dirs):
        s_ref, o_ref = (sf_ref, of_ref) if not rev else (sb_ref, ob_ref)
        qe = (q * jnp.exp2(b)).astype(bf16)
        order = range(nchunk) if not rev else range(nchunk - 1, -1, -1)
        edge = (C - 1) if not rev else 0
        upd = {}
        for j in order:
            b_edge = bbuf[d, j * C + edge:j * C + edge + 1, :]
            kdec = jnp.exp2(cc[j * C:(j + 1) * C] + b_edge).astype(bf16)
            upd[j] = (jnp.exp2(b_edge), _dot_tn(v16[j * C:(j + 1) * C], kdec))
        state = s_ref[...]
        for j in order:
            rows = slice(j * C, (j + 1) * C)
            o = _dot_nt(qe[rows], state.astype(bf16)) + _dot(intra[d][j], v16[rows])
            o_ref[rows, :] = o.astype(o_ref.dtype)
            state = upd[j][0] * state + upd[j][1]
        s_ref[...] = state


def _hgrn_scan(proj, lb, tb=HG_BLOCK):
    s = proj.shape[0]
    w = HG_HEADS * HG_D
    nb = s // tb
    hb = w // HG_D
    def fw(seg):
        return pl.BlockSpec((tb, HG_D), lambda h, n: (n, seg * hb + h))

    def bw(seg):
        return pl.BlockSpec((tb, HG_D), lambda h, n: (nb - 1 - n, seg * hb + h))

    return pl.pallas_call(
        _hgrn_block_kernel,
        grid=(HG_HEADS, nb),
        in_specs=[fw(0), fw(1), fw(2), bw(0), bw(1), bw(3),
                  pl.BlockSpec((2, HG_D), lambda h, n: (0, h))],
        out_specs=[pl.BlockSpec((tb, HG_D), lambda h, n: (n, h)),
                   pl.BlockSpec((tb, HG_D), lambda h, n: (nb - 1 - n, h))],
        out_shape=[jax.ShapeDtypeStruct((s, w), bf16), jax.ShapeDtypeStruct((s, w), bf16)],
        scratch_shapes=[pltpu.VMEM((HG_D, HG_D), f32), pltpu.VMEM((HG_D, HG_D), f32),
                        pltpu.VMEM((2, tb, HG_D), f32), pltpu.VMEM((2, tb, HG_D), f32)],
        compiler_params=_cparams(("parallel", "arbitrary")),
        name="hgrn_scan",
    )(proj, proj, proj, proj, proj, proj, lb)


def _lb_kernel(logit_ref, lb_ref, *, layer):
    x = logit_ref[...]
    e = jnp.exp(x - jnp.max(x, axis=1, keepdims=True))
    p = e / jnp.sum(e, axis=1, keepdims=True)
    acc = p[:, 0, :]
    for l in range(1, layer + 1):
        acc = acc + p[:, l, :]
    lb_ref[...] = acc


def _lower_bounds(logits, layer):
    z, _, w = logits.shape
    return pl.pallas_call(
        functools.partial(_lb_kernel, layer=layer),
        out_shape=jax.ShapeDtypeStruct((z, w), f32),
        name="hgrn_lower_bounds",
    )(logits)


def _merge_kernel(of_ref, ob_ref, hg_ref, ng_ref, y1_ref, y2_ref, w_ref, g0_ref, g1_ref, g2_ref, o_ref):
    acc = g1_ref[...].astype(f32) * _dot(y1_ref[...], w_ref[1])
    acc = acc + g2_ref[...].astype(f32) * _dot(y2_ref[...], w_ref[2])
    ng = ng_ref[...]
    y0 = []
    for h in range(HG_HEADS):
        sl = slice(h * HG_D, (h + 1) * HG_D)
        o = of_ref[:, sl].astype(f32) + ob_ref[:, sl].astype(f32)
        ms = jnp.mean(o * o, axis=-1, keepdims=True)
        y0.append((o * lax.rsqrt(ms + RMS_EPS) * ng * _sigmoid(hg_ref[:, sl])).astype(bf16))
    acc = acc + g0_ref[...].astype(f32) * _dot(jnp.concatenate(y0, axis=1), w_ref[0])
    o_ref[...] = acc.astype(o_ref.dtype)


def _merge(o_f, o_b, proj_hg, norm_g, y1, y2, w_branch, gsig, tm=TM_ROW):
    m, kb = y1.shape
    d = w_branch.shape[2]
    hg_gate_blk = 4
    yspec = pl.BlockSpec((tm, kb), lambda i: (i, 0))
    def gspec(b):
        return pl.BlockSpec((tm, d), lambda i: (i, b))

    return pl.pallas_call(
        _merge_kernel,
        grid=(m // tm,),
        in_specs=[yspec, yspec,
                  pl.BlockSpec((tm, kb), lambda i: (i, hg_gate_blk)),
                  pl.BlockSpec((1, HG_D), lambda i: (0, 0)),
                  yspec, yspec,
                  pl.BlockSpec((N_BRANCH, kb, d), lambda i: (0, 0, 0), pipeline_mode=pl.Buffered(1)),
                  gspec(0), gspec(1), gspec(2)],
        out_specs=pl.BlockSpec((tm, d), lambda i: (i, 0)),
        out_shape=jax.ShapeDtypeStruct((m, d), bf16),
        compiler_params=_cparams(("parallel",)),
        name="branch_merge",
    )(o_f, o_b, proj_hg, norm_g.reshape(1, HG_D), y1, y2, w_branch, gsig, gsig, gsig)


def _proj_ln_kernel(a_ref, w_ref, res_ref, rg_ref, rb_ref, g_ref, b_ref, o32_ref, o16_ref, *, alpha, res_is_raw):
    rows_per_group = a_ref.shape[0] // PROJ_LN_GROUPS
    groups = [slice(r * rows_per_group, (r + 1) * rows_per_group) for r in range(PROJ_LN_GROUPS)]
    proj = [_dot(a_ref[rows, :], w_ref[...]) for rows in groups]
    for rows, p in zip(groups, proj):
        res = res_ref[rows, :]
        if res_is_raw:
            res = _ln_rows(res, rg_ref[...], rb_ref[...])
        y = _ln_rows(alpha * res + p, g_ref[...], b_ref[...])
        o32_ref[rows, :] = y
        o16_ref[rows, :] = y.astype(bf16)


def _proj_residual_ln(a, w, res, res_ln, g, b, alpha, tm=TM_ROW):
    m, k = a.shape
    d = w.shape[1]
    rg, rb = res_ln if res_ln is not None else (g, b)
    vec = pl.BlockSpec((1, d), lambda i: (0, 0))
    return pl.pallas_call(
        functools.partial(_proj_ln_kernel, alpha=alpha, res_is_raw=res_ln is not None),
        grid=(m // tm,),
        in_specs=[pl.BlockSpec((tm, k), lambda i: (i, 0)),
                  pl.BlockSpec((k, d), lambda i: (0, 0)),
                  pl.BlockSpec((tm, d), lambda i: (i, 0)),
                  vec, vec, vec, vec],
        out_specs=[pl.BlockSpec((tm, d), lambda i: (i, 0)),
                   pl.BlockSpec((tm, d), lambda i: (i, 0))],
        out_shape=[jax.ShapeDtypeStruct((m, d), f32), jax.ShapeDtypeStruct((m, d), bf16)],
        compiler_params=_cparams(("parallel",)),
        name="out_proj_ln",
    )(a, w, res, rg.reshape(1, d), rb.reshape(1, d), g.reshape(1, d), b.reshape(1, d))


def _ffn_kernel(x_ref, wg_ref, wu_ref, wd_ref, res_ref, g_ref, b_ref, o_ref, *, alpha):
    j = pl.program_id(1)

    @pl.when(j == 0)
    def _():
        o_ref[...] = jnp.zeros_like(o_ref)

    x = x_ref[...]
    hid = _silu(_dot(x, wg_ref[...].astype(bf16))) * _dot(x, wu_ref[...].astype(bf16))
    o_ref[...] += _dot(hid.astype(bf16), wd_ref[...].astype(bf16))

    @pl.when(j == pl.num_programs(1) - 1)
    def _():
        o_ref[...] = _ln_rows(alpha * res_ref[...] + o_ref[...], g_ref[...], b_ref[...])


def _ffn_residual_ln(x16, x32, wg, wu, wd, g, b, alpha, tm=FFN_TM, tf=FFN_TF):
    m, d = x16.shape
    ff = wg.shape[1]
    once = pl.Buffered(1)
    return pl.pallas_call(
        functools.partial(_ffn_kernel, alpha=alpha),
        grid=(m // tm, ff // tf),
        in_specs=[pl.BlockSpec((tm, d), lambda i, j: (i, 0)),
                  pl.BlockSpec((d, tf), lambda i, j: (0, j)),
                  pl.BlockSpec((d, tf), lambda i, j: (0, j)),
                  pl.BlockSpec((tf, d), lambda i, j: (j, 0)),
                  pl.BlockSpec((tm, d), lambda i, j: (i, 0), pipeline_mode=once),
                  pl.BlockSpec((1, d), lambda i, j: (0, 0)),
                  pl.BlockSpec((1, d), lambda i, j: (0, 0))],
        out_specs=pl.BlockSpec((tm, d), lambda i, j: (i, 0)),
        out_shape=jax.ShapeDtypeStruct((m, d), f32),
        compiler_params=_cparams(("parallel", "arbitrary")),
        name="ffn_ln",
    )(x16, wg, wu, wd, x32, g.reshape(1, d), b.reshape(1, d))


def _rope_tables(positions):
    half = MLA_ROPE // 2
    inv_freq = jnp.power(ROPE_THETA, -jnp.arange(half, dtype=f32) / half)
    ang = positions.astype(f32)[..., None] * inv_freq
    cos, sin = jnp.cos(ang), jnp.sin(ang)
    z = jnp.zeros_like(cos)
    pad = jnp.zeros((ang.shape[0], LANES - MLA_ROPE), f32)
    c = jnp.concatenate([cos, cos, pad], axis=-1)
    s1 = jnp.concatenate([-sin, z, pad], axis=-1)
    s2 = jnp.concatenate([z, sin, pad], axis=-1)
    return c, s1, s2


def _layer(res, res_ln, h16, mem16, rope, lb, alpha, w_in, hgrn_norm_g, g_cq, g_ckv, w_uq, w_ukv, w_memkv,
           w_branch, w_o, ln1_g, ln1_b, w_gate, w_up, w_down, ln2_g, ln2_b):
    s, d = res.shape
    hgw = HG_HEADS * HG_D
    q_rank = g_cq.shape[0]
    kv_rank = g_ckv.shape[0]
    memw = MEM_HEADS * MEM_HD
    o_cq = 5 * hgw
    o_ckv = o_cq + q_rank
    o_kr = o_ckv + kv_rank
    o_qm = o_kr + MLA_ROPE
    o_gt = o_qm + memw
    rope_c, rope_s1, rope_s2 = rope

    wt = jnp.swapaxes(w_in, 0, 1)

    proj_hg = _matmul(h16, wt, wt_rows=(0, o_cq), tm=TM_IN_PROJ, tn=TN_IN_PROJ, out_dtype=f32,
                      name="in_proj_hgrn")
    gsig = _matmul(h16, wt, wt_rows=(o_gt, N_BRANCH * d), tm=TM_IN_PROJ, tn=TN_IN_PROJ, out_dtype=bf16,
                   epilogue=_sigmoid, name="in_proj_gates")
    pr = _in_proj_rest(h16, wt, o_qm, memw, o_cq, o_ckv, o_kr, g_cq, g_ckv, rope)
    c_qm = 0
    c_cq = c_qm + memw
    c_ckv = c_cq + q_rank
    c_kr = c_ckv + kv_rank

    o_f, o_b = _hgrn_scan(proj_hg, lb)

    zq = jnp.zeros((q_rank, MLA_HEADS, MLA_QK_PAD - MLA_QK), f32)
    w_q = jnp.concatenate([w_uq.reshape(q_rank, MLA_HEADS, MLA_QK), zq], axis=-1)
    w_q = w_q.reshape(q_rank, MLA_HEADS * MLA_QK_PAD).astype(bf16)
    w_kv3 = w_ukv.reshape(kv_rank, MLA_HEADS, MLA_NOPE + MLA_V)
    w_kv = jnp.concatenate([w_kv3[:, :, :MLA_NOPE].reshape(kv_rank, -1),
                            w_kv3[:, :, MLA_NOPE:].reshape(kv_rank, -1)], axis=-1).astype(bf16)
    q = _mla_q(pr, c_cq, w_q, rope_c, rope_s1, rope_s2)
    k, v = _mla_kv(pr, c_ckv, c_kr, w_kv)
    y_mla = _mla_attention(q, k, v)

    kv_mem = _matmul(mem16, w_memkv, tm=mem16.shape[0], tn=TN_IN_PROJ, out_dtype=bf16, name="mem_kv_proj")
    y_mem = _mem_attention(pr, c_qm, kv_mem[:, :memw], kv_mem[:, memw:])

    merged = _merge(o_f, o_b, proj_hg, hgrn_norm_g, y_mla, y_mem, w_branch.astype(bf16), gsig)
    x32, x16 = _proj_residual_ln(merged, w_o.astype(bf16), res, res_ln, ln1_g, ln1_b, alpha)
    return _ffn_residual_ln(x16, x32, w_gate, w_up, w_down, ln2_g, ln2_b, alpha)


def kernel(x, mem, positions, ln_emb_g, ln_emb_b, hgrn_lb_logits, w_in, hgrn_norm_g, mla_g_cq, mla_g_ckv,
           mla_w_uq, mla_w_ukv, mem_w_kv, w_branch, w_o, ln1_g, ln1_b, w_ffn_gate, w_ffn_up, w_ffn_down,
           ln2_g, ln2_b):
    bsz = x.shape[0]
    depth = w_in.shape[0]
    alpha = (2.0 * depth) ** 0.25
    outs = []
    for bi in range(bsz):
        rope = _rope_tables(positions[bi])
        mem16 = mem[bi].astype(bf16)
        h16 = _layernorm_bf16(x[bi], ln_emb_g, ln_emb_b)
        res, res_ln = x[bi], (ln_emb_g, ln_emb_b)
        for l in range(depth):
            lb = _lower_bounds(hgrn_lb_logits, l)
            res = _layer(res, res_ln, h16, mem16, rope, lb, alpha, w_in[l], hgrn_norm_g[l], mla_g_cq[l],
                         mla_g_ckv[l], mla_w_uq[l], mla_w_ukv[l], mem_w_kv[l], w_branch[l], w_o[l], ln1_g[l],
                         ln1_b[l], w_ffn_gate[l], w_ffn_up[l], w_ffn_down[l], ln2_g[l], ln2_b[l])
            res_ln = None
            if l + 1 < depth:
                h16 = res.astype(bf16)
        outs.append(res)
    return jnp.stack(outs)
```

```python
import functools

import jax
import jax.numpy as jnp
from jax import lax
from jax.experimental import pallas as pl
from jax.experimental.pallas import tpu as pltpu

HG_HEADS = 8
HG_D = 128
MLA_HEADS = 8
MLA_NOPE = 128
MLA_ROPE = 64
MLA_V = 128
MLA_QK = MLA_NOPE + MLA_ROPE
MLA_QK_PAD = 256
LOG2E = 1.4426950408889634
MEM_HEADS = 4
MEM_HD = 256
N_BRANCH = 3
ROPE_THETA = 10000.0
LN_EPS = 1e-5
RMS_EPS = 1e-6

LANES = 128
SUBLANES = 8
VMEM_LIMIT = 56 * 1024 * 1024

TM_IN_PROJ = 2048
TN_IN_PROJ = 512
TM_ROW = 512
TM_LIGHT = 1024
ATTN_TQ = 512
ATTN_TK = 256
FFN_TM = 1024
FFN_TF = 256
PROJ_LN_GROUPS = 2
HG_BLOCK = 1024
HG_CHUNK = 64
HG_SUB = SUBLANES

bf16 = jnp.bfloat16
f32 = jnp.float32


def _cparams(sem):
    return pltpu.CompilerParams(dimension_semantics=sem, vmem_limit_bytes=VMEM_LIMIT)


def _dot(a, b):
    return jnp.dot(a, b, preferred_element_type=f32)


def _dot_nt(a, b):
    return lax.dot_general(a, b, (((1,), (1,)), ((), ())), preferred_element_type=f32)


def _dot_tn(a, b):
    return lax.dot_general(a, b, (((0,), (0,)), ((), ())), preferred_element_type=f32)


def _ln_rows(x, g, b):
    mu = jnp.mean(x, axis=-1, keepdims=True)
    xc = x - mu
    var = jnp.mean(xc * xc, axis=-1, keepdims=True)
    return xc * lax.rsqrt(var + LN_EPS) * g + b


def _ln_kernel(x_ref, g_ref, b_ref, o16_ref):
    o16_ref[...] = _ln_rows(x_ref[...], g_ref[...], b_ref[...]).astype(bf16)


def _layernorm_bf16(x, g, b, tm=TM_LIGHT):
    m, d = x.shape
    return pl.pallas_call(
        _ln_kernel,
        grid=(m // tm,),
        in_specs=[pl.BlockSpec((tm, d), lambda i: (i, 0)),
                  pl.BlockSpec((1, d), lambda i: (0, 0)),
                  pl.BlockSpec((1, d), lambda i: (0, 0))],
        out_specs=pl.BlockSpec((tm, d), lambda i: (i, 0)),
        out_shape=jax.ShapeDtypeStruct((m, d), bf16),
        compiler_params=_cparams(("parallel",)),
        name="ln_embed",
    )(x, g.reshape(1, d), b.reshape(1, d))


def _mm_kernel(a_ref, w_ref, o_ref, *, epilogue, w_transposed):
    w = w_ref[...].astype(bf16)
    acc = _dot_nt(a_ref[...], w) if w_transposed else _dot(a_ref[...], w)
    o_ref[...] = epilogue(acc).astype(o_ref.dtype)


def _matmul(a, w, *, tm, tn, out_dtype, epilogue=lambda acc: acc, name, wt_rows=None):
    m, k = a.shape
    if wt_rows is None:
        n = w.shape[1]
        w_spec = pl.BlockSpec((k, tn), lambda i, j: (0, j))
    else:
        row0, n = wt_rows
        assert row0 % SUBLANES == 0 and w.shape[1] == k
        w_spec = pl.BlockSpec((pl.Element(tn), pl.Element(k)),
                              lambda i, j: ((row0 // SUBLANES + j * (tn // SUBLANES)) * SUBLANES, 0))
    return pl.pallas_call(
        functools.partial(_mm_kernel, epilogue=epilogue, w_transposed=wt_rows is not None),
        grid=(m // tm, n // tn),
        in_specs=[pl.BlockSpec((tm, k), lambda i, j: (i, 0)), w_spec],
        out_specs=pl.BlockSpec((tm, tn), lambda i, j: (i, j)),
        out_shape=jax.ShapeDtypeStruct((m, n), out_dtype),
        compiler_params=_cparams(("parallel", "parallel")),
        name=name,
    )(a, w)


def _sigmoid(x):
    return 0.5 * jnp.tanh(0.5 * x) + 0.5


def _silu(x):
    return x * _sigmoid(x)


def _rms_epilogue(acc, g):
    ms = jnp.mean(acc * acc, axis=-1, keepdims=True)
    return acc * lax.rsqrt(ms + RMS_EPS) * g


def _rope_block(blk, c, s1, s2):
    half = MLA_ROPE // 2
    return blk * c + pltpu.roll(blk, LANES - half, 1) * s1 + pltpu.roll(blk, half, 1) * s2


def _in_proj_rest_kernel(a_ref, w_ref, gcq_ref, gckv_ref, c_ref, s1_ref, s2_ref, o_ref, *, n_qmem):
    j = pl.program_id(1)
    acc = _dot_nt(a_ref[...], w_ref[...].astype(bf16))

    @pl.when(j < n_qmem)
    def _():
        o_ref[...] = (acc * (MEM_HD ** -0.5)).astype(o_ref.dtype)

    @pl.when(j == n_qmem)
    def _():
        o_ref[...] = _rms_epilogue(acc, gcq_ref[...]).astype(o_ref.dtype)

    @pl.when(j == n_qmem + 1)
    def _():
        o_ref[...] = _rms_epilogue(acc, gckv_ref[...]).astype(o_ref.dtype)

    @pl.when(j == n_qmem + 2)
    def _():
        o_ref[:, :LANES] = _rope_block(acc[:, :LANES], c_ref[...], s1_ref[...], s2_ref[...]).astype(o_ref.dtype)
        o_ref[:, LANES:] = jnp.zeros((acc.shape[0], acc.shape[1] - LANES), o_ref.dtype)


def _in_proj_rest(h16, wt, o_qm, n_qm, o_cq, o_ckv, o_kr, g_cq, g_ckv, rope, tm=TM_IN_PROJ, tn=TN_IN_PROJ):
    m, k = h16.shape
    n_qmem = n_qm // tn
    assert n_qm % tn == 0 and g_cq.shape[0] == tn and g_ckv.shape[0] == tn
    assert all(o % SUBLANES == 0 for o in (o_qm, o_cq, o_ckv, o_kr))
    n_tiles = n_qmem + 3

    def w_row(i, j):
        r = jnp.where(j < n_qmem, o_qm // SUBLANES + j * (tn // SUBLANES),
                      jnp.where(j == n_qmem, o_cq // SUBLANES,
                                jnp.where(j == n_qmem + 1, o_ckv // SUBLANES, o_kr // SUBLANES)))
        return (r * SUBLANES, 0)

    vec = pl.BlockSpec((1, tn), lambda i, j: (0, 0))
    tab = pl.BlockSpec((tm, LANES), lambda i, j: (i, 0))
    return pl.pallas_call(
        functools.partial(_in_proj_rest_kernel, n_qmem=n_qmem),
        grid=(m // tm, n_tiles),
        in_specs=[pl.BlockSpec((tm, k), lambda i, j: (i, 0)),
                  pl.BlockSpec((pl.Element(tn), pl.Element(k)), w_row),
                  vec, vec, tab, tab, tab],
        out_specs=pl.BlockSpec((tm, tn), lambda i, j: (i, j)),
        out_shape=jax.ShapeDtypeStruct((m, n_tiles * tn), bf16),
        compiler_params=_cparams(("parallel", "arbitrary")),
        name="in_proj_rest",
    )(h16, wt, g_cq.reshape(1, tn), g_ckv.reshape(1, tn), *rope)


def _mla_q_kernel(cq_ref, w_ref, c_ref, s1_ref, s2_ref, o_ref):
    q = _dot(cq_ref[...], w_ref[...]) * (MLA_QK ** -0.5 * LOG2E)
    c, s1, s2 = c_ref[...], s1_ref[...], s2_ref[...]
    for h in range(MLA_HEADS):
        base = h * MLA_QK_PAD
        o_ref[:, base:base + MLA_NOPE] = q[:, base:base + MLA_NOPE].astype(bf16)
        o_ref[:, base + MLA_NOPE:base + MLA_QK_PAD] = _rope_block(
            q[:, base + MLA_NOPE:base + MLA_QK_PAD], c, s1, s2).astype(bf16)


def _mla_q(pr, cq_col, w_q, rope_c, rope_s1, rope_s2, tm=TM_LIGHT):
    m = pr.shape[0]
    r, n = w_q.shape
    assert cq_col % r == 0
    cqn = pr
    return pl.pallas_call(
        _mla_q_kernel,
        grid=(m // tm,),
        in_specs=[pl.BlockSpec((tm, r), lambda i: (i, cq_col // r)),
                  pl.BlockSpec((r, n), lambda i: (0, 0)),
                  pl.BlockSpec((tm, LANES), lambda i: (i, 0)),
                  pl.BlockSpec((tm, LANES), lambda i: (i, 0)),
                  pl.BlockSpec((tm, LANES), lambda i: (i, 0))],
        out_specs=pl.BlockSpec((tm, n), lambda i: (i, 0)),
        out_shape=jax.ShapeDtypeStruct((m, n), bf16),
        compiler_params=_cparams(("parallel",)),
        name="mla_q_up",
    )(cqn, w_q, rope_c, rope_s1, rope_s2)


def _mla_kv_kernel(ckv_ref, w_ref, kr_ref, k_ref, v_ref):
    kv = _dot(ckv_ref[...], w_ref[...])
    kr = kr_ref[...]
    nk = MLA_HEADS * MLA_NOPE
    for h in range(MLA_HEADS):
        base = h * MLA_QK_PAD
        k_ref[:, base:base + MLA_NOPE] = kv[:, h * MLA_NOPE:(h + 1) * MLA_NOPE].astype(bf16)
        k_ref[:, base + MLA_NOPE:base + MLA_QK_PAD] = kr
    v_ref[...] = kv[:, nk:].astype(bf16)


def _mla_kv(pr, ckv_col, kr_col, w_kv, tm=TM_LIGHT):
    m = pr.shape[0]
    r, n = w_kv.shape
    assert ckv_col % r == 0 and kr_col % LANES == 0
    ckvn = krope = pr
    nk = MLA_HEADS * MLA_QK_PAD
    nv = MLA_HEADS * MLA_V
    return pl.pallas_call(
        _mla_kv_kernel,
        grid=(m // tm,),
        in_specs=[pl.BlockSpec((tm, r), lambda i: (i, ckv_col // r)),
                  pl.BlockSpec((r, n), lambda i: (0, 0)),
                  pl.BlockSpec((tm, LANES), lambda i: (i, kr_col // LANES))],
        out_specs=[pl.BlockSpec((tm, nk), lambda i: (i, 0)),
                   pl.BlockSpec((tm, nv), lambda i: (i, 0))],
        out_shape=[jax.ShapeDtypeStruct((m, nk), bf16), jax.ShapeDtypeStruct((m, nv), bf16)],
        compiler_params=_cparams(("parallel",)),
        name="mla_kv_up",
    )(ckvn, w_kv, krope)


def _mla_attn_kernel(q_ref, k_ref, v_ref, o_ref, s_ref, *, tq, tk):
    nq = q_ref.shape[0] // tq
    nkv = k_ref.shape[0] // tk

    def scores(q, c):
        return _dot_nt(q, k_ref[c * tk:(c + 1) * tk, :])

    def update(s, c, m, l, acc):
        m_new = jnp.maximum(m, jnp.max(s, axis=-1, keepdims=True))
        p = jnp.exp2(s - m_new)
        a = jnp.exp2(m - m_new)
        psum = p[:, :LANES]
        for g in range(1, tk // LANES):
            psum = psum + p[:, g * LANES:(g + 1) * LANES]
        l = a * l + psum
        acc = a * acc + _dot(p.astype(bf16), v_ref[c * tk:(c + 1) * tk, :])
        return m_new, l, acc

    s_ref[...] = scores(q_ref[0:tq, :], 0)

    def body(blk, carry):
        off = pl.multiple_of(blk * tq, tq)
        q = q_ref[pl.ds(off, tq), :]
        m = jnp.full((tq, 1), -jnp.inf, f32)
        l = jnp.zeros((tq, LANES), f32)
        acc = jnp.zeros((tq, MLA_V), f32)
        s_next = s_ref[...]
        for c in range(nkv):
            s_cur = s_next
            if c + 1 < nkv:
                s_next = scores(q, c + 1)
            else:
                nxt = pl.multiple_of(jnp.minimum(blk + 1, nq - 1) * tq, tq)
                s_ref[...] = scores(q_ref[pl.ds(nxt, tq), :], 0)
            m, l, acc = update(s_cur, c, m, l, acc)
        o_ref[pl.ds(off, tq), :] = (acc / jnp.sum(l, axis=-1, keepdims=True)).astype(o_ref.dtype)
        return carry

    lax.fori_loop(0, nq, body, 0)


def _mla_attention(q, k, v, tq=ATTN_TQ, tk=ATTN_TK):
    s = q.shape[0]
    return pl.pallas_call(
        functools.partial(_mla_attn_kernel, tq=tq, tk=tk),
        grid=(MLA_HEADS,),
        in_specs=[pl.BlockSpec((s, MLA_QK_PAD), lambda h: (0, h)),
                  pl.BlockSpec((s, MLA_QK_PAD), lambda h: (0, h)),
                  pl.BlockSpec((s, MLA_V), lambda h: (0, h))],
        out_specs=pl.BlockSpec((s, MLA_V), lambda h: (0, h)),
        out_shape=jax.ShapeDtypeStruct((s, MLA_HEADS * MLA_V), bf16),
        scratch_shapes=[pltpu.VMEM((tq, tk), f32)],
        compiler_params=_cparams(("parallel",)),
        name="mla_attention",
    )(q, k, v)


def _mem_attn_kernel(q_ref, k_ref, v_ref, o_ref):
    for h in range(MEM_HEADS):
        sl = slice(h * MEM_HD, (h + 1) * MEM_HD)
        s = _dot_nt(q_ref[:, sl], k_ref[:, sl])
        m = jnp.max(s, axis=-1, keepdims=True)
        p = jnp.exp(s - m)
        l = jnp.sum(p, axis=-1, keepdims=True)
        o = _dot(p.astype(bf16), v_ref[:, sl])
        o_ref[:, sl] = (o / l).astype(o_ref.dtype)


def _mem_attention(pr, q_col, k, v, tm=TM_LIGHT):
    s = pr.shape[0]
    nm, w = k.shape
    assert q_col % w == 0
    q = pr
    return pl.pallas_call(
        _mem_attn_kernel,
        grid=(s // tm,),
        in_specs=[pl.BlockSpec((tm, w), lambda i: (i, q_col // w)),
                  pl.BlockSpec((nm, w), lambda i: (0, 0)),
                  pl.BlockSpec((nm, w), lambda i: (0, 0))],
        out_specs=pl.BlockSpec((tm, w), lambda i: (i, 0)),
        out_shape=jax.ShapeDtypeStruct((s, w), bf16),
        compiler_params=_cparams(("parallel",)),
        name="mem_attention",
    )(q, k, v)


def _hgrn_levels(rev):
    levels = []
    sz = HG_CHUNK // 2
    while sz >= HG_SUB:
        blocks = []
        for base in range(0, HG_CHUNK, 2 * sz):
            if not rev:
                blocks.append((base + sz, base, sz, base + sz - 1))
            else:
                blocks.append((base, base + sz, sz, base + sz))
        levels.append(blocks)
        sz //= 2
    return levels


def _hgrn_intra(q_c, b_c, c_c, bbuf, cbuf, row0, *, rev):
    C = HG_CHUNK
    nb = C // HG_SUB
    lane = lax.broadcasted_iota(jnp.int32, (HG_SUB, C), 1)
    sub = lax.broadcasted_iota(jnp.int32, (HG_SUB, C), 0)
    blocks = []
    for i in range(nb):
        r0 = i * HG_SUB
        q_i = q_c[r0:r0 + HG_SUB]
        b_i = b_c[r0:r0 + HG_SUB]
        a_blk = jnp.zeros((HG_SUB, C), f32)
        for s in range(HG_SUB):
            w = q_i * jnp.exp2(b_i + cbuf[row0 + r0 + s:row0 + r0 + s + 1, :])
            a_blk = jnp.where(lane == r0 + s, jnp.sum(w, axis=-1, keepdims=True), a_blk)
        keep = (sub >= lane - r0) if not rev else (sub <= lane - r0)
        blocks.append(jnp.where(keep, a_blk, 0.0))
    for level in _hgrn_levels(rev):
        sz = level[0][2]
        qd, kd, valid = [], [], None
        k_pos = 0
        rq = lax.broadcasted_iota(jnp.int32, (len(level) * sz, C), 0)
        ck = lax.broadcasted_iota(jnp.int32, (len(level) * sz, C), 1)
        for j, (q0, k0, _, piv_row) in enumerate(level):
            piv = bbuf[row0 + piv_row:row0 + piv_row + 1, :]
            qd.append(q_c[q0:q0 + sz] * jnp.exp2(b_c[q0:q0 + sz] - piv))
            if k0 > k_pos:
                kd.append(jnp.zeros((k0 - k_pos, q_c.shape[1]), f32))
            kd.append(jnp.exp2(c_c[k0:k0 + sz] + piv))
            k_pos = k0 + sz
            ok = (rq >= j * sz) & (rq < (j + 1) * sz) & (ck >= k0) & (ck < k0 + sz)
            valid = ok if valid is None else (valid | ok)
        if k_pos < C:
            kd.append(jnp.zeros((C - k_pos, q_c.shape[1]), f32))
        r = _dot_nt(jnp.concatenate(qd, axis=0).astype(bf16), jnp.concatenate(kd, axis=0).astype(bf16))
        if len(level) > 1:
            r = jnp.where(valid, r, 0.0)
        for j, (q0, _, _, _) in enumerate(level):
            for t in range(sz // HG_SUB):
                i = q0 // HG_SUB + t
                blocks[i] = blocks[i] + r[j * sz + t * HG_SUB:j * sz + (t + 1) * HG_SUB]
    return jnp.concatenate(blocks, axis=0)


def _hgrn_block_kernel(qf_ref, vf_ref, ff_ref, qb_ref, vb_ref, fb_ref, lb_ref, of_ref, ob_ref,
                       sf_ref, sb_ref, bbuf, cbuf):
    n = pl.program_id(1)
    C = HG_CHUNK
    T = qf_ref.shape[0]
    nchunk = T // C

    @pl.when(n == 0)
    def _():
        sf_ref[...] = jnp.zeros_like(sf_ref)
        sb_ref[...] = jnp.zeros_like(sb_ref)

    r = lax.broadcasted_iota(jnp.int32, (C, C), 0)
    c = lax.broadcasted_iota(jnp.int32, (C, C), 1)
    dirs = []
    for d, (q_ref, v_ref, f_ref, rev) in enumerate(((qf_ref, vf_ref, ff_ref, False), (qb_ref, vb_ref, fb_ref, True))):
        tri = ((c <= r) if not rev else (c >= r)).astype(bf16)
        lb = lb_ref[d:d + 1, :]
        q = _silu(q_ref[...])
        v16 = v_ref[...].astype(bf16)
        f = lb + (1.0 - lb) * _sigmoid(f_ref[...])
        g = jnp.log(f) * LOG2E
        g_hi = g.astype(bf16)
        g_lo = (g - g_hi.astype(f32)).astype(bf16)
        b = jnp.concatenate([_dot(tri, g_hi[j * C:(j + 1) * C]) + _dot(tri, g_lo[j * C:(j + 1) * C])
                             for j in range(nchunk)], axis=0)
        cc = jnp.log(1.0 - f) * LOG2E - b
        bbuf[d] = b
        cbuf[d] = cc
        dirs.append((q, v16, b, cc, rev))

    intra = []
    for d, (q, v16, b, cc, rev) in enumerate(dirs):
        intra.append([_hgrn_intra(q[j * C:(j + 1) * C], b[j * C:(j + 1) * C], cc[j * C:(j + 1) * C],
                                  bbuf.at[d], cbuf.at[d], j * C, rev=rev).astype(bf16) for j in range(nchunk)])

    for d, (q, v16, b, cc, rev) in enumerate(dirs):
        s_ref, o_ref = (sf_ref, of_ref) if not rev else (sb_ref, ob_ref)
        qe = (q * jnp.exp2(b)).astype(bf16)
        order = range(nchunk) if not rev else range(nchunk - 1, -1, -1)
        edge = (C - 1) if not rev else 0
        upd = {}
        for j in order:
            b_edge = bbuf[d, j * C + edge:j * C + edge + 1, :]
            kdec = jnp.exp2(cc[j * C:(j + 1) * C] + b_edge).astype(bf16)
            upd[j] = (jnp.exp2(b_edge), _dot_tn(v16[j * C:(j + 1) * C], kdec))
        state = s_ref[...]
        for j in order:
            rows = slice(j * C, (j + 1) * C)
            o = _dot_nt(qe[rows], state.astype(bf16)) + _dot(intra[d][j], v16[rows])
            o_ref[rows, :] = o.astype(o_ref.dtype)
            state = upd[j][0] * state + upd[j][1]
        s_ref[...] = state


def _hgrn_scan(proj, lb, tb=HG_BLOCK):
    s = proj.shape[0]
    w = HG_HEADS * HG_D
    nb = s // tb
    hb = w // HG_D
    def fw(seg):
        return pl.BlockSpec((tb, HG_D), lambda h, n: (n, seg * hb + h))

    def bw(seg):
        return pl.BlockSpec((tb, HG_D), lambda h, n: (nb - 1 - n, seg * hb + h))

    return pl.pallas_call(
        _hgrn_block_kernel,
        grid=(HG_HEADS, nb),
        in_specs=[fw(0), fw(1), fw(2), bw(0), bw(1), bw(3),
                  pl.BlockSpec((2, HG_D), lambda h, n: (0, h))],
        out_specs=[pl.BlockSpec((tb, HG_D), lambda h, n: (n, h)),
                   pl.BlockSpec((tb, HG_D), lambda h, n: (nb - 1 - n, h))],
        out_shape=[jax.ShapeDtypeStruct((s, w), bf16), jax.ShapeDtypeStruct((s, w), bf16)],
        scratch_shapes=[pltpu.VMEM((HG_D, HG_D), f32), pltpu.VMEM((HG_D, HG_D), f32),
                        pltpu.VMEM((2, tb, HG_D), f32), pltpu.VMEM((2, tb, HG_D), f32)],
        compiler_params=_cparams(("parallel", "arbitrary")),
        name="hgrn_scan",
    )(proj, proj, proj, proj, proj, proj, lb)


def _lb_kernel(logit_ref, lb_ref, *, layer):
    x = logit_ref[...]
    e = jnp.exp(x - jnp.max(x, axis=1, keepdims=True))
    p = e / jnp.sum(e, axis=1, keepdims=True)
    acc = p[:, 0, :]
    for l in range(1, layer + 1):
        acc = acc + p[:, l, :]
    lb_ref[...] = acc


def _lower_bounds(logits, layer):
    z, _, w = logits.shape
    return pl.pallas_call(
        functools.partial(_lb_kernel, layer=layer),
        out_shape=jax.ShapeDtypeStruct((z, w), f32),
        name="hgrn_lower_bounds",
    )(logits)


def _merge_kernel(of_ref, ob_ref, hg_ref, ng_ref, y1_ref, y2_ref, w_ref, g0_ref, g1_ref, g2_ref, o_ref):
    acc = g1_ref[...].astype(f32) * _dot(y1_ref[...], w_ref[1])
    acc = acc + g2_ref[...].astype(f32) * _dot(y2_ref[...], w_ref[2])
    ng = ng_ref[...]
    y0 = []
    for h in range(HG_HEADS):
        sl = slice(h * HG_D, (h + 1) * HG_D)
        o = of_ref[:, sl].astype(f32) + ob_ref[:, sl].astype(f32)
        ms = jnp.mean(o * o, axis=-1, keepdims=True)
        y0.append((o * lax.rsqrt(ms + RMS_EPS) * ng * _sigmoid(hg_ref[:, sl])).astype(bf16))
    acc = acc + g0_ref[...].astype(f32) * _dot(jnp.concatenate(y0, axis=1), w_ref[0])
    o_ref[...] = acc.astype(o_ref.dtype)


def _merge(o_f, o_b, proj_hg, norm_g, y1, y2, w_branch, gsig, tm=TM_ROW):
    m, kb = y1.shape
    d = w_branch.shape[2]
    hg_gate_blk = 4
    yspec = pl.BlockSpec((tm, kb), lambda i: (i, 0))
    def gspec(b):
        return pl.BlockSpec((tm, d), lambda i: (i, b))

    return pl.pallas_call(
        _merge_kernel,
        grid=(m // tm,),
        in_specs=[yspec, yspec,
                  pl.BlockSpec((tm, kb), lambda i: (i, hg_gate_blk)),
                  pl.BlockSpec((1, HG_D), lambda i: (0, 0)),
                  yspec, yspec,
                  pl.BlockSpec((N_BRANCH, kb, d), lambda i: (0, 0, 0), pipeline_mode=pl.Buffered(1)),
                  gspec(0), gspec(1), gspec(2)],
        out_specs=pl.BlockSpec((tm, d), lambda i: (i, 0)),
        out_shape=jax.ShapeDtypeStruct((m, d), bf16),
        compiler_params=_cparams(("parallel",)),
        name="branch_merge",
    )(o_f, o_b, proj_hg, norm_g.reshape(1, HG_D), y1, y2, w_branch, gsig, gsig, gsig)


def _proj_ln_kernel(a_ref, w_ref, res_ref, rg_ref, rb_ref, g_ref, b_ref, o32_ref, o16_ref, *, alpha, res_is_raw):
    rows_per_group = a_ref.shape[0] // PROJ_LN_GROUPS
    groups = [slice(r * rows_per_group, (r + 1) * rows_per_group) for r in range(PROJ_LN_GROUPS)]
    proj = [_dot(a_ref[rows, :], w_ref[...]) for rows in groups]
    for rows, p in zip(groups, proj):
        res = res_ref[rows, :]
        if res_is_raw:
            res = _ln_rows(res, rg_ref[...], rb_ref[...])
        y = _ln_rows(alpha * res + p, g_ref[...], b_ref[...])
        o32_ref[rows, :] = y
        o16_ref[rows, :] = y.astype(bf16)


def _proj_residual_ln(a, w, res, res_ln, g, b, alpha, tm=TM_ROW):
    m, k = a.shape
    d = w.shape[1]
    rg, rb = res_ln if res_ln is not None else (g, b)
    vec = pl.BlockSpec((1, d), lambda i: (0, 0))
    return pl.pallas_call(
        functools.partial(_proj_ln_kernel, alpha=alpha, res_is_raw=res_ln is not None),
        grid=(m // tm,),
        in_specs=[pl.BlockSpec((tm, k), lambda i: (i, 0)),
                  pl.BlockSpec((k, d), lambda i: (0, 0)),
                  pl.BlockSpec((tm, d), lambda i: (i, 0)),
                  vec, vec, vec, vec],
        out_specs=[pl.BlockSpec((tm, d), lambda i: (i, 0)),
                   pl.BlockSpec((tm, d), lambda i: (i, 0))],
        out_shape=[jax.ShapeDtypeStruct((m, d), f32), jax.ShapeDtypeStruct((m, d), bf16)],
        compiler_params=_cparams(("parallel",)),
        name="out_proj_ln",
    )(a, w, res, rg.reshape(1, d), rb.reshape(1, d), g.reshape(1, d), b.reshape(1, d))


def _ffn_kernel(x_ref, wg_ref, wu_ref, wd_ref, res_ref, g_ref, b_ref, o_ref, *, alpha):
    j = pl.program_id(1)
    rw = res_ref.shape[1]

    @pl.when(j == 0)
    def _():
        o_ref[...] = jnp.zeros_like(o_ref)

    x = x_ref[...]
    hid = _silu(_dot(x, wg_ref[...].astype(bf16))) * _dot(x, wu_ref[...].astype(bf16))
    o_ref[...] += _dot(hid.astype(bf16), wd_ref[...].astype(bf16))

    @pl.when(j < o_ref.shape[1] // rw)
    def _():
        cols = pl.ds(pl.multiple_of(j * rw, rw), rw)
        o_ref[:, cols] += alpha * res_ref[...]

    @pl.when(j == pl.num_programs(1) - 1)
    def _():
        o_ref[...] = _ln_rows(o_ref[...], g_ref[...], b_ref[...])


def _ffn_residual_ln(x16, x32, wg, wu, wd, g, b, alpha, tm=FFN_TM, tf=FFN_TF):
    m, d = x16.shape
    ff = wg.shape[1]
    n_res = d // LANES
    assert n_res <= ff // tf
    return pl.pallas_call(
        functools.partial(_ffn_kernel, alpha=alpha),
        grid=(m // tm, ff // tf),
        in_specs=[pl.BlockSpec((tm, d), lambda i, j: (i, 0)),
                  pl.BlockSpec((d, tf), lambda i, j: (0, j)),
                  pl.BlockSpec((d, tf), lambda i, j: (0, j)),
                  pl.BlockSpec((tf, d), lambda i, j: (j, 0)),
                  pl.BlockSpec((tm, LANES), lambda i, j: (i, jnp.minimum(j, n_res - 1))),
                  pl.BlockSpec((1, d), lambda i, j: (0, 0)),
                  pl.BlockSpec((1, d), lambda i, j: (0, 0))],
        out_specs=pl.BlockSpec((tm, d), lambda i, j: (i, 0)),
        out_shape=jax.ShapeDtypeStruct((m, d), f32),
        compiler_params=_cparams(("parallel", "arbitrary")),
        name="ffn_ln",
    )(x16, wg, wu, wd, x32, g.reshape(1, d), b.reshape(1, d))


def _rope_tables(positions):
    half = MLA_ROPE // 2
    inv_freq = jnp.power(ROPE_THETA, -jnp.arange(half, dtype=f32) / half)
    ang = positions.astype(f32)[..., None] * inv_freq
    cos, sin = jnp.cos(ang), jnp.sin(ang)
    z = jnp.zeros_like(cos)
    pad = jnp.zeros((ang.shape[0], LANES - MLA_ROPE), f32)
    c = jnp.concatenate([cos, cos, pad], axis=-1)
    s1 = jnp.concatenate([-sin, z, pad], axis=-1)
    s2 = jnp.concatenate([z, sin, pad], axis=-1)
    return c, s1, s2


def _layer(res, res_ln, h16, mem16, rope, lb, alpha, w_in, hgrn_norm_g, g_cq, g_ckv, w_uq, w_ukv, w_memkv,
           w_branch, w_o, ln1_g, ln1_b, w_gate, w_up, w_down, ln2_g, ln2_b):
    s, d = res.shape
    hgw = HG_HEADS * HG_D
    q_rank = g_cq.shape[0]
    kv_rank = g_ckv.shape[0]
    memw = MEM_HEADS * MEM_HD
    o_cq = 5 * hgw
    o_ckv = o_cq + q_rank
    o_kr = o_ckv + kv_rank
    o_qm = o_kr + MLA_ROPE
    o_gt = o_qm + memw
    rope_c, rope_s1, rope_s2 = rope

    wt = jnp.swapaxes(w_in, 0, 1)

    proj_hg = _matmul(h16, wt, wt_rows=(0, o_cq), tm=TM_IN_PROJ, tn=TN_IN_PROJ, out_dtype=f32,
                      name="in_proj_hgrn")
    gsig = _matmul(h16, wt, wt_rows=(o_gt, N_BRANCH * d), tm=TM_IN_PROJ, tn=TN_IN_PROJ, out_dtype=bf16,
                   epilogue=_sigmoid, name="in_proj_gates")
    pr = _in_proj_rest(h16, wt, o_qm, memw, o_cq, o_ckv, o_kr, g_cq, g_ckv, rope)
    c_qm = 0
    c_cq = c_qm + memw
    c_ckv = c_cq + q_rank
    c_kr = c_ckv + kv_rank

    o_f, o_b = _hgrn_scan(proj_hg, lb)

    zq = jnp.zeros((q_rank, MLA_HEADS, MLA_QK_PAD - MLA_QK), f32)
    w_q = jnp.concatenate([w_uq.reshape(q_rank, MLA_HEADS, MLA_QK), zq], axis=-1)
    w_q = w_q.reshape(q_rank, MLA_HEADS * MLA_QK_PAD).astype(bf16)
    w_kv3 = w_ukv.reshape(kv_rank, MLA_HEADS, MLA_NOPE + MLA_V)
    w_kv = jnp.concatenate([w_kv3[:, :, :MLA_NOPE].reshape(kv_rank, -1),
                            w_kv3[:, :, MLA_NOPE:].reshape(kv_rank, -1)], axis=-1).astype(bf16)
    q = _mla_q(pr, c_cq, w_q, rope_c, rope_s1, rope_s2)
    k, v = _mla_kv(pr, c_ckv, c_kr, w_kv)
    y_mla = _mla_attention(q, k, v)

    kv_mem = _matmul(mem16, w_memkv, tm=mem16.shape[0], tn=TN_IN_PROJ, out_dtype=bf16, name="mem_kv_proj")
    y_mem = _mem_attention(pr, c_qm, kv_mem[:, :memw], kv_mem[:, memw:])

    merged = _merge(o_f, o_b, proj_hg, hgrn_norm_g, y_mla, y_mem, w_branch.astype(bf16), gsig)
    x32, x16 = _proj_residual_ln(merged, w_o.astype(bf16), res, res_ln, ln1_g, ln1_b, alpha)
    return _ffn_residual_ln(x16, x32, w_gate, w_up, w_down, ln2_g, ln2_b, alpha)


def kernel(x, mem, positions, ln_emb_g, ln_emb_b, hgrn_lb_logits, w_in, hgrn_norm_g, mla_g_cq, mla_g_ckv,
           mla_w_uq, mla_w_ukv, mem_w_kv, w_branch, w_o, ln1_g, ln1_b, w_ffn_gate, w_ffn_up, w_ffn_down,
           ln2_g, ln2_b):
    bsz = x.shape[0]
    depth = w_in.shape[0]
    alpha = (2.0 * depth) ** 0.25
    outs = []
    for bi in range(bsz):
        rope = _rope_tables(positions[bi])
        mem16 = mem[bi].astype(bf16)
        h16 = _layernorm_bf16(x[bi], ln_emb_g, ln_emb_b)
        res, res_ln = x[bi], (ln_emb_g, ln_emb_b)
        for l in range(depth):
            lb = _lower_bounds(hgrn_lb_logits, l)
            res = _layer(res, res_ln, h16, mem16, rope, lb, alpha, w_in[l], hgrn_norm_g[l], mla_g_cq[l],
                         mla_g_ckv[l], mla_w_uq[l], mla_w_ukv[l], mem_w_kv[l], w_branch[l], w_o[l], ln1_g[l],
                         ln1_b[l], w_ffn_gate[l], w_ffn_up[l], w_ffn_down[l], ln2_g[l], ln2_b[l])
            res_ln = None
            if l + 1 < depth:
                h16 = res.astype(bf16)
        outs.append(res)
    return jnp.stack(outs)
```

```python
import functools

import jax
import jax.numpy as jnp
from jax import lax
from jax.experimental import pallas as pl
from jax.experimental.pallas import tpu as pltpu

HG_HEADS = 8
HG_D = 128
MLA_HEADS = 8
MLA_NOPE = 128
MLA_ROPE = 64
MLA_V = 128
MLA_QK = MLA_NOPE + MLA_ROPE
MLA_QK_PAD = 256
LOG2E = 1.4426950408889634
MEM_HEADS = 4
MEM_HD = 256
N_BRANCH = 3
ROPE_THETA = 10000.0
LN_EPS = 1e-5
RMS_EPS = 1e-6

LANES = 128
SUBLANES = 8
VMEM_LIMIT = 56 * 1024 * 1024

TM_IN_PROJ = 2048
TN_IN_PROJ = 512
TM_ROW = 512
TM_LIGHT = 1024
ATTN_TQ = 512
ATTN_TK = 256
FFN_TM = 1024
FFN_TF = 256
PROJ_LN_GROUPS = 2
HG_BLOCK = 2048
HG_CHUNK = 64
HG_SUB = SUBLANES

bf16 = jnp.bfloat16
f32 = jnp.float32


def _cparams(sem):
    return pltpu.CompilerParams(dimension_semantics=sem, vmem_limit_bytes=VMEM_LIMIT)


def _dot(a, b):
    return jnp.dot(a, b, preferred_element_type=f32)


def _dot_nt(a, b):
    return lax.dot_general(a, b, (((1,), (1,)), ((), ())), preferred_element_type=f32)


def _dot_tn(a, b):
    return lax.dot_general(a, b, (((0,), (0,)), ((), ())), preferred_element_type=f32)


def _ln_rows(x, g, b):
    mu = jnp.mean(x, axis=-1, keepdims=True)
    xc = x - mu
    var = jnp.mean(xc * xc, axis=-1, keepdims=True)
    return xc * lax.rsqrt(var + LN_EPS) * g + b


def _ln_kernel(x_ref, g_ref, b_ref, o16_ref):
    o16_ref[...] = _ln_rows(x_ref[...], g_ref[...], b_ref[...]).astype(bf16)


def _layernorm_bf16(x, g, b, tm=TM_LIGHT):
    m, d = x.shape
    return pl.pallas_call(
        _ln_kernel,
        grid=(m // tm,),
        in_specs=[pl.BlockSpec((tm, d), lambda i: (i, 0)),
                  pl.BlockSpec((1, d), lambda i: (0, 0)),
                  pl.BlockSpec((1, d), lambda i: (0, 0))],
        out_specs=pl.BlockSpec((tm, d), lambda i: (i, 0)),
        out_shape=jax.ShapeDtypeStruct((m, d), bf16),
        compiler_params=_cparams(("parallel",)),
        name="ln_embed",
    )(x, g.reshape(1, d), b.reshape(1, d))


def _mm_kernel(a_ref, w_ref, o_ref, *, epilogue, w_transposed):
    w = w_ref[...].astype(bf16)
    acc = _dot_nt(a_ref[...], w) if w_transposed else _dot(a_ref[...], w)
    o_ref[...] = epilogue(acc).astype(o_ref.dtype)


def _matmul(a, w, *, tm, tn, out_dtype, epilogue=lambda acc: acc, name, wt_rows=None):
    m, k = a.shape
    if wt_rows is None:
        n = w.shape[1]
        w_spec = pl.BlockSpec((k, tn), lambda i, j: (0, j))
    else:
        row0, n = wt_rows
        assert row0 % SUBLANES == 0 and w.shape[1] == k
        w_spec = pl.BlockSpec((pl.Element(tn), pl.Element(k)),
                              lambda i, j: ((row0 // SUBLANES + j * (tn // SUBLANES)) * SUBLANES, 0))
    return pl.pallas_call(
        functools.partial(_mm_kernel, epilogue=epilogue, w_transposed=wt_rows is not None),
        grid=(m // tm, n // tn),
        in_specs=[pl.BlockSpec((tm, k), lambda i, j: (i, 0)), w_spec],
        out_specs=pl.BlockSpec((tm, tn), lambda i, j: (i, j)),
        out_shape=jax.ShapeDtypeStruct((m, n), out_dtype),
        compiler_params=_cparams(("parallel", "parallel")),
        name=name,
    )(a, w)


def _sigmoid(x):
    return 0.5 * jnp.tanh(0.5 * x) + 0.5


def _silu(x):
    return x * _sigmoid(x)


def _rms_epilogue(acc, g):
    ms = jnp.mean(acc * acc, axis=-1, keepdims=True)
    return acc * lax.rsqrt(ms + RMS_EPS) * g


def _rope_block(blk, c, s1, s2):
    half = MLA_ROPE // 2
    return blk * c + pltpu.roll(blk, LANES - half, 1) * s1 + pltpu.roll(blk, half, 1) * s2


def _in_proj_rest_kernel(a_ref, w_ref, gcq_ref, gckv_ref, c_ref, s1_ref, s2_ref, o_ref, *, n_qmem):
    j = pl.program_id(1)

    def product(w_rows):
        return _dot_nt(a_ref[...], w_ref[:w_rows, :].astype(bf16))

    tn = w_ref.shape[0]

    @pl.when(j < n_qmem)
    def _():
        o_ref[...] = (product(tn) * (MEM_HD ** -0.5)).astype(o_ref.dtype)

    @pl.when(j == n_qmem)
    def _():
        o_ref[...] = _rms_epilogue(product(tn), gcq_ref[...]).astype(o_ref.dtype)

    @pl.when(j == n_qmem + 1)
    def _():
        o_ref[...] = _rms_epilogue(product(tn), gckv_ref[...]).astype(o_ref.dtype)

    @pl.when(j == n_qmem + 2)
    def _():
        o_ref[:, :LANES] = _rope_block(product(LANES), c_ref[...], s1_ref[...], s2_ref[...]).astype(o_ref.dtype)
        o_ref[:, LANES:] = jnp.zeros((o_ref.shape[0], tn - LANES), o_ref.dtype)


def _in_proj_rest(h16, wt, o_qm, n_qm, o_cq, o_ckv, o_kr, g_cq, g_ckv, rope, tm=TM_IN_PROJ, tn=TN_IN_PROJ):
    m, k = h16.shape
    n_qmem = n_qm // tn
    assert n_qm % tn == 0 and g_cq.shape[0] == tn and g_ckv.shape[0] == tn
    assert all(o % SUBLANES == 0 for o in (o_qm, o_cq, o_ckv, o_kr))
    n_tiles = n_qmem + 3

    def w_row(i, j):
        r = jnp.where(j < n_qmem, o_qm // SUBLANES + j * (tn // SUBLANES),
                      jnp.where(j == n_qmem, o_cq // SUBLANES,
                                jnp.where(j == n_qmem + 1, o_ckv // SUBLANES, o_kr // SUBLANES)))
        return (r * SUBLANES, 0)

    vec = pl.BlockSpec((1, tn), lambda i, j: (0, 0))
    tab = pl.BlockSpec((tm, LANES), lambda i, j: (i, 0))
    return pl.pallas_call(
        functools.partial(_in_proj_rest_kernel, n_qmem=n_qmem),
        grid=(m // tm, n_tiles),
        in_specs=[pl.BlockSpec((tm, k), lambda i, j: (i, 0)),
                  pl.BlockSpec((pl.Element(tn), pl.Element(k)), w_row),
                  vec, vec, tab, tab, tab],
        out_specs=pl.BlockSpec((tm, tn), lambda i, j: (i, j)),
        out_shape=jax.ShapeDtypeStruct((m, n_tiles * tn), bf16),
        compiler_params=_cparams(("parallel", "arbitrary")),
        name="in_proj_rest",
    )(h16, wt, g_cq.reshape(1, tn), g_ckv.reshape(1, tn), *rope)


def _mla_q_kernel(cq_ref, w_ref, c_ref, s1_ref, s2_ref, o_ref):
    q = _dot(cq_ref[...], w_ref[...]) * (MLA_QK ** -0.5 * LOG2E)
    c, s1, s2 = c_ref[...], s1_ref[...], s2_ref[...]
    for h in range(MLA_HEADS):
        base = h * MLA_QK_PAD
        o_ref[:, base:base + MLA_NOPE] = q[:, base:base + MLA_NOPE].astype(bf16)
        o_ref[:, base + MLA_NOPE:base + MLA_QK_PAD] = _rope_block(
            q[:, base + MLA_NOPE:base + MLA_QK_PAD], c, s1, s2).astype(bf16)


def _mla_q(pr, cq_col, w_q, rope_c, rope_s1, rope_s2, tm=TM_LIGHT):
    m = pr.shape[0]
    r, n = w_q.shape
    assert cq_col % r == 0
    cqn = pr
    return pl.pallas_call(
        _mla_q_kernel,
        grid=(m // tm,),
        in_specs=[pl.BlockSpec((tm, r), lambda i: (i, cq_col // r)),
                  pl.BlockSpec((r, n), lambda i: (0, 0)),
                  pl.BlockSpec((tm, LANES), lambda i: (i, 0)),
                  pl.BlockSpec((tm, LANES), lambda i: (i, 0)),
                  pl.BlockSpec((tm, LANES), lambda i: (i, 0))],
        out_specs=pl.BlockSpec((tm, n), lambda i: (i, 0)),
        out_shape=jax.ShapeDtypeStruct((m, n), bf16),
        compiler_params=_cparams(("parallel",)),
        name="mla_q_up",
    )(cqn, w_q, rope_c, rope_s1, rope_s2)


def _mla_kv_kernel(ckv_ref, w_ref, kr_ref, k_ref, v_ref):
    kv = _dot(ckv_ref[...], w_ref[...])
    kr = kr_ref[...]
    nk = MLA_HEADS * MLA_NOPE
    for h in range(MLA_HEADS):
        base = h * MLA_QK_PAD
        k_ref[:, base:base + MLA_NOPE] = kv[:, h * MLA_NOPE:(h + 1) * MLA_NOPE].astype(bf16)
        k_ref[:, base + MLA_NOPE:base + MLA_QK_PAD] = kr
    v_ref[...] = kv[:, nk:].astype(bf16)


def _mla_kv(pr, ckv_col, kr_col, w_kv, tm=TM_LIGHT):
    m = pr.shape[0]
    r, n = w_kv.shape
    assert ckv_col % r == 0 and kr_col % LANES == 0
    ckvn = krope = pr
    nk = MLA_HEADS * MLA_QK_PAD
    nv = MLA_HEADS * MLA_V
    return pl.pallas_call(
        _mla_kv_kernel,
        grid=(m // tm,),
        in_specs=[pl.BlockSpec((tm, r), lambda i: (i, ckv_col // r)),
                  pl.BlockSpec((r, n), lambda i: (0, 0)),
                  pl.BlockSpec((tm, LANES), lambda i: (i, kr_col // LANES))],
        out_specs=[pl.BlockSpec((tm, nk), lambda i: (i, 0)),
                   pl.BlockSpec((tm, nv), lambda i: (i, 0))],
        out_shape=[jax.ShapeDtypeStruct((m, nk), bf16), jax.ShapeDtypeStruct((m, nv), bf16)],
        compiler_params=_cparams(("parallel",)),
        name="mla_kv_up",
    )(ckvn, w_kv, krope)


def _mla_attn_kernel(q_ref, k_ref, v_ref, o_ref, s_ref, *, tq, tk):
    nq = q_ref.shape[0] // tq
    nkv = k_ref.shape[0] // tk

    def scores(q, c):
        return _dot_nt(q, k_ref[c * tk:(c + 1) * tk, :])

    def update(s, c, m, l, acc):
        m_new = jnp.maximum(m, jnp.max(s, axis=-1, keepdims=True))
        p = jnp.exp2(s - m_new)
        a = jnp.exp2(m - m_new)
        psum = p[:, :LANES]
        for g in range(1, tk // LANES):
            psum = psum + p[:, g * LANES:(g + 1) * LANES]
        l = a * l + psum
        acc = a * acc + _dot(p.astype(bf16), v_ref[c * tk:(c + 1) * tk, :])
        return m_new, l, acc

    s_ref[...] = scores(q_ref[0:tq, :], 0)

    def body(blk, carry):
        off = pl.multiple_of(blk * tq, tq)
        q = q_ref[pl.ds(off, tq), :]
        m = jnp.full((tq, 1), -jnp.inf, f32)
        l = jnp.zeros((tq, LANES), f32)
        acc = jnp.zeros((tq, MLA_V), f32)
        s_next = s_ref[...]
        for c in range(nkv):
            s_cur = s_next
            if c + 1 < nkv:
                s_next = scores(q, c + 1)
            else:
                nxt = pl.multiple_of(jnp.minimum(blk + 1, nq - 1) * tq, tq)
                s_ref[...] = scores(q_ref[pl.ds(nxt, tq), :], 0)
            m, l, acc = update(s_cur, c, m, l, acc)
        o_ref[pl.ds(off, tq), :] = (acc / jnp.sum(l, axis=-1, keepdims=True)).astype(o_ref.dtype)
        return carry

    lax.fori_loop(0, nq, body, 0)


def _mla_attention(q, k, v, tq=ATTN_TQ, tk=ATTN_TK):
    s = q.shape[0]
    return pl.pallas_call(
        functools.partial(_mla_attn_kernel, tq=tq, tk=tk),
        grid=(MLA_HEADS,),
        in_specs=[pl.BlockSpec((s, MLA_QK_PAD), lambda h: (0, h)),
                  pl.BlockSpec((s, MLA_QK_PAD), lambda h: (0, h)),
                  pl.BlockSpec((s, MLA_V), lambda h: (0, h))],
        out_specs=pl.BlockSpec((s, MLA_V), lambda h: (0, h)),
        out_shape=jax.ShapeDtypeStruct((s, MLA_HEADS * MLA_V), bf16),
        scratch_shapes=[pltpu.VMEM((tq, tk), f32)],
        compiler_params=_cparams(("parallel",)),
        name="mla_attention",
    )(q, k, v)


def _mem_attn_kernel(q_ref, k_ref, v_ref, o_ref):
    for h in range(MEM_HEADS):
        sl = slice(h * MEM_HD, (h + 1) * MEM_HD)
        s = _dot_nt(q_ref[:, sl], k_ref[:, sl])
        m = jnp.max(s, axis=-1, keepdims=True)
        p = jnp.exp(s - m)
        l = jnp.sum(p, axis=-1, keepdims=True)
        o = _dot(p.astype(bf16), v_ref[:, sl])
        o_ref[:, sl] = (o / l).astype(o_ref.dtype)


def _mem_attention(pr, q_col, k, v, tm=TM_LIGHT):
    s = pr.shape[0]
    nm, w = k.shape
    assert q_col % w == 0
    q = pr
    return pl.pallas_call(
        _mem_attn_kernel,
        grid=(s // tm,),
        in_specs=[pl.BlockSpec((tm, w), lambda i: (i, q_col // w)),
                  pl.BlockSpec((nm, w), lambda i: (0, 0)),
                  pl.BlockSpec((nm, w), lambda i: (0, 0))],
        out_specs=pl.BlockSpec((tm, w), lambda i: (i, 0)),
        out_shape=jax.ShapeDtypeStruct((s, w), bf16),
        compiler_params=_cparams(("parallel",)),
        name="mem_attention",
    )(q, k, v)


def _hgrn_levels(rev):
    levels = []
    sz = HG_CHUNK // 2
    while sz >= HG_SUB:
        blocks = []
        for base in range(0, HG_CHUNK, 2 * sz):
            if not rev:
                blocks.append((base + sz, base, sz, base + sz - 1))
            else:
                blocks.append((base, base + sz, sz, base + sz))
        levels.append(blocks)
        sz //= 2
    return levels


def _hgrn_intra(q_c, b_c, c_c, bbuf, cbuf, row0, *, rev):
    C = HG_CHUNK
    nb = C // HG_SUB
    lane = lax.broadcasted_iota(jnp.int32, (HG_SUB, C), 1)
    sub = lax.broadcasted_iota(jnp.int32, (HG_SUB, C), 0)
    blocks = []
    for i in range(nb):
        r0 = i * HG_SUB
        q_i = q_c[r0:r0 + HG_SUB]
        b_i = b_c[r0:r0 + HG_SUB]
        a_blk = jnp.zeros((HG_SUB, C), f32)
        for s in range(HG_SUB):
            w = q_i * jnp.exp2(b_i + cbuf[row0 + r0 + s:row0 + r0 + s + 1, :])
            a_blk = jnp.where(lane == r0 + s, jnp.sum(w, axis=-1, keepdims=True), a_blk)
        keep = (sub >= lane - r0) if not rev else (sub <= lane - r0)
        blocks.append(jnp.where(keep, a_blk, 0.0))
    for level in _hgrn_levels(rev):
        sz = level[0][2]
        qd, kd, valid = [], [], None
        k_pos = 0
        rq = lax.broadcasted_iota(jnp.int32, (len(level) * sz, C), 0)
        ck = lax.broadcasted_iota(jnp.int32, (len(level) * sz, C), 1)
        for j, (q0, k0, _, piv_row) in enumerate(level):
            piv = bbuf[row0 + piv_row:row0 + piv_row + 1, :]
            qd.append(q_c[q0:q0 + sz] * jnp.exp2(b_c[q0:q0 + sz] - piv))
            if k0 > k_pos:
                kd.append(jnp.zeros((k0 - k_pos, q_c.shape[1]), f32))
            kd.append(jnp.exp2(c_c[k0:k0 + sz] + piv))
            k_pos = k0 + sz
            ok = (rq >= j * sz) & (rq < (j + 1) * sz) & (ck >= k0) & (ck < k0 + sz)
            valid = ok if valid is None else (valid | ok)
        if k_pos < C:
            kd.append(jnp.zeros((C - k_pos, q_c.shape[1]), f32))
        r = _dot_nt(jnp.concatenate(qd, axis=0).astype(bf16), jnp.concatenate(kd, axis=0).astype(bf16))
        if len(level) > 1:
            r = jnp.where(valid, r, 0.0)
        for j, (q0, _, _, _) in enumerate(level):
            for t in range(sz // HG_SUB):
                i = q0 // HG_SUB + t
                blocks[i] = blocks[i] + r[j * sz + t * HG_SUB:j * sz + (t + 1) * HG_SUB]
    return jnp.concatenate(blocks, axis=0)


def _hgrn_block_kernel(qf_ref, vf_ref, ff_ref, qb_ref, vb_ref, fb_ref, lb_ref, of_ref, ob_ref,
                       sf_ref, sb_ref, bbuf, cbuf):
    n = pl.program_id(1)
    C = HG_CHUNK
    T = qf_ref.shape[0]
    nchunk = T // C

    @pl.when(n == 0)
    def _():
        sf_ref[...] = jnp.zeros_like(sf_ref)
        sb_ref[...] = jnp.zeros_like(sb_ref)

    r = lax.broadcasted_iota(jnp.int32, (C, C), 0)
    c = lax.broadcasted_iota(jnp.int32, (C, C), 1)
    dirs = []
    for d, (q_ref, v_ref, f_ref, rev) in enumerate(((qf_ref, vf_ref, ff_ref, False), (qb_ref, vb_ref, fb_ref, True))):
        tri = ((c <= r) if not rev else (c >= r)).astype(bf16)
        lb = lb_ref[d:d + 1, :]
        q = _silu(q_ref[...])
        v16 = v_ref[...].astype(bf16)
        f = lb + (1.0 - lb) * _sigmoid(f_ref[...])
        g = jnp.log(f) * LOG2E
        g_hi = g.astype(bf16)
        g_lo = (g - g_hi.astype(f32)).astype(bf16)
        b = jnp.concatenate([_dot(tri, g_hi[j * C:(j + 1) * C]) + _dot(tri, g_lo[j * C:(j + 1) * C])
                             for j in range(nchunk)], axis=0)
        cc = jnp.log(1.0 - f) * LOG2E - b
        bbuf[d] = b
        cbuf[d] = cc
        dirs.append((q, v16, b, cc, rev))

    intra = []
    for d, (q, v16, b, cc, rev) in enumerate(dirs):
        intra.append([_hgrn_intra(q[j * C:(j + 1) * C], b[j * C:(j + 1) * C], cc[j * C:(j + 1) * C],
                                  bbuf.at[d], cbuf.at[d], j * C, rev=rev).astype(bf16) for j in range(nchunk)])

    for d, (q, v16, b, cc, rev) in enumerate(dirs):
        s_ref, o_ref = (sf_ref, of_ref) if not rev else (sb_ref, ob_ref)
        qe = (q * jnp.exp2(b)).astype(bf16)
        order = range(nchunk) if not rev else range(nchunk - 1, -1, -1)
        edge = (C - 1) if not rev else 0
        upd = {}
        for j in order:
            b_edge = bbuf[d, j * C + edge:j * C + edge + 1, :]
            kdec = jnp.exp2(cc[j * C:(j + 1) * C] + b_edge).astype(bf16)
            upd[j] = (jnp.exp2(b_edge), _dot_tn(v16[j * C:(j + 1) * C], kdec))
        state = s_ref[...]
        for j in order:
            rows = slice(j * C, (j + 1) * C)
            o = _dot_nt(qe[rows], state.astype(bf16)) + _dot(intra[d][j], v16[rows])
            o_ref[rows, :] = o.astype(o_ref.dtype)
            state = upd[j][0] * state + upd[j][1]
        s_ref[...] = state


def _hgrn_scan(proj, lb, tb=HG_BLOCK):
    s = proj.shape[0]
    w = HG_HEADS * HG_D
    nb = s // tb
    hb = w // HG_D
    def fw(seg):
        return pl.BlockSpec((tb, HG_D), lambda h, n: (n, seg * hb + h))

    def bw(seg):
        return pl.BlockSpec((tb, HG_D), lambda h, n: (nb - 1 - n, seg * hb + h))

    return pl.pallas_call(
        _hgrn_block_kernel,
        grid=(HG_HEADS, nb),
        in_specs=[fw(0), fw(1), fw(2), bw(0), bw(1), bw(3),
                  pl.BlockSpec((2, HG_D), lambda h, n: (0, h))],
        out_specs=[pl.BlockSpec((tb, HG_D), lambda h, n: (n, h)),
                   pl.BlockSpec((tb, HG_D), lambda h, n: (nb - 1 - n, h))],
        out_shape=[jax.ShapeDtypeStruct((s, w), bf16), jax.ShapeDtypeStruct((s, w), bf16)],
        scratch_shapes=[pltpu.VMEM((HG_D, HG_D), f32), pltpu.VMEM((HG_D, HG_D), f32),
                        pltpu.VMEM((2, tb, HG_D), f32), pltpu.VMEM((2, tb, HG_D), f32)],
        compiler_params=_cparams(("parallel", "arbitrary")),
        name="hgrn_scan",
    )(proj, proj, proj, proj, proj, proj, lb)


def _lb_kernel(logit_ref, lb_ref, *, layer):
    x = logit_ref[...]
    e = jnp.exp(x - jnp.max(x, axis=1, keepdims=True))
    p = e / jnp.sum(e, axis=1, keepdims=True)
    acc = p[:, 0, :]
    for l in range(1, layer + 1):
        acc = acc + p[:, l, :]
    lb_ref[...] = acc


def _lower_bounds(logits, layer):
    z, _, w = logits.shape
    return pl.pallas_call(
        functools.partial(_lb_kernel, layer=layer),
        out_shape=jax.ShapeDtypeStruct((z, w), f32),
        name="hgrn_lower_bounds",
    )(logits)


def _merge_kernel(of_ref, ob_ref, hg_ref, ng_ref, y1_ref, y2_ref, w_ref, g0_ref, g1_ref, g2_ref, o_ref):
    acc = g1_ref[...].astype(f32) * _dot(y1_ref[...], w_ref[1])
    acc = acc + g2_ref[...].astype(f32) * _dot(y2_ref[...], w_ref[2])
    ng = ng_ref[...]
    y0 = []
    for h in range(HG_HEADS):
        sl = slice(h * HG_D, (h + 1) * HG_D)
        o = of_ref[:, sl].astype(f32) + ob_ref[:, sl].astype(f32)
        ms = jnp.mean(o * o, axis=-1, keepdims=True)
        y0.append((o * lax.rsqrt(ms + RMS_EPS) * ng * _sigmoid(hg_ref[:, sl])).astype(bf16))
    acc = acc + g0_ref[...].astype(f32) * _dot(jnp.concatenate(y0, axis=1), w_ref[0])
    o_ref[...] = acc.astype(o_ref.dtype)


def _merge(o_f, o_b, proj_hg, norm_g, y1, y2, w_branch, gsig, tm=TM_ROW):
    m, kb = y1.shape
    d = w_branch.shape[2]
    hg_gate_blk = 4
    yspec = pl.BlockSpec((tm, kb), lambda i: (i, 0))
    def gspec(b):
        return pl.BlockSpec((tm, d), lambda i: (i, b))

    return pl.pallas_call(
        _merge_kernel,
        grid=(m // tm,),
        in_specs=[yspec, yspec,
                  pl.BlockSpec((tm, kb), lambda i: (i, hg_gate_blk)),
                  pl.BlockSpec((1, HG_D), lambda i: (0, 0)),
                  yspec, yspec,
                  pl.BlockSpec((N_BRANCH, kb, d), lambda i: (0, 0, 0), pipeline_mode=pl.Buffered(1)),
                  gspec(0), gspec(1), gspec(2)],
        out_specs=pl.BlockSpec((tm, d), lambda i: (i, 0)),
        out_shape=jax.ShapeDtypeStruct((m, d), bf16),
        compiler_params=_cparams(("parallel",)),
        name="branch_merge",
    )(o_f, o_b, proj_hg, norm_g.reshape(1, HG_D), y1, y2, w_branch, gsig, gsig, gsig)


def _proj_ln_kernel(a_ref, w_ref, res_ref, rg_ref, rb_ref, g_ref, b_ref, o32_ref, o16_ref, *, alpha, res_is_raw):
    rows_per_group = a_ref.shape[0] // PROJ_LN_GROUPS
    groups = [slice(r * rows_per_group, (r + 1) * rows_per_group) for r in range(PROJ_LN_GROUPS)]
    proj = [_dot(a_ref[rows, :], w_ref[...]) for rows in groups]
    for rows, p in zip(groups, proj):
        res = res_ref[rows, :]
        if res_is_raw:
            res = _ln_rows(res, rg_ref[...], rb_ref[...])
        y = _ln_rows(alpha * res + p, g_ref[...], b_ref[...])
        o32_ref[rows, :] = y
        o16_ref[rows, :] = y.astype(bf16)


def _proj_residual_ln(a, w, res, res_ln, g, b, alpha, tm=TM_ROW):
    m, k = a.shape
    d = w.shape[1]
    rg, rb = res_ln if res_ln is not None else (g, b)
    vec = pl.BlockSpec((1, d), lambda i: (0, 0))
    return pl.pallas_call(
        functools.partial(_proj_ln_kernel, alpha=alpha, res_is_raw=res_ln is not None),
        grid=(m // tm,),
        in_specs=[pl.BlockSpec((tm, k), lambda i: (i, 0)),
                  pl.BlockSpec((k, d), lambda i: (0, 0)),
                  pl.BlockSpec((tm, d), lambda i: (i, 0)),
                  vec, vec, vec, vec],
        out_specs=[pl.BlockSpec((tm, d), lambda i: (i, 0)),
                   pl.BlockSpec((tm, d), lambda i: (i, 0))],
        out_shape=[jax.ShapeDtypeStruct((m, d), f32), jax.ShapeDtypeStruct((m, d), bf16)],
        compiler_params=_cparams(("parallel",)),
        name="out_proj_ln",
    )(a, w, res, rg.reshape(1, d), rb.reshape(1, d), g.reshape(1, d), b.reshape(1, d))


def _ffn_kernel(x_ref, wg_ref, wu_ref, wd_ref, res_ref, g_ref, b_ref, o_ref, *, alpha):
    j = pl.program_id(1)
    rw = res_ref.shape[1]

    @pl.when(j == 0)
    def _():
        o_ref[...] = jnp.zeros_like(o_ref)

    x = x_ref[...]
    hid = _silu(_dot(x, wg_ref[...].astype(bf16))) * _dot(x, wu_ref[...].astype(bf16))
    o_ref[...] += _dot(hid.astype(bf16), wd_ref[...].astype(bf16))

    @pl.when(j < o_ref.shape[1] // rw)
    def _():
        cols = pl.ds(pl.multiple_of(j * rw, rw), rw)
        o_ref[:, cols] += alpha * res_ref[...]

    @pl.when(j == pl.num_programs(1) - 1)
    def _():
        o_ref[...] = _ln_rows(o_ref[...], g_ref[...], b_ref[...])


def _ffn_residual_ln(x16, x32, wg, wu, wd, g, b, alpha, tm=FFN_TM, tf=FFN_TF):
    m, d = x16.shape
    ff = wg.shape[1]
    n_res = d // LANES
    assert n_res <= ff // tf
    return pl.pallas_call(
        functools.partial(_ffn_kernel, alpha=alpha),
        grid=(m // tm, ff // tf),
        in_specs=[pl.BlockSpec((tm, d), lambda i, j: (i, 0)),
                  pl.BlockSpec((d, tf), lambda i, j: (0, j)),
                  pl.BlockSpec((d, tf), lambda i, j: (0, j)),
                  pl.BlockSpec((tf, d), lambda i, j: (j, 0)),
                  pl.BlockSpec((tm, LANES), lambda i, j: (i, jnp.minimum(j, n_res - 1))),
                  pl.BlockSpec((1, d), lambda i, j: (0, 0)),
                  pl.BlockSpec((1, d), lambda i, j: (0, 0))],
        out_specs=pl.BlockSpec((tm, d), lambda i, j: (i, 0)),
        out_shape=jax.ShapeDtypeStruct((m, d), f32),
        compiler_params=_cparams(("parallel", "arbitrary")),
        name="ffn_ln",
    )(x16, wg, wu, wd, x32, g.reshape(1, d), b.reshape(1, d))


def _rope_tables(positions):
    half = MLA_ROPE // 2
    inv_freq = jnp.power(ROPE_THETA, -jnp.arange(half, dtype=f32) / half)
    ang = positions.astype(f32)[..., None] * inv_freq
    cos, sin = jnp.cos(ang), jnp.sin(ang)
    z = jnp.zeros_like(cos)
    pad = jnp.zeros((ang.shape[0], LANES - MLA_ROPE), f32)
    c = jnp.concatenate([cos, cos, pad], axis=-1)
    s1 = jnp.concatenate([-sin, z, pad], axis=-1)
    s2 = jnp.concatenate([z, sin, pad], axis=-1)
    return c, s1, s2


def _layer(res, res_ln, h16, mem16, rope, lb, alpha, w_in, hgrn_norm_g, g_cq, g_ckv, w_uq, w_ukv, w_memkv,
           w_branch, w_o, ln1_g, ln1_b, w_gate, w_up, w_down, ln2_g, ln2_b):
    s, d = res.shape
    hgw = HG_HEADS * HG_D
    q_rank = g_cq.shape[0]
    kv_rank = g_ckv.shape[0]
    memw = MEM_HEADS * MEM_HD
    o_cq = 5 * hgw
    o_ckv = o_cq + q_rank
    o_kr = o_ckv + kv_rank
    o_qm = o_kr + MLA_ROPE
    o_gt = o_qm + memw
    rope_c, rope_s1, rope_s2 = rope

    wt = jnp.swapaxes(w_in, 0, 1)

    proj_hg = _matmul(h16, wt, wt_rows=(0, o_cq), tm=TM_IN_PROJ, tn=TN_IN_PROJ, out_dtype=f32,
                      name="in_proj_hgrn")
    gsig = _matmul(h16, wt, wt_rows=(o_gt, N_BRANCH * d), tm=TM_IN_PROJ, tn=TN_IN_PROJ, out_dtype=bf16,
                   epilogue=_sigmoid, name="in_proj_gates")
    pr = _in_proj_rest(h16, wt, o_qm, memw, o_cq, o_ckv, o_kr, g_cq, g_ckv, rope)
    c_qm = 0
    c_cq = c_qm + memw
    c_ckv = c_cq + q_rank
    c_kr = c_ckv + kv_rank

    o_f, o_b = _hgrn_scan(proj_hg, lb)

    zq = jnp.zeros((q_rank, MLA_HEADS, MLA_QK_PAD - MLA_QK), f32)
    w_q = jnp.concatenate([w_uq.reshape(q_rank, MLA_HEADS, MLA_QK), zq], axis=-1)
    w_q = w_q.reshape(q_rank, MLA_HEADS * MLA_QK_PAD).astype(bf16)
    w_kv3 = w_ukv.reshape(kv_rank, MLA_HEADS, MLA_NOPE + MLA_V)
    w_kv = jnp.concatenate([w_kv3[:, :, :MLA_NOPE].reshape(kv_rank, -1),
                            w_kv3[:, :, MLA_NOPE:].reshape(kv_rank, -1)], axis=-1).astype(bf16)
    q = _mla_q(pr, c_cq, w_q, rope_c, rope_s1, rope_s2)
    k, v = _mla_kv(pr, c_ckv, c_kr, w_kv)
    y_mla = _mla_attention(q, k, v)

    kv_mem = _matmul(mem16, w_memkv, tm=mem16.shape[0], tn=TN_IN_PROJ, out_dtype=bf16, name="mem_kv_proj")
    y_mem = _mem_attention(pr, c_qm, kv_mem[:, :memw], kv_mem[:, memw:])

    merged = _merge(o_f, o_b, proj_hg, hgrn_norm_g, y_mla, y_mem, w_branch.astype(bf16), gsig)
    x32, x16 = _proj_residual_ln(merged, w_o.astype(bf16), res, res_ln, ln1_g, ln1_b, alpha)
    return _ffn_residual_ln(x16, x32, w_gate, w_up, w_down, ln2_g, ln2_b, alpha)


def kernel(x, mem, positions, ln_emb_g, ln_emb_b, hgrn_lb_logits, w_in, hgrn_norm_g, mla_g_cq, mla_g_ckv,
           mla_w_uq, mla_w_ukv, mem_w_kv, w_branch, w_o, ln1_g, ln1_b, w_ffn_gate, w_ffn_up, w_ffn_down,
           ln2_g, ln2_b):
    bsz = x.shape[0]
    depth = w_in.shape[0]
    alpha = (2.0 * depth) ** 0.25
    outs = []
    for bi in range(bsz):
        rope = _rope_tables(positions[bi])
        mem16 = mem[bi].astype(bf16)
        h16 = _layernorm_bf16(x[bi], ln_emb_g, ln_emb_b)
        res, res_ln = x[bi], (ln_emb_g, ln_emb_b)
        for l in range(depth):
            lb = _lower_bounds(hgrn_lb_logits, l)
            res = _layer(res, res_ln, h16, mem16, rope, lb, alpha, w_in[l], hgrn_norm_g[l], mla_g_cq[l],
                         mla_g_ckv[l], mla_w_uq[l], mla_w_ukv[l], mem_w_kv[l], w_branch[l], w_o[l], ln1_g[l],
                         ln1_b[l], w_ffn_gate[l], w_ffn_up[l], w_ffn_down[l], ln2_g[l], ln2_b[l])
            res_ln = None
            if l + 1 < depth:
                h16 = res.astype(bf16)
        outs.append(res)
    return jnp.stack(outs)
```

```python
import functools

import jax
import jax.numpy as jnp
from jax import lax
from jax.experimental import pallas as pl
from jax.experimental.pallas import tpu as pltpu

HG_HEADS = 8
HG_D = 128
MLA_HEADS = 8
MLA_NOPE = 128
MLA_ROPE = 64
MLA_V = 128
MLA_QK = MLA_NOPE + MLA_ROPE
MLA_QK_PAD = 256
LOG2E = 1.4426950408889634
MEM_HEADS = 4
MEM_HD = 256
N_BRANCH = 3
ROPE_THETA = 10000.0
LN_EPS = 1e-5
RMS_EPS = 1e-6

LANES = 128
SUBLANES = 8
VMEM_LIMIT = 56 * 1024 * 1024

TM_IN_PROJ = 2048
TN_IN_PROJ = 512
TN_GATES = 1024
TM_ROW = 512
TM_LIGHT = 1024
ATTN_TQ = 512
ATTN_TK = 256
FFN_TM = 1024
FFN_TF = 256
PROJ_LN_GROUPS = 2
HG_BLOCK = 2048
HG_CHUNK = 64
HG_SUB = SUBLANES

bf16 = jnp.bfloat16
f32 = jnp.float32


def _cparams(sem):
    return pltpu.CompilerParams(dimension_semantics=sem, vmem_limit_bytes=VMEM_LIMIT)


def _dot(a, b):
    return jnp.dot(a, b, preferred_element_type=f32)


def _dot_nt(a, b):
    return lax.dot_general(a, b, (((1,), (1,)), ((), ())), preferred_element_type=f32)


def _dot_tn(a, b):
    return lax.dot_general(a, b, (((0,), (0,)), ((), ())), preferred_element_type=f32)


def _ln_rows(x, g, b):
    mu = jnp.mean(x, axis=-1, keepdims=True)
    xc = x - mu
    var = jnp.mean(xc * xc, axis=-1, keepdims=True)
    return xc * lax.rsqrt(var + LN_EPS) * g + b


def _ln_kernel(x_ref, g_ref, b_ref, o16_ref):
    o16_ref[...] = _ln_rows(x_ref[...], g_ref[...], b_ref[...]).astype(bf16)


def _layernorm_bf16(x, g, b, tm=TM_LIGHT):
    m, d = x.shape
    return pl.pallas_call(
        _ln_kernel,
        grid=(m // tm,),
        in_specs=[pl.BlockSpec((tm, d), lambda i: (i, 0)),
                  pl.BlockSpec((1, d), lambda i: (0, 0)),
                  pl.BlockSpec((1, d), lambda i: (0, 0))],
        out_specs=pl.BlockSpec((tm, d), lambda i: (i, 0)),
        out_shape=jax.ShapeDtypeStruct((m, d), bf16),
        compiler_params=_cparams(("parallel",)),
        name="ln_embed",
    )(x, g.reshape(1, d), b.reshape(1, d))


def _mm_kernel(a_ref, w_ref, o_ref, *, epilogue, w_transposed):
    w = w_ref[...].astype(bf16)
    acc = _dot_nt(a_ref[...], w) if w_transposed else _dot(a_ref[...], w)
    o_ref[...] = epilogue(acc).astype(o_ref.dtype)


def _matmul(a, w, *, tm, tn, out_dtype, epilogue=lambda acc: acc, name, wt_rows=None):
    m, k = a.shape
    if wt_rows is None:
        n = w.shape[1]
        w_spec = pl.BlockSpec((k, tn), lambda i, j: (0, j))
    else:
        row0, n = wt_rows
        assert row0 % SUBLANES == 0 and w.shape[1] == k
        w_spec = pl.BlockSpec((pl.Element(tn), pl.Element(k)),
                              lambda i, j: ((row0 // SUBLANES + j * (tn // SUBLANES)) * SUBLANES, 0))
    return pl.pallas_call(
        functools.partial(_mm_kernel, epilogue=epilogue, w_transposed=wt_rows is not None),
        grid=(m // tm, n // tn),
        in_specs=[pl.BlockSpec((tm, k), lambda i, j: (i, 0)), w_spec],
        out_specs=pl.BlockSpec((tm, tn), lambda i, j: (i, j)),
        out_shape=jax.ShapeDtypeStruct((m, n), out_dtype),
        compiler_params=_cparams(("parallel", "parallel")),
        name=name,
    )(a, w)


def _sigmoid(x):
    return 0.5 * jnp.tanh(0.5 * x) + 0.5


def _silu(x):
    return x * _sigmoid(x)


def _rms_epilogue(acc, g):
    ms = jnp.mean(acc * acc, axis=-1, keepdims=True)
    return acc * lax.rsqrt(ms + RMS_EPS) * g


def _rope_block(blk, c, s1, s2):
    half = MLA_ROPE // 2
    return blk * c + pltpu.roll(blk, LANES - half, 1) * s1 + pltpu.roll(blk, half, 1) * s2


def _in_proj_rest_kernel(a_ref, w_ref, gcq_ref, gckv_ref, c_ref, s1_ref, s2_ref, o_ref, *, n_qmem):
    j = pl.program_id(1)

    def product(w_rows):
        return _dot_nt(a_ref[...], w_ref[:w_rows, :].astype(bf16))

    tn = w_ref.shape[0]

    @pl.when(j < n_qmem)
    def _():
        o_ref[...] = (product(tn) * (MEM_HD ** -0.5)).astype(o_ref.dtype)

    @pl.when(j == n_qmem)
    def _():
        o_ref[...] = _rms_epilogue(product(tn), gcq_ref[...]).astype(o_ref.dtype)

    @pl.when(j == n_qmem + 1)
    def _():
        o_ref[...] = _rms_epilogue(product(tn), gckv_ref[...]).astype(o_ref.dtype)

    @pl.when(j == n_qmem + 2)
    def _():
        o_ref[:, :LANES] = _rope_block(product(LANES), c_ref[...], s1_ref[...], s2_ref[...]).astype(o_ref.dtype)
        o_ref[:, LANES:] = jnp.zeros((o_ref.shape[0], tn - LANES), o_ref.dtype)


def _in_proj_rest(h16, wt, o_qm, n_qm, o_cq, o_ckv, o_kr, g_cq, g_ckv, rope, tm=TM_IN_PROJ, tn=TN_IN_PROJ):
    m, k = h16.shape
    n_qmem = n_qm // tn
    assert n_qm % tn == 0 and g_cq.shape[0] == tn and g_ckv.shape[0] == tn
    assert all(o % SUBLANES == 0 for o in (o_qm, o_cq, o_ckv, o_kr))
    n_tiles = n_qmem + 3

    def w_row(i, j):
        r = jnp.where(j < n_qmem, o_qm // SUBLANES + j * (tn // SUBLANES),
                      jnp.where(j == n_qmem, o_cq // SUBLANES,
                                jnp.where(j == n_qmem + 1, o_ckv // SUBLANES, o_kr // SUBLANES)))
        return (r * SUBLANES, 0)

    vec = pl.BlockSpec((1, tn), lambda i, j: (0, 0))
    tab = pl.BlockSpec((tm, LANES), lambda i, j: (i, 0))
    return pl.pallas_call(
        functools.partial(_in_proj_rest_kernel, n_qmem=n_qmem),
        grid=(m // tm, n_tiles),
        in_specs=[pl.BlockSpec((tm, k), lambda i, j: (i, 0)),
                  pl.BlockSpec((pl.Element(tn), pl.Element(k)), w_row),
                  vec, vec, tab, tab, tab],
        out_specs=pl.BlockSpec((tm, tn), lambda i, j: (i, j)),
        out_shape=jax.ShapeDtypeStruct((m, n_tiles * tn), bf16),
        compiler_params=_cparams(("parallel", "arbitrary")),
        name="in_proj_rest",
    )(h16, wt, g_cq.reshape(1, tn), g_ckv.reshape(1, tn), *rope)


def _mla_q_kernel(cq_ref, w_ref, c_ref, s1_ref, s2_ref, o_ref):
    q = _dot(cq_ref[...], w_ref[...]) * (MLA_QK ** -0.5 * LOG2E)
    c, s1, s2 = c_ref[...], s1_ref[...], s2_ref[...]
    for h in range(MLA_HEADS):
        base = h * MLA_QK_PAD
        o_ref[:, base:base + MLA_NOPE] = q[:, base:base + MLA_NOPE].astype(bf16)
        o_ref[:, base + MLA_NOPE:base + MLA_QK_PAD] = _rope_block(
            q[:, base + MLA_NOPE:base + MLA_QK_PAD], c, s1, s2).astype(bf16)


def _mla_q(pr, cq_col, w_q, rope_c, rope_s1, rope_s2, tm=TM_LIGHT):
    m = pr.shape[0]
    r, n = w_q.shape
    assert cq_col % r == 0
    cqn = pr
    return pl.pallas_call(
        _mla_q_kernel,
        grid=(m // tm,),
        in_specs=[pl.BlockSpec((tm, r), lambda i: (i, cq_col // r)),
                  pl.BlockSpec((r, n), lambda i: (0, 0)),
                  pl.BlockSpec((tm, LANES), lambda i: (i, 0)),
                  pl.BlockSpec((tm, LANES), lambda i: (i, 0)),
                  pl.BlockSpec((tm, LANES), lambda i: (i, 0))],
        out_specs=pl.BlockSpec((tm, n), lambda i: (i, 0)),
        out_shape=jax.ShapeDtypeStruct((m, n), bf16),
        compiler_params=_cparams(("parallel",)),
        name="mla_q_up",
    )(cqn, w_q, rope_c, rope_s1, rope_s2)


def _mla_kv_kernel(ckv_ref, w_ref, kr_ref, k_ref, v_ref):
    kv = _dot(ckv_ref[...], w_ref[...])
    kr = kr_ref[...]
    nk = MLA_HEADS * MLA_NOPE
    for h in range(MLA_HEADS):
        base = h * MLA_QK_PAD
        k_ref[:, base:base + MLA_NOPE] = kv[:, h * MLA_NOPE:(h + 1) * MLA_NOPE].astype(bf16)
        k_ref[:, base + MLA_NOPE:base + MLA_QK_PAD] = kr
    v_ref[...] = kv[:, nk:].astype(bf16)


def _mla_kv(pr, ckv_col, kr_col, w_kv, tm=TM_LIGHT):
    m = pr.shape[0]
    r, n = w_kv.shape
    assert ckv_col % r == 0 and kr_col % LANES == 0
    ckvn = krope = pr
    nk = MLA_HEADS * MLA_QK_PAD
    nv = MLA_HEADS * MLA_V
    return pl.pallas_call(
        _mla_kv_kernel,
        grid=(m // tm,),
        in_specs=[pl.BlockSpec((tm, r), lambda i: (i, ckv_col // r)),
                  pl.BlockSpec((r, n), lambda i: (0, 0)),
                  pl.BlockSpec((tm, LANES), lambda i: (i, kr_col // LANES))],
        out_specs=[pl.BlockSpec((tm, nk), lambda i: (i, 0)),
                   pl.BlockSpec((tm, nv), lambda i: (i, 0))],
        out_shape=[jax.ShapeDtypeStruct((m, nk), bf16), jax.ShapeDtypeStruct((m, nv), bf16)],
        compiler_params=_cparams(("parallel",)),
        name="mla_kv_up",
    )(ckvn, w_kv, krope)


def _mla_attn_kernel(q_ref, k_ref, v_ref, o_ref, s_ref, *, tq, tk):
    nq = q_ref.shape[0] // tq
    nkv = k_ref.shape[0] // tk

    def scores(q, c):
        return _dot_nt(q, k_ref[c * tk:(c + 1) * tk, :])

    def update(s, c, m, l, acc):
        m_new = jnp.maximum(m, jnp.max(s, axis=-1, keepdims=True))
        p = jnp.exp2(s - m_new)
        a = jnp.exp2(m - m_new)
        psum = p[:, :LANES]
        for g in range(1, tk // LANES):
            psum = psum + p[:, g * LANES:(g + 1) * LANES]
        l = a * l + psum
        acc = a * acc + _dot(p.astype(bf16), v_ref[c * tk:(c + 1) * tk, :])
        return m_new, l, acc

    s_ref[...] = scores(q_ref[0:tq, :], 0)

    def body(blk, carry):
        off = pl.multiple_of(blk * tq, tq)
        q = q_ref[pl.ds(off, tq), :]
        m = jnp.full((tq, 1), -jnp.inf, f32)
        l = jnp.zeros((tq, LANES), f32)
        acc = jnp.zeros((tq, MLA_V), f32)
        s_next = s_ref[...]
        for c in range(nkv):
            s_cur = s_next
            if c + 1 < nkv:
                s_next = scores(q, c + 1)
            else:
                nxt = pl.multiple_of(jnp.minimum(blk + 1, nq - 1) * tq, tq)
                s_ref[...] = scores(q_ref[pl.ds(nxt, tq), :], 0)
            m, l, acc = update(s_cur, c, m, l, acc)
        o_ref[pl.ds(off, tq), :] = (acc / jnp.sum(l, axis=-1, keepdims=True)).astype(o_ref.dtype)
        return carry

    lax.fori_loop(0, nq, body, 0)


def _mla_attention(q, k, v, tq=ATTN_TQ, tk=ATTN_TK):
    s = q.shape[0]
    return pl.pallas_call(
        functools.partial(_mla_attn_kernel, tq=tq, tk=tk),
        grid=(MLA_HEADS,),
        in_specs=[pl.BlockSpec((s, MLA_QK_PAD), lambda h: (0, h)),
                  pl.BlockSpec((s, MLA_QK_PAD), lambda h: (0, h)),
                  pl.BlockSpec((s, MLA_V), lambda h: (0, h))],
        out_specs=pl.BlockSpec((s, MLA_V), lambda h: (0, h)),
        out_shape=jax.ShapeDtypeStruct((s, MLA_HEADS * MLA_V), bf16),
        scratch_shapes=[pltpu.VMEM((tq, tk), f32)],
        compiler_params=_cparams(("parallel",)),
        name="mla_attention",
    )(q, k, v)


def _mem_attn_kernel(q_ref, k_ref, v_ref, o_ref):
    for h in range(MEM_HEADS):
        sl = slice(h * MEM_HD, (h + 1) * MEM_HD)
        s = _dot_nt(q_ref[:, sl], k_ref[:, sl])
        m = jnp.max(s, axis=-1, keepdims=True)
        p = jnp.exp(s - m)
        l = jnp.sum(p, axis=-1, keepdims=True)
        o = _dot(p.astype(bf16), v_ref[:, sl])
        o_ref[:, sl] = (o / l).astype(o_ref.dtype)


def _mem_attention(pr, q_col, k, v, tm=TM_LIGHT):
    s = pr.shape[0]
    nm, w = k.shape
    assert q_col % w == 0
    q = pr
    return pl.pallas_call(
        _mem_attn_kernel,
        grid=(s // tm,),
        in_specs=[pl.BlockSpec((tm, w), lambda i: (i, q_col // w)),
                  pl.BlockSpec((nm, w), lambda i: (0, 0)),
                  pl.BlockSpec((nm, w), lambda i: (0, 0))],
        out_specs=pl.BlockSpec((tm, w), lambda i: (i, 0)),
        out_shape=jax.ShapeDtypeStruct((s, w), bf16),
        compiler_params=_cparams(("parallel",)),
        name="mem_attention",
    )(q, k, v)


def _hgrn_levels(rev):
    levels = []
    sz = HG_CHUNK // 2
    while sz >= HG_SUB:
        blocks = []
        for base in range(0, HG_CHUNK, 2 * sz):
            if not rev:
                blocks.append((base + sz, base, sz, base + sz - 1))
            else:
                blocks.append((base, base + sz, sz, base + sz))
        levels.append(blocks)
        sz //= 2
    return levels


def _hgrn_intra(q_c, b_c, c_c, bbuf, cbuf, row0, *, rev):
    C = HG_CHUNK
    nb = C // HG_SUB
    lane = lax.broadcasted_iota(jnp.int32, (HG_SUB, C), 1)
    sub = lax.broadcasted_iota(jnp.int32, (HG_SUB, C), 0)
    blocks = []
    for i in range(nb):
        r0 = i * HG_SUB
        q_i = q_c[r0:r0 + HG_SUB]
        b_i = b_c[r0:r0 + HG_SUB]
        a_blk = jnp.zeros((HG_SUB, C), f32)
        for s in range(HG_SUB):
            w = q_i * jnp.exp2(b_i + cbuf[row0 + r0 + s:row0 + r0 + s + 1, :])
            a_blk = jnp.where(lane == r0 + s, jnp.sum(w, axis=-1, keepdims=True), a_blk)
        keep = (sub >= lane - r0) if not rev else (sub <= lane - r0)
        blocks.append(jnp.where(keep, a_blk, 0.0))
    for level in _hgrn_levels(rev):
        sz = level[0][2]
        qd, kd, valid = [], [], None
        k_pos = 0
        rq = lax.broadcasted_iota(jnp.int32, (len(level) * sz, C), 0)
        ck = lax.broadcasted_iota(jnp.int32, (len(level) * sz, C), 1)
        for j, (q0, k0, _, piv_row) in enumerate(level):
            piv = bbuf[row0 + piv_row:row0 + piv_row + 1, :]
            qd.append(q_c[q0:q0 + sz] * jnp.exp2(b_c[q0:q0 + sz] - piv))
            if k0 > k_pos:
                kd.append(jnp.zeros((k0 - k_pos, q_c.shape[1]), f32))
            kd.append(jnp.exp2(c_c[k0:k0 + sz] + piv))
            k_pos = k0 + sz
            ok = (rq >= j * sz) & (rq < (j + 1) * sz) & (ck >= k0) & (ck < k0 + sz)
            valid = ok if valid is None else (valid | ok)
        if k_pos < C:
            kd.append(jnp.zeros((C - k_pos, q_c.shape[1]), f32))
        r = _dot_nt(jnp.concatenate(qd, axis=0).astype(bf16), jnp.concatenate(kd, axis=0).astype(bf16))
        if len(level) > 1:
            r = jnp.where(valid, r, 0.0)
        for j, (q0, _, _, _) in enumerate(level):
            for t in range(sz // HG_SUB):
                i = q0 // HG_SUB + t
                blocks[i] = blocks[i] + r[j * sz + t * HG_SUB:j * sz + (t + 1) * HG_SUB]
    return jnp.concatenate(blocks, axis=0)


def _hgrn_block_kernel(qf_ref, vf_ref, ff_ref, qb_ref, vb_ref, fb_ref, lb_ref, of_ref, ob_ref,
                       sf_ref, sb_ref, bbuf, cbuf):
    n = pl.program_id(1)
    C = HG_CHUNK
    T = qf_ref.shape[0]
    nchunk = T // C

    @pl.when(n == 0)
    def _():
        sf_ref[...] = jnp.zeros_like(sf_ref)
        sb_ref[...] = jnp.zeros_like(sb_ref)

    r = lax.broadcasted_iota(jnp.int32, (C, C), 0)
    c = lax.broadcasted_iota(jnp.int32, (C, C), 1)
    dirs = []
    for d, (q_ref, v_ref, f_ref, rev) in enumerate(((qf_ref, vf_ref, ff_ref, False), (qb_ref, vb_ref, fb_ref, True))):
        tri = ((c <= r) if not rev else (c >= r)).astype(bf16)
        lb = lb_ref[d:d + 1, :]
        q = _silu(q_ref[...])
        v16 = v_ref[...].astype(bf16)
        f = lb + (1.0 - lb) * _sigmoid(f_ref[...])
        g = jnp.log(f) * LOG2E
        g_hi = g.astype(bf16)
        g_lo = (g - g_hi.astype(f32)).astype(bf16)
        b = jnp.concatenate([_dot(tri, g_hi[j * C:(j + 1) * C]) + _dot(tri, g_lo[j * C:(j + 1) * C])
                             for j in range(nchunk)], axis=0)
        cc = jnp.log(1.0 - f) * LOG2E - b
        bbuf[d] = b
        cbuf[d] = cc
        dirs.append((q, v16, b, cc, rev))

    intra = []
    for d, (q, v16, b, cc, rev) in enumerate(dirs):
        intra.append([_hgrn_intra(q[j * C:(j + 1) * C], b[j * C:(j + 1) * C], cc[j * C:(j + 1) * C],
                                  bbuf.at[d], cbuf.at[d], j * C, rev=rev).astype(bf16) for j in range(nchunk)])

    for d, (q, v16, b, cc, rev) in enumerate(dirs):
        s_ref, o_ref = (sf_ref, of_ref) if not rev else (sb_ref, ob_ref)
        qe = (q * jnp.exp2(b)).astype(bf16)
        order = range(nchunk) if not rev else range(nchunk - 1, -1, -1)
        edge = (C - 1) if not rev else 0
        upd = {}
        for j in order:
            b_edge = bbuf[d, j * C + edge:j * C + edge + 1, :]
            kdec = jnp.exp2(cc[j * C:(j + 1) * C] + b_edge).astype(bf16)
            upd[j] = (jnp.exp2(b_edge), _dot_tn(v16[j * C:(j + 1) * C], kdec))
        state = s_ref[...]
        for j in order:
            rows = slice(j * C, (j + 1) * C)
            o = _dot_nt(qe[rows], state.astype(bf16)) + _dot(intra[d][j], v16[rows])
            o_ref[rows, :] = o.astype(o_ref.dtype)
            state = upd[j][0] * state + upd[j][1]
        s_ref[...] = state


def _hgrn_scan(proj, lb, tb=HG_BLOCK):
    s = proj.shape[0]
    w = HG_HEADS * HG_D
    nb = s // tb
    hb = w // HG_D
    def fw(seg):
        return pl.BlockSpec((tb, HG_D), lambda h, n: (n, seg * hb + h))

    def bw(seg):
        return pl.BlockSpec((tb, HG_D), lambda h, n: (nb - 1 - n, seg * hb + h))

    return pl.pallas_call(
        _hgrn_block_kernel,
        grid=(HG_HEADS, nb),
        in_specs=[fw(0), fw(1), fw(2), bw(0), bw(1), bw(3),
                  pl.BlockSpec((2, HG_D), lambda h, n: (0, h))],
        out_specs=[pl.BlockSpec((tb, HG_D), lambda h, n: (n, h)),
                   pl.BlockSpec((tb, HG_D), lambda h, n: (nb - 1 - n, h))],
        out_shape=[jax.ShapeDtypeStruct((s, w), bf16), jax.ShapeDtypeStruct((s, w), bf16)],
        scratch_shapes=[pltpu.VMEM((HG_D, HG_D), f32), pltpu.VMEM((HG_D, HG_D), f32),
                        pltpu.VMEM((2, tb, HG_D), f32), pltpu.VMEM((2, tb, HG_D), f32)],
        compiler_params=_cparams(("parallel", "arbitrary")),
        name="hgrn_scan",
    )(proj, proj, proj, proj, proj, proj, lb)


def _lb_kernel(logit_ref, lb_ref, *, layer):
    x = logit_ref[...]
    e = jnp.exp(x - jnp.max(x, axis=1, keepdims=True))
    p = e / jnp.sum(e, axis=1, keepdims=True)
    acc = p[:, 0, :]
    for l in range(1, layer + 1):
        acc = acc + p[:, l, :]
    lb_ref[...] = acc


def _lower_bounds(logits, layer):
    z, _, w = logits.shape
    return pl.pallas_call(
        functools.partial(_lb_kernel, layer=layer),
        out_shape=jax.ShapeDtypeStruct((z, w), f32),
        name="hgrn_lower_bounds",
    )(logits)


def _merge_kernel(of_ref, ob_ref, hg_ref, ng_ref, y1_ref, y2_ref, w_ref, g0_ref, g1_ref, g2_ref, o_ref):
    acc = g1_ref[...].astype(f32) * _dot(y1_ref[...], w_ref[1])
    acc = acc + g2_ref[...].astype(f32) * _dot(y2_ref[...], w_ref[2])
    ng = ng_ref[...]
    y0 = []
    for h in range(HG_HEADS):
        sl = slice(h * HG_D, (h + 1) * HG_D)
        o = of_ref[:, sl].astype(f32) + ob_ref[:, sl].astype(f32)
        ms = jnp.mean(o * o, axis=-1, keepdims=True)
        y0.append((o * lax.rsqrt(ms + RMS_EPS) * ng * _sigmoid(hg_ref[:, sl])).astype(bf16))
    acc = acc + g0_ref[...].astype(f32) * _dot(jnp.concatenate(y0, axis=1), w_ref[0])
    o_ref[...] = acc.astype(o_ref.dtype)


def _merge(o_f, o_b, proj_hg, norm_g, y1, y2, w_branch, gsig, tm=TM_ROW):
    m, kb = y1.shape
    d = w_branch.shape[2]
    hg_gate_blk = 4
    yspec = pl.BlockSpec((tm, kb), lambda i: (i, 0))
    def gspec(b):
        return pl.BlockSpec((tm, d), lambda i: (i, b))

    return pl.pallas_call(
        _merge_kernel,
        grid=(m // tm,),
        in_specs=[yspec, yspec,
                  pl.BlockSpec((tm, kb), lambda i: (i, hg_gate_blk)),
                  pl.BlockSpec((1, HG_D), lambda i: (0, 0)),
                  yspec, yspec,
                  pl.BlockSpec((N_BRANCH, kb, d), lambda i: (0, 0, 0), pipeline_mode=pl.Buffered(1)),
                  gspec(0), gspec(1), gspec(2)],
        out_specs=pl.BlockSpec((tm, d), lambda i: (i, 0)),
        out_shape=jax.ShapeDtypeStruct((m, d), bf16),
        compiler_params=_cparams(("parallel",)),
        name="branch_merge",
    )(o_f, o_b, proj_hg, norm_g.reshape(1, HG_D), y1, y2, w_branch, gsig, gsig, gsig)


def _proj_ln_kernel(a_ref, w_ref, res_ref, rg_ref, rb_ref, g_ref, b_ref, o32_ref, o16_ref, *, alpha, res_is_raw):
    rows_per_group = a_ref.shape[0] // PROJ_LN_GROUPS
    groups = [slice(r * rows_per_group, (r + 1) * rows_per_group) for r in range(PROJ_LN_GROUPS)]
    proj = [_dot(a_ref[rows, :], w_ref[...]) for rows in groups]
    for rows, p in zip(groups, proj):
        res = res_ref[rows, :]
        if res_is_raw:
            res = _ln_rows(res, rg_ref[...], rb_ref[...])
        y = _ln_rows(alpha * res + p, g_ref[...], b_ref[...])
        o32_ref[rows, :] = y
        o16_ref[rows, :] = y.astype(bf16)


def _proj_residual_ln(a, w, res, res_ln, g, b, alpha, tm=TM_ROW):
    m, k = a.shape
    d = w.shape[1]
    rg, rb = res_ln if res_ln is not None else (g, b)
    vec = pl.BlockSpec((1, d), lambda i: (0, 0))
    return pl.pallas_call(
        functools.partial(_proj_ln_kernel, alpha=alpha, res_is_raw=res_ln is not None),
        grid=(m // tm,),
        in_specs=[pl.BlockSpec((tm, k), lambda i: (i, 0)),
                  pl.BlockSpec((k, d), lambda i: (0, 0)),
                  pl.BlockSpec((tm, d), lambda i: (i, 0)),
                  vec, vec, vec, vec],
        out_specs=[pl.BlockSpec((tm, d), lambda i: (i, 0)),
                   pl.BlockSpec((tm, d), lambda i: (i, 0))],
        out_shape=[jax.ShapeDtypeStruct((m, d), f32), jax.ShapeDtypeStruct((m, d), bf16)],
        compiler_params=_cparams(("parallel",)),
        name="out_proj_ln",
    )(a, w, res, rg.reshape(1, d), rb.reshape(1, d), g.reshape(1, d), b.reshape(1, d))


def _ffn_kernel(x_ref, wg_ref, wu_ref, wd_ref, res_ref, g_ref, b_ref, o_ref, *, alpha):
    j = pl.program_id(1)
    rw = res_ref.shape[1]

    @pl.when(j == 0)
    def _():
        o_ref[...] = jnp.zeros_like(o_ref)

    x = x_ref[...]
    hid = _silu(_dot(x, wg_ref[...].astype(bf16))) * _dot(x, wu_ref[...].astype(bf16))
    o_ref[...] += _dot(hid.astype(bf16), wd_ref[...].astype(bf16))

    @pl.when(j < o_ref.shape[1] // rw)
    def _():
        cols = pl.ds(pl.multiple_of(j * rw, rw), rw)
        o_ref[:, cols] += alpha * res_ref[...]

    @pl.when(j == pl.num_programs(1) - 1)
    def _():
        o_ref[...] = _ln_rows(o_ref[...], g_ref[...], b_ref[...])


def _ffn_residual_ln(x16, x32, wg, wu, wd, g, b, alpha, tm=FFN_TM, tf=FFN_TF):
    m, d = x16.shape
    ff = wg.shape[1]
    n_res = d // LANES
    assert n_res <= ff // tf
    return pl.pallas_call(
        functools.partial(_ffn_kernel, alpha=alpha),
        grid=(m // tm, ff // tf),
        in_specs=[pl.BlockSpec((tm, d), lambda i, j: (i, 0)),
                  pl.BlockSpec((d, tf), lambda i, j: (0, j)),
                  pl.BlockSpec((d, tf), lambda i, j: (0, j)),
                  pl.BlockSpec((tf, d), lambda i, j: (j, 0)),
                  pl.BlockSpec((tm, LANES), lambda i, j: (i, jnp.minimum(j, n_res - 1))),
                  pl.BlockSpec((1, d), lambda i, j: (0, 0)),
                  pl.BlockSpec((1, d), lambda i, j: (0, 0))],
        out_specs=pl.BlockSpec((tm, d), lambda i, j: (i, 0)),
        out_shape=jax.ShapeDtypeStruct((m, d), f32),
        compiler_params=_cparams(("parallel", "arbitrary")),
        name="ffn_ln",
    )(x16, wg, wu, wd, x32, g.reshape(1, d), b.reshape(1, d))


def _rope_tables(positions):
    half = MLA_ROPE // 2
    inv_freq = jnp.power(ROPE_THETA, -jnp.arange(half, dtype=f32) / half)
    ang = positions.astype(f32)[..., None] * inv_freq
    cos, sin = jnp.cos(ang), jnp.sin(ang)
    z = jnp.zeros_like(cos)
    pad = jnp.zeros((ang.shape[0], LANES - MLA_ROPE), f32)
    c = jnp.concatenate([cos, cos, pad], axis=-1)
    s1 = jnp.concatenate([-sin, z, pad], axis=-1)
    s2 = jnp.concatenate([z, sin, pad], axis=-1)
    return c, s1, s2


def _layer(res, res_ln, h16, mem16, rope, lb, alpha, w_in, hgrn_norm_g, g_cq, g_ckv, w_uq, w_ukv, w_memkv,
           w_branch, w_o, ln1_g, ln1_b, w_gate, w_up, w_down, ln2_g, ln2_b):
    s, d = res.shape
    hgw = HG_HEADS * HG_D
    q_rank = g_cq.shape[0]
    kv_rank = g_ckv.shape[0]
    memw = MEM_HEADS * MEM_HD
    o_cq = 5 * hgw
    o_ckv = o_cq + q_rank
    o_kr = o_ckv + kv_rank
    o_qm = o_kr + MLA_ROPE
    o_gt = o_qm + memw
    rope_c, rope_s1, rope_s2 = rope

    wt = jnp.swapaxes(w_in, 0, 1)

    proj_hg = _matmul(h16, wt, wt_rows=(0, o_cq), tm=TM_IN_PROJ, tn=TN_IN_PROJ, out_dtype=f32,
                      name="in_proj_hgrn")
    gsig = _matmul(h16, wt, wt_rows=(o_gt, N_BRANCH * d), tm=TM_IN_PROJ, tn=TN_GATES, out_dtype=bf16,
                   epilogue=_sigmoid, name="in_proj_gates")
    pr = _in_proj_rest(h16, wt, o_qm, memw, o_cq, o_ckv, o_kr, g_cq, g_ckv, rope)
    c_qm = 0
    c_cq = c_qm + memw
    c_ckv = c_cq + q_rank
    c_kr = c_ckv + kv_rank

    o_f, o_b = _hgrn_scan(proj_hg, lb)

    zq = jnp.zeros((q_rank, MLA_HEADS, MLA_QK_PAD - MLA_QK), f32)
    w_q = jnp.concatenate([w_uq.reshape(q_rank, MLA_HEADS, MLA_QK), zq], axis=-1)
    w_q = w_q.reshape(q_rank, MLA_HEADS * MLA_QK_PAD).astype(bf16)
    w_kv3 = w_ukv.reshape(kv_rank, MLA_HEADS, MLA_NOPE + MLA_V)
    w_kv = jnp.concatenate([w_kv3[:, :, :MLA_NOPE].reshape(kv_rank, -1),
                            w_kv3[:, :, MLA_NOPE:].reshape(kv_rank, -1)], axis=-1).astype(bf16)
    q = _mla_q(pr, c_cq, w_q, rope_c, rope_s1, rope_s2)
    k, v = _mla_kv(pr, c_ckv, c_kr, w_kv)
    y_mla = _mla_attention(q, k, v)

    kv_mem = _matmul(mem16, w_memkv, tm=mem16.shape[0], tn=TN_IN_PROJ, out_dtype=bf16, name="mem_kv_proj")
    y_mem = _mem_attention(pr, c_qm, kv_mem[:, :memw], kv_mem[:, memw:])

    merged = _merge(o_f, o_b, proj_hg, hgrn_norm_g, y_mla, y_mem, w_branch.astype(bf16), gsig)
    x32, x16 = _proj_residual_ln(merged, w_o.astype(bf16), res, res_ln, ln1_g, ln1_b, alpha)
    return _ffn_residual_ln(x16, x32, w_gate, w_up, w_down, ln2_g, ln2_b, alpha)


def kernel(x, mem, positions, ln_emb_g, ln_emb_b, hgrn_lb_logits, w_in, hgrn_norm_g, mla_g_cq, mla_g_ckv,
           mla_w_uq, mla_w_ukv, mem_w_kv, w_branch, w_o, ln1_g, ln1_b, w_ffn_gate, w_ffn_up, w_ffn_down,
           ln2_g, ln2_b):
    bsz = x.shape[0]
    depth = w_in.shape[0]
    alpha = (2.0 * depth) ** 0.25
    outs = []
    for bi in range(bsz):
        rope = _rope_tables(positions[bi])
        mem16 = mem[bi].astype(bf16)
        h16 = _layernorm_bf16(x[bi], ln_emb_g, ln_emb_b)
        res, res_ln = x[bi], (ln_emb_g, ln_emb_b)
        for l in range(depth):
            lb = _lower_bounds(hgrn_lb_logits, l)
            res = _layer(res, res_ln, h16, mem16, rope, lb, alpha, w_in[l], hgrn_norm_g[l], mla_g_cq[l],
                         mla_g_ckv[l], mla_w_uq[l], mla_w_ukv[l], mem_w_kv[l], w_branch[l], w_o[l], ln1_g[l],
                         ln1_b[l], w_ffn_gate[l], w_ffn_up[l], w_ffn_down[l], ln2_g[l], ln2_b[l])
            res_ln = None
            if l + 1 < depth:
                h16 = res.astype(bf16)
        outs.append(res)
    return jnp.stack(outs)
```

```python
import functools

import jax
import jax.numpy as jnp
from jax import lax
from jax.experimental import pallas as pl
from jax.experimental.pallas import tpu as pltpu

HG_HEADS = 8
HG_D = 128
MLA_HEADS = 8
MLA_NOPE = 128
MLA_ROPE = 64
MLA_V = 128
MLA_QK = MLA_NOPE + MLA_ROPE
MLA_QK_PAD = 256
LOG2E = 1.4426950408889634
MEM_HEADS = 4
MEM_HD = 256
N_BRANCH = 3
ROPE_THETA = 10000.0
LN_EPS = 1e-5
RMS_EPS = 1e-6

LANES = 128
SUBLANES = 8
VMEM_LIMIT = 56 * 1024 * 1024

TM_IN_PROJ = 2048
TN_IN_PROJ = 512
TN_GATES = 1024
TM_ROW = 512
TM_LIGHT = 1024
ATTN_TQ = 512
ATTN_TK = 256
FFN_TM = 1024
FFN_TF = 256
PROJ_LN_GROUPS = 2
HG_BLOCK = 2048
HG_CHUNK = 64
HG_SUB = SUBLANES

bf16 = jnp.bfloat16
f32 = jnp.float32


def _cparams(sem):
    return pltpu.CompilerParams(dimension_semantics=sem, vmem_limit_bytes=VMEM_LIMIT)


def _dot(a, b):
    return jnp.dot(a, b, preferred_element_type=f32)


def _dot_nt(a, b):
    return lax.dot_general(a, b, (((1,), (1,)), ((), ())), preferred_element_type=f32)


def _dot_tn(a, b):
    return lax.dot_general(a, b, (((0,), (0,)), ((), ())), preferred_element_type=f32)


def _ln_rows(x, g, b):
    mu = jnp.mean(x, axis=-1, keepdims=True)
    xc = x - mu
    var = jnp.mean(xc * xc, axis=-1, keepdims=True)
    return xc * lax.rsqrt(var + LN_EPS) * g + b


def _ln_kernel(x_ref, g_ref, b_ref, o16_ref):
    o16_ref[...] = _ln_rows(x_ref[...], g_ref[...], b_ref[...]).astype(bf16)


def _layernorm_bf16(x, g, b, tm=TM_LIGHT):
    m, d = x.shape
    return pl.pallas_call(
        _ln_kernel,
        grid=(m // tm,),
        in_specs=[pl.BlockSpec((tm, d), lambda i: (i, 0)),
                  pl.BlockSpec((1, d), lambda i: (0, 0)),
                  pl.BlockSpec((1, d), lambda i: (0, 0))],
        out_specs=pl.BlockSpec((tm, d), lambda i: (i, 0)),
        out_shape=jax.ShapeDtypeStruct((m, d), bf16),
        compiler_params=_cparams(("parallel",)),
        name="ln_embed",
    )(x, g.reshape(1, d), b.reshape(1, d))


def _mm_kernel(a_ref, w_ref, o_ref, *, epilogue, w_transposed):
    w = w_ref[...].astype(bf16)
    acc = _dot_nt(a_ref[...], w) if w_transposed else _dot(a_ref[...], w)
    o_ref[...] = epilogue(acc).astype(o_ref.dtype)


def _matmul(a, w, *, tm, tn, out_dtype, epilogue=lambda acc: acc, name, wt_rows=None):
    m, k = a.shape
    if wt_rows is None:
        n = w.shape[1]
        w_spec = pl.BlockSpec((k, tn), lambda i, j: (0, j))
    else:
        row0, n = wt_rows
        assert row0 % SUBLANES == 0 and w.shape[1] == k
        w_spec = pl.BlockSpec((pl.Element(tn), pl.Element(k)),
                              lambda i, j: ((row0 // SUBLANES + j * (tn // SUBLANES)) * SUBLANES, 0))
    return pl.pallas_call(
        functools.partial(_mm_kernel, epilogue=epilogue, w_transposed=wt_rows is not None),
        grid=(m // tm, n // tn),
        in_specs=[pl.BlockSpec((tm, k), lambda i, j: (i, 0)), w_spec],
        out_specs=pl.BlockSpec((tm, tn), lambda i, j: (i, j)),
        out_shape=jax.ShapeDtypeStruct((m, n), out_dtype),
        compiler_params=_cparams(("parallel", "parallel")),
        name=name,
    )(a, w)


def _sigmoid(x):
    return 0.5 * jnp.tanh(0.5 * x) + 0.5


def _silu(x):
    return x * _sigmoid(x)


def _rms_epilogue(acc, g):
    ms = jnp.mean(acc * acc, axis=-1, keepdims=True)
    return acc * lax.rsqrt(ms + RMS_EPS) * g


def _rope_block(blk, c, s1, s2):
    half = MLA_ROPE // 2
    return blk * c + pltpu.roll(blk, LANES - half, 1) * s1 + pltpu.roll(blk, half, 1) * s2


def _in_proj_rest_kernel(a_ref, w_ref, gcq_ref, gckv_ref, c_ref, s1_ref, s2_ref, o_ref, *, n_qmem):
    j = pl.program_id(1)

    def product(w_rows):
        return _dot_nt(a_ref[...], w_ref[:w_rows, :].astype(bf16))

    tn = w_ref.shape[0]

    @pl.when(j < n_qmem)
    def _():
        o_ref[...] = (product(tn) * (MEM_HD ** -0.5)).astype(o_ref.dtype)

    @pl.when(j == n_qmem)
    def _():
        o_ref[...] = _rms_epilogue(product(tn), gcq_ref[...]).astype(o_ref.dtype)

    @pl.when(j == n_qmem + 1)
    def _():
        o_ref[...] = _rms_epilogue(product(tn), gckv_ref[...]).astype(o_ref.dtype)

    @pl.when(j == n_qmem + 2)
    def _():
        o_ref[:, :LANES] = _rope_block(product(LANES), c_ref[...], s1_ref[...], s2_ref[...]).astype(o_ref.dtype)
        o_ref[:, LANES:] = jnp.zeros((o_ref.shape[0], tn - LANES), o_ref.dtype)


def _in_proj_rest(h16, wt, o_qm, n_qm, o_cq, o_ckv, o_kr, g_cq, g_ckv, rope, tm=TM_IN_PROJ, tn=TN_IN_PROJ):
    m, k = h16.shape
    n_qmem = n_qm // tn
    assert n_qm % tn == 0 and g_cq.shape[0] == tn and g_ckv.shape[0] == tn
    assert all(o % SUBLANES == 0 for o in (o_qm, o_cq, o_ckv, o_kr))
    n_tiles = n_qmem + 3

    def w_row(i, j):
        r = jnp.where(j < n_qmem, o_qm // SUBLANES + j * (tn // SUBLANES),
                      jnp.where(j == n_qmem, o_cq // SUBLANES,
                                jnp.where(j == n_qmem + 1, o_ckv // SUBLANES, o_kr // SUBLANES)))
        return (r * SUBLANES, 0)

    vec = pl.BlockSpec((1, tn), lambda i, j: (0, 0))
    tab = pl.BlockSpec((tm, LANES), lambda i, j: (i, 0))
    return pl.pallas_call(
        functools.partial(_in_proj_rest_kernel, n_qmem=n_qmem),
        grid=(m // tm, n_tiles),
        in_specs=[pl.BlockSpec((tm, k), lambda i, j: (i, 0)),
                  pl.BlockSpec((pl.Element(tn), pl.Element(k)), w_row),
                  vec, vec, tab, tab, tab],
        out_specs=pl.BlockSpec((tm, tn), lambda i, j: (i, j)),
        out_shape=jax.ShapeDtypeStruct((m, n_tiles * tn), bf16),
        compiler_params=_cparams(("parallel", "arbitrary")),
        name="in_proj_rest",
    )(h16, wt, g_cq.reshape(1, tn), g_ckv.reshape(1, tn), *rope)


def _mla_q_kernel(cq_ref, w_ref, c_ref, s1_ref, s2_ref, o_ref):
    q = _dot(cq_ref[...], w_ref[...]) * (MLA_QK ** -0.5 * LOG2E)
    c, s1, s2 = c_ref[...], s1_ref[...], s2_ref[...]
    for h in range(MLA_HEADS):
        base = h * MLA_QK_PAD
        o_ref[:, base:base + MLA_NOPE] = q[:, base:base + MLA_NOPE].astype(bf16)
        o_ref[:, base + MLA_NOPE:base + MLA_QK_PAD] = _rope_block(
            q[:, base + MLA_NOPE:base + MLA_QK_PAD], c, s1, s2).astype(bf16)


def _mla_q(pr, cq_col, w_q, rope_c, rope_s1, rope_s2, tm=TM_LIGHT):
    m = pr.shape[0]
    r, n = w_q.shape
    assert cq_col % r == 0
    cqn = pr
    return pl.pallas_call(
        _mla_q_kernel,
        grid=(m // tm,),
        in_specs=[pl.BlockSpec((tm, r), lambda i: (i, cq_col // r)),
                  pl.BlockSpec((r, n), lambda i: (0, 0)),
                  pl.BlockSpec((tm, LANES), lambda i: (i, 0)),
                  pl.BlockSpec((tm, LANES), lambda i: (i, 0)),
                  pl.BlockSpec((tm, LANES), lambda i: (i, 0))],
        out_specs=pl.BlockSpec((tm, n), lambda i: (i, 0)),
        out_shape=jax.ShapeDtypeStruct((m, n), bf16),
        compiler_params=_cparams(("parallel",)),
        name="mla_q_up",
    )(cqn, w_q, rope_c, rope_s1, rope_s2)


def _mla_kv_kernel(ckv_ref, w_ref, kr_ref, k_ref, v_ref):
    kv = _dot(ckv_ref[...], w_ref[...])
    kr = kr_ref[...]
    nk = MLA_HEADS * MLA_NOPE
    for h in range(MLA_HEADS):
        base = h * MLA_QK_PAD
        k_ref[:, base:base + MLA_NOPE] = kv[:, h * MLA_NOPE:(h + 1) * MLA_NOPE].astype(bf16)
        k_ref[:, base + MLA_NOPE:base + MLA_QK_PAD] = kr
    v_ref[...] = kv[:, nk:].astype(bf16)


def _mla_kv(pr, ckv_col, kr_col, w_kv, tm=TM_LIGHT):
    m = pr.shape[0]
    r, n = w_kv.shape
    assert ckv_col % r == 0 and kr_col % LANES == 0
    ckvn = krope = pr
    nk = MLA_HEADS * MLA_QK_PAD
    nv = MLA_HEADS * MLA_V
    return pl.pallas_call(
        _mla_kv_kernel,
        grid=(m // tm,),
        in_specs=[pl.BlockSpec((tm, r), lambda i: (i, ckv_col // r)),
                  pl.BlockSpec((r, n), lambda i: (0, 0)),
                  pl.BlockSpec((tm, LANES), lambda i: (i, kr_col // LANES))],
        out_specs=[pl.BlockSpec((tm, nk), lambda i: (i, 0)),
                   pl.BlockSpec((tm, nv), lambda i: (i, 0))],
        out_shape=[jax.ShapeDtypeStruct((m, nk), bf16), jax.ShapeDtypeStruct((m, nv), bf16)],
        compiler_params=_cparams(("parallel",)),
        name="mla_kv_up",
    )(ckvn, w_kv, krope)


def _mla_attn_kernel(q_ref, k_ref, v_ref, o_ref, s_ref, *, tq, tk):
    nq = q_ref.shape[0] // tq
    nkv = k_ref.shape[0] // tk

    def scores(q, c):
        return _dot_nt(q, k_ref[c * tk:(c + 1) * tk, :])

    def update(s, c, m, l, acc):
        m_new = jnp.maximum(m, jnp.broadcast_to(jnp.max(s, axis=-1, keepdims=True), m.shape))
        a = jnp.exp2(m - m_new)
        p = [jnp.exp2(s[:, g * LANES:(g + 1) * LANES] - m_new) for g in range(tk // LANES)]
        l = a * l + functools.reduce(lambda x, y: x + y, p)
        acc = a * acc + _dot(jnp.concatenate(p, axis=1).astype(bf16), v_ref[c * tk:(c + 1) * tk, :])
        return m_new, l, acc

    s_ref[...] = scores(q_ref[0:tq, :], 0)

    def body(blk, carry):
        off = pl.multiple_of(blk * tq, tq)
        q = q_ref[pl.ds(off, tq), :]
        m = jnp.full((tq, LANES), -jnp.inf, f32)
        l = jnp.zeros((tq, LANES), f32)
        acc = jnp.zeros((tq, MLA_V), f32)
        s_next = s_ref[...]
        for c in range(nkv):
            s_cur = s_next
            if c + 1 < nkv:
                s_next = scores(q, c + 1)
            else:
                nxt = pl.multiple_of(jnp.minimum(blk + 1, nq - 1) * tq, tq)
                s_ref[...] = scores(q_ref[pl.ds(nxt, tq), :], 0)
            m, l, acc = update(s_cur, c, m, l, acc)
        o_ref[pl.ds(off, tq), :] = (acc / jnp.sum(l, axis=-1, keepdims=True)).astype(o_ref.dtype)
        return carry

    lax.fori_loop(0, nq, body, 0)


def _mla_attention(q, k, v, tq=ATTN_TQ, tk=ATTN_TK):
    s = q.shape[0]
    return pl.pallas_call(
        functools.partial(_mla_attn_kernel, tq=tq, tk=tk),
        grid=(MLA_HEADS,),
        in_specs=[pl.BlockSpec((s, MLA_QK_PAD), lambda h: (0, h)),
                  pl.BlockSpec((s, MLA_QK_PAD), lambda h: (0, h)),
                  pl.BlockSpec((s, MLA_V), lambda h: (0, h))],
        out_specs=pl.BlockSpec((s, MLA_V), lambda h: (0, h)),
        out_shape=jax.ShapeDtypeStruct((s, MLA_HEADS * MLA_V), bf16),
        scratch_shapes=[pltpu.VMEM((tq, tk), f32)],
        compiler_params=_cparams(("parallel",)),
        name="mla_attention",
    )(q, k, v)


def _mem_attn_kernel(q_ref, k_ref, v_ref, o_ref):
    for h in range(MEM_HEADS):
        sl = slice(h * MEM_HD, (h + 1) * MEM_HD)
        s = _dot_nt(q_ref[:, sl], k_ref[:, sl])
        m = jnp.max(s, axis=-1, keepdims=True)
        p = jnp.exp(s - m)
        l = jnp.sum(p, axis=-1, keepdims=True)
        o = _dot(p.astype(bf16), v_ref[:, sl])
        o_ref[:, sl] = (o / l).astype(o_ref.dtype)


def _mem_attention(pr, q_col, k, v, tm=TM_LIGHT):
    s = pr.shape[0]
    nm, w = k.shape
    assert q_col % w == 0
    q = pr
    return pl.pallas_call(
        _mem_attn_kernel,
        grid=(s // tm,),
        in_specs=[pl.BlockSpec((tm, w), lambda i: (i, q_col // w)),
                  pl.BlockSpec((nm, w), lambda i: (0, 0)),
                  pl.BlockSpec((nm, w), lambda i: (0, 0))],
        out_specs=pl.BlockSpec((tm, w), lambda i: (i, 0)),
        out_shape=jax.ShapeDtypeStruct((s, w), bf16),
        compiler_params=_cparams(("parallel",)),
        name="mem_attention",
    )(q, k, v)


def _hgrn_levels(rev):
    levels = []
    sz = HG_CHUNK // 2
    while sz >= HG_SUB:
        blocks = []
        for base in range(0, HG_CHUNK, 2 * sz):
            if not rev:
                blocks.append((base + sz, base, sz, base + sz - 1))
            else:
                blocks.append((base, base + sz, sz, base + sz))
        levels.append(blocks)
        sz //= 2
    return levels


def _hgrn_intra(q_c, b_c, c_c, bbuf, cbuf, row0, *, rev):
    C = HG_CHUNK
    nb = C // HG_SUB
    lane = lax.broadcasted_iota(jnp.int32, (HG_SUB, C), 1)
    sub = lax.broadcasted_iota(jnp.int32, (HG_SUB, C), 0)
    blocks = []
    for i in range(nb):
        r0 = i * HG_SUB
        q_i = q_c[r0:r0 + HG_SUB]
        b_i = b_c[r0:r0 + HG_SUB]
        a_blk = jnp.zeros((HG_SUB, C), f32)
        for s in range(HG_SUB):
            w = q_i * jnp.exp2(b_i + cbuf[row0 + r0 + s:row0 + r0 + s + 1, :])
            a_blk = jnp.where(lane == r0 + s, jnp.sum(w, axis=-1, keepdims=True), a_blk)
        keep = (sub >= lane - r0) if not rev else (sub <= lane - r0)
        blocks.append(jnp.where(keep, a_blk, 0.0))
    for level in _hgrn_levels(rev):
        sz = level[0][2]
        qd, kd, valid = [], [], None
        k_pos = 0
        rq = lax.broadcasted_iota(jnp.int32, (len(level) * sz, C), 0)
        ck = lax.broadcasted_iota(jnp.int32, (len(level) * sz, C), 1)
        for j, (q0, k0, _, piv_row) in enumerate(level):
            piv = bbuf[row0 + piv_row:row0 + piv_row + 1, :]
            qd.append(q_c[q0:q0 + sz] * jnp.exp2(b_c[q0:q0 + sz] - piv))
            if k0 > k_pos:
                kd.append(jnp.zeros((k0 - k_pos, q_c.shape[1]), f32))
            kd.append(jnp.exp2(c_c[k0:k0 + sz] + piv))
            k_pos = k0 + sz
            ok = (rq >= j * sz) & (rq < (j + 1) * sz) & (ck >= k0) & (ck < k0 + sz)
            valid = ok if valid is None else (valid | ok)
        if k_pos < C:
            kd.append(jnp.zeros((C - k_pos, q_c.shape[1]), f32))
        r = _dot_nt(jnp.concatenate(qd, axis=0).astype(bf16), jnp.concatenate(kd, axis=0).astype(bf16))
        if len(level) > 1:
            r = jnp.where(valid, r, 0.0)
        for j, (q0, _, _, _) in enumerate(level):
            for t in range(sz // HG_SUB):
                i = q0 // HG_SUB + t
                blocks[i] = blocks[i] + r[j * sz + t * HG_SUB:j * sz + (t + 1) * HG_SUB]
    return jnp.concatenate(blocks, axis=0)


def _hgrn_block_kernel(qf_ref, vf_ref, ff_ref, qb_ref, vb_ref, fb_ref, lb_ref, of_ref, ob_ref,
                       sf_ref, sb_ref, bbuf, cbuf):
    n = pl.program_id(1)
    C = HG_CHUNK
    T = qf_ref.shape[0]
    nchunk = T // C

    @pl.when(n == 0)
    def _():
        sf_ref[...] = jnp.zeros_like(sf_ref)
        sb_ref[...] = jnp.zeros_like(sb_ref)

    r = lax.broadcasted_iota(jnp.int32, (C, C), 0)
    c = lax.broadcasted_iota(jnp.int32, (C, C), 1)
    dirs = []
    for d, (q_ref, v_ref, f_ref, rev) in enumerate(((qf_ref, vf_ref, ff_ref, False), (qb_ref, vb_ref, fb_ref, True))):
        tri = ((c <= r) if not rev else (c >= r)).astype(bf16)
        lb = lb_ref[d:d + 1, :]
        q = _silu(q_ref[...])
        v16 = v_ref[...].astype(bf16)
        f = lb + (1.0 - lb) * _sigmoid(f_ref[...])
        g = jnp.log(f) * LOG2E
        g_hi = g.astype(bf16)
        g_lo = (g - g_hi.astype(f32)).astype(bf16)
        b = jnp.concatenate([_dot(tri, g_hi[j * C:(j + 1) * C]) + _dot(tri, g_lo[j * C:(j + 1) * C])
                             for j in range(nchunk)], axis=0)
        cc = jnp.log(1.0 - f) * LOG2E - b
        bbuf[d] = b
        cbuf[d] = cc
        dirs.append((q, v16, b, cc, rev))

    intra = []
    for d, (q, v16, b, cc, rev) in enumerate(dirs):
        intra.append([_hgrn_intra(q[j * C:(j + 1) * C], b[j * C:(j + 1) * C], cc[j * C:(j + 1) * C],
                                  bbuf.at[d], cbuf.at[d], j * C, rev=rev).astype(bf16) for j in range(nchunk)])

    for d, (q, v16, b, cc, rev) in enumerate(dirs):
        s_ref, o_ref = (sf_ref, of_ref) if not rev else (sb_ref, ob_ref)
        qe = (q * jnp.exp2(b)).astype(bf16)
        order = range(nchunk) if not rev else range(nchunk - 1, -1, -1)
        edge = (C - 1) if not rev else 0
        upd = {}
        for j in order:
            b_edge = bbuf[d, j * C + edge:j * C + edge + 1, :]
            kdec = jnp.exp2(cc[j * C:(j + 1) * C] + b_edge).astype(bf16)
            upd[j] = (jnp.exp2(b_edge), _dot_tn(v16[j * C:(j + 1) * C], kdec))
        state = s_ref[...]
        for j in order:
            rows = slice(j * C, (j + 1) * C)
            o = _dot_nt(qe[rows], state.astype(bf16)) + _dot(intra[d][j], v16[rows])
            o_ref[rows, :] = o.astype(o_ref.dtype)
            state = upd[j][0] * state + upd[j][1]
        s_ref[...] = state


def _hgrn_scan(proj, lb, tb=HG_BLOCK):
    s = proj.shape[0]
    w = HG_HEADS * HG_D
    nb = s // tb
    hb = w // HG_D
    def fw(seg):
        return pl.BlockSpec((tb, HG_D), lambda h, n: (n, seg * hb + h))

    def bw(seg):
        return pl.BlockSpec((tb, HG_D), lambda h, n: (nb - 1 - n, seg * hb + h))

    return pl.pallas_call(
        _hgrn_block_kernel,
        grid=(HG_HEADS, nb),
        in_specs=[fw(0), fw(1), fw(2), bw(0), bw(1), bw(3),
                  pl.BlockSpec((2, HG_D), lambda h, n: (0, h))],
        out_specs=[pl.BlockSpec((tb, HG_D), lambda h, n: (n, h)),
                   pl.BlockSpec((tb, HG_D), lambda h, n: (nb - 1 - n, h))],
        out_shape=[jax.ShapeDtypeStruct((s, w), bf16), jax.ShapeDtypeStruct((s, w), bf16)],
        scratch_shapes=[pltpu.VMEM((HG_D, HG_D), f32), pltpu.VMEM((HG_D, HG_D), f32),
                        pltpu.VMEM((2, tb, HG_D), f32), pltpu.VMEM((2, tb, HG_D), f32)],
        compiler_params=_cparams(("parallel", "arbitrary")),
        name="hgrn_scan",
    )(proj, proj, proj, proj, proj, proj, lb)


def _lb_kernel(logit_ref, lb_ref, *, layer):
    x = logit_ref[...]
    e = jnp.exp(x - jnp.max(x, axis=1, keepdims=True))
    p = e / jnp.sum(e, axis=1, keepdims=True)
    acc = p[:, 0, :]
    for l in range(1, layer + 1):
        acc = acc + p[:, l, :]
    lb_ref[...] = acc


def _lower_bounds(logits, layer):
    z, _, w = logits.shape
    return pl.pallas_call(
        functools.partial(_lb_kernel, layer=layer),
        out_shape=jax.ShapeDtypeStruct((z, w), f32),
        name="hgrn_lower_bounds",
    )(logits)


def _merge_kernel(of_ref, ob_ref, hg_ref, ng_ref, y1_ref, y2_ref, w_ref, g0_ref, g1_ref, g2_ref, o_ref):
    acc = g1_ref[...].astype(f32) * _dot(y1_ref[...], w_ref[1])
    acc = acc + g2_ref[...].astype(f32) * _dot(y2_ref[...], w_ref[2])
    ng = ng_ref[...]
    y0 = []
    for h in range(HG_HEADS):
        sl = slice(h * HG_D, (h + 1) * HG_D)
        o = of_ref[:, sl].astype(f32) + ob_ref[:, sl].astype(f32)
        ms = jnp.mean(o * o, axis=-1, keepdims=True)
        y0.append((o * lax.rsqrt(ms + RMS_EPS) * ng * _sigmoid(hg_ref[:, sl])).astype(bf16))
    acc = acc + g0_ref[...].astype(f32) * _dot(jnp.concatenate(y0, axis=1), w_ref[0])
    o_ref[...] = acc.astype(o_ref.dtype)


def _merge(o_f, o_b, proj_hg, norm_g, y1, y2, w_branch, gsig, tm=TM_ROW):
    m, kb = y1.shape
    d = w_branch.shape[2]
    hg_gate_blk = 4
    yspec = pl.BlockSpec((tm, kb), lambda i: (i, 0))
    def gspec(b):
        return pl.BlockSpec((tm, d), lambda i: (i, b))

    return pl.pallas_call(
        _merge_kernel,
        grid=(m // tm,),
        in_specs=[yspec, yspec,
                  pl.BlockSpec((tm, kb), lambda i: (i, hg_gate_blk)),
                  pl.BlockSpec((1, HG_D), lambda i: (0, 0)),
                  yspec, yspec,
                  pl.BlockSpec((N_BRANCH, kb, d), lambda i: (0, 0, 0), pipeline_mode=pl.Buffered(1)),
                  gspec(0), gspec(1), gspec(2)],
        out_specs=pl.BlockSpec((tm, d), lambda i: (i, 0)),
        out_shape=jax.ShapeDtypeStruct((m, d), bf16),
        compiler_params=_cparams(("parallel",)),
        name="branch_merge",
    )(o_f, o_b, proj_hg, norm_g.reshape(1, HG_D), y1, y2, w_branch, gsig, gsig, gsig)


def _proj_ln_kernel(a_ref, w_ref, res_ref, rg_ref, rb_ref, g_ref, b_ref, o32_ref, o16_ref, *, alpha, res_is_raw):
    rows_per_group = a_ref.shape[0] // PROJ_LN_GROUPS
    groups = [slice(r * rows_per_group, (r + 1) * rows_per_group) for r in range(PROJ_LN_GROUPS)]
    proj = [_dot(a_ref[rows, :], w_ref[...]) for rows in groups]
    for rows, p in zip(groups, proj):
        res = res_ref[rows, :]
        if res_is_raw:
            res = _ln_rows(res, rg_ref[...], rb_ref[...])
        y = _ln_rows(alpha * res + p, g_ref[...], b_ref[...])
        o32_ref[rows, :] = y
        o16_ref[rows, :] = y.astype(bf16)


def _proj_residual_ln(a, w, res, res_ln, g, b, alpha, tm=TM_ROW):
    m, k = a.shape
    d = w.shape[1]
    rg, rb = res_ln if res_ln is not None else (g, b)
    vec = pl.BlockSpec((1, d), lambda i: (0, 0))
    return pl.pallas_call(
        functools.partial(_proj_ln_kernel, alpha=alpha, res_is_raw=res_ln is not None),
        grid=(m // tm,),
        in_specs=[pl.BlockSpec((tm, k), lambda i: (i, 0)),
                  pl.BlockSpec((k, d), lambda i: (0, 0)),
                  pl.BlockSpec((tm, d), lambda i: (i, 0)),
                  vec, vec, vec, vec],
        out_specs=[pl.BlockSpec((tm, d), lambda i: (i, 0)),
                   pl.BlockSpec((tm, d), lambda i: (i, 0))],
        out_shape=[jax.ShapeDtypeStruct((m, d), f32), jax.ShapeDtypeStruct((m, d), bf16)],
        compiler_params=_cparams(("parallel",)),
        name="out_proj_ln",
    )(a, w, res, rg.reshape(1, d), rb.reshape(1, d), g.reshape(1, d), b.reshape(1, d))


def _ffn_kernel(x_ref, wg_ref, wu_ref, wd_ref, res_ref, g_ref, b_ref, o_ref, *, alpha):
    j = pl.program_id(1)
    rw = res_ref.shape[1]

    @pl.when(j == 0)
    def _():
        o_ref[...] = jnp.zeros_like(o_ref)

    x = x_ref[...]
    hid = _silu(_dot(x, wg_ref[...].astype(bf16))) * _dot(x, wu_ref[...].astype(bf16))
    o_ref[...] += _dot(hid.astype(bf16), wd_ref[...].astype(bf16))

    @pl.when(j < o_ref.shape[1] // rw)
    def _():
        cols = pl.ds(pl.multiple_of(j * rw, rw), rw)
        o_ref[:, cols] += alpha * res_ref[...]

    @pl.when(j == pl.num_programs(1) - 1)
    def _():
        o_ref[...] = _ln_rows(o_ref[...], g_ref[...], b_ref[...])


def _ffn_residual_ln(x16, x32, wg, wu, wd, g, b, alpha, tm=FFN_TM, tf=FFN_TF):
    m, d = x16.shape
    ff = wg.shape[1]
    n_res = d // LANES
    assert n_res <= ff // tf
    return pl.pallas_call(
        functools.partial(_ffn_kernel, alpha=alpha),
        grid=(m // tm, ff // tf),
        in_specs=[pl.BlockSpec((tm, d), lambda i, j: (i, 0)),
                  pl.BlockSpec((d, tf), lambda i, j: (0, j)),
                  pl.BlockSpec((d, tf), lambda i, j: (0, j)),
                  pl.BlockSpec((tf, d), lambda i, j: (j, 0)),
                  pl.BlockSpec((tm, LANES), lambda i, j: (i, jnp.minimum(j, n_res - 1))),
                  pl.BlockSpec((1, d), lambda i, j: (0, 0)),
                  pl.BlockSpec((1, d), lambda i, j: (0, 0))],
        out_specs=pl.BlockSpec((tm, d), lambda i, j: (i, 0)),
        out_shape=jax.ShapeDtypeStruct((m, d), f32),
        compiler_params=_cparams(("parallel", "arbitrary")),
        name="ffn_ln",
    )(x16, wg, wu, wd, x32, g.reshape(1, d), b.reshape(1, d))


def _rope_tables(positions):
    half = MLA_ROPE // 2
    inv_freq = jnp.power(ROPE_THETA, -jnp.arange(half, dtype=f32) / half)
    ang = positions.astype(f32)[..., None] * inv_freq
    cos, sin = jnp.cos(ang), jnp.sin(ang)
    z = jnp.zeros_like(cos)
    pad = jnp.zeros((ang.shape[0], LANES - MLA_ROPE), f32)
    c = jnp.concatenate([cos, cos, pad], axis=-1)
    s1 = jnp.concatenate([-sin, z, pad], axis=-1)
    s2 = jnp.concatenate([z, sin, pad], axis=-1)
    return c, s1, s2


def _layer(res, res_ln, h16, mem16, rope, lb, alpha, w_in, hgrn_norm_g, g_cq, g_ckv, w_uq, w_ukv, w_memkv,
           w_branch, w_o, ln1_g, ln1_b, w_gate, w_up, w_down, ln2_g, ln2_b):
    s, d = res.shape
    hgw = HG_HEADS * HG_D
    q_rank = g_cq.shape[0]
    kv_rank = g_ckv.shape[0]
    memw = MEM_HEADS * MEM_HD
    o_cq = 5 * hgw
    o_ckv = o_cq + q_rank
    o_kr = o_ckv + kv_rank
    o_qm = o_kr + MLA_ROPE
    o_gt = o_qm + memw
    rope_c, rope_s1, rope_s2 = rope

    wt = jnp.swapaxes(w_in, 0, 1)

    proj_hg = _matmul(h16, wt, wt_rows=(0, o_cq), tm=TM_IN_PROJ, tn=TN_IN_PROJ, out_dtype=f32,
                      name="in_proj_hgrn")
    gsig = _matmul(h16, wt, wt_rows=(o_gt, N_BRANCH * d), tm=TM_IN_PROJ, tn=TN_GATES, out_dtype=bf16,
                   epilogue=_sigmoid, name="in_proj_gates")
    pr = _in_proj_rest(h16, wt, o_qm, memw, o_cq, o_ckv, o_kr, g_cq, g_ckv, rope)
    c_qm = 0
    c_cq = c_qm + memw
    c_ckv = c_cq + q_rank
    c_kr = c_ckv + kv_rank

    o_f, o_b = _hgrn_scan(proj_hg, lb)

    zq = jnp.zeros((q_rank, MLA_HEADS, MLA_QK_PAD - MLA_QK), f32)
    w_q = jnp.concatenate([w_uq.reshape(q_rank, MLA_HEADS, MLA_QK), zq], axis=-1)
    w_q = w_q.reshape(q_rank, MLA_HEADS * MLA_QK_PAD).astype(bf16)
    w_kv3 = w_ukv.reshape(kv_rank, MLA_HEADS, MLA_NOPE + MLA_V)
    w_kv = jnp.concatenate([w_kv3[:, :, :MLA_NOPE].reshape(kv_rank, -1),
                            w_kv3[:, :, MLA_NOPE:].reshape(kv_rank, -1)], axis=-1).astype(bf16)
    q = _mla_q(pr, c_cq, w_q, rope_c, rope_s1, rope_s2)
    k, v = _mla_kv(pr, c_ckv, c_kr, w_kv)
    y_mla = _mla_attention(q, k, v)

    kv_mem = _matmul(mem16, w_memkv, tm=mem16.shape[0], tn=TN_IN_PROJ, out_dtype=bf16, name="mem_kv_proj")
    y_mem = _mem_attention(pr, c_qm, kv_mem[:, :memw], kv_mem[:, memw:])

    merged = _merge(o_f, o_b, proj_hg, hgrn_norm_g, y_mla, y_mem, w_branch.astype(bf16), gsig)
    x32, x16 = _proj_residual_ln(merged, w_o.astype(bf16), res, res_ln, ln1_g, ln1_b, alpha)
    return _ffn_residual_ln(x16, x32, w_gate, w_up, w_down, ln2_g, ln2_b, alpha)


def kernel(x, mem, positions, ln_emb_g, ln_emb_b, hgrn_lb_logits, w_in, hgrn_norm_g, mla_g_cq, mla_g_ckv,
           mla_w_uq, mla_w_ukv, mem_w_kv, w_branch, w_o, ln1_g, ln1_b, w_ffn_gate, w_ffn_up, w_ffn_down,
           ln2_g, ln2_b):
    bsz = x.shape[0]
    depth = w_in.shape[0]
    alpha = (2.0 * depth) ** 0.25
    outs = []
    for bi in range(bsz):
        rope = _rope_tables(positions[bi])
        mem16 = mem[bi].astype(bf16)
        h16 = _layernorm_bf16(x[bi], ln_emb_g, ln_emb_b)
        res, res_ln = x[bi], (ln_emb_g, ln_emb_b)
        for l in range(depth):
            lb = _lower_bounds(hgrn_lb_logits, l)
            res = _layer(res, res_ln, h16, mem16, rope, lb, alpha, w_in[l], hgrn_norm_g[l], mla_g_cq[l],
                         mla_g_ckv[l], mla_w_uq[l], mla_w_ukv[l], mem_w_kv[l], w_branch[l], w_o[l], ln1_g[l],
                         ln1_b[l], w_ffn_gate[l], w_ffn_up[l], w_ffn_down[l], ln2_g[l], ln2_b[l])
            res_ln = None
            if l + 1 < depth:
                h16 = res.astype(bf16)
        outs.append(res)
    return jnp.stack(outs)
```
